```python
import jax, jax.numpy as jnp
from jax import lax
import numpy as np

D_MODEL = 2048
BATCH = 2
SEQ = 16384
DEPTH = 2

HEAD_DIM = 128
ROPE_THETA = 10000.0
EPS = 1e-6
Q_BLOCK = 128
D_FF = 4 * D_MODEL
NEG_INF = -1e30
POS_INF = 1e30

NSA_HEADS = 8
NSA_KV_GROUPS = 2
NSA_CMP_LEN = 32
NSA_CMP_STRIDE = 16
NSA_SEL_LEN = 64
NSA_SEL_TOPK = 16
NSA_WINDOW = 512
NSA_CMP_HIDDEN = HEAD_DIM

MLA_HEADS = 8
MLA_Q_RANK = 512
MLA_KV_RANK = 256
MLA_NOPE = 128
MLA_ROPE = 64
MLA_V = 128

DIL_PATTERNS = ((128, 1), (512, 4), (2048, 16))
DIL_HEADS = 8

A_Q_COLS = NSA_HEADS * HEAD_DIM
A_KV_COLS = 3 * 2 * NSA_KV_GROUPS * HEAD_DIM
A_GATE_COLS = 3 * NSA_HEADS
B_COLS = MLA_Q_RANK + MLA_KV_RANK + MLA_ROPE
EVEN_IN = A_Q_COLS + A_KV_COLS + A_GATE_COLS + B_COLS
EVEN_OUT = NSA_HEADS * HEAD_DIM + MLA_HEADS * MLA_V
ODD_IN = len(DIL_PATTERNS) * 3 * DIL_HEADS * HEAD_DIM
ODD_OUT = DIL_HEADS * HEAD_DIM

kernel_name = "hybrid_nsa_mla_dilated_trunk"


def rms_norm(x, g):
    xf = x.astype(jnp.float32)
    y = xf * lax.rsqrt(jnp.mean(xf * xf, axis=-1, keepdims=True) + EPS)
    return (y * g.astype(jnp.float32)).astype(x.dtype)


def rope_tables(seq, dim):
    inv = 1.0 / (ROPE_THETA ** (jnp.arange(0, dim, 2, dtype=jnp.float32) / dim))
    ang = jnp.arange(seq, dtype=jnp.float32)[:, None] * inv[None, :]
    return jnp.cos(ang), jnp.sin(ang)


def apply_rope(x, cos, sin):
    half = x.shape[-1] // 2
    c = cos[None, :, None, :].astype(x.dtype)
    s = sin[None, :, None, :].astype(x.dtype)
    x1, x2 = x[..., :half], x[..., half:]
    return jnp.concatenate([x1 * c - x2 * s, x1 * s + x2 * c], axis=-1)


def masked_softmax(s, mask):
    return jax.nn.softmax(jnp.where(mask, s, NEG_INF), axis=-1) * mask


def nsa_compress(k, pe, w1, w2):
    B, S, G, d = k.shape
    n_cmp = (S - NSA_CMP_LEN) // NSA_CMP_STRIDE + 1
    idx = jnp.arange(n_cmp)[:, None] * NSA_CMP_STRIDE + jnp.arange(NSA_CMP_LEN)[None, :]
    blocks = k[:, idx] + pe[None, None, :, None, :].astype(k.dtype)
    flat = blocks.transpose(0, 1, 3, 2, 4).reshape(B, n_cmp, G, NSA_CMP_LEN * d)
    return jax.nn.gelu(flat @ w1) @ w2


def nsa_attention(q, k_cmp, v_cmp, k_slc, v_slc, k_win, v_win, gate):
    B, S, H, d = q.shape
    G = k_slc.shape[2]
    Hg = H // G
    n_cmp = k_cmp.shape[1]
    n_slc = S // NSA_SEL_LEN
    top_k = min(NSA_SEL_TOPK, n_slc)
    scale = d ** -0.5
    cmp_last = jnp.arange(n_cmp) * NSA_CMP_STRIDE + (NSA_CMP_LEN - 1)
    ratio = NSA_SEL_LEN // NSA_CMP_STRIDE
    offs = np.arange(1 - NSA_CMP_LEN // NSA_CMP_STRIDE, ratio)
    ov = np.arange(n_slc)[:, None] * ratio + offs[None, :]
    ov_ok = jnp.asarray((ov >= 0) & (ov < n_cmp))
    ov = jnp.asarray(np.clip(ov, 0, n_cmp - 1))
    ks_b = k_slc.reshape(B, n_slc, NSA_SEL_LEN, G, d).transpose(0, 3, 1, 2, 4)
    vs_b = v_slc.reshape(B, n_slc, NSA_SEL_LEN, G, d).transpose(0, 3, 1, 2, 4)
    kw_p = jnp.pad(k_win, ((0, 0), (NSA_WINDOW, 0), (0, 0), (0, 0)))
    vw_p = jnp.pad(v_win, ((0, 0), (NSA_WINDOW, 0), (0, 0), (0, 0)))
    b_ix = jnp.arange(B)[:, None, None, None]
    g_ix = jnp.arange(G)[None, :, None, None]
    blk = jnp.arange(n_slc)
    sel_tok = jnp.arange(NSA_SEL_LEN)
    win_off = jnp.arange(Q_BLOCK + NSA_WINDOW) - NSA_WINDOW

    def one_block(s0):
        t = s0 + jnp.arange(Q_BLOCK)
        qb = lax.dynamic_slice_in_dim(q, s0, Q_BLOCK, 1).reshape(B, Q_BLOCK, G, Hg, d)
        gb = lax.dynamic_slice_in_dim(gate, s0, Q_BLOCK, 1).reshape(B, Q_BLOCK, G, Hg, 3)
        s_c = jnp.einsum('bqghd,bngd->bghqn', qb, k_cmp).astype(jnp.float32) * scale
        p_c = masked_softmax(s_c, cmp_last[None, :] <= t[:, None])
        o_c = jnp.einsum('bghqn,bngd->bqghd', p_c.astype(v_cmp.dtype), v_cmp)
        imp = jnp.sum(jnp.where(ov_ok, p_c[..., ov], 0.0), axis=(2, 5))
        cur = (t // NSA_SEL_LEN)[:, None]
        forced = (blk[None, :] == 0) | (blk[None, :] == cur) | (blk[None, :] == cur - 1)
        causal = blk[None, :] * NSA_SEL_LEN <= t[:, None]
        score = jnp.where(forced, POS_INF, jnp.where(causal, imp, NEG_INF))
        top_val, top_idx = lax.top_k(score, top_k)
        sel_ok = top_val > 0.5 * NEG_INF
        kg = ks_b[b_ix, g_ix, top_idx]
        vg = vs_b[b_ix, g_ix, top_idx]
        s_s = jnp.einsum('bqghd,bgqkld->bghqkl', qb, kg).astype(jnp.float32) * scale
        tok = top_idx[..., None] * NSA_SEL_LEN + sel_tok
        m_s = (tok <= t[None, None, :, None, None]) & sel_ok[..., None]
        p_s = masked_softmax(s_s.reshape(B, G, Hg, Q_BLOCK, -1), m_s.reshape(B, G, 1, Q_BLOCK, -1))
        o_s = jnp.einsum('bghqm,bgqmd->bqghd', p_s.astype(vg.dtype), vg.reshape(B, G, Q_BLOCK, -1, d))
        kw = lax.dynamic_slice_in_dim(kw_p, s0, Q_BLOCK + NSA_WINDOW, 1)
        vw = lax.dynamic_slice_in_dim(vw_p, s0, Q_BLOCK + NSA_WINDOW, 1)
        pos = s0 + win_off
        rel = t[:, None] - pos[None, :]
        m_w = (rel >= 0) & (rel < NSA_WINDOW) & (pos[None, :] >= 0)
        s_w = jnp.einsum('bqghd,bkgd->bghqk', qb, kw).astype(jnp.float32) * scale
        p_w = masked_softmax(s_w, m_w)
        o_w = jnp.einsum('bghqk,bkgd->bqghd', p_w.astype(vw.dtype), vw)
        o = gb[..., 0:1] * o_c + gb[..., 1:2] * o_s + gb[..., 2:3] * o_w
        return o.reshape(B, Q_BLOCK, H, d)

    out = lax.map(one_block, jnp.arange(S // Q_BLOCK) * Q_BLOCK)
    return out.transpose(1, 0, 2, 3, 4).reshape(B, S, H, d)


def mla_attention(c_q, c_kv, k_rope, q_norm, w_uq, kv_norm, w_ukv):
    B, S, _ = c_q.shape
    cos_r, sin_r = rope_tables(S, MLA_ROPE)
    q = (rms_norm(c_q, q_norm) @ w_uq).reshape(B, S, MLA_HEADS, MLA_NOPE + MLA_ROPE)
    q_nope = q[..., :MLA_NOPE]
    q_pe = apply_rope(q[..., MLA_NOPE:], cos_r, sin_r)
    kv = (rms_norm(c_kv, kv_norm) @ w_ukv).reshape(B, S, MLA_HEADS, MLA_NOPE + MLA_V)
    k_nope, v = kv[..., :MLA_NOPE], kv[..., MLA_NOPE:]
    k_pe = apply_rope(k_rope[:, :, None, :], cos_r, sin_r)[:, :, 0]
    scale = (MLA_NOPE + MLA_ROPE) ** -0.5
    key_pos = jnp.arange(S)

    def one_block(s0):
        t = s0 + jnp.arange(Q_BLOCK)
        qn = lax.dynamic_slice_in_dim(q_nope, s0, Q_BLOCK, 1)
        qp = lax.dynamic_slice_in_dim(q_pe, s0, Q_BLOCK, 1)
        s = (jnp.einsum('bqhd,bkhd->bhqk', qn, k_nope)
             + jnp.einsum('bqhd,bkd->bhqk', qp, k_pe)).astype(jnp.float32) * scale
        p = masked_softmax(s, key_pos[None, :] <= t[:, None])
        return jnp.einsum('bhqk,bkhd->bqhd', p.astype(v.dtype), v)

    out = lax.map(one_block, jnp.arange(S // Q_BLOCK) * Q_BLOCK)
    return out.transpose(1, 0, 2, 3, 4).reshape(B, S, MLA_HEADS, MLA_V)


def dilated_attention(qs, ks, vs):
    B, S, H, d = qs[0].shape
    scale = d ** -0.5
    kps = [jnp.pad(k, ((0, 0), (w, 0), (0, 0), (0, 0))) for k, (w, r) in zip(ks, DIL_PATTERNS)]
    vps = [jnp.pad(v, ((0, 0), (w, 0), (0, 0), (0, 0))) for v, (w, r) in zip(vs, DIL_PATTERNS)]

    def one_block(s0):
        t = s0 + jnp.arange(Q_BLOCK)
        outs, lses = [], []
        for (w, r), q, kp, vp in zip(DIL_PATTERNS, qs, kps, vps):
            dist = jnp.arange(w // r + 1) * r
            idx = t[:, None] + w - dist[None, :]
            kg = kp[:, idx]
            vg = vp[:, idx]
            qb = lax.dynamic_slice_in_dim(q, s0, Q_BLOCK, 1)
            s = jnp.einsum('bqhd,bqnhd->bhqn', qb, kg).astype(jnp.float32) * scale
            s = jnp.where((t[:, None] - dist[None, :]) >= 0, s, NEG_INF)
            m = jnp.max(s, axis=-1, keepdims=True)
            e = jnp.exp(s - m)
            den = jnp.sum(e, axis=-1, keepdims=True)
            outs.append(jnp.einsum('bhqn,bqnhd->bqhd', (e / den).astype(vg.dtype), vg))
            lses.append((m + jnp.log(den))[..., 0])
        alpha = jax.nn.softmax(jnp.stack(lses), axis=0).transpose(0, 1, 3, 2)[..., None]
        o = jnp.stack(outs)
        return jnp.sum(alpha.astype(o.dtype) * o, axis=0)

    out = lax.map(one_block, jnp.arange(S // Q_BLOCK) * Q_BLOCK)
    return out.transpose(1, 0, 2, 3, 4).reshape(B, S, H, d)


def even_mixer(h, w_in, cmp_pe_k, cmp_w1_k, cmp_w2_k, cmp_pe_v, cmp_w1_v, cmp_w2_v,
               mla_q_norm, mla_w_uq, mla_kv_norm, mla_w_ukv, w_out):
    B, S, _ = h.shape
    cos, sin = rope_tables(S, HEAD_DIM)
    z = h @ w_in
    o1 = A_Q_COLS
    o2 = o1 + A_KV_COLS
    o3 = o2 + A_GATE_COLS
    o4 = o3 + MLA_Q_RANK
    o5 = o4 + MLA_KV_RANK
    q_a = apply_rope(z[..., :o1].reshape(B, S, NSA_HEADS, HEAD_DIM), cos, sin)
    kv_a = z[..., o1:o2].reshape(B, S, 3, 2, NSA_KV_GROUPS, HEAD_DIM)
    gate = jax.nn.sigmoid(z[..., o2:o3]).reshape(B, S, NSA_HEADS, 3)
    k_cmp = nsa_compress(apply_rope(kv_a[:, :, 0, 0], cos, sin), cmp_pe_k, cmp_w1_k, cmp_w2_k)
    v_cmp = nsa_compress(kv_a[:, :, 0, 1], cmp_pe_v, cmp_w1_v, cmp_w2_v)
    k_slc = apply_rope(kv_a[:, :, 1, 0], cos, sin)
    k_win = apply_rope(kv_a[:, :, 2, 0], cos, sin)
    o_a = nsa_attention(q_a, k_cmp, v_cmp, k_slc, kv_a[:, :, 1, 1], k_win, kv_a[:, :, 2, 1], gate)
    o_b = mla_attention(z[..., o3:o4], z[..., o4:o5], z[..., o5:],
                        mla_q_norm, mla_w_uq, mla_kv_norm, mla_w_ukv)
    o = jnp.concatenate([o_a.reshape(B, S, -1), o_b.reshape(B, S, -1)], axis=-1)
    return o @ w_out


def odd_mixer(h, w_in, w_out):
    B, S, _ = h.shape
    cos, sin = rope_tables(S, HEAD_DIM)
    z = (h @ w_in).reshape(B, S, len(DIL_PATTERNS), 3, DIL_HEADS, HEAD_DIM)
    qs = [apply_rope(z[:, :, g, 0], cos, sin) for g in range(len(DIL_PATTERNS))]
    ks = [apply_rope(z[:, :, g, 1], cos, sin) for g in range(len(DIL_PATTERNS))]
    vs = [z[:, :, g, 2] for g in range(len(DIL_PATTERNS))]
    o = dilated_attention(qs, ks, vs)
    return o.reshape(B, S, ODD_OUT) @ w_out


def sq_relu_mlp(h, w1, w2):
    return jnp.square(jax.nn.relu(h @ w1)) @ w2


def setup_inputs(seed: int = 0) -> dict:
    key = jax.random.key(seed)
    it = iter(list(jax.random.split(key, 40)))

    def dense(shape, fan_in):
        return jax.random.normal(next(it), shape, jnp.float32) * fan_in ** -0.5

    def gain(n):
        return 1.0 + 0.05 * jax.random.normal(next(it), (n,), jnp.float32)

    def small(shape):
        return 0.1 * jax.random.normal(next(it), shape, jnp.float32)

    x = jax.random.normal(next(it), (BATCH, SEQ, D_MODEL), jnp.float32)
    cmp_in = NSA_CMP_LEN * HEAD_DIM
    return {
        "x": x,
        "l0_norm_mix_pre": gain(D_MODEL),
        "l0_w_in": dense((D_MODEL, EVEN_IN), D_MODEL),
        "l0_cmp_pe_k": small((NSA_CMP_LEN, HEAD_DIM)),
        "l0_cmp_w1_k": dense((cmp_in, NSA_CMP_HIDDEN), cmp_in),
        "l0_cmp_w2_k": dense((NSA_CMP_HIDDEN, HEAD_DIM), NSA_CMP_HIDDEN),
        "l0_cmp_pe_v": small((NSA_CMP_LEN, HEAD_DIM)),
        "l0_cmp_w1_v": dense((cmp_in, NSA_CMP_HIDDEN), cmp_in),
        "l0_cmp_w2_v": dense((NSA_CMP_HIDDEN, HEAD_DIM), NSA_CMP_HIDDEN),
        "l0_mla_q_norm": gain(MLA_Q_RANK),
        "l0_mla_w_uq": dense((MLA_Q_RANK, MLA_HEADS * (MLA_NOPE + MLA_ROPE)), MLA_Q_RANK),
        "l0_mla_kv_norm": gain(MLA_KV_RANK),
        "l0_mla_w_ukv": dense((MLA_KV_RANK, MLA_HEADS * (MLA_NOPE + MLA_V)), MLA_KV_RANK),
        "l0_w_out": dense((EVEN_OUT, D_MODEL), EVEN_OUT),
        "l0_norm_mix_post": gain(D_MODEL),
        "l0_norm_ffn_pre": gain(D_MODEL),
        "l0_w_ff1": dense((D_MODEL, D_FF), D_MODEL),
        "l0_w_ff2": dense((D_FF, D_MODEL), D_FF),
        "l0_norm_ffn_post": gain(D_MODEL),
        "l1_norm_mix_pre": gain(D_MODEL),
        "l1_w_in": dense((D_MODEL, ODD_IN), D_MODEL),
        "l1_w_out": dense((ODD_OUT, D_MODEL), ODD_OUT),
        "l1_norm_mix_post": gain(D_MODEL),
        "l1_norm_ffn_pre": gain(D_MODEL),
        "l1_w_ff1": dense((D_MODEL, D_FF), D_MODEL),
        "l1_w_ff2": dense((D_FF, D_MODEL), D_FF),
        "l1_norm_ffn_post": gain(D_MODEL),
    }


def reference(x, l0_norm_mix_pre, l0_w_in, l0_cmp_pe_k, l0_cmp_w1_k, l0_cmp_w2_k,
              l0_cmp_pe_v, l0_cmp_w1_v, l0_cmp_w2_v, l0_mla_q_norm, l0_mla_w_uq,
              l0_mla_kv_norm, l0_mla_w_ukv, l0_w_out, l0_norm_mix_post, l0_norm_ffn_pre,
              l0_w_ff1, l0_w_ff2, l0_norm_ffn_post, l1_norm_mix_pre, l1_w_in, l1_w_out,
              l1_norm_mix_post, l1_norm_ffn_pre, l1_w_ff1, l1_w_ff2, l1_norm_ffn_post):
    even_mix = (l0_w_in, l0_cmp_pe_k, l0_cmp_w1_k, l0_cmp_w2_k, l0_cmp_pe_v, l0_cmp_w1_v,
                l0_cmp_w2_v, l0_mla_q_norm, l0_mla_w_uq, l0_mla_kv_norm, l0_mla_w_ukv, l0_w_out)
    odd_mix = (l1_w_in, l1_w_out)
    norms = ((l0_norm_mix_pre, l0_norm_mix_post, l0_norm_ffn_pre, l0_norm_ffn_post),
             (l1_norm_mix_pre, l1_norm_mix_post, l1_norm_ffn_pre, l1_norm_ffn_post))
    ffns = ((l0_w_ff1, l0_w_ff2), (l1_w_ff1, l1_w_ff2))
    h = x
    for layer in range(DEPTH):
        g_mix_pre, g_mix_post, g_ffn_pre, g_ffn_post = norms[layer]
        u = rms_norm(h, g_mix_pre)
        if layer % 2 == 0:
            m = even_mixer(u, *even_mix)
        else:
            m = odd_mixer(u, *odd_mix)
        h = h + rms_norm(m, g_mix_post)
        f = sq_relu_mlp(rms_norm(h, g_ffn_pre), *ffns[layer])
        h = h + rms_norm(f, g_ffn_post)
    return h
```

```python
import functools

import numpy as np
import jax
import jax.numpy as jnp
from jax import lax
from jax.experimental import pallas as pl
from jax.experimental.pallas import tpu as pltpu

F32 = jnp.float32
BF16 = jnp.bfloat16

HEAD_DIM = 128
LANES = 128
ROPE_THETA = 10000.0
EPS = 1e-6
NEG_INF = -1e30
POS_INF = 1e30
PICKED = -3e38

NSA_HEADS = 8
NSA_KV_GROUPS = 2
NSA_GROUP_HEADS = NSA_HEADS // NSA_KV_GROUPS
NSA_CMP_LEN = 32
NSA_CMP_STRIDE = 16
NSA_SEL_LEN = 64
NSA_SEL_TOPK = 16
NSA_WINDOW = 512

MLA_HEADS = 8
MLA_Q_RANK = 512
MLA_KV_RANK = 256
MLA_NOPE = 128
MLA_ROPE = 64
MLA_V = 128

DIL_PATTERNS = ((128, 1), (512, 4), (2048, 16))
DIL_HEADS = 8
DIL_SPAN = 128

VMEM_LIMIT = 56 * 1024 * 1024

Z0_Q = 0
Z0_KCMP = 1024
Z0_KSLC = 1280
Z0_KWIN = 1536
Z0_VCMP = 1792
Z0_VSLC = 2048
Z0_VWIN = 2304
Z0_CQ = 2560
Z0_CKV = 3072
Z0_GATE = 3328
Z0_KROPE = 3456
Z0_COLS = 3584

TAB_ROPE128, TAB_IDENT, TAB_ROPE64 = 0, 1, 2


def _cparams(sem):
    return pltpu.CompilerParams(dimension_semantics=sem, vmem_limit_bytes=VMEM_LIMIT)


def _rope_tables(seq):
    def cs(dim):
        inv = 1.0 / (ROPE_THETA ** (jnp.arange(0, dim, 2, dtype=F32) / dim))
        ang = jnp.arange(seq, dtype=F32)[:, None] * inv[None, :]
        return jnp.cos(ang), jnp.sin(ang)

    c128, s128 = cs(HEAD_DIM)
    c64, s64 = cs(MLA_ROPE)
    one = jnp.ones((seq, 32), F32)
    zero = jnp.zeros((seq, 32), F32)
    ident_c = jnp.ones((seq, LANES), F32)
    ident_s = jnp.zeros((seq, LANES), F32)
    r128_c = jnp.concatenate([c128, c128], axis=1)
    r128_s = jnp.concatenate([-s128, s128], axis=1)
    r64_c = jnp.concatenate([c64, one, c64, one], axis=1)
    r64_s = jnp.concatenate([-s64, zero, s64, zero], axis=1)
    cos = jnp.stack([jnp.concatenate([r128_c, r128_c], 1), jnp.concatenate([ident_c, ident_c], 1),
                     jnp.concatenate([ident_c, r64_c], 1)])
    sin = jnp.stack([jnp.concatenate([r128_s, r128_s], 1), jnp.concatenate([ident_s, ident_s], 1),
                     jnp.concatenate([ident_s, r64_s], 1)])
    return cos, sin


def _proj_body(*refs, rope, tn):
    if rope:
        _, x_ref, g_ref, w_ref, c_ref, s_ref, o_ref, xn_ref = refs
    else:
        x_ref, g_ref, w_ref, o_ref, xn_ref = refs

    @pl.when(pl.program_id(1) == 0)
    def _():
        x = x_ref[...].astype(F32)
        y = x * lax.rsqrt(jnp.mean(x * x, axis=-1, keepdims=True) + EPS)
        xn_ref[...] = (y * g_ref[...]).astype(BF16)

    acc = jnp.dot(xn_ref[...], w_ref[...], preferred_element_type=F32)
    if rope:
        parts = []
        for k in range(tn // LANES):
            sl = slice(k * LANES, (k + 1) * LANES)
            a = acc[:, sl]
            parts.append(a * c_ref[:, sl] + pltpu.roll(a, LANES // 2, 1) * s_ref[:, sl])
        acc = jnp.concatenate(parts, axis=1)
    o_ref[...] = acc.astype(o_ref.dtype)


def _proj(x, x_col, d, g, w, seq, tabs=None, cos=None, sin=None, tm=1024, tn=256):
    t = x.shape[0]
    n = w.shape[1]
    tm = min(tm, seq)
    assert t % tm == 0 and seq % tm == 0 and n % tn == 0
    rope = tabs is not None
    g2 = g.reshape(1, d).astype(F32)
    body = functools.partial(_proj_body, rope=rope, tn=tn)
    scratch = [pltpu.VMEM((tm, d), BF16)]
    out_shape = jax.ShapeDtypeStruct((t, n), BF16)
    if rope:
        spb = seq // tm
        grid_spec = pltpu.PrefetchScalarGridSpec(
            num_scalar_prefetch=1, grid=(t // tm, n // tn),
            in_specs=[pl.BlockSpec((tm, d), lambda i, j, tab: (i, x_col)),
                      pl.BlockSpec((1, d), lambda i, j, tab: (0, 0)),
                      pl.BlockSpec((d, tn), lambda i, j, tab: (0, j)),
                      pl.BlockSpec((None, tm, tn), lambda i, j, tab: (tab[j], i % spb, 0)),
                      pl.BlockSpec((None, tm, tn), lambda i, j, tab: (tab[j], i % spb, 0))],
            out_specs=pl.BlockSpec((tm, tn), lambda i, j, tab: (i, j)),
            scratch_shapes=scratch)
        return pl.pallas_call(body, grid_spec=grid_spec, out_shape=out_shape,
                              compiler_params=_cparams(("parallel", "arbitrary")))(
            jnp.asarray(tabs, jnp.int32), x, g2, w, cos, sin)
    return pl.pallas_call(
        body, grid=(t // tm, n // tn),
        in_specs=[pl.BlockSpec((tm, d), lambda i, j: (i, x_col)),
                  pl.BlockSpec((1, d), lambda i, j: (0, 0)),
                  pl.BlockSpec((d, tn), lambda i, j: (0, j))],
        out_specs=pl.BlockSpec((tm, tn), lambda i, j: (i, j)),
        scratch_shapes=scratch, out_shape=out_shape,
        compiler_params=_cparams(("parallel", "arbitrary")))(x, g2, w)


def _compress_body(c_ref, w1_ref, w2_ref, pe_ref, o_ref, *, nc):
    half = NSA_CMP_STRIDE * HEAD_DIM
    c = c_ref[...]
    a = jnp.dot(c, w1_ref[:half, :], preferred_element_type=F32)
    b = jnp.dot(c, w1_ref[half:, :], preferred_element_type=F32)
    pe = pe_ref[...]
    pe_hi = pe.astype(BF16)
    pe_lo = (pe - pe_hi.astype(F32)).astype(BF16)
    pe_term = (jnp.dot(pe_hi, w1_ref[...], preferred_element_type=F32)
               + jnp.dot(pe_lo, w1_ref[...], preferred_element_type=F32))
    hid = a + pltpu.roll(b, nc - 1, 0) + pe_term[0:1, :]
    act = jax.nn.gelu(hid)
    o_ref[...] = jnp.dot(act.astype(BF16), w2_ref[...], preferred_element_type=F32).astype(o_ref.dtype)


def _compress(chunks, w1, w2, pe):
    _, b, g, nc, cw = chunks.shape
    return pl.pallas_call(
        functools.partial(_compress_body, nc=nc), grid=(2, b, g),
        in_specs=[pl.BlockSpec((None, None, None, nc, cw), lambda s, i, j: (s, i, j, 0, 0)),
                  pl.BlockSpec((None, 2 * cw, HEAD_DIM), lambda s, i, j: (s, 0, 0)),
                  pl.BlockSpec((None, HEAD_DIM, HEAD_DIM), lambda s, i, j: (s, 0, 0)),
                  pl.BlockSpec((None, 8, 2 * cw), lambda s, i, j: (s, 0, 0))],
        out_specs=pl.BlockSpec((None, None, None, nc, HEAD_DIM), lambda s, i, j: (s, i, j, 0, 0)),
        out_shape=jax.ShapeDtypeStruct((2, b, g, nc, HEAD_DIM), BF16),
        compiler_params=_cparams(("parallel", "parallel", "parallel")))(chunks, w1, w2, pe)


def _nsa_cmp_body(q_ref, k_ref, v_ref, m_ref, o_ref, sel_ref, *, tq, nc, nsp, topk):
    scale = HEAD_DIM ** -0.5
    t = pl.program_id(2) * tq + lax.broadcasted_iota(jnp.int32, (tq, 1), 0)
    n = lax.broadcasted_iota(jnp.int32, (1, nc), 1)
    vis = (n * NSA_CMP_STRIDE + (NSA_CMP_LEN - 1)) <= t
    visf = vis.astype(F32)
    k = k_ref[...]
    v = v_ref[...]
    psum = jnp.zeros((tq, nc), F32)
    for h in range(NSA_GROUP_HEADS):
        sl = slice(h * HEAD_DIM, (h + 1) * HEAD_DIM)
        s = lax.dot_general(q_ref[:, sl], k, (((1,), (1,)), ((), ())), preferred_element_type=F32) * scale
        s = jnp.where(vis, s, NEG_INF)
        e = jnp.exp(s - jnp.max(s, axis=-1, keepdims=True))
        p = e / jnp.sum(e, axis=-1, keepdims=True) * visf
        o_ref[:, sl] = jnp.dot(p.astype(BF16), v, preferred_element_type=F32).astype(o_ref.dtype)
        psum = psum + p
    hi = psum.astype(BF16)
    lo = (psum - hi.astype(F32)).astype(BF16)
    imp = (jnp.dot(hi, m_ref[...], preferred_element_type=F32)
           + jnp.dot(lo, m_ref[...], preferred_element_type=F32))

    blk = lax.broadcasted_iota(jnp.int32, (1, nsp), 1)
    blkf = blk.astype(F32)
    cur = lax.shift_right_logical(t, 6)
    forced = (blk == 0) | (blk == cur) | (blk == cur - 1)
    causal = blk * NSA_SEL_LEN <= t
    score = jnp.where(forced, POS_INF, jnp.where(causal, imp, NEG_INF))

    def take_one(_, carry):
        score, sel = carry
        mx = jnp.max(score, axis=-1, keepdims=True)
        first = jnp.min(jnp.where(score == mx, blkf, float(nsp)), axis=-1, keepdims=True)
        pick = blkf == first
        sel = jnp.where(pick & (mx > 0.5 * NEG_INF), 1.0, sel)
        return jnp.where(pick, PICKED, score), sel

    _, sel = lax.fori_loop(0, topk, take_one, (score, jnp.zeros((tq, nsp), F32)))
    sel_ref[...] = sel.astype(sel_ref.dtype)


def _nsa_cmp(z0, kv_cmp, ovl, b, seq, tq=256):
    nc = kv_cmp.shape[3]
    nsp = ovl.shape[1]
    tq = min(tq, seq)
    nq = seq // tq
    gw = NSA_GROUP_HEADS * HEAD_DIM
    topk = min(NSA_SEL_TOPK, seq // NSA_SEL_LEN)
    body = functools.partial(_nsa_cmp_body, tq=tq, nc=nc, nsp=nsp, topk=topk)
    return pl.pallas_call(
        body, grid=(b, NSA_KV_GROUPS, nq),
        in_specs=[pl.BlockSpec((tq, gw), lambda bi, g, i: (bi * nq + i, g)),
                  pl.BlockSpec((None, None, None, nc, HEAD_DIM), lambda bi, g, i: (0, bi, g, 0, 0)),
                  pl.BlockSpec((None, None, None, nc, HEAD_DIM), lambda bi, g, i: (1, bi, g, 0, 0)),
                  pl.BlockSpec((nc, nsp), lambda bi, g, i: (0, 0))],
        out_specs=[pl.BlockSpec((tq, gw), lambda bi, g, i: (bi * nq + i, g)),
                   pl.BlockSpec((None, None, tq, nsp), lambda bi, g, i: (bi, g, i, 0))],
        out_shape=[jax.ShapeDtypeStruct((b * seq, NSA_HEADS * HEAD_DIM), BF16),
                   jax.ShapeDtypeStruct((b, NSA_KV_GROUPS, seq, nsp), BF16)],
        compiler_params=_cparams(("parallel", "parallel", "parallel")))(z0, kv_cmp, kv_cmp, ovl)


def _nsa_win_body(q_ref, kp_ref, kc_ref, vp_ref, vc_ref, o_ref, *, tq):
    scale = HEAD_DIM ** -0.5
    i = pl.program_id(2)
    t = i * tq + lax.broadcasted_iota(jnp.int32, (tq, 1), 0)
    pos = (i - 1) * tq + lax.broadcasted_iota(jnp.int32, (1, 2 * tq), 1)
    rel = t - pos
    mask = (rel >= 0) & (rel < NSA_WINDOW) & (pos >= 0)
    k = jnp.concatenate([kp_ref[...], kc_ref[...]], axis=0)
    v = jnp.concatenate([vp_ref[...], vc_ref[...]], axis=0)
    for h in range(NSA_GROUP_HEADS):
        sl = slice(h * HEAD_DIM, (h + 1) * HEAD_DIM)
        s = lax.dot_general(q_ref[:, sl], k, (((1,), (1,)), ((), ())), preferred_element_type=F32) * scale
        s = jnp.where(mask, s, NEG_INF)
        e = jnp.exp(s - jnp.max(s, axis=-1, keepdims=True))
        p = e / jnp.sum(e, axis=-1, keepdims=True)
        o_ref[:, sl] = jnp.dot(p.astype(BF16), v, preferred_element_type=F32).astype(o_ref.dtype)


def _nsa_win(z0, b, seq):
    tq = NSA_WINDOW
    assert seq % tq == 0
    nq = seq // tq
    gw = NSA_GROUP_HEADS * HEAD_DIM
    kcol, vcol = Z0_KWIN // HEAD_DIM, Z0_VWIN // HEAD_DIM
    prev = lambda bi, i: bi * nq + jnp.maximum(i - 1, 0)
    return pl.pallas_call(
        functools.partial(_nsa_win_body, tq=tq), grid=(b, NSA_KV_GROUPS, nq),
        in_specs=[pl.BlockSpec((tq, gw), lambda bi, g, i: (bi * nq + i, g)),
                  pl.BlockSpec((tq, HEAD_DIM), lambda bi, g, i: (prev(bi, i), kcol + g)),
                  pl.BlockSpec((tq, HEAD_DIM), lambda bi, g, i: (bi * nq + i, kcol + g)),
                  pl.BlockSpec((tq, HEAD_DIM), lambda bi, g, i: (prev(bi, i), vcol + g)),
                  pl.BlockSpec((tq, HEAD_DIM), lambda bi, g, i: (bi * nq + i, vcol + g))],
        out_specs=pl.BlockSpec((tq, gw), lambda bi, g, i: (bi * nq + i, g)),
        out_shape=jax.ShapeDtypeStruct((b * seq, NSA_HEADS * HEAD_DIM), BF16),
        compiler_params=_cparams(("parallel", "parallel", "parallel")))(z0, z0, z0, z0, z0)


def _tri_schedule(nq):
    qi = np.concatenate([np.full(i + 1, i) for i in range(nq)]).astype(np.int32)
    kj = np.concatenate([np.arange(i + 1) for i in range(nq)]).astype(np.int32)
    return jnp.asarray(qi), jnp.asarray(kj)


def _online_softmax_step(s, v, m_scr, l_scr, acc_scr, h):
    m_prev = m_scr[h]
    m_new = jnp.maximum(m_prev, jnp.max(s, axis=-1, keepdims=True))
    alpha = jnp.exp(m_prev - m_new)
    p = jnp.exp(s - m_new)
    l_scr[h] = alpha * l_scr[h] + jnp.sum(p, axis=-1, keepdims=True)
    acc_scr[h] = alpha * acc_scr[h] + jnp.dot(p.astype(BF16), v, preferred_element_type=F32)
    m_scr[h] = m_new


def _nsa_sel_body(qi_ref, kj_ref, q_ref, k_ref, v_ref, sel_ref, o_ref, m_scr, l_scr, acc_scr, *, tq, nsp):
    scale = HEAD_DIM ** -0.5
    step = pl.program_id(2)
    qi = qi_ref[step]
    kj = kj_ref[step]

    @pl.when(kj == 0)
    def _():
        m_scr[...] = jnp.full(m_scr.shape, NEG_INF, F32)
        l_scr[...] = jnp.zeros(l_scr.shape, F32)
        acc_scr[...] = jnp.zeros(acc_scr.shape, F32)

    blk_row = lax.broadcasted_iota(jnp.int32, (nsp, tq), 0)
    key_blk = kj * (tq // NSA_SEL_LEN) + lax.shift_right_logical(
        lax.broadcasted_iota(jnp.int32, (nsp, tq), 1), 6)
    expand = jnp.where(blk_row == key_blk, 1.0, 0.0).astype(BF16)
    chosen = jnp.dot(sel_ref[...], expand, preferred_element_type=F32)
    t = qi * tq + lax.broadcasted_iota(jnp.int32, (tq, 1), 0)
    pos = kj * tq + lax.broadcasted_iota(jnp.int32, (1, tq), 1)
    mask = (chosen > 0.5) & (pos <= t)
    k = k_ref[...]
    v = v_ref[...]
    for h in range(NSA_GROUP_HEADS):
        sl = slice(h * HEAD_DIM, (h + 1) * HEAD_DIM)
        s = lax.dot_general(q_ref[:, sl], k, (((1,), (1,)), ((), ())), preferred_element_type=F32) * scale
        _online_softmax_step(jnp.where(mask, s, NEG_INF), v, m_scr, l_scr, acc_scr, h)

    @pl.when(kj == qi)
    def _():
        for h in range(NSA_GROUP_HEADS):
            o_ref[:, h * HEAD_DIM:(h + 1) * HEAD_DIM] = (acc_scr[h] / l_scr[h]).astype(o_ref.dtype)


def _nsa_sel(z0, sel, b, seq, tq=512):
    tq = min(tq, seq)
    nq = seq // tq
    nsp = sel.shape[3]
    gw = NSA_GROUP_HEADS * HEAD_DIM
    kcol, vcol = Z0_KSLC // HEAD_DIM, Z0_VSLC // HEAD_DIM
    qi, kj = _tri_schedule(nq)
    grid_spec = pltpu.PrefetchScalarGridSpec(
        num_scalar_prefetch=2, grid=(b, NSA_KV_GROUPS, int(qi.shape[0])),
        in_specs=[pl.BlockSpec((tq, gw), lambda bi, g, s, qi, kj: (bi * nq + qi[s], g)),
                  pl.BlockSpec((tq, HEAD_DIM), lambda bi, g, s, qi, kj: (bi * nq + kj[s], kcol + g)),
                  pl.BlockSpec((tq, HEAD_DIM), lambda bi, g, s, qi, kj: (bi * nq + kj[s], vcol + g)),
                  pl.BlockSpec((None, None, tq, nsp), lambda bi, g, s, qi, kj: (bi, g, qi[s], 0))],
        out_specs=pl.BlockSpec((tq, gw), lambda bi, g, s, qi, kj: (bi * nq + qi[s], g)),
        scratch_shapes=[pltpu.VMEM((NSA_GROUP_HEADS, tq, 1), F32), pltpu.VMEM((NSA_GROUP_HEADS, tq, 1), F32),
                        pltpu.VMEM((NSA_GROUP_HEADS, tq, HEAD_DIM), F32)])
    return pl.pallas_call(
        functools.partial(_nsa_sel_body, tq=tq, nsp=nsp), grid_spec=grid_spec,
        out_shape=jax.ShapeDtypeStruct((b * seq, NSA_HEADS * HEAD_DIM), BF16),
        compiler_params=_cparams(("parallel", "parallel", "arbitrary")))(qi, kj, z0, z0, z0, sel)


def _mla_body(qi_ref, kj_ref, q_ref, kn_ref, kp_ref, v_ref, o_ref, m_scr, l_scr, acc_scr, *, tq):
    scale = (MLA_NOPE + MLA_ROPE) ** -0.5
    step = pl.program_id(2)
    qi = qi_ref[step]
    kj = kj_ref[step]

    @pl.when(kj == 0)
    def _():
        m_scr[...] = jnp.full(m_scr.shape, NEG_INF, F32)
        l_scr[...] = jnp.zeros(l_scr.shape, F32)
        acc_scr[...] = jnp.zeros(acc_scr.shape, F32)

    k = jnp.concatenate([kn_ref[...], kp_ref[...]], axis=1)
    s = lax.dot_general(q_ref[...], k, (((1,), (1,)), ((), ())), preferred_element_type=F32) * scale
    t = qi * tq + lax.broadcasted_iota(jnp.int32, (tq, 1), 0)
    pos = kj * tq + lax.broadcasted_iota(jnp.int32, (1, tq), 1)
    _online_softmax_step(jnp.where(pos <= t, s, NEG_INF), v_ref[...], m_scr, l_scr, acc_scr, 0)

    @pl.when(kj == qi)
    def _():
        o_ref[...] = (acc_scr[0] / l_scr[0]).astype(o_ref.dtype)


def _mla(qcat, kv, z0, b, seq, tq=512):
    tq = min(tq, seq)
    nq = seq // tq
    qw = 2 * LANES
    kpcol = Z0_KROPE // LANES
    qi, kj = _tri_schedule(nq)
    grid_spec = pltpu.PrefetchScalarGridSpec(
        num_scalar_prefetch=2, grid=(b, MLA_HEADS, int(qi.shape[0])),
        in_specs=[pl.BlockSpec((tq, qw), lambda bi, h, s, qi, kj: (bi * nq + qi[s], h)),
                  pl.BlockSpec((tq, MLA_NOPE), lambda bi, h, s, qi, kj: (bi * nq + kj[s], h)),
                  pl.BlockSpec((tq, LANES), lambda bi, h, s, qi, kj: (bi * nq + kj[s], kpcol)),
                  pl.BlockSpec((tq, MLA_V), lambda bi, h, s, qi, kj: (bi * nq + kj[s], MLA_HEADS + h))],
        out_specs=pl.BlockSpec((tq, MLA_V), lambda bi, h, s, qi, kj: (bi * nq + qi[s], h)),
        scratch_shapes=[pltpu.VMEM((1, tq, 1), F32), pltpu.VMEM((1, tq, 1), F32),
                        pltpu.VMEM((1, tq, MLA_V), F32)])
    return pl.pallas_call(
        functools.partial(_mla_body, tq=tq), grid_spec=grid_spec,
        out_shape=jax.ShapeDtypeStruct((b * seq, MLA_HEADS * MLA_V), BF16),
        compiler_params=_cparams(("parallel", "parallel", "arbitrary")))(qi, kj, qcat, kv, z0, kv)


def _dil_body(q_ref, kp_ref, kc_ref, vp_ref, vc_ref, o_ref, lse_ref, *, tq):
    scale = HEAD_DIM ** -0.5
    i = pl.program_id(2)
    u = i * tq + lax.broadcasted_iota(jnp.int32, (tq, 1), 0)
    pos = i * tq - DIL_SPAN + lax.broadcasted_iota(jnp.int32, (1, tq + DIL_SPAN), 1)
    rel = u - pos
    mask = (rel >= 0) & (rel <= DIL_SPAN) & (pos >= 0)
    lane = lax.broadcasted_iota(jnp.int32, (tq, LANES), 1)
    lse_all = jnp.zeros((tq, LANES), F32)
    for h in range(DIL_HEADS):
        sl = slice(h * HEAD_DIM, (h + 1) * HEAD_DIM)
        k = jnp.concatenate([kp_ref[:, sl], kc_ref[:, sl]], axis=0)
        v = jnp.concatenate([vp_ref[:, sl], vc_ref[:, sl]], axis=0)
        s = lax.dot_general(q_ref[:, sl], k, (((1,), (1,)), ((), ())), preferred_element_type=F32) * scale
        s = jnp.where(mask, s, NEG_INF)
        m = jnp.max(s, axis=-1, keepdims=True)
        e = jnp.exp(s - m)
        den = jnp.sum(e, axis=-1, keepdims=True)
        o_ref[:, sl] = jnp.dot((e / den).astype(BF16), v, preferred_element_type=F32).astype(o_ref.dtype)
        lse_all = jnp.where(lane == h, m + jnp.log(den), lse_all)
    lse_ref[...] = lse_all


def _dilated(z1, p, b, seq, tq=512):
    _, r = DIL_PATTERNS[p]
    sub = seq // r
    tq = min(tq, sub)
    assert sub % tq == 0 and tq % DIL_SPAN == 0
    nq = sub // tq
    hw = DIL_HEADS * HEAD_DIM
    ncol = z1.shape[1] // hw
    zv = z1.reshape(b, sub, r * z1.shape[1])
    qc, kc, vc = 2 * p, 2 * p + 1, 6 + p
    ratio = tq // DIL_SPAN
    prev = lambda i: jnp.maximum(i * ratio - 1, 0)
    o, lse = pl.pallas_call(
        functools.partial(_dil_body, tq=tq), grid=(b, r, nq),
        in_specs=[pl.BlockSpec((None, tq, hw), lambda bi, c, i: (bi, i, c * ncol + qc)),
                  pl.BlockSpec((None, DIL_SPAN, hw), lambda bi, c, i: (bi, prev(i), c * ncol + kc)),
                  pl.BlockSpec((None, tq, hw), lambda bi, c, i: (bi, i, c * ncol + kc)),
                  pl.BlockSpec((None, DIL_SPAN, hw), lambda bi, c, i: (bi, prev(i), c * ncol + vc)),
                  pl.BlockSpec((None, tq, hw), lambda bi, c, i: (bi, i, c * ncol + vc))],
        out_specs=[pl.BlockSpec((None, tq, hw), lambda bi, c, i: (bi, i, c)),
                   pl.BlockSpec((None, tq, LANES), lambda bi, c, i: (bi, i, c))],
        out_shape=[jax.ShapeDtypeStruct((b, sub, r * hw), BF16),
                   jax.ShapeDtypeStruct((b, sub, r * LANES), F32)],
        compiler_params=_cparams(("parallel", "parallel", "parallel")))(zv, zv, zv, zv, zv)
    return o.reshape(b * seq, hw), lse.reshape(b * seq, LANES)


def _finish(h_ref, m, g_ref, o_ref):
    y = m * lax.rsqrt(jnp.mean(m * m, axis=-1, keepdims=True) + EPS)
    o_ref[...] = h_ref[...] + y * g_ref[...]


def _out0_body(h_ref, oc_ref, os_ref, ow_ref, gate_ref, ob_ref, wa_ref, wb_ref, g_ref, o_ref):
    gate = jax.nn.sigmoid(gate_ref[...].astype(F32))
    parts = []
    for h in range(NSA_HEADS):
        sl = slice(h * HEAD_DIM, (h + 1) * HEAD_DIM)
        parts.append(gate[:, 3 * h:3 * h + 1] * oc_ref[:, sl].astype(F32)
                     + gate[:, 3 * h + 1:3 * h + 2] * os_ref[:, sl].astype(F32)
                     + gate[:, 3 * h + 2:3 * h + 3] * ow_ref[:, sl].astype(F32))
    oa = jnp.concatenate(parts, axis=1).astype(BF16)
    m = (jnp.dot(oa, wa_ref[...], preferred_element_type=F32)
         + jnp.dot(ob_ref[...], wb_ref[...], preferred_element_type=F32))
    _finish(h_ref, m, g_ref, o_ref)


def _out1_body(h_ref, o0_ref, o1_ref, o2_ref, l0_ref, l1_ref, l2_ref, w_ref, g_ref, o_ref):
    l0, l1, l2 = l0_ref[...], l1_ref[...], l2_ref[...]
    mx = jnp.maximum(jnp.maximum(l0, l1), l2)
    e0, e1, e2 = jnp.exp(l0 - mx), jnp.exp(l1 - mx), jnp.exp(l2 - mx)
    tot = e0 + e1 + e2
    a0, a1, a2 = e0 / tot, e1 / tot, e2 / tot
    parts = []
    for h in range(DIL_HEADS):
        sl = slice(h * HEAD_DIM, (h + 1) * HEAD_DIM)
        parts.append(a0[:, h:h + 1] * o0_ref[:, sl].astype(F32)
                     + a1[:, h:h + 1] * o1_ref[:, sl].astype(F32)
                     + a2[:, h:h + 1] * o2_ref[:, sl].astype(F32))
    o = jnp.concatenate(parts, axis=1).astype(BF16)
    _finish(h_ref, jnp.dot(o, w_ref[...], preferred_element_type=F32), g_ref, o_ref)


def _row_spec(tm, w, col=0):
    return pl.BlockSpec((tm, w), lambda i: (i, col))


def _full_spec(shape):
    return pl.BlockSpec(shape, lambda i: (0,) * len(shape))


def _out0(h, o_c, o_s, o_w, z0, o_b, wa, wb, g, tm=256):
    t, d = h.shape
    tm = min(tm, t)
    ow = NSA_HEADS * HEAD_DIM
    return pl.pallas_call(
        _out0_body, grid=(t // tm,),
        in_specs=[_row_spec(tm, d), _row_spec(tm, ow), _row_spec(tm, ow), _row_spec(tm, ow),
                  _row_spec(tm, LANES, Z0_GATE // LANES), _row_spec(tm, ow),
                  _full_spec(wa.shape), _full_spec(wb.shape), _full_spec((1, d))],
        out_specs=_row_spec(tm, d), out_shape=jax.ShapeDtypeStruct((t, d), F32),
        compiler_params=_cparams(("parallel",)))(h, o_c, o_s, o_w, z0, o_b, wa, wb, g.reshape(1, d))


def _out1(h, os_, lses, w, g, tm=256):
    t, d = h.shape
    tm = min(tm, t)
    ow = DIL_HEADS * HEAD_DIM
    return pl.pallas_call(
        _out1_body, grid=(t // tm,),
        in_specs=[_row_spec(tm, d)] + [_row_spec(tm, ow)] * 3 + [_row_spec(tm, LANES)] * 3
                 + [_full_spec(w.shape), _full_spec((1, d))],
        out_specs=_row_spec(tm, d), out_shape=jax.ShapeDtypeStruct((t, d), F32),
        compiler_params=_cparams(("parallel",)))(h, *os_, *lses, w, g.reshape(1, d))


def _mlp_body(h_ref, g1_ref, w1_ref, w2_ref, g2_ref, o_ref, xn_ref, acc_ref):
    f = pl.program_id(1)

    @pl.when(f == 0)
    def _():
        x = h_ref[...]
        y = x * lax.rsqrt(jnp.mean(x * x, axis=-1, keepdims=True) + EPS)
        xn_ref[...] = (y * g1_ref[...]).astype(BF16)
        acc_ref[...] = jnp.zeros(acc_ref.shape, F32)

    a = jnp.maximum(jnp.dot(xn_ref[...], w1_ref[...], preferred_element_type=F32), 0.0)
    acc_ref[...] += jnp.dot((a * a).astype(BF16), w2_ref[...], preferred_element_type=F32)

    @pl.when(f == pl.num_programs(1) - 1)
    def _():
        _finish(h_ref, acc_ref[...], g2_ref, o_ref)


def _mlp(h, g1, w1, w2, g2, tm=512, tf=512):
    t, d = h.shape
    ff = w1.shape[1]
    tm = min(tm, t)
    return pl.pallas_call(
        _mlp_body, grid=(t // tm, ff // tf),
        in_specs=[pl.BlockSpec((tm, d), lambda i, f: (i, 0)),
                  pl.BlockSpec((1, d), lambda i, f: (0, 0)),
                  pl.BlockSpec((d, tf), lambda i, f: (0, f)),
                  pl.BlockSpec((tf, d), lambda i, f: (f, 0)),
                  pl.BlockSpec((1, d), lambda i, f: (0, 0))],
        out_specs=pl.BlockSpec((tm, d), lambda i, f: (i, 0)),
        out_shape=jax.ShapeDtypeStruct((t, d), F32),
        scratch_shapes=[pltpu.VMEM((tm, d), BF16), pltpu.VMEM((tm, d), F32)],
        compiler_params=_cparams(("parallel", "arbitrary")))(h, g1.reshape(1, d), w1, w2, g2.reshape(1, d))


def _rope64_tile(w):
    z = jnp.zeros((w.shape[0], 32), w.dtype)
    return jnp.concatenate([w[:, :32], z, w[:, 32:], z], axis=1)


def _layer0_w_in(w_in):
    d = w_in.shape[0]
    o1 = NSA_HEADS * HEAD_DIM
    o2 = o1 + 3 * 2 * NSA_KV_GROUPS * HEAD_DIM
    o3 = o2 + 3 * NSA_HEADS
    o4 = o3 + MLA_Q_RANK
    o5 = o4 + MLA_KV_RANK
    kv = w_in[:, o1:o2].reshape(d, 3, 2, NSA_KV_GROUPS * HEAD_DIM)
    kv = kv.transpose(0, 2, 1, 3).reshape(d, o2 - o1)
    gate = jnp.pad(w_in[:, o2:o3], ((0, 0), (0, LANES - (o3 - o2))))
    w = jnp.concatenate([w_in[:, :o1], kv, w_in[:, o3:o4], w_in[:, o4:o5], gate, _rope64_tile(w_in[:, o5:])], 1)
    assert w.shape[1] == Z0_COLS
    return w.astype(BF16)


def _mla_w_uq(w_uq):
    d = w_uq.shape[0]
    w = w_uq.reshape(d, MLA_HEADS, MLA_NOPE + MLA_ROPE)
    tiles = [jnp.concatenate([w[:, h, :MLA_NOPE], _rope64_tile(w[:, h, MLA_NOPE:])], 1) for h in range(MLA_HEADS)]
    return jnp.concatenate(tiles, axis=1).astype(BF16)


def _mla_w_ukv(w_ukv):
    d = w_ukv.shape[0]
    w = w_ukv.reshape(d, MLA_HEADS, MLA_NOPE + MLA_V)
    return jnp.concatenate([w[:, :, :MLA_NOPE].reshape(d, -1), w[:, :, MLA_NOPE:].reshape(d, -1)], 1).astype(BF16)


def _layer1_w_in(w_in):
    d = w_in.shape[0]
    hw = DIL_HEADS * HEAD_DIM
    w = w_in.reshape(d, len(DIL_PATTERNS), 3, hw)
    qk = w[:, :, :2].reshape(d, -1)
    v = w[:, :, 2].reshape(d, -1)
    return jnp.concatenate([qk, v], axis=1).astype(BF16)


def _overlap_matrix(nc, n_cmp, n_slc, nsp):
    ratio = NSA_SEL_LEN // NSA_CMP_STRIDE
    m = np.zeros((nc, nsp), np.float32)
    for off in range(1 - NSA_CMP_LEN // NSA_CMP_STRIDE, ratio):
        n = np.arange(n_slc) * ratio + off
        ok = (n >= 0) & (n < n_cmp)
        m[n[ok], np.arange(n_slc)[ok]] = 1.0
    return jnp.asarray(m, BF16)


def kernel(x, l0_norm_mix_pre, l0_w_in, l0_cmp_pe_k, l0_cmp_w1_k, l0_cmp_w2_k, l0_cmp_pe_v, l0_cmp_w1_v, l0_cmp_w2_v, l0_mla_q_norm, l0_mla_w_uq, l0_mla_kv_norm, l0_mla_w_ukv, l0_w_out, l0_norm_mix_post, l0_norm_ffn_pre, l0_w_ff1, l0_w_ff2, l0_norm_ffn_post, l1_norm_mix_pre, l1_w_in, l1_w_out, l1_norm_mix_post, l1_norm_ffn_pre, l1_w_ff1, l1_w_ff2, l1_norm_ffn_post):
    b, seq, d = x.shape
    t = b * seq
    assert seq % NSA_WINDOW == 0 and seq % (DIL_PATTERNS[-1][1] * DIL_SPAN) == 0
    cos, sin = _rope_tables(seq)
    h = x.reshape(t, d)

    tabs0 = [TAB_ROPE128] * 7 + [TAB_IDENT] * 6 + [TAB_ROPE64]
    z0 = _proj(h, 0, d, l0_norm_mix_pre, _layer0_w_in(l0_w_in), seq, tabs0, cos, sin)

    nc = seq // NSA_CMP_STRIDE
    n_cmp = (seq - NSA_CMP_LEN) // NSA_CMP_STRIDE + 1
    n_slc = seq // NSA_SEL_LEN
    nsp = -(-n_slc // LANES) * LANES
    gd = NSA_KV_GROUPS * HEAD_DIM

    def chunks(col):
        c = z0[:, col:col + gd].reshape(b, nc, NSA_CMP_STRIDE, NSA_KV_GROUPS, HEAD_DIM)
        return c.transpose(0, 3, 1, 2, 4).reshape(b, NSA_KV_GROUPS, nc, NSA_CMP_STRIDE * HEAD_DIM)

    pe = jnp.stack([l0_cmp_pe_k.reshape(1, -1), l0_cmp_pe_v.reshape(1, -1)])
    kv_cmp = _compress(jnp.stack([chunks(Z0_KCMP), chunks(Z0_VCMP)]),
                       jnp.stack([l0_cmp_w1_k, l0_cmp_w1_v]).astype(BF16),
                       jnp.stack([l0_cmp_w2_k, l0_cmp_w2_v]).astype(BF16),
                       jnp.broadcast_to(pe, (2, 8, pe.shape[-1])).astype(F32))
    o_c, sel = _nsa_cmp(z0, kv_cmp, _overlap_matrix(nc, n_cmp, n_slc, nsp), b, seq)
    o_w = _nsa_win(z0, b, seq)
    o_s = _nsa_sel(z0, sel, b, seq)

    qcat = _proj(z0, Z0_CQ // MLA_Q_RANK, MLA_Q_RANK, l0_mla_q_norm, _mla_w_uq(l0_mla_w_uq), seq,
                 [TAB_ROPE64] * MLA_HEADS, cos, sin)
    kv = _proj(z0, Z0_CKV // MLA_KV_RANK, MLA_KV_RANK, l0_mla_kv_norm, _mla_w_ukv(l0_mla_w_ukv), seq)
    o_b = _mla(qcat, kv, z0, b, seq)

    ow = NSA_HEADS * HEAD_DIM
    w_out0 = l0_w_out.astype(BF16)
    h = _out0(h, o_c, o_s, o_w, z0, o_b, w_out0[:ow], w_out0[ow:], l0_norm_mix_post)
    h = _mlp(h, l0_norm_ffn_pre, l0_w_ff1.astype(BF16), l0_w_ff2.astype(BF16), l0_norm_ffn_post)

    tabs1 = [TAB_ROPE128] * 24 + [TAB_IDENT] * 12
    z1 = _proj(h, 0, d, l1_norm_mix_pre, _layer1_w_in(l1_w_in), seq, tabs1, cos, sin)
    outs = [_dilated(z1, p, b, seq) for p in range(len(DIL_PATTERNS))]
    h = _out1(h, [o for o, _ in outs], [l for _, l in outs], l1_w_out.astype(BF16), l1_norm_mix_post)
    h = _mlp(h, l1_norm_ffn_pre, l1_w_ff1.astype(BF16), l1_w_ff2.astype(BF16), l1_norm_ffn_post)
    return h.reshape(b, seq, d)
```

```python
import functools

import numpy as np
import jax
import jax.numpy as jnp
from jax import lax
from jax.experimental import pallas as pl
from jax.experimental.pallas import tpu as pltpu

F32 = jnp.float32
BF16 = jnp.bfloat16

HEAD_DIM = 128
LANES = 128
ROPE_THETA = 10000.0
EPS = 1e-6
NEG_INF = -1e30
POS_INF = 1e30
PICKED = -3e38

NSA_HEADS = 8
NSA_KV_GROUPS = 2
NSA_GROUP_HEADS = NSA_HEADS // NSA_KV_GROUPS
NSA_CMP_LEN = 32
NSA_CMP_STRIDE = 16
NSA_SEL_LEN = 64
NSA_SEL_TOPK = 16
NSA_WINDOW = 512

MLA_HEADS = 8
MLA_Q_RANK = 512
MLA_KV_RANK = 256
MLA_NOPE = 128
MLA_ROPE = 64
MLA_V = 128

DIL_PATTERNS = ((128, 1), (512, 4), (2048, 16))
DIL_HEADS = 8
DIL_SPAN = 128

VMEM_LIMIT = 56 * 1024 * 1024

Z0_Q = 0
Z0_KCMP = 1024
Z0_KSLC = 1280
Z0_KWIN = 1536
Z0_VCMP = 1792
Z0_VSLC = 2048
Z0_VWIN = 2304
Z0_CQ = 2560
Z0_CKV = 3072
Z0_GATE = 3328
Z0_KROPE = 3456
Z0_COLS = 3584

TAB_ROPE128, TAB_IDENT, TAB_ROPE64, TAB_ROPE128_Q, TAB_ROPE64_Q = 0, 1, 2, 3, 4

LOG2E = 1.4426950408889634
LN2 = 0.6931471805599453
QSCALE_128 = HEAD_DIM ** -0.5 * LOG2E
QSCALE_MLA = (MLA_NOPE + MLA_ROPE) ** -0.5 * LOG2E


def _cparams(sem):
    return pltpu.CompilerParams(dimension_semantics=sem, vmem_limit_bytes=VMEM_LIMIT)


def _rope_tables(seq):
    def cs(dim):
        inv = 1.0 / (ROPE_THETA ** (jnp.arange(0, dim, 2, dtype=F32) / dim))
        ang = jnp.arange(seq, dtype=F32)[:, None] * inv[None, :]
        return jnp.cos(ang), jnp.sin(ang)

    c128, s128 = cs(HEAD_DIM)
    c64, s64 = cs(MLA_ROPE)
    one = jnp.ones((seq, 32), F32)
    zero = jnp.zeros((seq, 32), F32)
    ident_c = jnp.ones((seq, LANES), F32)
    ident_s = jnp.zeros((seq, LANES), F32)
    r128_c = jnp.concatenate([c128, c128], axis=1)
    r128_s = jnp.concatenate([-s128, s128], axis=1)
    r64_c = jnp.concatenate([c64, one, c64, one], axis=1)
    r64_s = jnp.concatenate([-s64, zero, s64, zero], axis=1)
    cos = [jnp.concatenate([r128_c, r128_c], 1), jnp.concatenate([ident_c, ident_c], 1),
           jnp.concatenate([ident_c, r64_c], 1)]
    sin = [jnp.concatenate([r128_s, r128_s], 1), jnp.concatenate([ident_s, ident_s], 1),
           jnp.concatenate([ident_s, r64_s], 1)]
    cos += [cos[TAB_ROPE128] * QSCALE_128, cos[TAB_ROPE64] * QSCALE_MLA]
    sin += [sin[TAB_ROPE128] * QSCALE_128, sin[TAB_ROPE64] * QSCALE_MLA]
    return jnp.stack(cos), jnp.stack(sin)


def _proj_body(*refs, rope, tn):
    if rope:
        _, x_ref, g_ref, w_ref, c_ref, s_ref, o_ref, xn_ref = refs
    else:
        x_ref, g_ref, w_ref, o_ref, xn_ref = refs

    @pl.when(pl.program_id(1) == 0)
    def _():
        x = x_ref[...].astype(F32)
        y = x * lax.rsqrt(jnp.mean(x * x, axis=-1, keepdims=True) + EPS)
        xn_ref[...] = (y * g_ref[...]).astype(BF16)

    acc = jnp.dot(xn_ref[...], w_ref[...], preferred_element_type=F32)
    if rope:
        parts = []
        for k in range(tn // LANES):
            sl = slice(k * LANES, (k + 1) * LANES)
            a = acc[:, sl]
            parts.append(a * c_ref[:, sl] + pltpu.roll(a, LANES // 2, 1) * s_ref[:, sl])
        acc = jnp.concatenate(parts, axis=1)
    o_ref[...] = acc.astype(o_ref.dtype)


def _proj(x, x_col, d, g, w, seq, tabs=None, cos=None, sin=None, tm=1024, tn=256):
    t = x.shape[0]
    n = w.shape[1]
    tm = min(tm, seq)
    assert t % tm == 0 and seq % tm == 0 and n % tn == 0
    rope = tabs is not None
    g2 = g.reshape(1, d).astype(F32)
    body = functools.partial(_proj_body, rope=rope, tn=tn)
    scratch = [pltpu.VMEM((tm, d), BF16)]
    out_shape = jax.ShapeDtypeStruct((t, n), BF16)
    if rope:
        spb = seq // tm
        grid_spec = pltpu.PrefetchScalarGridSpec(
            num_scalar_prefetch=1, grid=(t // tm, n // tn),
            in_specs=[pl.BlockSpec((tm, d), lambda i, j, tab: (i, x_col)),
                      pl.BlockSpec((1, d), lambda i, j, tab: (0, 0)),
                      pl.BlockSpec((d, tn), lambda i, j, tab: (0, j)),
                      pl.BlockSpec((None, tm, tn), lambda i, j, tab: (tab[j], i % spb, 0)),
                      pl.BlockSpec((None, tm, tn), lambda i, j, tab: (tab[j], i % spb, 0))],
            out_specs=pl.BlockSpec((tm, tn), lambda i, j, tab: (i, j)),
            scratch_shapes=scratch)
        return pl.pallas_call(body, grid_spec=grid_spec, out_shape=out_shape,
                              compiler_params=_cparams(("parallel", "arbitrary")))(
            jnp.asarray(tabs, jnp.int32), x, g2, w, cos, sin)
    return pl.pallas_call(
        body, grid=(t // tm, n // tn),
        in_specs=[pl.BlockSpec((tm, d), lambda i, j: (i, x_col)),
                  pl.BlockSpec((1, d), lambda i, j: (0, 0)),
                  pl.BlockSpec((d, tn), lambda i, j: (0, j))],
        out_specs=pl.BlockSpec((tm, tn), lambda i, j: (i, j)),
        scratch_shapes=scratch, out_shape=out_shape,
        compiler_params=_cparams(("parallel", "arbitrary")))(x, g2, w)


def _compress_body(c_ref, w1_ref, w2_ref, pe_ref, o_ref, *, nc):
    half = NSA_CMP_STRIDE * HEAD_DIM
    c = c_ref[...]
    a = jnp.dot(c, w1_ref[:half, :], preferred_element_type=F32)
    b = jnp.dot(c, w1_ref[half:, :], preferred_element_type=F32)
    pe = pe_ref[...]
    pe_hi = pe.astype(BF16)
    pe_lo = (pe - pe_hi.astype(F32)).astype(BF16)
    pe_term = (jnp.dot(pe_hi, w1_ref[...], preferred_element_type=F32)
               + jnp.dot(pe_lo, w1_ref[...], preferred_element_type=F32))
    hid = a + pltpu.roll(b, nc - 1, 0) + pe_term[0:1, :]
    act = jax.nn.gelu(hid)
    o_ref[...] = jnp.dot(act.astype(BF16), w2_ref[...], preferred_element_type=F32).astype(o_ref.dtype)


def _compress(chunks, w1, w2, pe):
    _, b, g, nc, cw = chunks.shape
    return pl.pallas_call(
        functools.partial(_compress_body, nc=nc), grid=(2, b, g),
        in_specs=[pl.BlockSpec((None, None, None, nc, cw), lambda s, i, j: (s, i, j, 0, 0)),
                  pl.BlockSpec((None, 2 * cw, HEAD_DIM), lambda s, i, j: (s, 0, 0)),
                  pl.BlockSpec((None, HEAD_DIM, HEAD_DIM), lambda s, i, j: (s, 0, 0)),
                  pl.BlockSpec((None, 8, 2 * cw), lambda s, i, j: (s, 0, 0))],
        out_specs=pl.BlockSpec((None, None, None, nc, HEAD_DIM), lambda s, i, j: (s, i, j, 0, 0)),
        out_shape=jax.ShapeDtypeStruct((2, b, g, nc, HEAD_DIM), BF16),
        compiler_params=_cparams(("parallel", "parallel", "parallel")))(chunks, w1, w2, pe)


def _nsa_cmp_body(q_ref, k_ref, v_ref, m_ref, o_ref, bias_ref, *, tq, nc, nsp, topk):
    t = pl.program_id(2) * tq + lax.broadcasted_iota(jnp.int32, (tq, 1), 0)
    n = lax.broadcasted_iota(jnp.int32, (1, nc), 1)
    vis = (n * NSA_CMP_STRIDE + (NSA_CMP_LEN - 1)) <= t
    visf = vis.astype(F32)
    k = k_ref[...]
    v = v_ref[...]
    psum = jnp.zeros((tq, nc), F32)
    for h in range(NSA_GROUP_HEADS):
        sl = slice(h * HEAD_DIM, (h + 1) * HEAD_DIM)
        s = lax.dot_general(q_ref[:, sl], k, (((1,), (1,)), ((), ())), preferred_element_type=F32)
        s = jnp.where(vis, s, NEG_INF)
        e = jnp.exp2(s - jnp.max(s, axis=-1, keepdims=True))
        p = e / jnp.sum(e, axis=-1, keepdims=True) * visf
        o_ref[:, sl] = jnp.dot(p.astype(BF16), v, preferred_element_type=F32).astype(o_ref.dtype)
        psum = psum + p
    hi = psum.astype(BF16)
    lo = (psum - hi.astype(F32)).astype(BF16)
    imp = (jnp.dot(hi, m_ref[...], preferred_element_type=F32)
           + jnp.dot(lo, m_ref[...], preferred_element_type=F32))

    blk = lax.broadcasted_iota(jnp.int32, (1, nsp), 1)
    blkf = blk.astype(F32)
    cur = lax.shift_right_logical(t, 6)
    forced = (blk == 0) | (blk == cur) | (blk == cur - 1)
    causal = blk * NSA_SEL_LEN <= t
    score = jnp.where(forced, POS_INF, jnp.where(causal, imp, NEG_INF))

    def take_one(_, carry):
        score, sel = carry
        mx = jnp.max(score, axis=-1, keepdims=True)
        first = jnp.min(jnp.where(score == mx, blkf, float(nsp)), axis=-1, keepdims=True)
        pick = blkf == first
        sel = jnp.where(pick & (mx > 0.5 * NEG_INF), 1.0, sel)
        return jnp.where(pick, PICKED, score), sel

    _, sel = lax.fori_loop(0, topk, take_one, (score, jnp.zeros((tq, nsp), F32)))
    bias_ref[...] = jnp.where(sel > 0.5, 0.0, NEG_INF).astype(bias_ref.dtype)


def _nsa_cmp(z0, kv_cmp, ovl, b, seq, tq=256):
    nc = kv_cmp.shape[3]
    nsp = ovl.shape[1]
    tq = min(tq, seq)
    nq = seq // tq
    gw = NSA_GROUP_HEADS * HEAD_DIM
    topk = min(NSA_SEL_TOPK, seq // NSA_SEL_LEN)
    body = functools.partial(_nsa_cmp_body, tq=tq, nc=nc, nsp=nsp, topk=topk)
    return pl.pallas_call(
        body, grid=(b, NSA_KV_GROUPS, nq),
        in_specs=[pl.BlockSpec((tq, gw), lambda bi, g, i: (bi * nq + i, g)),
                  pl.BlockSpec((None, None, None, nc, HEAD_DIM), lambda bi, g, i: (0, bi, g, 0, 0)),
                  pl.BlockSpec((None, None, None, nc, HEAD_DIM), lambda bi, g, i: (1, bi, g, 0, 0)),
                  pl.BlockSpec((nc, nsp), lambda bi, g, i: (0, 0))],
        out_specs=[pl.BlockSpec((tq, gw), lambda bi, g, i: (bi * nq + i, g)),
                   pl.BlockSpec((None, None, tq, nsp), lambda bi, g, i: (bi, g, i, 0))],
        out_shape=[jax.ShapeDtypeStruct((b * seq, NSA_HEADS * HEAD_DIM), BF16),
                   jax.ShapeDtypeStruct((b, NSA_KV_GROUPS, seq, nsp), BF16)],
        compiler_params=_cparams(("parallel", "parallel", "parallel")))(z0, kv_cmp, kv_cmp, ovl)


def _nsa_win_body(q_ref, kp_ref, kc_ref, vp_ref, vc_ref, o_ref, *, tq):
    i = pl.program_id(2)
    t = i * tq + lax.broadcasted_iota(jnp.int32, (tq, 1), 0)
    pos = (i - 1) * tq + lax.broadcasted_iota(jnp.int32, (1, 2 * tq), 1)
    rel = t - pos
    mask = (rel >= 0) & (rel < NSA_WINDOW) & (pos >= 0)
    k = jnp.concatenate([kp_ref[...], kc_ref[...]], axis=0)
    v = jnp.concatenate([vp_ref[...], vc_ref[...]], axis=0)
    for h in range(NSA_GROUP_HEADS):
        sl = slice(h * HEAD_DIM, (h + 1) * HEAD_DIM)
        s = lax.dot_general(q_ref[:, sl], k, (((1,), (1,)), ((), ())), preferred_element_type=F32)
        s = jnp.where(mask, s, NEG_INF)
        e = jnp.exp2(s - jnp.max(s, axis=-1, keepdims=True))
        p = e / jnp.sum(e, axis=-1, keepdims=True)
        o_ref[:, sl] = jnp.dot(p.astype(BF16), v, preferred_element_type=F32).astype(o_ref.dtype)


def _nsa_win(z0, b, seq):
    tq = NSA_WINDOW
    assert seq % tq == 0
    nq = seq // tq
    gw = NSA_GROUP_HEADS * HEAD_DIM
    kcol, vcol = Z0_KWIN // HEAD_DIM, Z0_VWIN // HEAD_DIM
    prev = lambda bi, i: bi * nq + jnp.maximum(i - 1, 0)
    return pl.pallas_call(
        functools.partial(_nsa_win_body, tq=tq), grid=(b, NSA_KV_GROUPS, nq),
        in_specs=[pl.BlockSpec((tq, gw), lambda bi, g, i: (bi * nq + i, g)),
                  pl.BlockSpec((tq, HEAD_DIM), lambda bi, g, i: (prev(bi, i), kcol + g)),
                  pl.BlockSpec((tq, HEAD_DIM), lambda bi, g, i: (bi * nq + i, kcol + g)),
                  pl.BlockSpec((tq, HEAD_DIM), lambda bi, g, i: (prev(bi, i), vcol + g)),
                  pl.BlockSpec((tq, HEAD_DIM), lambda bi, g, i: (bi * nq + i, vcol + g))],
        out_specs=pl.BlockSpec((tq, gw), lambda bi, g, i: (bi * nq + i, g)),
        out_shape=jax.ShapeDtypeStruct((b * seq, NSA_HEADS * HEAD_DIM), BF16),
        compiler_params=_cparams(("parallel", "parallel", "parallel")))(z0, z0, z0, z0, z0)


def _tri_schedule(nq):
    qi = np.concatenate([np.full(i + 1, i) for i in range(nq)]).astype(np.int32)
    kj = np.concatenate([np.arange(i + 1) for i in range(nq)]).astype(np.int32)
    return jnp.asarray(qi), jnp.asarray(kj)


def _flash_init(m_scr, l_scr, acc_scr):
    m_scr[...] = jnp.full(m_scr.shape, NEG_INF, F32)
    l_scr[...] = jnp.zeros(l_scr.shape, F32)
    acc_scr[...] = jnp.zeros(acc_scr.shape, F32)


def _flash_update(scores, vs, m_scr, l_scr, acc_scr):
    nh = len(scores)
    m_prev = [m_scr[h] for h in range(nh)]
    m_new = [jnp.maximum(m_prev[h], jnp.max(scores[h], axis=-1, keepdims=True)) for h in range(nh)]
    ps = [jnp.exp2(scores[h] - m_new[h]) for h in range(nh)]
    alphas = [jnp.exp2(m_prev[h] - m_new[h]) for h in range(nh)]
    for h in range(nh):
        l_scr[h] = alphas[h] * l_scr[h] + jnp.sum(ps[h], axis=-1, keepdims=True)
        acc_scr[h] = alphas[h] * acc_scr[h] + jnp.dot(ps[h].astype(BF16), vs[h], preferred_element_type=F32)
        m_scr[h] = m_new[h]


def _flash_tile(qi, kj, tq, scores, vs, m_scr, l_scr, acc_scr, write_out):
    @pl.when(kj < qi)
    def _():
        _flash_update(scores, vs, m_scr, l_scr, acc_scr)

    @pl.when(kj == qi)
    def _():
        row = lax.broadcasted_iota(jnp.int32, (tq, 1), 0)
        col = lax.broadcasted_iota(jnp.int32, (1, tq), 1)
        _flash_update([jnp.where(col <= row, s, NEG_INF) for s in scores], vs, m_scr, l_scr, acc_scr)
        write_out()


def _nsa_sel_body(qi_ref, kj_ref, q_ref, k_ref, v_ref, oh_ref, bias_ref, o_ref,
                  qa_scr, m_scr, l_scr, acc_scr, *, tq, per_slab):
    step = pl.program_id(2)
    qi = qi_ref[step]
    kj = kj_ref[step]

    @pl.when(kj == 0)
    def _():
        _flash_init(m_scr, l_scr, acc_scr)
        for h in range(NSA_GROUP_HEADS):
            qa_scr[h, :, :HEAD_DIM] = q_ref[:, h * HEAD_DIM:(h + 1) * HEAD_DIM]

    @pl.when(kj % per_slab == 0)
    def _():
        for h in range(NSA_GROUP_HEADS):
            qa_scr[h, :, HEAD_DIM:] = bias_ref[...]

    k = jnp.concatenate([k_ref[...], oh_ref[...]], axis=1)
    v = v_ref[...]
    scores = [lax.dot_general(qa_scr[h], k, (((1,), (1,)), ((), ())), preferred_element_type=F32)
              for h in range(NSA_GROUP_HEADS)]

    def write_out():
        for h in range(NSA_GROUP_HEADS):
            o_ref[:, h * HEAD_DIM:(h + 1) * HEAD_DIM] = (acc_scr[h] / l_scr[h]).astype(o_ref.dtype)

    _flash_tile(qi, kj, tq, scores, [v] * NSA_GROUP_HEADS, m_scr, l_scr, acc_scr, write_out)


def _nsa_sel(z0, bias, onehot, b, seq, tq=512):
    tq = min(tq, seq)
    nq = seq // tq
    gw = NSA_GROUP_HEADS * HEAD_DIM
    kcol, vcol = Z0_KSLC // HEAD_DIM, Z0_VSLC // HEAD_DIM
    per_slab = max(LANES * NSA_SEL_LEN // tq, 1)
    qi, kj = _tri_schedule(nq)
    grid_spec = pltpu.PrefetchScalarGridSpec(
        num_scalar_prefetch=2, grid=(b, NSA_KV_GROUPS, int(qi.shape[0])),
        in_specs=[pl.BlockSpec((tq, gw), lambda bi, g, s, qi, kj: (bi * nq + qi[s], g)),
                  pl.BlockSpec((tq, HEAD_DIM), lambda bi, g, s, qi, kj: (bi * nq + kj[s], kcol + g)),
                  pl.BlockSpec((tq, HEAD_DIM), lambda bi, g, s, qi, kj: (bi * nq + kj[s], vcol + g)),
                  pl.BlockSpec((tq, LANES), lambda bi, g, s, qi, kj: (kj[s], 0)),
                  pl.BlockSpec((None, None, tq, LANES),
                               lambda bi, g, s, qi, kj: (bi, g, qi[s], kj[s] // per_slab))],
        out_specs=pl.BlockSpec((tq, gw), lambda bi, g, s, qi, kj: (bi * nq + qi[s], g)),
        scratch_shapes=[pltpu.VMEM((NSA_GROUP_HEADS, tq, 2 * HEAD_DIM), BF16),
                        pltpu.VMEM((NSA_GROUP_HEADS, tq, 1), F32), pltpu.VMEM((NSA_GROUP_HEADS, tq, 1), F32),
                        pltpu.VMEM((NSA_GROUP_HEADS, tq, HEAD_DIM), F32)])
    return pl.pallas_call(
        functools.partial(_nsa_sel_body, tq=tq, per_slab=per_slab), grid_spec=grid_spec,
        out_shape=jax.ShapeDtypeStruct((b * seq, NSA_HEADS * HEAD_DIM), BF16),
        compiler_params=_cparams(("parallel", "parallel", "arbitrary")))(qi, kj, z0, z0, z0, onehot, bias)


MLA_STEP_HEADS = 2


def _mla_body(qi_ref, kj_ref, q_ref, kn_ref, kp_ref, v_ref, o_ref, m_scr, l_scr, acc_scr, *, tq):
    step = pl.program_id(2)
    qi = qi_ref[step]
    kj = kj_ref[step]

    @pl.when(kj == 0)
    def _():
        _flash_init(m_scr, l_scr, acc_scr)

    kp = kp_ref[...]
    scores, vs = [], []
    for h in range(MLA_STEP_HEADS):
        k = jnp.concatenate([kn_ref[:, h * MLA_NOPE:(h + 1) * MLA_NOPE], kp], axis=1)
        scores.append(lax.dot_general(q_ref[:, h * 2 * LANES:(h + 1) * 2 * LANES], k,
                                      (((1,), (1,)), ((), ())), preferred_element_type=F32))
        vs.append(v_ref[:, h * MLA_V:(h + 1) * MLA_V])

    def write_out():
        for h in range(MLA_STEP_HEADS):
            o_ref[:, h * MLA_V:(h + 1) * MLA_V] = (acc_scr[h] / l_scr[h]).astype(o_ref.dtype)

    _flash_tile(qi, kj, tq, scores, vs, m_scr, l_scr, acc_scr, write_out)


def _mla(qcat, kv, z0, b, seq, tq=512):
    tq = min(tq, seq)
    nq = seq // tq
    nh = MLA_STEP_HEADS
    ngrp = MLA_HEADS // nh
    kpcol = Z0_KROPE // LANES
    qi, kj = _tri_schedule(nq)
    grid_spec = pltpu.PrefetchScalarGridSpec(
        num_scalar_prefetch=2, grid=(b, ngrp, int(qi.shape[0])),
        in_specs=[pl.BlockSpec((tq, nh * 2 * LANES), lambda bi, h, s, qi, kj: (bi * nq + qi[s], h)),
                  pl.BlockSpec((tq, nh * MLA_NOPE), lambda bi, h, s, qi, kj: (bi * nq + kj[s], h)),
                  pl.BlockSpec((tq, LANES), lambda bi, h, s, qi, kj: (bi * nq + kj[s], kpcol)),
                  pl.BlockSpec((tq, nh * MLA_V), lambda bi, h, s, qi, kj: (bi * nq + kj[s], ngrp + h))],
        out_specs=pl.BlockSpec((tq, nh * MLA_V), lambda bi, h, s, qi, kj: (bi * nq + qi[s], h)),
        scratch_shapes=[pltpu.VMEM((nh, tq, 1), F32), pltpu.VMEM((nh, tq, 1), F32),
                        pltpu.VMEM((nh, tq, MLA_V), F32)])
    return pl.pallas_call(
        functools.partial(_mla_body, tq=tq), grid_spec=grid_spec,
        out_shape=jax.ShapeDtypeStruct((b * seq, MLA_HEADS * MLA_V), BF16),
        compiler_params=_cparams(("parallel", "parallel", "arbitrary")))(qi, kj, qcat, kv, z0, kv)


def _dil_body(q_ref, kp_ref, kc_ref, vp_ref, vc_ref, o_ref, lse_ref, *, tq):
    i = pl.program_id(2)
    u = i * tq + lax.broadcasted_iota(jnp.int32, (tq, 1), 0)
    pos = i * tq - DIL_SPAN + lax.broadcasted_iota(jnp.int32, (1, tq + DIL_SPAN), 1)
    rel = u - pos
    mask = (rel >= 0) & (rel <= DIL_SPAN) & (pos >= 0)
    lane = lax.broadcasted_iota(jnp.int32, (tq, LANES), 1)
    lse_all = jnp.zeros((tq, LANES), F32)
    for h in range(DIL_HEADS):
        sl = slice(h * HEAD_DIM, (h + 1) * HEAD_DIM)
        k = jnp.concatenate([kp_ref[:, sl], kc_ref[:, sl]], axis=0)
        v = jnp.concatenate([vp_ref[:, sl], vc_ref[:, sl]], axis=0)
        s = lax.dot_general(q_ref[:, sl], k, (((1,), (1,)), ((), ())), preferred_element_type=F32)
        s = jnp.where(mask, s, NEG_INF)
        m = jnp.max(s, axis=-1, keepdims=True)
        e = jnp.exp2(s - m)
        den = jnp.sum(e, axis=-1, keepdims=True)
        o_ref[:, sl] = jnp.dot((e / den).astype(BF16), v, preferred_element_type=F32).astype(o_ref.dtype)
        lse_all = jnp.where(lane == h, m * LN2 + jnp.log(den), lse_all)
    lse_ref[...] = lse_all


def _dilated(z1, p, b, seq, tq=512):
    _, r = DIL_PATTERNS[p]
    sub = seq // r
    tq = min(tq, sub)
    assert sub % tq == 0 and tq % DIL_SPAN == 0
    nq = sub // tq
    hw = DIL_HEADS * HEAD_DIM
    ncol = z1.shape[1] // hw
    zv = z1.reshape(b, sub, r * z1.shape[1])
    qc, kc, vc = 2 * p, 2 * p + 1, 6 + p
    ratio = tq // DIL_SPAN
    prev = lambda i: jnp.maximum(i * ratio - 1, 0)
    o, lse = pl.pallas_call(
        functools.partial(_dil_body, tq=tq), grid=(b, r, nq),
        in_specs=[pl.BlockSpec((None, tq, hw), lambda bi, c, i: (bi, i, c * ncol + qc)),
                  pl.BlockSpec((None, DIL_SPAN, hw), lambda bi, c, i: (bi, prev(i), c * ncol + kc)),
                  pl.BlockSpec((None, tq, hw), lambda bi, c, i: (bi, i, c * ncol + kc)),
                  pl.BlockSpec((None, DIL_SPAN, hw), lambda bi, c, i: (bi, prev(i), c * ncol + vc)),
                  pl.BlockSpec((None, tq, hw), lambda bi, c, i: (bi, i, c * ncol + vc))],
        out_specs=[pl.BlockSpec((None, tq, hw), lambda bi, c, i: (bi, i, c)),
                   pl.BlockSpec((None, tq, LANES), lambda bi, c, i: (bi, i, c))],
        out_shape=[jax.ShapeDtypeStruct((b, sub, r * hw), BF16),
                   jax.ShapeDtypeStruct((b, sub, r * LANES), F32)],
        compiler_params=_cparams(("parallel", "parallel", "parallel")))(zv, zv, zv, zv, zv)
    return o.reshape(b * seq, hw), lse.reshape(b * seq, LANES)


def _finish(h_ref, m, g_ref, o_ref):
    y = m * lax.rsqrt(jnp.mean(m * m, axis=-1, keepdims=True) + EPS)
    o_ref[...] = h_ref[...] + y * g_ref[...]


def _out0_body(h_ref, oc_ref, os_ref, ow_ref, gate_ref, ob_ref, wa_ref, wb_ref, g_ref, o_ref):
    gate = jax.nn.sigmoid(gate_ref[...].astype(F32))
    parts = []
    for h in range(NSA_HEADS):
        sl = slice(h * HEAD_DIM, (h + 1) * HEAD_DIM)
        parts.append(gate[:, 3 * h:3 * h + 1] * oc_ref[:, sl].astype(F32)
                     + gate[:, 3 * h + 1:3 * h + 2] * os_ref[:, sl].astype(F32)
                     + gate[:, 3 * h + 2:3 * h + 3] * ow_ref[:, sl].astype(F32))
    oa = jnp.concatenate(parts, axis=1).astype(BF16)
    m = (jnp.dot(oa, wa_ref[...], preferred_element_type=F32)
         + jnp.dot(ob_ref[...], wb_ref[...], preferred_element_type=F32))
    _finish(h_ref, m, g_ref, o_ref)


def _out1_body(h_ref, o0_ref, o1_ref, o2_ref, l0_ref, l1_ref, l2_ref, w_ref, g_ref, o_ref):
    l0, l1, l2 = l0_ref[...], l1_ref[...], l2_ref[...]
    mx = jnp.maximum(jnp.maximum(l0, l1), l2)
    e0, e1, e2 = jnp.exp(l0 - mx), jnp.exp(l1 - mx), jnp.exp(l2 - mx)
    tot = e0 + e1 + e2
    a0, a1, a2 = e0 / tot, e1 / tot, e2 / tot
    parts = []
    for h in range(DIL_HEADS):
        sl = slice(h * HEAD_DIM, (h + 1) * HEAD_DIM)
        parts.append(a0[:, h:h + 1] * o0_ref[:, sl].astype(F32)
                     + a1[:, h:h + 1] * o1_ref[:, sl].astype(F32)
                     + a2[:, h:h + 1] * o2_ref[:, sl].astype(F32))
    o = jnp.concatenate(parts, axis=1).astype(BF16)
    _finish(h_ref, jnp.dot(o, w_ref[...], preferred_element_type=F32), g_ref, o_ref)


def _row_spec(tm, w, col=0):
    return pl.BlockSpec((tm, w), lambda i: (i, col))


def _full_spec(shape):
    return pl.BlockSpec(shape, lambda i: (0,) * len(shape))


def _out0(h, o_c, o_s, o_w, z0, o_b, wa, wb, g, tm=256):
    t, d = h.shape
    tm = min(tm, t)
    ow = NSA_HEADS * HEAD_DIM
    return pl.pallas_call(
        _out0_body, grid=(t // tm,),
        in_specs=[_row_spec(tm, d), _row_spec(tm, ow), _row_spec(tm, ow), _row_spec(tm, ow),
                  _row_spec(tm, LANES, Z0_GATE // LANES), _row_spec(tm, ow),
                  _full_spec(wa.shape), _full_spec(wb.shape), _full_spec((1, d))],
        out_specs=_row_spec(tm, d), out_shape=jax.ShapeDtypeStruct((t, d), F32),
        compiler_params=_cparams(("parallel",)))(h, o_c, o_s, o_w, z0, o_b, wa, wb, g.reshape(1, d))


def _out1(h, os_, lses, w, g, tm=256):
    t, d = h.shape
    tm = min(tm, t)
    ow = DIL_HEADS * HEAD_DIM
    return pl.pallas_call(
        _out1_body, grid=(t // tm,),
        in_specs=[_row_spec(tm, d)] + [_row_spec(tm, ow)] * 3 + [_row_spec(tm, LANES)] * 3
                 + [_full_spec(w.shape), _full_spec((1, d))],
        out_specs=_row_spec(tm, d), out_shape=jax.ShapeDtypeStruct((t, d), F32),
        compiler_params=_cparams(("parallel",)))(h, *os_, *lses, w, g.reshape(1, d))


def _mlp_body(h_ref, g1_ref, w1_ref, w2_ref, g2_ref, o_ref, xn_ref, acc_ref):
    f = pl.program_id(1)

    @pl.when(f == 0)
    def _():
        x = h_ref[...]
        y = x * lax.rsqrt(jnp.mean(x * x, axis=-1, keepdims=True) + EPS)
        xn_ref[...] = (y * g1_ref[...]).astype(BF16)
        acc_ref[...] = jnp.zeros(acc_ref.shape, F32)

    a = jnp.maximum(jnp.dot(xn_ref[...], w1_ref[...], preferred_element_type=F32), 0.0)
    acc_ref[...] += jnp.dot((a * a).astype(BF16), w2_ref[...], preferred_element_type=F32)

    @pl.when(f == pl.num_programs(1) - 1)
    def _():
        _finish(h_ref, acc_ref[...], g2_ref, o_ref)


def _mlp(h, g1, w1, w2, g2, tm=512, tf=512):
    t, d = h.shape
    ff = w1.shape[1]
    tm = min(tm, t)
    return pl.pallas_call(
        _mlp_body, grid=(t // tm, ff // tf),
        in_specs=[pl.BlockSpec((tm, d), lambda i, f: (i, 0)),
                  pl.BlockSpec((1, d), lambda i, f: (0, 0)),
                  pl.BlockSpec((d, tf), lambda i, f: (0, f)),
                  pl.BlockSpec((tf, d), lambda i, f: (f, 0)),
                  pl.BlockSpec((1, d), lambda i, f: (0, 0))],
        out_specs=pl.BlockSpec((tm, d), lambda i, f: (i, 0)),
        out_shape=jax.ShapeDtypeStruct((t, d), F32),
        scratch_shapes=[pltpu.VMEM((tm, d), BF16), pltpu.VMEM((tm, d), F32)],
        compiler_params=_cparams(("parallel", "arbitrary")))(h, g1.reshape(1, d), w1, w2, g2.reshape(1, d))


def _rope64_tile(w):
    z = jnp.zeros((w.shape[0], 32), w.dtype)
    return jnp.concatenate([w[:, :32], z, w[:, 32:], z], axis=1)


def _layer0_w_in(w_in):
    d = w_in.shape[0]
    o1 = NSA_HEADS * HEAD_DIM
    o2 = o1 + 3 * 2 * NSA_KV_GROUPS * HEAD_DIM
    o3 = o2 + 3 * NSA_HEADS
    o4 = o3 + MLA_Q_RANK
    o5 = o4 + MLA_KV_RANK
    kv = w_in[:, o1:o2].reshape(d, 3, 2, NSA_KV_GROUPS * HEAD_DIM)
    kv = kv.transpose(0, 2, 1, 3).reshape(d, o2 - o1)
    gate = jnp.pad(w_in[:, o2:o3], ((0, 0), (0, LANES - (o3 - o2))))
    w = jnp.concatenate([w_in[:, :o1], kv, w_in[:, o3:o4], w_in[:, o4:o5], gate, _rope64_tile(w_in[:, o5:])], 1)
    assert w.shape[1] == Z0_COLS
    return w.astype(BF16)


def _mla_w_uq(w_uq):
    d = w_uq.shape[0]
    w = w_uq.reshape(d, MLA_HEADS, MLA_NOPE + MLA_ROPE)
    tiles = [jnp.concatenate([w[:, h, :MLA_NOPE], _rope64_tile(w[:, h, MLA_NOPE:])], 1) for h in range(MLA_HEADS)]
    return jnp.concatenate(tiles, axis=1).astype(BF16)


def _mla_w_ukv(w_ukv):
    d = w_ukv.shape[0]
    w = w_ukv.reshape(d, MLA_HEADS, MLA_NOPE + MLA_V)
    return jnp.concatenate([w[:, :, :MLA_NOPE].reshape(d, -1), w[:, :, MLA_NOPE:].reshape(d, -1)], 1).astype(BF16)


def _layer1_w_in(w_in):
    d = w_in.shape[0]
    hw = DIL_HEADS * HEAD_DIM
    w = w_in.reshape(d, len(DIL_PATTERNS), 3, hw)
    qk = w[:, :, :2].reshape(d, -1)
    v = w[:, :, 2].reshape(d, -1)
    return jnp.concatenate([qk, v], axis=1).astype(BF16)


def _overlap_matrix(nc, n_cmp, n_slc, nsp):
    ratio = NSA_SEL_LEN // NSA_CMP_STRIDE
    m = np.zeros((nc, nsp), np.float32)
    for off in range(1 - NSA_CMP_LEN // NSA_CMP_STRIDE, ratio):
        n = np.arange(n_slc) * ratio + off
        ok = (n >= 0) & (n < n_cmp)
        m[n[ok], np.arange(n_slc)[ok]] = 1.0
    return jnp.asarray(m, BF16)


def kernel(x, l0_norm_mix_pre, l0_w_in, l0_cmp_pe_k, l0_cmp_w1_k, l0_cmp_w2_k, l0_cmp_pe_v, l0_cmp_w1_v, l0_cmp_w2_v, l0_mla_q_norm, l0_mla_w_uq, l0_mla_kv_norm, l0_mla_w_ukv, l0_w_out, l0_norm_mix_post, l0_norm_ffn_pre, l0_w_ff1, l0_w_ff2, l0_norm_ffn_post, l1_norm_mix_pre, l1_w_in, l1_w_out, l1_norm_mix_post, l1_norm_ffn_pre, l1_w_ff1, l1_w_ff2, l1_norm_ffn_post):
    b, seq, d = x.shape
    t = b * seq
    assert seq % NSA_WINDOW == 0 and seq % (DIL_PATTERNS[-1][1] * DIL_SPAN) == 0
    cos, sin = _rope_tables(seq)
    h = x.reshape(t, d)

    tabs0 = [TAB_ROPE128_Q] * 4 + [TAB_ROPE128] * 3 + [TAB_IDENT] * 6 + [TAB_ROPE64]
    z0 = _proj(h, 0, d, l0_norm_mix_pre, _layer0_w_in(l0_w_in), seq, tabs0, cos, sin)

    nc = seq // NSA_CMP_STRIDE
    n_cmp = (seq - NSA_CMP_LEN) // NSA_CMP_STRIDE + 1
    n_slc = seq // NSA_SEL_LEN
    nsp = -(-n_slc // LANES) * LANES
    gd = NSA_KV_GROUPS * HEAD_DIM

    def chunks(col):
        c = z0[:, col:col + gd].reshape(b, nc, NSA_CMP_STRIDE, NSA_KV_GROUPS, HEAD_DIM)
        return c.transpose(0, 3, 1, 2, 4).reshape(b, NSA_KV_GROUPS, nc, NSA_CMP_STRIDE * HEAD_DIM)

    pe = jnp.stack([l0_cmp_pe_k.reshape(1, -1), l0_cmp_pe_v.reshape(1, -1)])
    kv_cmp = _compress(jnp.stack([chunks(Z0_KCMP), chunks(Z0_VCMP)]),
                       jnp.stack([l0_cmp_w1_k, l0_cmp_w1_v]).astype(BF16),
                       jnp.stack([l0_cmp_w2_k, l0_cmp_w2_v]).astype(BF16),
                       jnp.broadcast_to(pe, (2, 8, pe.shape[-1])).astype(F32))
    o_c, bias = _nsa_cmp(z0, kv_cmp, _overlap_matrix(nc, n_cmp, n_slc, nsp), b, seq)
    o_w = _nsa_win(z0, b, seq)
    blk_lane = (jnp.arange(seq, dtype=jnp.int32) // NSA_SEL_LEN) % LANES
    onehot = (blk_lane[:, None] == jnp.arange(LANES, dtype=jnp.int32)[None, :]).astype(BF16)
    o_s = _nsa_sel(z0, bias, onehot, b, seq)

    qcat = _proj(z0, Z0_CQ // MLA_Q_RANK, MLA_Q_RANK, l0_mla_q_norm, _mla_w_uq(l0_mla_w_uq), seq,
                 [TAB_ROPE64_Q] * MLA_HEADS, cos, sin)
    kv = _proj(z0, Z0_CKV // MLA_KV_RANK, MLA_KV_RANK, l0_mla_kv_norm, _mla_w_ukv(l0_mla_w_ukv), seq)
    o_b = _mla(qcat, kv, z0, b, seq)

    ow = NSA_HEADS * HEAD_DIM
    w_out0 = l0_w_out.astype(BF16)
    h = _out0(h, o_c, o_s, o_w, z0, o_b, w_out0[:ow], w_out0[ow:], l0_norm_mix_post)
    h = _mlp(h, l0_norm_ffn_pre, l0_w_ff1.astype(BF16), l0_w_ff2.astype(BF16), l0_norm_ffn_post)

    tabs1 = ([TAB_ROPE128_Q] * 4 + [TAB_ROPE128] * 4) * 3 + [TAB_IDENT] * 12
    z1 = _proj(h, 0, d, l1_norm_mix_pre, _layer1_w_in(l1_w_in), seq, tabs1, cos, sin)
    outs = [_dilated(z1, p, b, seq) for p in range(len(DIL_PATTERNS))]
    h = _out1(h, [o for o, _ in outs], [l for _, l in outs], l1_w_out.astype(BF16), l1_norm_mix_post)
    h = _mlp(h, l1_norm_ffn_pre, l1_w_ff1.astype(BF16), l1_w_ff2.astype(BF16), l1_norm_ffn_post)
    return h.reshape(b, seq, d)
```

```python
import functools

import numpy as np
import jax
import jax.numpy as jnp
from jax import lax
from jax.experimental import pallas as pl
from jax.experimental.pallas import tpu as pltpu

F32 = jnp.float32
BF16 = jnp.bfloat16

HEAD_DIM = 128
LANES = 128
ROPE_THETA = 10000.0
EPS = 1e-6
NEG_INF = -1e30
POS_INF = 1e30
PICKED = -3e38

NSA_HEADS = 8
NSA_KV_GROUPS = 2
NSA_GROUP_HEADS = NSA_HEADS // NSA_KV_GROUPS
NSA_CMP_LEN = 32
NSA_CMP_STRIDE = 16
NSA_SEL_LEN = 64
NSA_SEL_TOPK = 16
NSA_WINDOW = 512

MLA_HEADS = 8
MLA_Q_RANK = 512
MLA_KV_RANK = 256
MLA_NOPE = 128
MLA_ROPE = 64
MLA_V = 128

DIL_PATTERNS = ((128, 1), (512, 4), (2048, 16))
DIL_HEADS = 8
DIL_SPAN = 128

VMEM_LIMIT = 56 * 1024 * 1024

Z0_Q = 0
Z0_KCMP = 1024
Z0_KSLC = 1280
Z0_KWIN = 1536
Z0_VCMP = 1792
Z0_VSLC = 2048
Z0_VWIN = 2304
Z0_CQ = 2560
Z0_CKV = 3072
Z0_GATE = 3328
Z0_KROPE = 3456
Z0_COLS = 3584

TAB_ROPE128, TAB_IDENT, TAB_ROPE64, TAB_ROPE128_Q, TAB_ROPE64_Q = 0, 1, 2, 3, 4

LOG2E = 1.4426950408889634
LN2 = 0.6931471805599453
QSCALE_128 = HEAD_DIM ** -0.5 * LOG2E
QSCALE_MLA = (MLA_NOPE + MLA_ROPE) ** -0.5 * LOG2E


def _cparams(sem):
    return pltpu.CompilerParams(dimension_semantics=sem, vmem_limit_bytes=VMEM_LIMIT)


def _rope_tables(seq):
    def cs(dim):
        inv = 1.0 / (ROPE_THETA ** (jnp.arange(0, dim, 2, dtype=F32) / dim))
        ang = jnp.arange(seq, dtype=F32)[:, None] * inv[None, :]
        return jnp.cos(ang), jnp.sin(ang)

    c128, s128 = cs(HEAD_DIM)
    c64, s64 = cs(MLA_ROPE)
    one = jnp.ones((seq, 32), F32)
    zero = jnp.zeros((seq, 32), F32)
    ident_c = jnp.ones((seq, LANES), F32)
    ident_s = jnp.zeros((seq, LANES), F32)
    r128_c = jnp.concatenate([c128, c128], axis=1)
    r128_s = jnp.concatenate([-s128, s128], axis=1)
    r64_c = jnp.concatenate([c64, one, c64, one], axis=1)
    r64_s = jnp.concatenate([-s64, zero, s64, zero], axis=1)
    cos = [jnp.concatenate([r128_c, r128_c], 1), jnp.concatenate([ident_c, ident_c], 1),
           jnp.concatenate([ident_c, r64_c], 1)]
    sin = [jnp.concatenate([r128_s, r128_s], 1), jnp.concatenate([ident_s, ident_s], 1),
           jnp.concatenate([ident_s, r64_s], 1)]
    cos += [cos[TAB_ROPE128] * QSCALE_128, cos[TAB_ROPE64] * QSCALE_MLA]
    sin += [sin[TAB_ROPE128] * QSCALE_128, sin[TAB_ROPE64] * QSCALE_MLA]
    return jnp.stack(cos), jnp.stack(sin)


def _proj_body(*refs, rope, tn):
    if rope:
        _, x_ref, g_ref, w_ref, c_ref, s_ref, o_ref, xn_ref = refs
    else:
        x_ref, g_ref, w_ref, o_ref, xn_ref = refs

    @pl.when(pl.program_id(1) == 0)
    def _():
        x = x_ref[...].astype(F32)
        y = x * lax.rsqrt(jnp.mean(x * x, axis=-1, keepdims=True) + EPS)
        xn_ref[...] = (y * g_ref[...]).astype(BF16)

    acc = jnp.dot(xn_ref[...], w_ref[...], preferred_element_type=F32)
    if rope:
        parts = []
        for k in range(tn // LANES):
            sl = slice(k * LANES, (k + 1) * LANES)
            a = acc[:, sl]
            parts.append(a * c_ref[:, sl] + pltpu.roll(a, LANES // 2, 1) * s_ref[:, sl])
        acc = jnp.concatenate(parts, axis=1)
    o_ref[...] = acc.astype(o_ref.dtype)


def _proj(x, x_col, d, g, w, seq, tabs=None, cos=None, sin=None, tm=1024, tn=256, out_dtype=BF16):
    t = x.shape[0]
    n = w.shape[1]
    tm = min(tm, seq)
    assert t % tm == 0 and seq % tm == 0 and n % tn == 0
    rope = tabs is not None
    g2 = g.reshape(1, d).astype(F32)
    body = functools.partial(_proj_body, rope=rope, tn=tn)
    scratch = [pltpu.VMEM((tm, d), BF16)]
    out_shape = jax.ShapeDtypeStruct((t, n), out_dtype)
    if rope:
        spb = seq // tm
        grid_spec = pltpu.PrefetchScalarGridSpec(
            num_scalar_prefetch=1, grid=(t // tm, n // tn),
            in_specs=[pl.BlockSpec((tm, d), lambda i, j, tab: (i, x_col)),
                      pl.BlockSpec((1, d), lambda i, j, tab: (0, 0)),
                      pl.BlockSpec((d, tn), lambda i, j, tab: (0, j)),
                      pl.BlockSpec((None, tm, tn), lambda i, j, tab: (tab[j], i % spb, 0)),
                      pl.BlockSpec((None, tm, tn), lambda i, j, tab: (tab[j], i % spb, 0))],
            out_specs=pl.BlockSpec((tm, tn), lambda i, j, tab: (i, j)),
            scratch_shapes=scratch)
        return pl.pallas_call(body, grid_spec=grid_spec, out_shape=out_shape,
                              compiler_params=_cparams(("parallel", "arbitrary")))(
            jnp.asarray(tabs, jnp.int32), x, g2, w, cos, sin)
    return pl.pallas_call(
        body, grid=(t // tm, n // tn),
        in_specs=[pl.BlockSpec((tm, d), lambda i, j: (i, x_col)),
                  pl.BlockSpec((1, d), lambda i, j: (0, 0)),
                  pl.BlockSpec((d, tn), lambda i, j: (0, j))],
        out_specs=pl.BlockSpec((tm, tn), lambda i, j: (i, j)),
        scratch_shapes=scratch, out_shape=out_shape,
        compiler_params=_cparams(("parallel", "arbitrary")))(x, g2, w)


def _compress_body(c_ref, w1_ref, w2_ref, pe_ref, o_ref, *, nc):
    half = NSA_CMP_STRIDE * HEAD_DIM
    c = c_ref[...]
    a = jnp.dot(c, w1_ref[:half, :], preferred_element_type=F32)
    b = jnp.dot(c, w1_ref[half:, :], preferred_element_type=F32)
    pe = pe_ref[...]
    pe_hi = pe.astype(BF16)
    pe_lo = (pe - pe_hi.astype(F32)).astype(BF16)
    pe_term = (jnp.dot(pe_hi, w1_ref[...], preferred_element_type=F32)
               + jnp.dot(pe_lo, w1_ref[...], preferred_element_type=F32))
    hid = a + pltpu.roll(b, nc - 1, 0) + pe_term[0:1, :]
    act = jax.nn.gelu(hid)
    o_ref[...] = jnp.dot(act.astype(BF16), w2_ref[...], preferred_element_type=F32).astype(o_ref.dtype)


def _compress(chunks, w1, w2, pe):
    _, b, g, nc, cw = chunks.shape
    return pl.pallas_call(
        functools.partial(_compress_body, nc=nc), grid=(2, b, g),
        in_specs=[pl.BlockSpec((None, None, None, nc, cw), lambda s, i, j: (s, i, j, 0, 0)),
                  pl.BlockSpec((None, 2 * cw, HEAD_DIM), lambda s, i, j: (s, 0, 0)),
                  pl.BlockSpec((None, HEAD_DIM, HEAD_DIM), lambda s, i, j: (s, 0, 0)),
                  pl.BlockSpec((None, 8, 2 * cw), lambda s, i, j: (s, 0, 0))],
        out_specs=pl.BlockSpec((None, None, None, nc, HEAD_DIM), lambda s, i, j: (s, i, j, 0, 0)),
        out_shape=jax.ShapeDtypeStruct((2, b, g, nc, HEAD_DIM), BF16),
        compiler_params=_cparams(("parallel", "parallel", "parallel")))(chunks, w1, w2, pe)


def _nsa_cmp_body(q_ref, k_ref, v_ref, m_ref, o_ref, bias_ref, *, tq, nc, nsp, topk):
    t = pl.program_id(2) * tq + lax.broadcasted_iota(jnp.int32, (tq, 1), 0)
    n = lax.broadcasted_iota(jnp.int32, (1, nc), 1)
    vis = (n * NSA_CMP_STRIDE + (NSA_CMP_LEN - 1)) <= t
    visf = vis.astype(F32)
    k = k_ref[...]
    v = v_ref[...]
    psum = jnp.zeros((tq, nc), F32)
    for h in range(NSA_GROUP_HEADS):
        sl = slice(h * HEAD_DIM, (h + 1) * HEAD_DIM)
        s = lax.dot_general(q_ref[:, sl], k, (((1,), (1,)), ((), ())), preferred_element_type=F32)
        s = jnp.where(vis, s, NEG_INF)
        e = jnp.exp2(s - jnp.max(s, axis=-1, keepdims=True))
        p = e / jnp.sum(e, axis=-1, keepdims=True) * visf
        o_ref[:, sl] = jnp.dot(p.astype(BF16), v, preferred_element_type=F32).astype(o_ref.dtype)
        psum = psum + p
    hi = psum.astype(BF16)
    lo = (psum - hi.astype(F32)).astype(BF16)
    imp = (jnp.dot(hi, m_ref[...], preferred_element_type=F32)
           + jnp.dot(lo, m_ref[...], preferred_element_type=F32))

    blk = lax.broadcasted_iota(jnp.int32, (1, nsp), 1)
    blkf = blk.astype(F32)
    cur = lax.shift_right_logical(t, 6)
    forced = (blk == 0) | (blk == cur) | (blk == cur - 1)
    causal = blk * NSA_SEL_LEN <= t
    score = jnp.where(forced, POS_INF, jnp.where(causal, imp, NEG_INF))

    def take_one(_, carry):
        score, sel = carry
        mx = jnp.max(score, axis=-1, keepdims=True)
        first = jnp.min(jnp.where(score == mx, blkf, float(nsp)), axis=-1, keepdims=True)
        pick = blkf == first
        sel = jnp.where(pick & (mx > 0.5 * NEG_INF), 1.0, sel)
        return jnp.where(pick, PICKED, score), sel

    _, sel = lax.fori_loop(0, topk, take_one, (score, jnp.zeros((tq, nsp), F32)))
    bias_ref[...] = jnp.where(sel > 0.5, 0.0, NEG_INF).astype(bias_ref.dtype)


def _nsa_cmp(z0, kv_cmp, ovl, b, seq, tq=256):
    nc = kv_cmp.shape[3]
    nsp = ovl.shape[1]
    tq = min(tq, seq)
    nq = seq // tq
    gw = NSA_GROUP_HEADS * HEAD_DIM
    topk = min(NSA_SEL_TOPK, seq // NSA_SEL_LEN)
    body = functools.partial(_nsa_cmp_body, tq=tq, nc=nc, nsp=nsp, topk=topk)
    return pl.pallas_call(
        body, grid=(b, NSA_KV_GROUPS, nq),
        in_specs=[pl.BlockSpec((tq, gw), lambda bi, g, i: (bi * nq + i, g)),
                  pl.BlockSpec((None, None, None, nc, HEAD_DIM), lambda bi, g, i: (0, bi, g, 0, 0)),
                  pl.BlockSpec((None, None, None, nc, HEAD_DIM), lambda bi, g, i: (1, bi, g, 0, 0)),
                  pl.BlockSpec((nc, nsp), lambda bi, g, i: (0, 0))],
        out_specs=[pl.BlockSpec((tq, gw), lambda bi, g, i: (bi * nq + i, g)),
                   pl.BlockSpec((None, None, tq, nsp), lambda bi, g, i: (bi, g, i, 0))],
        out_shape=[jax.ShapeDtypeStruct((b * seq, NSA_HEADS * HEAD_DIM), BF16),
                   jax.ShapeDtypeStruct((b, NSA_KV_GROUPS, seq, nsp), BF16)],
        compiler_params=_cparams(("parallel", "parallel", "parallel")))(z0, kv_cmp, kv_cmp, ovl)


def _nsa_win_body(q_ref, kp_ref, kc_ref, vp_ref, vc_ref, o_ref, *, tq):
    i = pl.program_id(2)
    t = i * tq + lax.broadcasted_iota(jnp.int32, (tq, 1), 0)
    pos = (i - 1) * tq + lax.broadcasted_iota(jnp.int32, (1, 2 * tq), 1)
    rel = t - pos
    mask = (rel >= 0) & (rel < NSA_WINDOW) & (pos >= 0)
    k = jnp.concatenate([kp_ref[...], kc_ref[...]], axis=0)
    v = jnp.concatenate([vp_ref[...], vc_ref[...]], axis=0)
    for h in range(NSA_GROUP_HEADS):
        sl = slice(h * HEAD_DIM, (h + 1) * HEAD_DIM)
        s = lax.dot_general(q_ref[:, sl], k, (((1,), (1,)), ((), ())), preferred_element_type=F32)
        s = jnp.where(mask, s, NEG_INF)
        e = jnp.exp2(s - jnp.max(s, axis=-1, keepdims=True))
        p = e / jnp.sum(e, axis=-1, keepdims=True)
        o_ref[:, sl] = jnp.dot(p.astype(BF16), v, preferred_element_type=F32).astype(o_ref.dtype)


def _nsa_win(z0, b, seq):
    tq = NSA_WINDOW
    assert seq % tq == 0
    nq = seq // tq
    gw = NSA_GROUP_HEADS * HEAD_DIM
    kcol, vcol = Z0_KWIN // HEAD_DIM, Z0_VWIN // HEAD_DIM
    prev = lambda bi, i: bi * nq + jnp.maximum(i - 1, 0)
    return pl.pallas_call(
        functools.partial(_nsa_win_body, tq=tq), grid=(b, NSA_KV_GROUPS, nq),
        in_specs=[pl.BlockSpec((tq, gw), lambda bi, g, i: (bi * nq + i, g)),
                  pl.BlockSpec((tq, HEAD_DIM), lambda bi, g, i: (prev(bi, i), kcol + g)),
                  pl.BlockSpec((tq, HEAD_DIM), lambda bi, g, i: (bi * nq + i, kcol + g)),
                  pl.BlockSpec((tq, HEAD_DIM), lambda bi, g, i: (prev(bi, i), vcol + g)),
                  pl.BlockSpec((tq, HEAD_DIM), lambda bi, g, i: (bi * nq + i, vcol + g))],
        out_specs=pl.BlockSpec((tq, gw), lambda bi, g, i: (bi * nq + i, g)),
        out_shape=jax.ShapeDtypeStruct((b * seq, NSA_HEADS * HEAD_DIM), BF16),
        compiler_params=_cparams(("parallel", "parallel", "parallel")))(z0, z0, z0, z0, z0)


def _tri_schedule(nq):
    qi = np.concatenate([np.full(i + 1, i) for i in range(nq)]).astype(np.int32)
    kj = np.concatenate([np.arange(i + 1) for i in range(nq)]).astype(np.int32)
    return jnp.asarray(qi), jnp.asarray(kj)


def _flash_init(m_scr, acc_scr):
    m_scr[...] = jnp.full(m_scr.shape, NEG_INF, F32)
    acc_scr[...] = jnp.zeros(acc_scr.shape, F32)


def _lane_tile(x, n):
    return jnp.concatenate([x] * n, axis=1) if n > 1 else x


def _flash_update(scores, vs, m_scr, acc_scr):
    nh = len(scores)
    reps = scores[0].shape[1] // LANES
    ones = jnp.ones((vs[0].shape[0], LANES), BF16)
    m_prev = [m_scr[h] for h in range(nh)]
    m_new = [jnp.maximum(m_prev[h], jnp.max(scores[h], axis=-1, keepdims=True)) for h in range(nh)]
    ps = [jnp.exp2((scores[h] - _lane_tile(m_new[h], reps)).astype(BF16)) for h in range(nh)]
    alphas = [jnp.exp2(m_prev[h] - m_new[h]) for h in range(nh)]
    for h in range(nh):
        v_aug = jnp.concatenate([vs[h], ones], axis=1)
        acc_scr[h] = (_lane_tile(alphas[h], acc_scr.shape[2] // LANES) * acc_scr[h]
                      + jnp.dot(ps[h], v_aug, preferred_element_type=F32))
        m_scr[h] = m_new[h]


def _flash_out(acc_scr, h, d):
    acc = acc_scr[h]
    return acc[:, :d] / acc[:, d:]


def _flash_tile(qi, kj, tq, scores, vs, m_scr, acc_scr, write_out):
    @pl.when(kj < qi)
    def _():
        _flash_update(scores, vs, m_scr, acc_scr)

    @pl.when(kj == qi)
    def _():
        row = lax.broadcasted_iota(jnp.int32, (tq, 1), 0)
        col = lax.broadcasted_iota(jnp.int32, (1, tq), 1)
        _flash_update([jnp.where(col <= row, s, NEG_INF) for s in scores], vs, m_scr, acc_scr)
        write_out()


def _nsa_sel_body(qi_ref, kj_ref, q_ref, k_ref, v_ref, oh_ref, bias_ref, o_ref,
                  qa_scr, m_scr, acc_scr, *, tq, per_slab):
    step = pl.program_id(2)
    qi = qi_ref[step]
    kj = kj_ref[step]

    @pl.when(kj == 0)
    def _():
        _flash_init(m_scr, acc_scr)
        for h in range(NSA_GROUP_HEADS):
            qa_scr[h, :, :HEAD_DIM] = q_ref[:, h * HEAD_DIM:(h + 1) * HEAD_DIM]

    @pl.when(kj % per_slab == 0)
    def _():
        for h in range(NSA_GROUP_HEADS):
            qa_scr[h, :, HEAD_DIM:] = bias_ref[...]

    k = jnp.concatenate([k_ref[...], oh_ref[...]], axis=1)
    v = v_ref[...]
    scores = [lax.dot_general(qa_scr[h], k, (((1,), (1,)), ((), ())), preferred_element_type=F32)
              for h in range(NSA_GROUP_HEADS)]

    def write_out():
        for h in range(NSA_GROUP_HEADS):
            o_ref[:, h * HEAD_DIM:(h + 1) * HEAD_DIM] = _flash_out(acc_scr, h, HEAD_DIM).astype(o_ref.dtype)

    _flash_tile(qi, kj, tq, scores, [v] * NSA_GROUP_HEADS, m_scr, acc_scr, write_out)


def _nsa_sel(z0, bias, onehot, b, seq, tq=512):
    tq = min(tq, seq)
    nq = seq // tq
    gw = NSA_GROUP_HEADS * HEAD_DIM
    kcol, vcol = Z0_KSLC // HEAD_DIM, Z0_VSLC // HEAD_DIM
    per_slab = max(LANES * NSA_SEL_LEN // tq, 1)
    qi, kj = _tri_schedule(nq)
    grid_spec = pltpu.PrefetchScalarGridSpec(
        num_scalar_prefetch=2, grid=(b, NSA_KV_GROUPS, int(qi.shape[0])),
        in_specs=[pl.BlockSpec((tq, gw), lambda bi, g, s, qi, kj: (bi * nq + qi[s], g)),
                  pl.BlockSpec((tq, HEAD_DIM), lambda bi, g, s, qi, kj: (bi * nq + kj[s], kcol + g)),
                  pl.BlockSpec((tq, HEAD_DIM), lambda bi, g, s, qi, kj: (bi * nq + kj[s], vcol + g)),
                  pl.BlockSpec((tq, LANES), lambda bi, g, s, qi, kj: (kj[s], 0)),
                  pl.BlockSpec((None, None, tq, LANES),
                               lambda bi, g, s, qi, kj: (bi, g, qi[s], kj[s] // per_slab))],
        out_specs=pl.BlockSpec((tq, gw), lambda bi, g, s, qi, kj: (bi * nq + qi[s], g)),
        scratch_shapes=[pltpu.VMEM((NSA_GROUP_HEADS, tq, 2 * HEAD_DIM), BF16),
                        pltpu.VMEM((NSA_GROUP_HEADS, tq, LANES), F32),
                        pltpu.VMEM((NSA_GROUP_HEADS, tq, HEAD_DIM + LANES), F32)])
    return pl.pallas_call(
        functools.partial(_nsa_sel_body, tq=tq, per_slab=per_slab), grid_spec=grid_spec,
        out_shape=jax.ShapeDtypeStruct((b * seq, NSA_HEADS * HEAD_DIM), BF16),
        compiler_params=_cparams(("parallel", "parallel", "arbitrary")))(qi, kj, z0, z0, z0, onehot, bias)


MLA_STEP_HEADS = 4


def _mla_body(qi_ref, kj_ref, q_ref, kn_ref, kp_ref, v_ref, o_ref, m_scr, acc_scr, *, tq):
    step = pl.program_id(2)
    qi = qi_ref[step]
    kj = kj_ref[step]

    @pl.when(kj == 0)
    def _():
        _flash_init(m_scr, acc_scr)

    kp = kp_ref[...]
    scores, vs = [], []
    for h in range(MLA_STEP_HEADS):
        k = jnp.concatenate([kn_ref[:, h * MLA_NOPE:(h + 1) * MLA_NOPE], kp], axis=1)
        scores.append(lax.dot_general(q_ref[:, h * 2 * LANES:(h + 1) * 2 * LANES], k,
                                      (((1,), (1,)), ((), ())), preferred_element_type=F32))
        vs.append(v_ref[:, h * MLA_V:(h + 1) * MLA_V])

    def write_out():
        for h in range(MLA_STEP_HEADS):
            o_ref[:, h * MLA_V:(h + 1) * MLA_V] = _flash_out(acc_scr, h, MLA_V).astype(o_ref.dtype)

    _flash_tile(qi, kj, tq, scores, vs, m_scr, acc_scr, write_out)


def _mla(qcat, kv, z0, b, seq, tq=512):
    tq = min(tq, seq)
    nq = seq // tq
    nh = MLA_STEP_HEADS
    ngrp = MLA_HEADS // nh
    kpcol = Z0_KROPE // LANES
    qi, kj = _tri_schedule(nq)
    grid_spec = pltpu.PrefetchScalarGridSpec(
        num_scalar_prefetch=2, grid=(b, ngrp, int(qi.shape[0])),
        in_specs=[pl.BlockSpec((tq, nh * 2 * LANES), lambda bi, h, s, qi, kj: (bi * nq + qi[s], h)),
                  pl.BlockSpec((tq, nh * MLA_NOPE), lambda bi, h, s, qi, kj: (bi * nq + kj[s], h)),
                  pl.BlockSpec((tq, LANES), lambda bi, h, s, qi, kj: (bi * nq + kj[s], kpcol)),
                  pl.BlockSpec((tq, nh * MLA_V), lambda bi, h, s, qi, kj: (bi * nq + kj[s], ngrp + h))],
        out_specs=pl.BlockSpec((tq, nh * MLA_V), lambda bi, h, s, qi, kj: (bi * nq + qi[s], h)),
        scratch_shapes=[pltpu.VMEM((nh, tq, LANES), F32), pltpu.VMEM((nh, tq, MLA_V + LANES), F32)])
    return pl.pallas_call(
        functools.partial(_mla_body, tq=tq), grid_spec=grid_spec,
        out_shape=jax.ShapeDtypeStruct((b * seq, MLA_HEADS * MLA_V), BF16),
        compiler_params=_cparams(("parallel", "parallel", "arbitrary")))(qi, kj, qcat, kv, z0, kv)


def _dil_body(q_ref, kp_ref, kc_ref, vp_ref, vc_ref, o_ref, lse_ref, *, tq):
    i = pl.program_id(2)
    u = i * tq + lax.broadcasted_iota(jnp.int32, (tq, 1), 0)
    pos = i * tq - DIL_SPAN + lax.broadcasted_iota(jnp.int32, (1, tq + DIL_SPAN), 1)
    rel = u - pos
    mask = (rel >= 0) & (rel <= DIL_SPAN) & (pos >= 0)
    lane = lax.broadcasted_iota(jnp.int32, (tq, LANES), 1)
    lse_all = jnp.zeros((tq, LANES), F32)
    for h in range(DIL_HEADS):
        sl = slice(h * HEAD_DIM, (h + 1) * HEAD_DIM)
        k = jnp.concatenate([kp_ref[:, sl], kc_ref[:, sl]], axis=0)
        v = jnp.concatenate([vp_ref[:, sl], vc_ref[:, sl]], axis=0)
        s = lax.dot_general(q_ref[:, sl], k, (((1,), (1,)), ((), ())), preferred_element_type=F32)
        s = jnp.where(mask, s, NEG_INF)
        m = jnp.max(s, axis=-1, keepdims=True)
        e = jnp.exp2(s - m)
        den = jnp.sum(e, axis=-1, keepdims=True)
        o_ref[:, sl] = jnp.dot((e / den).astype(BF16), v, preferred_element_type=F32).astype(o_ref.dtype)
        lse_all = jnp.where(lane == h, m * LN2 + jnp.log(den), lse_all)
    lse_ref[...] = lse_all


def _deinterleave_body(*refs, r, rows):
    x_refs, o_ref = refs[:-1], refs[-1]
    hw = len(x_refs) * LANES
    for h, x_ref in enumerate(x_refs):
        for c in range(r):
            o_ref[:, c * hw + h * LANES:c * hw + (h + 1) * LANES] = (
                x_ref[pl.ds(c, rows, stride=r), :].astype(o_ref.dtype))


def _deinterleave(z, col0, r, b, seq, rows=128):
    hw = DIL_HEADS * HEAD_DIM
    sub = seq // r
    rows = min(rows, sub)
    nblk = sub // rows
    return pl.pallas_call(
        functools.partial(_deinterleave_body, r=r, rows=rows), grid=(b, nblk, 3),
        in_specs=[pl.BlockSpec((r * rows, LANES),
                               lambda bi, i, j, h=h: (bi * nblk + i, (col0 + j) * DIL_HEADS + h))
                  for h in range(DIL_HEADS)],
        out_specs=pl.BlockSpec((None, rows, r * hw), lambda bi, i, j: (bi, i, j)),
        out_shape=jax.ShapeDtypeStruct((b, sub, 3 * r * hw), BF16),
        compiler_params=_cparams(("parallel", "parallel", "parallel")))(*([z] * DIL_HEADS))


def _dilated(zv, r, b, seq, tq=512):
    sub = seq // r
    tq = min(tq, sub)
    assert sub % tq == 0 and tq % DIL_SPAN == 0
    nq = sub // tq
    hw = DIL_HEADS * HEAD_DIM
    ratio = tq // DIL_SPAN
    prev = lambda i: jnp.maximum(i * ratio - 1, 0)
    o, lse = pl.pallas_call(
        functools.partial(_dil_body, tq=tq), grid=(b, r, nq),
        in_specs=[pl.BlockSpec((None, tq, hw), lambda bi, c, i: (bi, i, c)),
                  pl.BlockSpec((None, DIL_SPAN, hw), lambda bi, c, i: (bi, prev(i), r + c)),
                  pl.BlockSpec((None, tq, hw), lambda bi, c, i: (bi, i, r + c)),
                  pl.BlockSpec((None, DIL_SPAN, hw), lambda bi, c, i: (bi, prev(i), 2 * r + c)),
                  pl.BlockSpec((None, tq, hw), lambda bi, c, i: (bi, i, 2 * r + c))],
        out_specs=[pl.BlockSpec((None, tq, hw), lambda bi, c, i: (bi, i, c)),
                   pl.BlockSpec((None, tq, LANES), lambda bi, c, i: (bi, i, c))],
        out_shape=[jax.ShapeDtypeStruct((b, sub, r * hw), BF16),
                   jax.ShapeDtypeStruct((b, sub, r * LANES), F32)],
        compiler_params=_cparams(("parallel", "parallel", "parallel")))(zv, zv, zv, zv, zv)
    return o.reshape(b * seq, hw), lse.reshape(b * seq, LANES)


def _finish(h_ref, m, g_ref, o_ref):
    y = m * lax.rsqrt(jnp.mean(m * m, axis=-1, keepdims=True) + EPS)
    o_ref[...] = h_ref[...] + y * g_ref[...]


def _out0_body(h_ref, oc_ref, os_ref, ow_ref, gate_ref, ob_ref, wa_ref, wb_ref, g_ref, o_ref):
    gate = jax.nn.sigmoid(gate_ref[...].astype(F32))
    parts = []
    for h in range(NSA_HEADS):
        sl = slice(h * HEAD_DIM, (h + 1) * HEAD_DIM)
        parts.append(gate[:, 3 * h:3 * h + 1] * oc_ref[:, sl].astype(F32)
                     + gate[:, 3 * h + 1:3 * h + 2] * os_ref[:, sl].astype(F32)
                     + gate[:, 3 * h + 2:3 * h + 3] * ow_ref[:, sl].astype(F32))
    oa = jnp.concatenate(parts, axis=1).astype(BF16)
    m = (jnp.dot(oa, wa_ref[...], preferred_element_type=F32)
         + jnp.dot(ob_ref[...], wb_ref[...], preferred_element_type=F32))
    _finish(h_ref, m, g_ref, o_ref)


def _out1_body(h_ref, o0_ref, o1_ref, o2_ref, l0_ref, l1_ref, l2_ref, w_ref, g_ref, o_ref):
    l0, l1, l2 = l0_ref[...], l1_ref[...], l2_ref[...]
    mx = jnp.maximum(jnp.maximum(l0, l1), l2)
    e0, e1, e2 = jnp.exp(l0 - mx), jnp.exp(l1 - mx), jnp.exp(l2 - mx)
    tot = e0 + e1 + e2
    a0, a1, a2 = e0 / tot, e1 / tot, e2 / tot
    parts = []
    for h in range(DIL_HEADS):
        sl = slice(h * HEAD_DIM, (h + 1) * HEAD_DIM)
        parts.append(a0[:, h:h + 1] * o0_ref[:, sl].astype(F32)
                     + a1[:, h:h + 1] * o1_ref[:, sl].astype(F32)
                     + a2[:, h:h + 1] * o2_ref[:, sl].astype(F32))
    o = jnp.concatenate(parts, axis=1).astype(BF16)
    _finish(h_ref, jnp.dot(o, w_ref[...], preferred_element_type=F32), g_ref, o_ref)


def _row_spec(tm, w, col=0):
    return pl.BlockSpec((tm, w), lambda i: (i, col))


def _full_spec(shape):
    return pl.BlockSpec(shape, lambda i: (0,) * len(shape))


def _out0(h, o_c, o_s, o_w, z0, o_b, wa, wb, g, tm=256):
    t, d = h.shape
    tm = min(tm, t)
    ow = NSA_HEADS * HEAD_DIM
    return pl.pallas_call(
        _out0_body, grid=(t // tm,),
        in_specs=[_row_spec(tm, d), _row_spec(tm, ow), _row_spec(tm, ow), _row_spec(tm, ow),
                  _row_spec(tm, LANES, Z0_GATE // LANES), _row_spec(tm, ow),
                  _full_spec(wa.shape), _full_spec(wb.shape), _full_spec((1, d))],
        out_specs=_row_spec(tm, d), out_shape=jax.ShapeDtypeStruct((t, d), F32),
        compiler_params=_cparams(("parallel",)))(h, o_c, o_s, o_w, z0, o_b, wa, wb, g.reshape(1, d))


def _out1(h, os_, lses, w, g, tm=256):
    t, d = h.shape
    tm = min(tm, t)
    ow = DIL_HEADS * HEAD_DIM
    return pl.pallas_call(
        _out1_body, grid=(t // tm,),
        in_specs=[_row_spec(tm, d)] + [_row_spec(tm, ow)] * 3 + [_row_spec(tm, LANES)] * 3
                 + [_full_spec(w.shape), _full_spec((1, d))],
        out_specs=_row_spec(tm, d), out_shape=jax.ShapeDtypeStruct((t, d), F32),
        compiler_params=_cparams(("parallel",)))(h, *os_, *lses, w, g.reshape(1, d))


def _mlp_body(h_ref, g1_ref, w1_ref, w2_ref, g2_ref, o_ref, xn_ref, acc_ref):
    f = pl.program_id(1)

    @pl.when(f == 0)
    def _():
        x = h_ref[...]
        y = x * lax.rsqrt(jnp.mean(x * x, axis=-1, keepdims=True) + EPS)
        xn_ref[...] = (y * g1_ref[...]).astype(BF16)
        acc_ref[...] = jnp.zeros(acc_ref.shape, F32)

    a = jnp.maximum(jnp.dot(xn_ref[...], w1_ref[...], preferred_element_type=F32), 0.0)
    acc_ref[...] += jnp.dot((a * a).astype(BF16), w2_ref[...], preferred_element_type=F32)

    @pl.when(f == pl.num_programs(1) - 1)
    def _():
        _finish(h_ref, acc_ref[...], g2_ref, o_ref)


def _mlp(h, g1, w1, w2, g2, tm=512, tf=512):
    t, d = h.shape
    ff = w1.shape[1]
    tm = min(tm, t)
    return pl.pallas_call(
        _mlp_body, grid=(t // tm, ff // tf),
        in_specs=[pl.BlockSpec((tm, d), lambda i, f: (i, 0)),
                  pl.BlockSpec((1, d), lambda i, f: (0, 0)),
                  pl.BlockSpec((d, tf), lambda i, f: (0, f)),
                  pl.BlockSpec((tf, d), lambda i, f: (f, 0)),
                  pl.BlockSpec((1, d), lambda i, f: (0, 0))],
        out_specs=pl.BlockSpec((tm, d), lambda i, f: (i, 0)),
        out_shape=jax.ShapeDtypeStruct((t, d), F32),
        scratch_shapes=[pltpu.VMEM((tm, d), BF16), pltpu.VMEM((tm, d), F32)],
        compiler_params=_cparams(("parallel", "arbitrary")))(h, g1.reshape(1, d), w1, w2, g2.reshape(1, d))


def _rope64_tile(w):
    z = jnp.zeros((w.shape[0], 32), w.dtype)
    return jnp.concatenate([w[:, :32], z, w[:, 32:], z], axis=1)


def _layer0_w_in(w_in):
    d = w_in.shape[0]
    o1 = NSA_HEADS * HEAD_DIM
    o2 = o1 + 3 * 2 * NSA_KV_GROUPS * HEAD_DIM
    o3 = o2 + 3 * NSA_HEADS
    o4 = o3 + MLA_Q_RANK
    o5 = o4 + MLA_KV_RANK
    kv = w_in[:, o1:o2].reshape(d, 3, 2, NSA_KV_GROUPS * HEAD_DIM)
    kv = kv.transpose(0, 2, 1, 3).reshape(d, o2 - o1)
    gate = jnp.pad(w_in[:, o2:o3], ((0, 0), (0, LANES - (o3 - o2))))
    w = jnp.concatenate([w_in[:, :o1], kv, w_in[:, o3:o4], w_in[:, o4:o5], gate, _rope64_tile(w_in[:, o5:])], 1)
    assert w.shape[1] == Z0_COLS
    return w.astype(BF16)


def _mla_w_uq(w_uq):
    d = w_uq.shape[0]
    w = w_uq.reshape(d, MLA_HEADS, MLA_NOPE + MLA_ROPE)
    tiles = [jnp.concatenate([w[:, h, :MLA_NOPE], _rope64_tile(w[:, h, MLA_NOPE:])], 1) for h in range(MLA_HEADS)]
    return jnp.concatenate(tiles, axis=1).astype(BF16)


def _mla_w_ukv(w_ukv):
    d = w_ukv.shape[0]
    w = w_ukv.reshape(d, MLA_HEADS, MLA_NOPE + MLA_V)
    return jnp.concatenate([w[:, :, :MLA_NOPE].reshape(d, -1), w[:, :, MLA_NOPE:].reshape(d, -1)], 1).astype(BF16)


def _layer1_w_in(w_in):
    return w_in.astype(BF16)


def _overlap_matrix(nc, n_cmp, n_slc, nsp):
    ratio = NSA_SEL_LEN // NSA_CMP_STRIDE
    m = np.zeros((nc, nsp), np.float32)
    for off in range(1 - NSA_CMP_LEN // NSA_CMP_STRIDE, ratio):
        n = np.arange(n_slc) * ratio + off
        ok = (n >= 0) & (n < n_cmp)
        m[n[ok], np.arange(n_slc)[ok]] = 1.0
    return jnp.asarray(m, BF16)


def kernel(x, l0_norm_mix_pre, l0_w_in, l0_cmp_pe_k, l0_cmp_w1_k, l0_cmp_w2_k, l0_cmp_pe_v, l0_cmp_w1_v, l0_cmp_w2_v, l0_mla_q_norm, l0_mla_w_uq, l0_mla_kv_norm, l0_mla_w_ukv, l0_w_out, l0_norm_mix_post, l0_norm_ffn_pre, l0_w_ff1, l0_w_ff2, l0_norm_ffn_post, l1_norm_mix_pre, l1_w_in, l1_w_out, l1_norm_mix_post, l1_norm_ffn_pre, l1_w_ff1, l1_w_ff2, l1_norm_ffn_post):
    b, seq, d = x.shape
    t = b * seq
    assert seq % NSA_WINDOW == 0 and seq % (DIL_PATTERNS[-1][1] * DIL_SPAN) == 0
    cos, sin = _rope_tables(seq)
    h = x.reshape(t, d)

    tabs0 = [TAB_ROPE128_Q] * 4 + [TAB_ROPE128] * 3 + [TAB_IDENT] * 6 + [TAB_ROPE64]
    z0 = _proj(h, 0, d, l0_norm_mix_pre, _layer0_w_in(l0_w_in), seq, tabs0, cos, sin)

    nc = seq // NSA_CMP_STRIDE
    n_cmp = (seq - NSA_CMP_LEN) // NSA_CMP_STRIDE + 1
    n_slc = seq // NSA_SEL_LEN
    nsp = -(-n_slc // LANES) * LANES
    gd = NSA_KV_GROUPS * HEAD_DIM

    def chunks(col):
        c = z0[:, col:col + gd].reshape(b, nc, NSA_CMP_STRIDE, NSA_KV_GROUPS, HEAD_DIM)
        return c.transpose(0, 3, 1, 2, 4).reshape(b, NSA_KV_GROUPS, nc, NSA_CMP_STRIDE * HEAD_DIM)

    pe = jnp.stack([l0_cmp_pe_k.reshape(1, -1), l0_cmp_pe_v.reshape(1, -1)])
    kv_cmp = _compress(jnp.stack([chunks(Z0_KCMP), chunks(Z0_VCMP)]),
                       jnp.stack([l0_cmp_w1_k, l0_cmp_w1_v]).astype(BF16),
                       jnp.stack([l0_cmp_w2_k, l0_cmp_w2_v]).astype(BF16),
                       jnp.broadcast_to(pe, (2, 8, pe.shape[-1])).astype(F32))
    o_c, bias = _nsa_cmp(z0, kv_cmp, _overlap_matrix(nc, n_cmp, n_slc, nsp), b, seq)
    o_w = _nsa_win(z0, b, seq)
    blk_lane = (jnp.arange(seq, dtype=jnp.int32) // NSA_SEL_LEN) % LANES
    onehot = (blk_lane[:, None] == jnp.arange(LANES, dtype=jnp.int32)[None, :]).astype(BF16)
    o_s = _nsa_sel(z0, bias, onehot, b, seq)

    qcat = _proj(z0, Z0_CQ // MLA_Q_RANK, MLA_Q_RANK, l0_mla_q_norm, _mla_w_uq(l0_mla_w_uq), seq,
                 [TAB_ROPE64_Q] * MLA_HEADS, cos, sin)
    kv = _proj(z0, Z0_CKV // MLA_KV_RANK, MLA_KV_RANK, l0_mla_kv_norm, _mla_w_ukv(l0_mla_w_ukv), seq)
    o_b = _mla(qcat, kv, z0, b, seq)

    ow = NSA_HEADS * HEAD_DIM
    w_out0 = l0_w_out.astype(BF16)
    h = _out0(h, o_c, o_s, o_w, z0, o_b, w_out0[:ow], w_out0[ow:], l0_norm_mix_post)
    h = _mlp(h, l0_norm_ffn_pre, l0_w_ff1.astype(BF16), l0_w_ff2.astype(BF16), l0_norm_ffn_post)

    hw = DIL_HEADS * HEAD_DIM
    tabs_g = [TAB_ROPE128_Q] * 4 + [TAB_ROPE128] * 4 + [TAB_IDENT] * 4
    w1_in = _layer1_w_in(l1_w_in)
    z1a = _proj(h, 0, d, l1_norm_mix_pre, w1_in[:, :3 * hw], seq, tabs_g, cos, sin)
    z1b = _proj(h, 0, d, l1_norm_mix_pre, w1_in[:, 3 * hw:], seq, tabs_g * 2, cos, sin, out_dtype=F32)
    outs = [_dilated(z1a.reshape(b, seq, 3 * hw), 1, b, seq)]
    for p in range(1, len(DIL_PATTERNS)):
        r = DIL_PATTERNS[p][1]
        outs.append(_dilated(_deinterleave(z1b, 3 * (p - 1), r, b, seq), r, b, seq))
    h = _out1(h, [o for o, _ in outs], [l for _, l in outs], l1_w_out.astype(BF16), l1_norm_mix_post)
    h = _mlp(h, l1_norm_ffn_pre, l1_w_ff1.astype(BF16), l1_w_ff2.astype(BF16), l1_norm_ffn_post)
    return h.reshape(b, seq, d)
```

```python
import functools

import numpy as np
import jax
import jax.numpy as jnp
from jax import lax
from jax.experimental import pallas as pl
from jax.experimental.pallas import tpu as pltpu

F32 = jnp.float32
BF16 = jnp.bfloat16

HEAD_DIM = 128
LANES = 128
ROPE_THETA = 10000.0
EPS = 1e-6
NEG_INF = -1e30
POS_INF = 1e30
PICKED = -3e38

NSA_HEADS = 8
NSA_KV_GROUPS = 2
NSA_GROUP_HEADS = NSA_HEADS // NSA_KV_GROUPS
NSA_CMP_LEN = 32
NSA_CMP_STRIDE = 16
NSA_SEL_LEN = 64
NSA_SEL_TOPK = 16
NSA_WINDOW = 512

MLA_HEADS = 8
MLA_Q_RANK = 512
MLA_KV_RANK = 256
MLA_NOPE = 128
MLA_ROPE = 64
MLA_V = 128

DIL_PATTERNS = ((128, 1), (512, 4), (2048, 16))
DIL_HEADS = 8
DIL_SPAN = 128

VMEM_LIMIT = 56 * 1024 * 1024

Z0_Q = 0
Z0_KCMP = 1024
Z0_KSLC = 1280
Z0_KWIN = 1536
Z0_VCMP = 1792
Z0_VSLC = 2048
Z0_VWIN = 2304
Z0_CQ = 2560
Z0_CKV = 3072
Z0_GATE = 3328
Z0_KROPE = 3456
Z0_COLS = 3584

TAB_ROPE128, TAB_IDENT, TAB_ROPE64, TAB_ROPE128_Q, TAB_ROPE64_Q = 0, 1, 2, 3, 4

LOG2E = 1.4426950408889634
LN2 = 0.6931471805599453
QSCALE_128 = HEAD_DIM ** -0.5 * LOG2E
QSCALE_MLA = (MLA_NOPE + MLA_ROPE) ** -0.5 * LOG2E


def _cparams(sem):
    return pltpu.CompilerParams(dimension_semantics=sem, vmem_limit_bytes=VMEM_LIMIT)


def _rope_tables(seq):
    def cs(dim):
        inv = 1.0 / (ROPE_THETA ** (jnp.arange(0, dim, 2, dtype=F32) / dim))
        ang = jnp.arange(seq, dtype=F32)[:, None] * inv[None, :]
        return jnp.cos(ang), jnp.sin(ang)

    c128, s128 = cs(HEAD_DIM)
    c64, s64 = cs(MLA_ROPE)
    one = jnp.ones((seq, 32), F32)
    zero = jnp.zeros((seq, 32), F32)
    ident_c = jnp.ones((seq, LANES), F32)
    ident_s = jnp.zeros((seq, LANES), F32)
    r128_c = jnp.concatenate([c128, c128], axis=1)
    r128_s = jnp.concatenate([-s128, s128], axis=1)
    r64_c = jnp.concatenate([c64, one, c64, one], axis=1)
    r64_s = jnp.concatenate([-s64, zero, s64, zero], axis=1)
    cos = [jnp.concatenate([r128_c, r128_c], 1), jnp.concatenate([ident_c, ident_c], 1),
           jnp.concatenate([ident_c, r64_c], 1)]
    sin = [jnp.concatenate([r128_s, r128_s], 1), jnp.concatenate([ident_s, ident_s], 1),
           jnp.concatenate([ident_s, r64_s], 1)]
    cos += [cos[TAB_ROPE128] * QSCALE_128, cos[TAB_ROPE64] * QSCALE_MLA]
    sin += [sin[TAB_ROPE128] * QSCALE_128, sin[TAB_ROPE64] * QSCALE_MLA]
    return jnp.stack(cos), jnp.stack(sin)


def _proj_body(*refs, rope, tn):
    if rope:
        _, x_ref, g_ref, w_ref, c_ref, s_ref, o_ref, xn_ref = refs
    else:
        x_ref, g_ref, w_ref, o_ref, xn_ref = refs

    @pl.when(pl.program_id(1) == 0)
    def _():
        x = x_ref[...].astype(F32)
        y = x * lax.rsqrt(jnp.mean(x * x, axis=-1, keepdims=True) + EPS)
        xn_ref[...] = (y * g_ref[...]).astype(BF16)

    acc = jnp.dot(xn_ref[...], w_ref[...], preferred_element_type=F32)
    if rope:
        parts = []
        for k in range(tn // LANES):
            sl = slice(k * LANES, (k + 1) * LANES)
            a = acc[:, sl]
            parts.append(a * c_ref[:, sl] + pltpu.roll(a, LANES // 2, 1) * s_ref[:, sl])
        acc = jnp.concatenate(parts, axis=1)
    o_ref[...] = acc.astype(o_ref.dtype)


def _proj(x, x_col, d, g, w, seq, tabs=None, cos=None, sin=None, tm=1024, tn=256, out_dtype=BF16):
    t = x.shape[0]
    n = w.shape[1]
    tm = min(tm, seq)
    assert t % tm == 0 and seq % tm == 0 and n % tn == 0
    rope = tabs is not None
    g2 = g.reshape(1, d).astype(F32)
    body = functools.partial(_proj_body, rope=rope, tn=tn)
    scratch = [pltpu.VMEM((tm, d), BF16)]
    out_shape = jax.ShapeDtypeStruct((t, n), out_dtype)
    if rope:
        spb = seq // tm
        grid_spec = pltpu.PrefetchScalarGridSpec(
            num_scalar_prefetch=1, grid=(t // tm, n // tn),
            in_specs=[pl.BlockSpec((tm, d), lambda i, j, tab: (i, x_col)),
                      pl.BlockSpec((1, d), lambda i, j, tab: (0, 0)),
                      pl.BlockSpec((d, tn), lambda i, j, tab: (0, j)),
                      pl.BlockSpec((None, tm, tn), lambda i, j, tab: (tab[j], i % spb, 0)),
                      pl.BlockSpec((None, tm, tn), lambda i, j, tab: (tab[j], i % spb, 0))],
            out_specs=pl.BlockSpec((tm, tn), lambda i, j, tab: (i, j)),
            scratch_shapes=scratch)
        return pl.pallas_call(body, grid_spec=grid_spec, out_shape=out_shape,
                              compiler_params=_cparams(("parallel", "arbitrary")))(
            jnp.asarray(tabs, jnp.int32), x, g2, w, cos, sin)
    return pl.pallas_call(
        body, grid=(t // tm, n // tn),
        in_specs=[pl.BlockSpec((tm, d), lambda i, j: (i, x_col)),
                  pl.BlockSpec((1, d), lambda i, j: (0, 0)),
                  pl.BlockSpec((d, tn), lambda i, j: (0, j))],
        out_specs=pl.BlockSpec((tm, tn), lambda i, j: (i, j)),
        scratch_shapes=scratch, out_shape=out_shape,
        compiler_params=_cparams(("parallel", "arbitrary")))(x, g2, w)


def _compress_body(c_ref, w1_ref, w2_ref, pe_ref, o_ref, *, nc):
    half = NSA_CMP_STRIDE * HEAD_DIM
    c = c_ref[...]
    a = jnp.dot(c, w1_ref[:half, :], preferred_element_type=F32)
    b = jnp.dot(c, w1_ref[half:, :], preferred_element_type=F32)
    pe = pe_ref[...]
    pe_hi = pe.astype(BF16)
    pe_lo = (pe - pe_hi.astype(F32)).astype(BF16)
    pe_term = (jnp.dot(pe_hi, w1_ref[...], preferred_element_type=F32)
               + jnp.dot(pe_lo, w1_ref[...], preferred_element_type=F32))
    hid = a + pltpu.roll(b, nc - 1, 0) + pe_term[0:1, :]
    act = jax.nn.gelu(hid)
    o_ref[...] = jnp.dot(act.astype(BF16), w2_ref[...], preferred_element_type=F32).astype(o_ref.dtype)


def _compress(chunks, w1, w2, pe):
    _, b, g, nc, cw = chunks.shape
    return pl.pallas_call(
        functools.partial(_compress_body, nc=nc), grid=(2, b, g),
        in_specs=[pl.BlockSpec((None, None, None, nc, cw), lambda s, i, j: (s, i, j, 0, 0)),
                  pl.BlockSpec((None, 2 * cw, HEAD_DIM), lambda s, i, j: (s, 0, 0)),
                  pl.BlockSpec((None, HEAD_DIM, HEAD_DIM), lambda s, i, j: (s, 0, 0)),
                  pl.BlockSpec((None, 8, 2 * cw), lambda s, i, j: (s, 0, 0))],
        out_specs=pl.BlockSpec((None, None, None, nc, HEAD_DIM), lambda s, i, j: (s, i, j, 0, 0)),
        out_shape=jax.ShapeDtypeStruct((2, b, g, nc, HEAD_DIM), BF16),
        compiler_params=_cparams(("parallel", "parallel", "parallel")))(chunks, w1, w2, pe)


def _nsa_cmp_body(q_ref, k_ref, v_ref, m_ref, o_ref, bias_ref, *, tq, nc, nsp, topk):
    t = pl.program_id(2) * tq + lax.broadcasted_iota(jnp.int32, (tq, 1), 0)
    n = lax.broadcasted_iota(jnp.int32, (1, nc), 1)
    vis = (n * NSA_CMP_STRIDE + (NSA_CMP_LEN - 1)) <= t
    any_vis = (t >= NSA_CMP_LEN - 1).astype(F32)
    k = k_ref[...]
    v = v_ref[...]
    psum = jnp.zeros((tq, nc), F32)
    for h in range(NSA_GROUP_HEADS):
        sl = slice(h * HEAD_DIM, (h + 1) * HEAD_DIM)
        s = lax.dot_general(q_ref[:, sl], k, (((1,), (1,)), ((), ())), preferred_element_type=F32)
        s = jnp.where(vis, s, NEG_INF)
        e = jnp.exp2(s - jnp.max(s, axis=-1, keepdims=True))
        p = e * (any_vis / jnp.sum(e, axis=-1, keepdims=True))
        o_ref[:, sl] = jnp.dot(p.astype(BF16), v, preferred_element_type=F32).astype(o_ref.dtype)
        psum = psum + p
    hi = psum.astype(BF16)
    lo = (psum - hi.astype(F32)).astype(BF16)
    imp = (jnp.dot(hi, m_ref[...], preferred_element_type=F32)
           + jnp.dot(lo, m_ref[...], preferred_element_type=F32))

    blk = lax.broadcasted_iota(jnp.int32, (1, nsp), 1)
    blkf = blk.astype(F32)
    cur = lax.shift_right_logical(t, 6)
    forced = (blk == 0) | (blk == cur) | (blk == cur - 1)
    causal = blk * NSA_SEL_LEN <= t
    score = jnp.where(forced, PICKED, jnp.where(causal, imp, NEG_INF))

    def take_one(_, score):
        mx = jnp.max(score, axis=-1, keepdims=True)
        first = jnp.min(jnp.where(score == mx, blkf, float(nsp)), axis=-1, keepdims=True)
        return jnp.where(blkf == first, PICKED, score)

    score = lax.fori_loop(0, topk - 3, take_one, score)
    sel = (score == PICKED) & causal
    bias_ref[...] = jnp.where(sel, 0.0, NEG_INF).astype(bias_ref.dtype)


def _nsa_cmp(z0, kv_cmp, ovl, b, seq, tq=512):
    nc = kv_cmp.shape[3]
    nsp = ovl.shape[1]
    tq = min(tq, seq)
    nq = seq // tq
    gw = NSA_GROUP_HEADS * HEAD_DIM
    topk = min(NSA_SEL_TOPK, seq // NSA_SEL_LEN)
    assert topk >= 3
    body = functools.partial(_nsa_cmp_body, tq=tq, nc=nc, nsp=nsp, topk=topk)
    return pl.pallas_call(
        body, grid=(b, NSA_KV_GROUPS, nq),
        in_specs=[pl.BlockSpec((tq, gw), lambda bi, g, i: (bi * nq + i, g)),
                  pl.BlockSpec((None, None, None, nc, HEAD_DIM), lambda bi, g, i: (0, bi, g, 0, 0)),
                  pl.BlockSpec((None, None, None, nc, HEAD_DIM), lambda bi, g, i: (1, bi, g, 0, 0)),
                  pl.BlockSpec((nc, nsp), lambda bi, g, i: (0, 0))],
        out_specs=[pl.BlockSpec((tq, gw), lambda bi, g, i: (bi * nq + i, g)),
                   pl.BlockSpec((None, None, tq, nsp), lambda bi, g, i: (bi, g, i, 0))],
        out_shape=[jax.ShapeDtypeStruct((b * seq, NSA_HEADS * HEAD_DIM), BF16),
                   jax.ShapeDtypeStruct((b, NSA_KV_GROUPS, seq, nsp), BF16)],
        compiler_params=_cparams(("parallel", "parallel", "parallel")))(z0, kv_cmp, kv_cmp, ovl)


def _nsa_win_body(q_ref, kp_ref, kc_ref, vp_ref, vc_ref, o_ref, *, tq):
    i = pl.program_id(2)
    t = i * tq + lax.broadcasted_iota(jnp.int32, (tq, 1), 0)
    pos = (i - 1) * tq + lax.broadcasted_iota(jnp.int32, (1, 2 * tq), 1)
    rel = t - pos
    mask = (rel >= 0) & (rel < NSA_WINDOW) & (pos >= 0)
    k = jnp.concatenate([kp_ref[...], kc_ref[...]], axis=0)
    v = jnp.concatenate([vp_ref[...], vc_ref[...]], axis=0)
    for h in range(NSA_GROUP_HEADS):
        sl = slice(h * HEAD_DIM, (h + 1) * HEAD_DIM)
        s = lax.dot_general(q_ref[:, sl], k, (((1,), (1,)), ((), ())), preferred_element_type=F32)
        s = jnp.where(mask, s, NEG_INF)
        e = jnp.exp2(s - jnp.max(s, axis=-1, keepdims=True))
        p = e / jnp.sum(e, axis=-1, keepdims=True)
        o_ref[:, sl] = jnp.dot(p.astype(BF16), v, preferred_element_type=F32).astype(o_ref.dtype)


def _nsa_win(z0, b, seq):
    tq = NSA_WINDOW
    assert seq % tq == 0
    nq = seq // tq
    gw = NSA_GROUP_HEADS * HEAD_DIM
    kcol, vcol = Z0_KWIN // HEAD_DIM, Z0_VWIN // HEAD_DIM
    prev = lambda bi, i: bi * nq + jnp.maximum(i - 1, 0)
    return pl.pallas_call(
        functools.partial(_nsa_win_body, tq=tq), grid=(b, NSA_KV_GROUPS, nq),
        in_specs=[pl.BlockSpec((tq, gw), lambda bi, g, i: (bi * nq + i, g)),
                  pl.BlockSpec((tq, HEAD_DIM), lambda bi, g, i: (prev(bi, i), kcol + g)),
                  pl.BlockSpec((tq, HEAD_DIM), lambda bi, g, i: (bi * nq + i, kcol + g)),
                  pl.BlockSpec((tq, HEAD_DIM), lambda bi, g, i: (prev(bi, i), vcol + g)),
                  pl.BlockSpec((tq, HEAD_DIM), lambda bi, g, i: (bi * nq + i, vcol + g))],
        out_specs=pl.BlockSpec((tq, gw), lambda bi, g, i: (bi * nq + i, g)),
        out_shape=jax.ShapeDtypeStruct((b * seq, NSA_HEADS * HEAD_DIM), BF16),
        compiler_params=_cparams(("parallel", "parallel", "parallel")))(z0, z0, z0, z0, z0)


def _tri_schedule(nq):
    qi = np.concatenate([np.full(i + 1, i) for i in range(nq)]).astype(np.int32)
    kj = np.concatenate([np.arange(i + 1) for i in range(nq)]).astype(np.int32)
    return jnp.asarray(qi), jnp.asarray(kj)


def _flash_init(m_scr, acc_scr):
    m_scr[...] = jnp.full(m_scr.shape, NEG_INF, F32)
    acc_scr[...] = jnp.zeros(acc_scr.shape, F32)


def _lane_tile(x, n):
    return jnp.concatenate([x] * n, axis=1) if n > 1 else x


def _flash_update(scores, vs, m_scr, acc_scr):
    nh = len(scores)
    reps = scores[0].shape[1] // LANES
    ones = jnp.ones((vs[0].shape[0], LANES), BF16)
    m_prev = [m_scr[h] for h in range(nh)]
    m_new = [jnp.maximum(m_prev[h], jnp.max(scores[h], axis=-1, keepdims=True)) for h in range(nh)]
    ps = [jnp.exp2((scores[h] - _lane_tile(m_new[h], reps)).astype(BF16)) for h in range(nh)]
    alphas = [jnp.exp2(m_prev[h] - m_new[h]) for h in range(nh)]
    for h in range(nh):
        v_aug = jnp.concatenate([vs[h], ones], axis=1)
        acc_scr[h] = (_lane_tile(alphas[h], acc_scr.shape[2] // LANES) * acc_scr[h]
                      + jnp.dot(ps[h], v_aug, preferred_element_type=F32))
        m_scr[h] = m_new[h]


def _flash_out(acc_scr, h, d):
    acc = acc_scr[h]
    return acc[:, :d] / acc[:, d:]


def _flash_tile(qi, kj, tq, scores_and_values, m_scr, acc_scr, write_out):
    @pl.when(kj < qi)
    def _():
        scores, vs = scores_and_values()
        _flash_update(scores, vs, m_scr, acc_scr)

    @pl.when(kj == qi)
    def _():
        scores, vs = scores_and_values()
        row = lax.broadcasted_iota(jnp.int32, (tq, 1), 0)
        col = lax.broadcasted_iota(jnp.int32, (1, tq), 1)
        _flash_update([jnp.where(col <= row, s, NEG_INF) for s in scores], vs, m_scr, acc_scr)
        write_out()


def _nsa_sel_body(qi_ref, kj_ref, q_ref, k_ref, v_ref, oh_ref, bias_ref, o_ref,
                  qa_scr, m_scr, acc_scr, *, tq, per_slab):
    step = pl.program_id(2)
    qi = qi_ref[step]
    kj = kj_ref[step]

    @pl.when(kj == 0)
    def _():
        _flash_init(m_scr, acc_scr)
        for h in range(NSA_GROUP_HEADS):
            qa_scr[h, :, :HEAD_DIM] = q_ref[:, h * HEAD_DIM:(h + 1) * HEAD_DIM]

    @pl.when(kj % per_slab == 0)
    def _():
        for h in range(NSA_GROUP_HEADS):
            qa_scr[h, :, HEAD_DIM:] = bias_ref[...]

    def scores_and_values():
        k = jnp.concatenate([k_ref[...], oh_ref[...]], axis=1)
        scores = [lax.dot_general(qa_scr[h], k, (((1,), (1,)), ((), ())), preferred_element_type=F32)
                  for h in range(NSA_GROUP_HEADS)]
        return scores, [v_ref[...]] * NSA_GROUP_HEADS

    def write_out():
        for h in range(NSA_GROUP_HEADS):
            o_ref[:, h * HEAD_DIM:(h + 1) * HEAD_DIM] = _flash_out(acc_scr, h, HEAD_DIM).astype(o_ref.dtype)

    _flash_tile(qi, kj, tq, scores_and_values, m_scr, acc_scr, write_out)


def _nsa_sel(z0, bias, onehot, b, seq, tq=512):
    tq = min(tq, seq)
    nq = seq // tq
    gw = NSA_GROUP_HEADS * HEAD_DIM
    kcol, vcol = Z0_KSLC // HEAD_DIM, Z0_VSLC // HEAD_DIM
    per_slab = max(LANES * NSA_SEL_LEN // tq, 1)
    qi, kj = _tri_schedule(nq)
    grid_spec = pltpu.PrefetchScalarGridSpec(
        num_scalar_prefetch=2, grid=(b, NSA_KV_GROUPS, int(qi.shape[0])),
        in_specs=[pl.BlockSpec((tq, gw), lambda bi, g, s, qi, kj: (bi * nq + qi[s], g)),
                  pl.BlockSpec((tq, HEAD_DIM), lambda bi, g, s, qi, kj: (bi * nq + kj[s], kcol + g)),
                  pl.BlockSpec((tq, HEAD_DIM), lambda bi, g, s, qi, kj: (bi * nq + kj[s], vcol + g)),
                  pl.BlockSpec((tq, LANES), lambda bi, g, s, qi, kj: (kj[s], 0)),
                  pl.BlockSpec((None, None, tq, LANES),
                               lambda bi, g, s, qi, kj: (bi, g, qi[s], kj[s] // per_slab))],
        out_specs=pl.BlockSpec((tq, gw), lambda bi, g, s, qi, kj: (bi * nq + qi[s], g)),
        scratch_shapes=[pltpu.VMEM((NSA_GROUP_HEADS, tq, 2 * HEAD_DIM), BF16),
                        pltpu.VMEM((NSA_GROUP_HEADS, tq, LANES), F32),
                        pltpu.VMEM((NSA_GROUP_HEADS, tq, HEAD_DIM + LANES), F32)])
    return pl.pallas_call(
        functools.partial(_nsa_sel_body, tq=tq, per_slab=per_slab), grid_spec=grid_spec,
        out_shape=jax.ShapeDtypeStruct((b * seq, NSA_HEADS * HEAD_DIM), BF16),
        compiler_params=_cparams(("parallel", "parallel", "arbitrary")))(qi, kj, z0, z0, z0, onehot, bias)


MLA_STEP_HEADS = 4


def _mla_body(qi_ref, kj_ref, q_ref, kn_ref, kp_ref, v_ref, o_ref, m_scr, acc_scr, *, tq):
    step = pl.program_id(2)
    qi = qi_ref[step]
    kj = kj_ref[step]

    @pl.when(kj == 0)
    def _():
        _flash_init(m_scr, acc_scr)

    def scores_and_values():
        kp = kp_ref[...]
        scores, vs = [], []
        for h in range(MLA_STEP_HEADS):
            k = jnp.concatenate([kn_ref[:, h * MLA_NOPE:(h + 1) * MLA_NOPE], kp], axis=1)
            scores.append(lax.dot_general(q_ref[:, h * 2 * LANES:(h + 1) * 2 * LANES], k,
                                          (((1,), (1,)), ((), ())), preferred_element_type=F32))
            vs.append(v_ref[:, h * MLA_V:(h + 1) * MLA_V])
        return scores, vs

    def write_out():
        for h in range(MLA_STEP_HEADS):
            o_ref[:, h * MLA_V:(h + 1) * MLA_V] = _flash_out(acc_scr, h, MLA_V).astype(o_ref.dtype)

    _flash_tile(qi, kj, tq, scores_and_values, m_scr, acc_scr, write_out)


def _mla(qcat, kv, z0, b, seq, tq=512):
    tq = min(tq, seq)
    nq = seq // tq
    nh = MLA_STEP_HEADS
    ngrp = MLA_HEADS // nh
    kpcol = Z0_KROPE // LANES
    qi, kj = _tri_schedule(nq)
    grid_spec = pltpu.PrefetchScalarGridSpec(
        num_scalar_prefetch=2, grid=(b, ngrp, int(qi.shape[0])),
        in_specs=[pl.BlockSpec((tq, nh * 2 * LANES), lambda bi, h, s, qi, kj: (bi * nq + qi[s], h)),
                  pl.BlockSpec((tq, nh * MLA_NOPE), lambda bi, h, s, qi, kj: (bi * nq + kj[s], h)),
                  pl.BlockSpec((tq, LANES), lambda bi, h, s, qi, kj: (bi * nq + kj[s], kpcol)),
                  pl.BlockSpec((tq, nh * MLA_V), lambda bi, h, s, qi, kj: (bi * nq + kj[s], ngrp + h))],
        out_specs=pl.BlockSpec((tq, nh * MLA_V), lambda bi, h, s, qi, kj: (bi * nq + qi[s], h)),
        scratch_shapes=[pltpu.VMEM((nh, tq, LANES), F32), pltpu.VMEM((nh, tq, MLA_V + LANES), F32)])
    return pl.pallas_call(
        functools.partial(_mla_body, tq=tq), grid_spec=grid_spec,
        out_shape=jax.ShapeDtypeStruct((b * seq, MLA_HEADS * MLA_V), BF16),
        compiler_params=_cparams(("parallel", "parallel", "arbitrary")))(qi, kj, qcat, kv, z0, kv)


def _dil_body(q_ref, kp_ref, kc_ref, vp_ref, vc_ref, o_ref, lse_ref, *, tq):
    i = pl.program_id(2)
    u = i * tq + lax.broadcasted_iota(jnp.int32, (tq, 1), 0)
    pos = i * tq - DIL_SPAN + lax.broadcasted_iota(jnp.int32, (1, tq + DIL_SPAN), 1)
    rel = u - pos
    mask = (rel >= 0) & (rel <= DIL_SPAN) & (pos >= 0)
    lane = lax.broadcasted_iota(jnp.int32, (tq, LANES), 1)
    lse_all = jnp.zeros((tq, LANES), F32)
    for h in range(DIL_HEADS):
        sl = slice(h * HEAD_DIM, (h + 1) * HEAD_DIM)
        k = jnp.concatenate([kp_ref[:, sl], kc_ref[:, sl]], axis=0)
        v = jnp.concatenate([vp_ref[:, sl], vc_ref[:, sl]], axis=0)
        s = lax.dot_general(q_ref[:, sl], k, (((1,), (1,)), ((), ())), preferred_element_type=F32)
        s = jnp.where(mask, s, NEG_INF)
        m = jnp.max(s, axis=-1, keepdims=True)
        e = jnp.exp2(s - m)
        den = jnp.sum(e, axis=-1, keepdims=True)
        o_ref[:, sl] = jnp.dot((e / den).astype(BF16), v, preferred_element_type=F32).astype(o_ref.dtype)
        lse_all = jnp.where(lane == h, m * LN2 + jnp.log(den), lse_all)
    lse_ref[...] = lse_all


def _deinterleave_body(*refs, r, rows):
    x_refs, o_ref = refs[:-1], refs[-1]
    hw = len(x_refs) * LANES
    for h, x_ref in enumerate(x_refs):
        for c in range(r):
            o_ref[:, c * hw + h * LANES:c * hw + (h + 1) * LANES] = (
                x_ref[pl.ds(c, rows, stride=r), :].astype(o_ref.dtype))


def _deinterleave(z, col0, r, b, seq, rows=128):
    hw = DIL_HEADS * HEAD_DIM
    sub = seq // r
    rows = min(rows, sub)
    nblk = sub // rows
    return pl.pallas_call(
        functools.partial(_deinterleave_body, r=r, rows=rows), grid=(b, nblk, 3),
        in_specs=[pl.BlockSpec((r * rows, LANES),
                               lambda bi, i, j, h=h: (bi * nblk + i, (col0 + j) * DIL_HEADS + h))
                  for h in range(DIL_HEADS)],
        out_specs=pl.BlockSpec((None, rows, r * hw), lambda bi, i, j: (bi, i, j)),
        out_shape=jax.ShapeDtypeStruct((b, sub, 3 * r * hw), BF16),
        compiler_params=_cparams(("parallel", "parallel", "parallel")))(*([z] * DIL_HEADS))


def _dilated(zv, r, b, seq, tq=512):
    sub = seq // r
    tq = min(tq, sub)
    assert sub % tq == 0 and tq % DIL_SPAN == 0
    nq = sub // tq
    hw = DIL_HEADS * HEAD_DIM
    ratio = tq // DIL_SPAN
    prev = lambda i: jnp.maximum(i * ratio - 1, 0)
    o, lse = pl.pallas_call(
        functools.partial(_dil_body, tq=tq), grid=(b, r, nq),
        in_specs=[pl.BlockSpec((None, tq, hw), lambda bi, c, i: (bi, i, c)),
                  pl.BlockSpec((None, DIL_SPAN, hw), lambda bi, c, i: (bi, prev(i), r + c)),
                  pl.BlockSpec((None, tq, hw), lambda bi, c, i: (bi, i, r + c)),
                  pl.BlockSpec((None, DIL_SPAN, hw), lambda bi, c, i: (bi, prev(i), 2 * r + c)),
                  pl.BlockSpec((None, tq, hw), lambda bi, c, i: (bi, i, 2 * r + c))],
        out_specs=[pl.BlockSpec((None, tq, hw), lambda bi, c, i: (bi, i, c)),
                   pl.BlockSpec((None, tq, LANES), lambda bi, c, i: (bi, i, c))],
        out_shape=[jax.ShapeDtypeStruct((b, sub, r * hw), BF16),
                   jax.ShapeDtypeStruct((b, sub, r * LANES), F32)],
        compiler_params=_cparams(("parallel", "parallel", "parallel")))(zv, zv, zv, zv, zv)
    return o.reshape(b * seq, hw), lse.reshape(b * seq, LANES)


def _finish(h_ref, m, g_ref, o_ref):
    y = m * lax.rsqrt(jnp.mean(m * m, axis=-1, keepdims=True) + EPS)
    o_ref[...] = h_ref[...] + y * g_ref[...]


def _out0_body(h_ref, oc_ref, os_ref, ow_ref, gate_ref, ob_ref, wa_ref, wb_ref, g_ref, o_ref):
    gate = jax.nn.sigmoid(gate_ref[...].astype(F32))
    parts = []
    for h in range(NSA_HEADS):
        sl = slice(h * HEAD_DIM, (h + 1) * HEAD_DIM)
        parts.append(gate[:, 3 * h:3 * h + 1] * oc_ref[:, sl].astype(F32)
                     + gate[:, 3 * h + 1:3 * h + 2] * os_ref[:, sl].astype(F32)
                     + gate[:, 3 * h + 2:3 * h + 3] * ow_ref[:, sl].astype(F32))
    oa = jnp.concatenate(parts, axis=1).astype(BF16)
    m = (jnp.dot(oa, wa_ref[...], preferred_element_type=F32)
         + jnp.dot(ob_ref[...], wb_ref[...], preferred_element_type=F32))
    _finish(h_ref, m, g_ref, o_ref)


def _out1_body(h_ref, o0_ref, o1_ref, o2_ref, l0_ref, l1_ref, l2_ref, w_ref, g_ref, o_ref):
    l0, l1, l2 = l0_ref[...], l1_ref[...], l2_ref[...]
    mx = jnp.maximum(jnp.maximum(l0, l1), l2)
    e0, e1, e2 = jnp.exp(l0 - mx), jnp.exp(l1 - mx), jnp.exp(l2 - mx)
    tot = e0 + e1 + e2
    a0, a1, a2 = e0 / tot, e1 / tot, e2 / tot
    parts = []
    for h in range(DIL_HEADS):
        sl = slice(h * HEAD_DIM, (h + 1) * HEAD_DIM)
        parts.append(a0[:, h:h + 1] * o0_ref[:, sl].astype(F32)
                     + a1[:, h:h + 1] * o1_ref[:, sl].astype(F32)
                     + a2[:, h:h + 1] * o2_ref[:, sl].astype(F32))
    o = jnp.concatenate(parts, axis=1).astype(BF16)
    _finish(h_ref, jnp.dot(o, w_ref[...], preferred_element_type=F32), g_ref, o_ref)


def _row_spec(tm, w, col=0):
    return pl.BlockSpec((tm, w), lambda i: (i, col))


def _full_spec(shape):
    return pl.BlockSpec(shape, lambda i: (0,) * len(shape))


def _out0(h, o_c, o_s, o_w, z0, o_b, wa, wb, g, tm=256):
    t, d = h.shape
    tm = min(tm, t)
    ow = NSA_HEADS * HEAD_DIM
    return pl.pallas_call(
        _out0_body, grid=(t // tm,),
        in_specs=[_row_spec(tm, d), _row_spec(tm, ow), _row_spec(tm, ow), _row_spec(tm, ow),
                  _row_spec(tm, LANES, Z0_GATE // LANES), _row_spec(tm, ow),
                  _full_spec(wa.shape), _full_spec(wb.shape), _full_spec((1, d))],
        out_specs=_row_spec(tm, d), out_shape=jax.ShapeDtypeStruct((t, d), F32),
        compiler_params=_cparams(("parallel",)))(h, o_c, o_s, o_w, z0, o_b, wa, wb, g.reshape(1, d))


def _out1(h, os_, lses, w, g, tm=256):
    t, d = h.shape
    tm = min(tm, t)
    ow = DIL_HEADS * HEAD_DIM
    return pl.pallas_call(
        _out1_body, grid=(t // tm,),
        in_specs=[_row_spec(tm, d)] + [_row_spec(tm, ow)] * 3 + [_row_spec(tm, LANES)] * 3
                 + [_full_spec(w.shape), _full_spec((1, d))],
        out_specs=_row_spec(tm, d), out_shape=jax.ShapeDtypeStruct((t, d), F32),
        compiler_params=_cparams(("parallel",)))(h, *os_, *lses, w, g.reshape(1, d))


def _mlp_body(h_ref, g1_ref, w1_ref, w2_ref, g2_ref, o_ref, xn_ref, acc_ref):
    f = pl.program_id(1)

    @pl.when(f == 0)
    def _():
        x = h_ref[...]
        y = x * lax.rsqrt(jnp.mean(x * x, axis=-1, keepdims=True) + EPS)
        xn_ref[...] = (y * g1_ref[...]).astype(BF16)
        acc_ref[...] = jnp.zeros(acc_ref.shape, F32)

    a = jnp.maximum(jnp.dot(xn_ref[...], w1_ref[...], preferred_element_type=F32), 0.0)
    acc_ref[...] += jnp.dot((a * a).astype(BF16), w2_ref[...], preferred_element_type=F32)

    @pl.when(f == pl.num_programs(1) - 1)
    def _():
        _finish(h_ref, acc_ref[...], g2_ref, o_ref)


def _mlp(h, g1, w1, w2, g2, tm=512, tf=512):
    t, d = h.shape
    ff = w1.shape[1]
    tm = min(tm, t)
    return pl.pallas_call(
        _mlp_body, grid=(t // tm, ff // tf),
        in_specs=[pl.BlockSpec((tm, d), lambda i, f: (i, 0)),
                  pl.BlockSpec((1, d), lambda i, f: (0, 0)),
                  pl.BlockSpec((d, tf), lambda i, f: (0, f)),
                  pl.BlockSpec((tf, d), lambda i, f: (f, 0)),
                  pl.BlockSpec((1, d), lambda i, f: (0, 0))],
        out_specs=pl.BlockSpec((tm, d), lambda i, f: (i, 0)),
        out_shape=jax.ShapeDtypeStruct((t, d), F32),
        scratch_shapes=[pltpu.VMEM((tm, d), BF16), pltpu.VMEM((tm, d), F32)],
        compiler_params=_cparams(("parallel", "arbitrary")))(h, g1.reshape(1, d), w1, w2, g2.reshape(1, d))


def _rope64_tile(w):
    z = jnp.zeros((w.shape[0], 32), w.dtype)
    return jnp.concatenate([w[:, :32], z, w[:, 32:], z], axis=1)


def _layer0_w_in(w_in):
    d = w_in.shape[0]
    o1 = NSA_HEADS * HEAD_DIM
    o2 = o1 + 3 * 2 * NSA_KV_GROUPS * HEAD_DIM
    o3 = o2 + 3 * NSA_HEADS
    o4 = o3 + MLA_Q_RANK
    o5 = o4 + MLA_KV_RANK
    kv = w_in[:, o1:o2].reshape(d, 3, 2, NSA_KV_GROUPS * HEAD_DIM)
    kv = kv.transpose(0, 2, 1, 3).reshape(d, o2 - o1)
    gate = jnp.pad(w_in[:, o2:o3], ((0, 0), (0, LANES - (o3 - o2))))
    w = jnp.concatenate([w_in[:, :o1], kv, w_in[:, o3:o4], w_in[:, o4:o5], gate, _rope64_tile(w_in[:, o5:])], 1)
    assert w.shape[1] == Z0_COLS
    return w.astype(BF16)


def _mla_w_uq(w_uq):
    d = w_uq.shape[0]
    w = w_uq.reshape(d, MLA_HEADS, MLA_NOPE + MLA_ROPE)
    tiles = [jnp.concatenate([w[:, h, :MLA_NOPE], _rope64_tile(w[:, h, MLA_NOPE:])], 1) for h in range(MLA_HEADS)]
    return jnp.concatenate(tiles, axis=1).astype(BF16)


def _mla_w_ukv(w_ukv):
    d = w_ukv.shape[0]
    w = w_ukv.reshape(d, MLA_HEADS, MLA_NOPE + MLA_V)
    return jnp.concatenate([w[:, :, :MLA_NOPE].reshape(d, -1), w[:, :, MLA_NOPE:].reshape(d, -1)], 1).astype(BF16)


def _layer1_w_in(w_in):
    return w_in.astype(BF16)


def _overlap_matrix(nc, n_cmp, n_slc, nsp):
    ratio = NSA_SEL_LEN // NSA_CMP_STRIDE
    m = np.zeros((nc, nsp), np.float32)
    for off in range(1 - NSA_CMP_LEN // NSA_CMP_STRIDE, ratio):
        n = np.arange(n_slc) * ratio + off
        ok = (n >= 0) & (n < n_cmp)
        m[n[ok], np.arange(n_slc)[ok]] = 1.0
    return jnp.asarray(m, BF16)


def kernel(x, l0_norm_mix_pre, l0_w_in, l0_cmp_pe_k, l0_cmp_w1_k, l0_cmp_w2_k, l0_cmp_pe_v, l0_cmp_w1_v, l0_cmp_w2_v, l0_mla_q_norm, l0_mla_w_uq, l0_mla_kv_norm, l0_mla_w_ukv, l0_w_out, l0_norm_mix_post, l0_norm_ffn_pre, l0_w_ff1, l0_w_ff2, l0_norm_ffn_post, l1_norm_mix_pre, l1_w_in, l1_w_out, l1_norm_mix_post, l1_norm_ffn_pre, l1_w_ff1, l1_w_ff2, l1_norm_ffn_post):
    b, seq, d = x.shape
    t = b * seq
    assert seq % NSA_WINDOW == 0 and seq % (DIL_PATTERNS[-1][1] * DIL_SPAN) == 0
    cos, sin = _rope_tables(seq)
    h = x.reshape(t, d)

    tabs0 = [TAB_ROPE128_Q] * 4 + [TAB_ROPE128] * 3 + [TAB_IDENT] * 6 + [TAB_ROPE64]
    z0 = _proj(h, 0, d, l0_norm_mix_pre, _layer0_w_in(l0_w_in), seq, tabs0, cos, sin)

    nc = seq // NSA_CMP_STRIDE
    n_cmp = (seq - NSA_CMP_LEN) // NSA_CMP_STRIDE + 1
    n_slc = seq // NSA_SEL_LEN
    nsp = -(-n_slc // LANES) * LANES
    gd = NSA_KV_GROUPS * HEAD_DIM

    def chunks(col):
        c = z0[:, col:col + gd].reshape(b, nc, NSA_CMP_STRIDE, NSA_KV_GROUPS, HEAD_DIM)
        return c.transpose(0, 3, 1, 2, 4).reshape(b, NSA_KV_GROUPS, nc, NSA_CMP_STRIDE * HEAD_DIM)

    pe = jnp.stack([l0_cmp_pe_k.reshape(1, -1), l0_cmp_pe_v.reshape(1, -1)])
    kv_cmp = _compress(jnp.stack([chunks(Z0_KCMP), chunks(Z0_VCMP)]),
                       jnp.stack([l0_cmp_w1_k, l0_cmp_w1_v]).astype(BF16),
                       jnp.stack([l0_cmp_w2_k, l0_cmp_w2_v]).astype(BF16),
                       jnp.broadcast_to(pe, (2, 8, pe.shape[-1])).astype(F32))
    o_c, bias = _nsa_cmp(z0, kv_cmp, _overlap_matrix(nc, n_cmp, n_slc, nsp), b, seq)
    o_w = _nsa_win(z0, b, seq)
    blk_lane = (jnp.arange(seq, dtype=jnp.int32) // NSA_SEL_LEN) % LANES
    onehot = (blk_lane[:, None] == jnp.arange(LANES, dtype=jnp.int32)[None, :]).astype(BF16)
    o_s = _nsa_sel(z0, bias, onehot, b, seq)

    qcat = _proj(z0, Z0_CQ // MLA_Q_RANK, MLA_Q_RANK, l0_mla_q_norm, _mla_w_uq(l0_mla_w_uq), seq,
                 [TAB_ROPE64_Q] * MLA_HEADS, cos, sin)
    kv = _proj(z0, Z0_CKV // MLA_KV_RANK, MLA_KV_RANK, l0_mla_kv_norm, _mla_w_ukv(l0_mla_w_ukv), seq)
    o_b = _mla(qcat, kv, z0, b, seq)

    ow = NSA_HEADS * HEAD_DIM
    w_out0 = l0_w_out.astype(BF16)
    h = _out0(h, o_c, o_s, o_w, z0, o_b, w_out0[:ow], w_out0[ow:], l0_norm_mix_post)
    h = _mlp(h, l0_norm_ffn_pre, l0_w_ff1.astype(BF16), l0_w_ff2.astype(BF16), l0_norm_ffn_post)

    hw = DIL_HEADS * HEAD_DIM
    tabs_g = [TAB_ROPE128_Q] * 4 + [TAB_ROPE128] * 4 + [TAB_IDENT] * 4
    w1_in = _layer1_w_in(l1_w_in)
    z1a = _proj(h, 0, d, l1_norm_mix_pre, w1_in[:, :3 * hw], seq, tabs_g, cos, sin)
    z1b = _proj(h, 0, d, l1_norm_mix_pre, w1_in[:, 3 * hw:], seq, tabs_g * 2, cos, sin, out_dtype=F32)
    outs = [_dilated(z1a.reshape(b, seq, 3 * hw), 1, b, seq)]
    for p in range(1, len(DIL_PATTERNS)):
        r = DIL_PATTERNS[p][1]
        outs.append(_dilated(_deinterleave(z1b, 3 * (p - 1), r, b, seq), r, b, seq))
    h = _out1(h, [o for o, _ in outs], [l for _, l in outs], l1_w_out.astype(BF16), l1_norm_mix_post)
    h = _mlp(h, l1_norm_ffn_pre, l1_w_ff1.astype(BF16), l1_w_ff2.astype(BF16), l1_norm_ffn_post)
    return h.reshape(b, seq, d)
```

```python
import functools

import numpy as np
import jax
import jax.numpy as jnp
from jax import lax
from jax.experimental import pallas as pl
from jax.experimental.pallas import tpu as pltpu

F32 = jnp.float32
BF16 = jnp.bfloat16

HEAD_DIM = 128
LANES = 128
ROPE_THETA = 10000.0
EPS = 1e-6
NEG_INF = -1e30
POS_INF = 1e30
PICKED = -3e38

NSA_HEADS = 8
NSA_KV_GROUPS = 2
NSA_GROUP_HEADS = NSA_HEADS // NSA_KV_GROUPS
NSA_CMP_LEN = 32
NSA_CMP_STRIDE = 16
NSA_SEL_LEN = 64
NSA_SEL_TOPK = 16
NSA_WINDOW = 512

MLA_HEADS = 8
MLA_Q_RANK = 512
MLA_KV_RANK = 256
MLA_NOPE = 128
MLA_ROPE = 64
MLA_V = 128

DIL_PATTERNS = ((128, 1), (512, 4), (2048, 16))
DIL_HEADS = 8
DIL_SPAN = 128

VMEM_LIMIT = 56 * 1024 * 1024

Z0_Q = 0
Z0_KCMP = 1024
Z0_KSLC = 1280
Z0_KWIN = 1536
Z0_VCMP = 1792
Z0_VSLC = 2048
Z0_VWIN = 2304
Z0_CQ = 2560
Z0_CKV = 3072
Z0_GATE = 3328
Z0_KROPE = 3456
Z0_COLS = 3584

TAB_ROPE128, TAB_IDENT, TAB_ROPE64, TAB_ROPE128_Q, TAB_ROPE64_Q, TAB_IDENT_Q = 0, 1, 2, 3, 4, 5

LOG2E = 1.4426950408889634
LN2 = 0.6931471805599453
QSCALE_128 = HEAD_DIM ** -0.5 * LOG2E
QSCALE_MLA = (MLA_NOPE + MLA_ROPE) ** -0.5 * LOG2E


def _cparams(sem):
    return pltpu.CompilerParams(dimension_semantics=sem, vmem_limit_bytes=VMEM_LIMIT)


def _rope_tables(seq):
    def cs(dim):
        inv = 1.0 / (ROPE_THETA ** (jnp.arange(0, dim, 2, dtype=F32) / dim))
        ang = jnp.arange(seq, dtype=F32)[:, None] * inv[None, :]
        return jnp.cos(ang), jnp.sin(ang)

    c128, s128 = cs(HEAD_DIM)
    c64, s64 = cs(MLA_ROPE)
    one = jnp.ones((seq, 32), F32)
    zero = jnp.zeros((seq, 32), F32)
    ident_c = jnp.ones((seq, LANES), F32)
    ident_s = jnp.zeros((seq, LANES), F32)
    r128_c = jnp.concatenate([c128, c128], axis=1)
    r128_s = jnp.concatenate([-s128, s128], axis=1)
    r64_c = jnp.concatenate([c64, one, c64, one], axis=1)
    r64_s = jnp.concatenate([-s64, zero, s64, zero], axis=1)
    cos = [r128_c, ident_c, r64_c, r128_c * QSCALE_128, r64_c * QSCALE_MLA, ident_c * QSCALE_MLA]
    sin = [r128_s, ident_s, r64_s, r128_s * QSCALE_128, r64_s * QSCALE_MLA, ident_s]
    return jnp.stack(cos), jnp.stack(sin)


def _proj_body(*refs, rope, tn):
    nl = tn // LANES
    if rope:
        x_ref, g_ref, w_ref = refs[1:4]
        c_refs, s_refs = refs[4:4 + nl], refs[4 + nl:4 + 2 * nl]
        o_ref, xn_ref = refs[4 + 2 * nl:]
    else:
        x_ref, g_ref, w_ref, o_ref, xn_ref = refs

    @pl.when(pl.program_id(1) == 0)
    def _():
        x = x_ref[...].astype(F32)
        y = x * lax.rsqrt(jnp.mean(x * x, axis=-1, keepdims=True) + EPS)
        xn_ref[...] = (y * g_ref[...]).astype(BF16)

    acc = jnp.dot(xn_ref[...], w_ref[...], preferred_element_type=F32)
    if rope:
        parts = []
        for k in range(nl):
            a = acc[:, k * LANES:(k + 1) * LANES]
            parts.append(a * c_refs[k][...] + pltpu.roll(a, LANES // 2, 1) * s_refs[k][...])
        acc = jnp.concatenate(parts, axis=1)
    o_ref[...] = acc.astype(o_ref.dtype)


def _proj(x, x_col, d, g, w, seq, tabs=None, cos=None, sin=None, tm=1024, tn=512, out_dtype=BF16):
    t = x.shape[0]
    n = w.shape[1]
    tm = min(tm, seq)
    assert t % tm == 0 and seq % tm == 0 and n % tn == 0
    nl = tn // LANES
    rope = tabs is not None
    g2 = g.reshape(1, d).astype(F32)
    body = functools.partial(_proj_body, rope=rope, tn=tn)
    scratch = [pltpu.VMEM((tm, d), BF16)]
    out_shape = jax.ShapeDtypeStruct((t, n), out_dtype)
    if rope:
        spb = seq // tm
        assert len(tabs) == n // LANES
        tab_specs = [pl.BlockSpec((None, tm, LANES), lambda i, j, tab, k=k: (tab[j * nl + k], i % spb, 0))
                     for k in range(nl)]
        grid_spec = pltpu.PrefetchScalarGridSpec(
            num_scalar_prefetch=1, grid=(t // tm, n // tn),
            in_specs=[pl.BlockSpec((tm, d), lambda i, j, tab: (i, x_col)),
                      pl.BlockSpec((1, d), lambda i, j, tab: (0, 0)),
                      pl.BlockSpec((d, tn), lambda i, j, tab: (0, j))] + tab_specs + tab_specs,
            out_specs=pl.BlockSpec((tm, tn), lambda i, j, tab: (i, j)),
            scratch_shapes=scratch)
        return pl.pallas_call(body, grid_spec=grid_spec, out_shape=out_shape,
                              compiler_params=_cparams(("parallel", "arbitrary")))(
            jnp.asarray(tabs, jnp.int32), x, g2, w, *([cos] * nl), *([sin] * nl))
    return pl.pallas_call(
        body, grid=(t // tm, n // tn),
        in_specs=[pl.BlockSpec((tm, d), lambda i, j: (i, x_col)),
                  pl.BlockSpec((1, d), lambda i, j: (0, 0)),
                  pl.BlockSpec((d, tn), lambda i, j: (0, j))],
        out_specs=pl.BlockSpec((tm, tn), lambda i, j: (i, j)),
        scratch_shapes=scratch, out_shape=out_shape,
        compiler_params=_cparams(("parallel", "arbitrary")))(x, g2, w)


def _compress_body(c_ref, w1_ref, w2_ref, pe_ref, o_ref, *, nc):
    half = NSA_CMP_STRIDE * HEAD_DIM
    c = c_ref[...]
    a = jnp.dot(c, w1_ref[:half, :], preferred_element_type=F32)
    b = jnp.dot(c, w1_ref[half:, :], preferred_element_type=F32)
    pe = pe_ref[...]
    pe_hi = pe.astype(BF16)
    pe_lo = (pe - pe_hi.astype(F32)).astype(BF16)
    pe_term = (jnp.dot(pe_hi, w1_ref[...], preferred_element_type=F32)
               + jnp.dot(pe_lo, w1_ref[...], preferred_element_type=F32))
    hid = a + pltpu.roll(b, nc - 1, 0) + pe_term[0:1, :]
    act = jax.nn.gelu(hid)
    o_ref[...] = jnp.dot(act.astype(BF16), w2_ref[...], preferred_element_type=F32).astype(o_ref.dtype)


def _compress(chunks, w1, w2, pe):
    _, b, g, nc, cw = chunks.shape
    return pl.pallas_call(
        functools.partial(_compress_body, nc=nc), grid=(2, b, g),
        in_specs=[pl.BlockSpec((None, None, None, nc, cw), lambda s, i, j: (s, i, j, 0, 0)),
                  pl.BlockSpec((None, 2 * cw, HEAD_DIM), lambda s, i, j: (s, 0, 0)),
                  pl.BlockSpec((None, HEAD_DIM, HEAD_DIM), lambda s, i, j: (s, 0, 0)),
                  pl.BlockSpec((None, 8, 2 * cw), lambda s, i, j: (s, 0, 0))],
        out_specs=pl.BlockSpec((None, None, None, nc, HEAD_DIM), lambda s, i, j: (s, i, j, 0, 0)),
        out_shape=jax.ShapeDtypeStruct((2, b, g, nc, HEAD_DIM), BF16),
        compiler_params=_cparams(("parallel", "parallel", "parallel")))(chunks, w1, w2, pe)


def _nsa_cmp_body(q_ref, k_ref, v_ref, m_ref, o_ref, bias_ref, *, tq, nc, nsp, topk):
    t = pl.program_id(2) * tq + lax.broadcasted_iota(jnp.int32, (tq, 1), 0)
    n = lax.broadcasted_iota(jnp.int32, (1, nc), 1)
    vis = (n * NSA_CMP_STRIDE + (NSA_CMP_LEN - 1)) <= t
    any_vis = (t >= NSA_CMP_LEN - 1).astype(F32)
    k = k_ref[...]
    v = v_ref[...]
    psum = jnp.zeros((tq, nc), F32)
    for h in range(NSA_GROUP_HEADS):
        sl = slice(h * HEAD_DIM, (h + 1) * HEAD_DIM)
        s = lax.dot_general(q_ref[:, sl], k, (((1,), (1,)), ((), ())), preferred_element_type=F32)
        s = jnp.where(vis, s, NEG_INF)
        e = jnp.exp2(s - jnp.max(s, axis=-1, keepdims=True))
        p = e * (any_vis / jnp.sum(e, axis=-1, keepdims=True))
        o_ref[:, sl] = jnp.dot(p.astype(BF16), v, preferred_element_type=F32).astype(o_ref.dtype)
        psum = psum + p
    hi = psum.astype(BF16)
    lo = (psum - hi.astype(F32)).astype(BF16)
    imp = (jnp.dot(hi, m_ref[...], preferred_element_type=F32)
           + jnp.dot(lo, m_ref[...], preferred_element_type=F32))

    blk = lax.broadcasted_iota(jnp.int32, (1, nsp), 1)
    blkf = blk.astype(F32)
    cur = lax.shift_right_logical(t, 6)
    forced = (blk == 0) | (blk == cur) | (blk == cur - 1)
    causal = blk * NSA_SEL_LEN <= t
    score = jnp.where(forced, PICKED, jnp.where(causal, imp, NEG_INF))

    def take_one(_, score):
        mx = jnp.max(score, axis=-1, keepdims=True)
        first = jnp.min(jnp.where(score == mx, blkf, float(nsp)), axis=-1, keepdims=True)
        return jnp.where(blkf == first, PICKED, score)

    score = lax.fori_loop(0, topk - 3, take_one, score)
    sel = (score == PICKED) & causal
    bias_ref[...] = jnp.where(sel, 0.0, NEG_INF).astype(bias_ref.dtype)


def _nsa_cmp(z0, kv_cmp, ovl, b, seq, tq=512):
    nc = kv_cmp.shape[3]
    nsp = ovl.shape[1]
    tq = min(tq, seq)
    nq = seq // tq
    gw = NSA_GROUP_HEADS * HEAD_DIM
    topk = min(NSA_SEL_TOPK, seq // NSA_SEL_LEN)
    assert topk >= 3
    body = functools.partial(_nsa_cmp_body, tq=tq, nc=nc, nsp=nsp, topk=topk)
    return pl.pallas_call(
        body, grid=(b, NSA_KV_GROUPS, nq),
        in_specs=[pl.BlockSpec((tq, gw), lambda bi, g, i: (bi * nq + i, g)),
                  pl.BlockSpec((None, None, None, nc, HEAD_DIM), lambda bi, g, i: (0, bi, g, 0, 0)),
                  pl.BlockSpec((None, None, None, nc, HEAD_DIM), lambda bi, g, i: (1, bi, g, 0, 0)),
                  pl.BlockSpec((nc, nsp), lambda bi, g, i: (0, 0))],
        out_specs=[pl.BlockSpec((tq, gw), lambda bi, g, i: (bi * nq + i, g)),
                   pl.BlockSpec((None, None, tq, nsp), lambda bi, g, i: (bi, g, i, 0))],
        out_shape=[jax.ShapeDtypeStruct((b * seq, NSA_HEADS * HEAD_DIM), BF16),
                   jax.ShapeDtypeStruct((b, NSA_KV_GROUPS, seq, nsp), BF16)],
        compiler_params=_cparams(("parallel", "parallel", "parallel")))(z0, kv_cmp, kv_cmp, ovl)


def _nsa_win_body(q_ref, kp_ref, kc_ref, vp_ref, vc_ref, o_ref, *, tq):
    i = pl.program_id(2)
    t = i * tq + lax.broadcasted_iota(jnp.int32, (tq, 1), 0)
    pos = (i - 1) * tq + lax.broadcasted_iota(jnp.int32, (1, 2 * tq), 1)
    rel = t - pos
    mask = (rel >= 0) & (rel < NSA_WINDOW) & (pos >= 0)
    k = jnp.concatenate([kp_ref[...], kc_ref[...]], axis=0)
    v = jnp.concatenate([vp_ref[...], vc_ref[...]], axis=0)
    for h in range(NSA_GROUP_HEADS):
        sl = slice(h * HEAD_DIM, (h + 1) * HEAD_DIM)
        s = lax.dot_general(q_ref[:, sl], k, (((1,), (1,)), ((), ())), preferred_element_type=F32)
        s = jnp.where(mask, s, NEG_INF)
        e = jnp.exp2(s - jnp.max(s, axis=-1, keepdims=True))
        p = e / jnp.sum(e, axis=-1, keepdims=True)
        o_ref[:, sl] = jnp.dot(p.astype(BF16), v, preferred_element_type=F32).astype(o_ref.dtype)


def _nsa_win(z0, b, seq):
    tq = NSA_WINDOW
    assert seq % tq == 0
    nq = seq // tq
    gw = NSA_GROUP_HEADS * HEAD_DIM
    kcol, vcol = Z0_KWIN // HEAD_DIM, Z0_VWIN // HEAD_DIM
    prev = lambda bi, i: bi * nq + jnp.maximum(i - 1, 0)
    return pl.pallas_call(
        functools.partial(_nsa_win_body, tq=tq), grid=(b, NSA_KV_GROUPS, nq),
        in_specs=[pl.BlockSpec((tq, gw), lambda bi, g, i: (bi * nq + i, g)),
                  pl.BlockSpec((tq, HEAD_DIM), lambda bi, g, i: (prev(bi, i), kcol + g)),
                  pl.BlockSpec((tq, HEAD_DIM), lambda bi, g, i: (bi * nq + i, kcol + g)),
                  pl.BlockSpec((tq, HEAD_DIM), lambda bi, g, i: (prev(bi, i), vcol + g)),
                  pl.BlockSpec((tq, HEAD_DIM), lambda bi, g, i: (bi * nq + i, vcol + g))],
        out_specs=pl.BlockSpec((tq, gw), lambda bi, g, i: (bi * nq + i, g)),
        out_shape=jax.ShapeDtypeStruct((b * seq, NSA_HEADS * HEAD_DIM), BF16),
        compiler_params=_cparams(("parallel", "parallel", "parallel")))(z0, z0, z0, z0, z0)


def _tri_schedule(nq):
    qi = np.concatenate([np.full(i + 1, i) for i in range(nq)]).astype(np.int32)
    kj = np.concatenate([np.arange(i + 1) for i in range(nq)]).astype(np.int32)
    return jnp.asarray(qi), jnp.asarray(kj)


def _flash_init(m_scr, acc_scr):
    m_scr[...] = jnp.full(m_scr.shape, NEG_INF, F32)
    acc_scr[...] = jnp.zeros(acc_scr.shape, F32)


def _lane_tile(x, n):
    return jnp.concatenate([x] * n, axis=1) if n > 1 else x


def _flash_update(scores, vs, m_scr, acc_scr):
    nh = len(scores)
    reps = scores[0].shape[1] // LANES
    ones = jnp.ones((vs[0].shape[0], LANES), BF16)
    m_prev = [m_scr[h] for h in range(nh)]
    m_new = [jnp.maximum(m_prev[h], jnp.max(scores[h], axis=-1, keepdims=True)) for h in range(nh)]
    ps = [jnp.exp2((scores[h] - _lane_tile(m_new[h], reps)).astype(BF16)) for h in range(nh)]
    alphas = [jnp.exp2(m_prev[h] - m_new[h]) for h in range(nh)]
    for h in range(nh):
        v_aug = jnp.concatenate([vs[h], ones], axis=1)
        acc_scr[h] = (_lane_tile(alphas[h], acc_scr.shape[2] // LANES) * acc_scr[h]
                      + jnp.dot(ps[h], v_aug, preferred_element_type=F32))
        m_scr[h] = m_new[h]


def _flash_out(acc_scr, h, d):
    acc = acc_scr[h]
    return acc[:, :d] / acc[:, d:]


def _flash_tile(qi, kj, tq, scores_and_values, m_scr, acc_scr, write_out):
    @pl.when(kj < qi)
    def _():
        scores, vs = scores_and_values()
        _flash_update(scores, vs, m_scr, acc_scr)

    @pl.when(kj == qi)
    def _():
        scores, vs = scores_and_values()
        row = lax.broadcasted_iota(jnp.int32, (tq, 1), 0)
        col = lax.broadcasted_iota(jnp.int32, (1, tq), 1)
        _flash_update([jnp.where(col <= row, s, NEG_INF) for s in scores], vs, m_scr, acc_scr)
        write_out()


def _nsa_sel_body(qi_ref, kj_ref, q_ref, k_ref, v_ref, oh_ref, bias_ref, o_ref,
                  qa_scr, m_scr, acc_scr, *, tq, per_slab):
    step = pl.program_id(2)
    qi = qi_ref[step]
    kj = kj_ref[step]

    @pl.when(kj == 0)
    def _():
        _flash_init(m_scr, acc_scr)
        for h in range(NSA_GROUP_HEADS):
            qa_scr[h, :, :HEAD_DIM] = q_ref[:, h * HEAD_DIM:(h + 1) * HEAD_DIM]

    @pl.when(kj % per_slab == 0)
    def _():
        for h in range(NSA_GROUP_HEADS):
            qa_scr[h, :, HEAD_DIM:] = bias_ref[...]

    def scores_and_values():
        k = jnp.concatenate([k_ref[...], oh_ref[...]], axis=1)
        scores = [lax.dot_general(qa_scr[h], k, (((1,), (1,)), ((), ())), preferred_element_type=F32)
                  for h in range(NSA_GROUP_HEADS)]
        return scores, [v_ref[...]] * NSA_GROUP_HEADS

    def write_out():
        for h in range(NSA_GROUP_HEADS):
            o_ref[:, h * HEAD_DIM:(h + 1) * HEAD_DIM] = _flash_out(acc_scr, h, HEAD_DIM).astype(o_ref.dtype)

    _flash_tile(qi, kj, tq, scores_and_values, m_scr, acc_scr, write_out)


def _nsa_sel(z0, bias, onehot, b, seq, tq=1024):
    tq = min(tq, seq)
    nq = seq // tq
    gw = NSA_GROUP_HEADS * HEAD_DIM
    kcol, vcol = Z0_KSLC // HEAD_DIM, Z0_VSLC // HEAD_DIM
    per_slab = max(LANES * NSA_SEL_LEN // tq, 1)
    qi, kj = _tri_schedule(nq)
    grid_spec = pltpu.PrefetchScalarGridSpec(
        num_scalar_prefetch=2, grid=(b, NSA_KV_GROUPS, int(qi.shape[0])),
        in_specs=[pl.BlockSpec((tq, gw), lambda bi, g, s, qi, kj: (bi * nq + qi[s], g)),
                  pl.BlockSpec((tq, HEAD_DIM), lambda bi, g, s, qi, kj: (bi * nq + kj[s], kcol + g)),
                  pl.BlockSpec((tq, HEAD_DIM), lambda bi, g, s, qi, kj: (bi * nq + kj[s], vcol + g)),
                  pl.BlockSpec((tq, LANES), lambda bi, g, s, qi, kj: (kj[s], 0)),
                  pl.BlockSpec((None, None, tq, LANES),
                               lambda bi, g, s, qi, kj: (bi, g, qi[s], kj[s] // per_slab))],
        out_specs=pl.BlockSpec((tq, gw), lambda bi, g, s, qi, kj: (bi * nq + qi[s], g)),
        scratch_shapes=[pltpu.VMEM((NSA_GROUP_HEADS, tq, 2 * HEAD_DIM), BF16),
                        pltpu.VMEM((NSA_GROUP_HEADS, tq, LANES), F32),
                        pltpu.VMEM((NSA_GROUP_HEADS, tq, HEAD_DIM + LANES), F32)])
    return pl.pallas_call(
        functools.partial(_nsa_sel_body, tq=tq, per_slab=per_slab), grid_spec=grid_spec,
        out_shape=jax.ShapeDtypeStruct((b * seq, NSA_HEADS * HEAD_DIM), BF16),
        compiler_params=_cparams(("parallel", "parallel", "arbitrary")))(qi, kj, z0, z0, z0, onehot, bias)


MLA_STEP_HEADS = 4


def _mla_body(qi_ref, kj_ref, q_ref, kn_ref, kp_ref, v_ref, o_ref, m_scr, acc_scr, *, tq):
    step = pl.program_id(2)
    qi = qi_ref[step]
    kj = kj_ref[step]

    @pl.when(kj == 0)
    def _():
        _flash_init(m_scr, acc_scr)

    def scores_and_values():
        kp = kp_ref[...]
        scores, vs = [], []
        for h in range(MLA_STEP_HEADS):
            k = jnp.concatenate([kn_ref[:, h * MLA_NOPE:(h + 1) * MLA_NOPE], kp], axis=1)
            scores.append(lax.dot_general(q_ref[:, h * 2 * LANES:(h + 1) * 2 * LANES], k,
                                          (((1,), (1,)), ((), ())), preferred_element_type=F32))
            vs.append(v_ref[:, h * MLA_V:(h + 1) * MLA_V])
        return scores, vs

    def write_out():
        for h in range(MLA_STEP_HEADS):
            o_ref[:, h * MLA_V:(h + 1) * MLA_V] = _flash_out(acc_scr, h, MLA_V).astype(o_ref.dtype)

    _flash_tile(qi, kj, tq, scores_and_values, m_scr, acc_scr, write_out)


def _mla(qcat, kv, z0, b, seq, tq=1024):
    tq = min(tq, seq)
    nq = seq // tq
    nh = MLA_STEP_HEADS
    ngrp = MLA_HEADS // nh
    kpcol = Z0_KROPE // LANES
    qi, kj = _tri_schedule(nq)
    grid_spec = pltpu.PrefetchScalarGridSpec(
        num_scalar_prefetch=2, grid=(b, ngrp, int(qi.shape[0])),
        in_specs=[pl.BlockSpec((tq, nh * 2 * LANES), lambda bi, h, s, qi, kj: (bi * nq + qi[s], h)),
                  pl.BlockSpec((tq, nh * MLA_NOPE), lambda bi, h, s, qi, kj: (bi * nq + kj[s], h)),
                  pl.BlockSpec((tq, LANES), lambda bi, h, s, qi, kj: (bi * nq + kj[s], kpcol)),
                  pl.BlockSpec((tq, nh * MLA_V), lambda bi, h, s, qi, kj: (bi * nq + kj[s], ngrp + h))],
        out_specs=pl.BlockSpec((tq, nh * MLA_V), lambda bi, h, s, qi, kj: (bi * nq + qi[s], h)),
        scratch_shapes=[pltpu.VMEM((nh, tq, LANES), F32), pltpu.VMEM((nh, tq, MLA_V + LANES), F32)])
    return pl.pallas_call(
        functools.partial(_mla_body, tq=tq), grid_spec=grid_spec,
        out_shape=jax.ShapeDtypeStruct((b * seq, MLA_HEADS * MLA_V), BF16),
        compiler_params=_cparams(("parallel", "parallel", "arbitrary")))(qi, kj, qcat, kv, z0, kv)


def _dil_body(q_ref, kp_ref, kc_ref, vp_ref, vc_ref, o_ref, lse_ref, *, tq):
    i = pl.program_id(2)
    u = i * tq + lax.broadcasted_iota(jnp.int32, (tq, 1), 0)
    pos = i * tq - DIL_SPAN + lax.broadcasted_iota(jnp.int32, (1, tq + DIL_SPAN), 1)
    rel = u - pos
    mask = (rel >= 0) & (rel <= DIL_SPAN) & (pos >= 0)
    lane = lax.broadcasted_iota(jnp.int32, (tq, LANES), 1)
    lse_all = jnp.zeros((tq, LANES), F32)
    for h in range(DIL_HEADS):
        sl = slice(h * HEAD_DIM, (h + 1) * HEAD_DIM)
        k = jnp.concatenate([kp_ref[:, sl], kc_ref[:, sl]], axis=0)
        v = jnp.concatenate([vp_ref[:, sl], vc_ref[:, sl]], axis=0)
        s = lax.dot_general(q_ref[:, sl], k, (((1,), (1,)), ((), ())), preferred_element_type=F32)
        s = jnp.where(mask, s, NEG_INF)
        m = jnp.max(s, axis=-1, keepdims=True)
        e = jnp.exp2(s - m)
        den = jnp.sum(e, axis=-1, keepdims=True)
        o_ref[:, sl] = jnp.dot((e / den).astype(BF16), v, preferred_element_type=F32).astype(o_ref.dtype)
        lse_all = jnp.where(lane == h, m * LN2 + jnp.log(den), lse_all)
    lse_ref[...] = lse_all


def _deinterleave_body(*refs, r, rows):
    x_refs, o_ref = refs[:-1], refs[-1]
    hw = len(x_refs) * LANES
    for h, x_ref in enumerate(x_refs):
        for c in range(r):
            o_ref[:, c * hw + h * LANES:c * hw + (h + 1) * LANES] = (
                x_ref[pl.ds(c, rows, stride=r), :].astype(o_ref.dtype))


def _deinterleave(z, col0, r, b, seq, rows=128):
    hw = DIL_HEADS * HEAD_DIM
    sub = seq // r
    rows = min(rows, sub)
    nblk = sub // rows
    return pl.pallas_call(
        functools.partial(_deinterleave_body, r=r, rows=rows), grid=(b, nblk, 3),
        in_specs=[pl.BlockSpec((r * rows, LANES),
                               lambda bi, i, j, h=h: (bi * nblk + i, (col0 + j) * DIL_HEADS + h))
                  for h in range(DIL_HEADS)],
        out_specs=pl.BlockSpec((None, rows, r * hw), lambda bi, i, j: (bi, i, j)),
        out_shape=jax.ShapeDtypeStruct((b, sub, 3 * r * hw), BF16),
        compiler_params=_cparams(("parallel", "parallel", "parallel")))(*([z] * DIL_HEADS))


def _dilated(zv, r, b, seq, tq=512):
    sub = seq // r
    tq = min(tq, sub)
    assert sub % tq == 0 and tq % DIL_SPAN == 0
    nq = sub // tq
    hw = DIL_HEADS * HEAD_DIM
    ratio = tq // DIL_SPAN
    prev = lambda i: jnp.maximum(i * ratio - 1, 0)
    o, lse = pl.pallas_call(
        functools.partial(_dil_body, tq=tq), grid=(b, r, nq),
        in_specs=[pl.BlockSpec((None, tq, hw), lambda bi, c, i: (bi, i, c)),
                  pl.BlockSpec((None, DIL_SPAN, hw), lambda bi, c, i: (bi, prev(i), r + c)),
                  pl.BlockSpec((None, tq, hw), lambda bi, c, i: (bi, i, r + c)),
                  pl.BlockSpec((None, DIL_SPAN, hw), lambda bi, c, i: (bi, prev(i), 2 * r + c)),
                  pl.BlockSpec((None, tq, hw), lambda bi, c, i: (bi, i, 2 * r + c))],
        out_specs=[pl.BlockSpec((None, tq, hw), lambda bi, c, i: (bi, i, c)),
                   pl.BlockSpec((None, tq, LANES), lambda bi, c, i: (bi, i, c))],
        out_shape=[jax.ShapeDtypeStruct((b, sub, r * hw), BF16),
                   jax.ShapeDtypeStruct((b, sub, r * LANES), F32)],
        compiler_params=_cparams(("parallel", "parallel", "parallel")))(zv, zv, zv, zv, zv)
    return o.reshape(b * seq, hw), lse.reshape(b * seq, LANES)


def _finish(h_ref, m, g_ref, o_ref):
    y = m * lax.rsqrt(jnp.mean(m * m, axis=-1, keepdims=True) + EPS)
    o_ref[...] = h_ref[...] + y * g_ref[...]


def _out0_body(h_ref, oc_ref, os_ref, ow_ref, gate_ref, ob_ref, wa_ref, wb_ref, g_ref, o_ref):
    gate = jax.nn.sigmoid(gate_ref[...].astype(F32))
    parts = []
    for h in range(NSA_HEADS):
        sl = slice(h * HEAD_DIM, (h + 1) * HEAD_DIM)
        parts.append(gate[:, 3 * h:3 * h + 1] * oc_ref[:, sl].astype(F32)
                     + gate[:, 3 * h + 1:3 * h + 2] * os_ref[:, sl].astype(F32)
                     + gate[:, 3 * h + 2:3 * h + 3] * ow_ref[:, sl].astype(F32))
    oa = jnp.concatenate(parts, axis=1).astype(BF16)
    m = (jnp.dot(oa, wa_ref[...], preferred_element_type=F32)
         + jnp.dot(ob_ref[...], wb_ref[...], preferred_element_type=F32))
    _finish(h_ref, m, g_ref, o_ref)


def _out1_body(h_ref, o0_ref, o1_ref, o2_ref, l0_ref, l1_ref, l2_ref, w_ref, g_ref, o_ref):
    l0, l1, l2 = l0_ref[...], l1_ref[...], l2_ref[...]
    mx = jnp.maximum(jnp.maximum(l0, l1), l2)
    e0, e1, e2 = jnp.exp(l0 - mx), jnp.exp(l1 - mx), jnp.exp(l2 - mx)
    tot = e0 + e1 + e2
    a0, a1, a2 = e0 / tot, e1 / tot, e2 / tot
    parts = []
    for h in range(DIL_HEADS):
        sl = slice(h * HEAD_DIM, (h + 1) * HEAD_DIM)
        parts.append(a0[:, h:h + 1] * o0_ref[:, sl].astype(F32)
                     + a1[:, h:h + 1] * o1_ref[:, sl].astype(F32)
                     + a2[:, h:h + 1] * o2_ref[:, sl].astype(F32))
    o = jnp.concatenate(parts, axis=1).astype(BF16)
    _finish(h_ref, jnp.dot(o, w_ref[...], preferred_element_type=F32), g_ref, o_ref)


def _row_spec(tm, w, col=0):
    return pl.BlockSpec((tm, w), lambda i: (i, col))


def _full_spec(shape):
    return pl.BlockSpec(shape, lambda i: (0,) * len(shape))


def _out0(h, o_c, o_s, o_w, z0, o_b, wa, wb, g, tm=256):
    t, d = h.shape
    tm = min(tm, t)
    ow = NSA_HEADS * HEAD_DIM
    return pl.pallas_call(
        _out0_body, grid=(t // tm,),
        in_specs=[_row_spec(tm, d), _row_spec(tm, ow), _row_spec(tm, ow), _row_spec(tm, ow),
                  _row_spec(tm, LANES, Z0_GATE // LANES), _row_spec(tm, ow),
                  _full_spec(wa.shape), _full_spec(wb.shape), _full_spec((1, d))],
        out_specs=_row_spec(tm, d), out_shape=jax.ShapeDtypeStruct((t, d), F32),
        compiler_params=_cparams(("parallel",)))(h, o_c, o_s, o_w, z0, o_b, wa, wb, g.reshape(1, d))


def _out1(h, os_, lses, w, g, tm=256):
    t, d = h.shape
    tm = min(tm, t)
    ow = DIL_HEADS * HEAD_DIM
    return pl.pallas_call(
        _out1_body, grid=(t // tm,),
        in_specs=[_row_spec(tm, d)] + [_row_spec(tm, ow)] * 3 + [_row_spec(tm, LANES)] * 3
                 + [_full_spec(w.shape), _full_spec((1, d))],
        out_specs=_row_spec(tm, d), out_shape=jax.ShapeDtypeStruct((t, d), F32),
        compiler_params=_cparams(("parallel",)))(h, *os_, *lses, w, g.reshape(1, d))


def _mlp_body(h_ref, g1_ref, w1_ref, w2_ref, g2_ref, o_ref, xn_ref, acc_ref):
    f = pl.program_id(1)

    @pl.when(f == 0)
    def _():
        x = h_ref[...]
        y = x * lax.rsqrt(jnp.mean(x * x, axis=-1, keepdims=True) + EPS)
        xn_ref[...] = (y * g1_ref[...]).astype(BF16)
        acc_ref[...] = jnp.zeros(acc_ref.shape, F32)

    a = jnp.maximum(jnp.dot(xn_ref[...], w1_ref[...], preferred_element_type=F32), 0.0)
    acc_ref[...] += jnp.dot((a * a).astype(BF16), w2_ref[...], preferred_element_type=F32)

    @pl.when(f == pl.num_programs(1) - 1)
    def _():
        _finish(h_ref, acc_ref[...], g2_ref, o_ref)


def _mlp(h, g1, w1, w2, g2, tm=512, tf=1024):
    t, d = h.shape
    ff = w1.shape[1]
    tm = min(tm, t)
    return pl.pallas_call(
        _mlp_body, grid=(t // tm, ff // tf),
        in_specs=[pl.BlockSpec((tm, d), lambda i, f: (i, 0)),
                  pl.BlockSpec((1, d), lambda i, f: (0, 0)),
                  pl.BlockSpec((d, tf), lambda i, f: (0, f)),
                  pl.BlockSpec((tf, d), lambda i, f: (f, 0)),
                  pl.BlockSpec((1, d), lambda i, f: (0, 0))],
        out_specs=pl.BlockSpec((tm, d), lambda i, f: (i, 0)),
        out_shape=jax.ShapeDtypeStruct((t, d), F32),
        scratch_shapes=[pltpu.VMEM((tm, d), BF16), pltpu.VMEM((tm, d), F32)],
        compiler_params=_cparams(("parallel", "arbitrary")))(h, g1.reshape(1, d), w1, w2, g2.reshape(1, d))


def _rope64_tile(w):
    z = jnp.zeros((w.shape[0], 32), w.dtype)
    return jnp.concatenate([w[:, :32], z, w[:, 32:], z], axis=1)


def _layer0_w_in(w_in):
    d = w_in.shape[0]
    o1 = NSA_HEADS * HEAD_DIM
    o2 = o1 + 3 * 2 * NSA_KV_GROUPS * HEAD_DIM
    o3 = o2 + 3 * NSA_HEADS
    o4 = o3 + MLA_Q_RANK
    o5 = o4 + MLA_KV_RANK
    kv = w_in[:, o1:o2].reshape(d, 3, 2, NSA_KV_GROUPS * HEAD_DIM)
    kv = kv.transpose(0, 2, 1, 3).reshape(d, o2 - o1)
    gate = jnp.pad(w_in[:, o2:o3], ((0, 0), (0, LANES - (o3 - o2))))
    w = jnp.concatenate([w_in[:, :o1], kv, w_in[:, o3:o4], w_in[:, o4:o5], gate, _rope64_tile(w_in[:, o5:])], 1)
    assert w.shape[1] == Z0_COLS
    return w.astype(BF16)


def _mla_w_uq(w_uq):
    d = w_uq.shape[0]
    w = w_uq.reshape(d, MLA_HEADS, MLA_NOPE + MLA_ROPE)
    tiles = [jnp.concatenate([w[:, h, :MLA_NOPE], _rope64_tile(w[:, h, MLA_NOPE:])], 1) for h in range(MLA_HEADS)]
    return jnp.concatenate(tiles, axis=1).astype(BF16)


def _mla_w_ukv(w_ukv):
    d = w_ukv.shape[0]
    w = w_ukv.reshape(d, MLA_HEADS, MLA_NOPE + MLA_V)
    return jnp.concatenate([w[:, :, :MLA_NOPE].reshape(d, -1), w[:, :, MLA_NOPE:].reshape(d, -1)], 1).astype(BF16)


def _layer1_w_in(w_in):
    return w_in.astype(BF16)


def _overlap_matrix(nc, n_cmp, n_slc, nsp):
    ratio = NSA_SEL_LEN // NSA_CMP_STRIDE
    m = np.zeros((nc, nsp), np.float32)
    for off in range(1 - NSA_CMP_LEN // NSA_CMP_STRIDE, ratio):
        n = np.arange(n_slc) * ratio + off
        ok = (n >= 0) & (n < n_cmp)
        m[n[ok], np.arange(n_slc)[ok]] = 1.0
    return jnp.asarray(m, BF16)


def kernel(x, l0_norm_mix_pre, l0_w_in, l0_cmp_pe_k, l0_cmp_w1_k, l0_cmp_w2_k, l0_cmp_pe_v, l0_cmp_w1_v, l0_cmp_w2_v, l0_mla_q_norm, l0_mla_w_uq, l0_mla_kv_norm, l0_mla_w_ukv, l0_w_out, l0_norm_mix_post, l0_norm_ffn_pre, l0_w_ff1, l0_w_ff2, l0_norm_ffn_post, l1_norm_mix_pre, l1_w_in, l1_w_out, l1_norm_mix_post, l1_norm_ffn_pre, l1_w_ff1, l1_w_ff2, l1_norm_ffn_post):
    b, seq, d = x.shape
    t = b * seq
    assert seq % NSA_WINDOW == 0 and seq % (DIL_PATTERNS[-1][1] * DIL_SPAN) == 0
    cos, sin = _rope_tables(seq)
    h = x.reshape(t, d)

    tabs0 = ([TAB_ROPE128_Q] * NSA_HEADS + [TAB_ROPE128] * (3 * NSA_KV_GROUPS)
             + [TAB_IDENT] * ((Z0_KROPE - Z0_VCMP) // LANES) + [TAB_ROPE64])
    z0 = _proj(h, 0, d, l0_norm_mix_pre, _layer0_w_in(l0_w_in), seq, tabs0, cos, sin)

    nc = seq // NSA_CMP_STRIDE
    n_cmp = (seq - NSA_CMP_LEN) // NSA_CMP_STRIDE + 1
    n_slc = seq // NSA_SEL_LEN
    nsp = -(-n_slc // LANES) * LANES
    gd = NSA_KV_GROUPS * HEAD_DIM

    def chunks(col):
        c = z0[:, col:col + gd].reshape(b, nc, NSA_CMP_STRIDE, NSA_KV_GROUPS, HEAD_DIM)
        return c.transpose(0, 3, 1, 2, 4).reshape(b, NSA_KV_GROUPS, nc, NSA_CMP_STRIDE * HEAD_DIM)

    pe = jnp.stack([l0_cmp_pe_k.reshape(1, -1), l0_cmp_pe_v.reshape(1, -1)])
    kv_cmp = _compress(jnp.stack([chunks(Z0_KCMP), chunks(Z0_VCMP)]),
                       jnp.stack([l0_cmp_w1_k, l0_cmp_w1_v]).astype(BF16),
                       jnp.stack([l0_cmp_w2_k, l0_cmp_w2_v]).astype(BF16),
                       jnp.broadcast_to(pe, (2, 8, pe.shape[-1])).astype(F32))
    o_c, bias = _nsa_cmp(z0, kv_cmp, _overlap_matrix(nc, n_cmp, n_slc, nsp), b, seq)
    o_w = _nsa_win(z0, b, seq)
    blk_lane = (jnp.arange(seq, dtype=jnp.int32) // NSA_SEL_LEN) % LANES
    onehot = (blk_lane[:, None] == jnp.arange(LANES, dtype=jnp.int32)[None, :]).astype(BF16)
    o_s = _nsa_sel(z0, bias, onehot, b, seq)

    qcat = _proj(z0, Z0_CQ // MLA_Q_RANK, MLA_Q_RANK, l0_mla_q_norm, _mla_w_uq(l0_mla_w_uq), seq,
                 [TAB_IDENT_Q, TAB_ROPE64_Q] * MLA_HEADS, cos, sin)
    kv = _proj(z0, Z0_CKV // MLA_KV_RANK, MLA_KV_RANK, l0_mla_kv_norm, _mla_w_ukv(l0_mla_w_ukv), seq)
    o_b = _mla(qcat, kv, z0, b, seq)

    ow = NSA_HEADS * HEAD_DIM
    w_out0 = l0_w_out.astype(BF16)
    h = _out0(h, o_c, o_s, o_w, z0, o_b, w_out0[:ow], w_out0[ow:], l0_norm_mix_post)
    h = _mlp(h, l0_norm_ffn_pre, l0_w_ff1.astype(BF16), l0_w_ff2.astype(BF16), l0_norm_ffn_post)

    hw = DIL_HEADS * HEAD_DIM
    tabs_g = ([TAB_ROPE128_Q] * DIL_HEADS + [TAB_ROPE128] * DIL_HEADS
              + [TAB_IDENT] * DIL_HEADS)
    w1_in = _layer1_w_in(l1_w_in)
    z1a = _proj(h, 0, d, l1_norm_mix_pre, w1_in[:, :3 * hw], seq, tabs_g, cos, sin)
    z1b = _proj(h, 0, d, l1_norm_mix_pre, w1_in[:, 3 * hw:], seq, tabs_g * 2, cos, sin, out_dtype=F32)
    outs = [_dilated(z1a.reshape(b, seq, 3 * hw), 1, b, seq)]
    for p in range(1, len(DIL_PATTERNS)):
        r = DIL_PATTERNS[p][1]
        outs.append(_dilated(_deinterleave(z1b, 3 * (p - 1), r, b, seq), r, b, seq))
    h = _out1(h, [o for o, _ in outs], [l for _, l in outs], l1_w_out.astype(BF16), l1_norm_mix_post)
    h = _mlp(h, l1_norm_ffn_pre, l1_w_ff1.astype(BF16), l1_w_ff2.astype(BF16), l1_norm_ffn_post)
    return h.reshape(b, seq, d)
```

```python
import functools

import numpy as np
import jax
import jax.numpy as jnp
from jax import lax
from jax.experimental import pallas as pl
from jax.experimental.pallas import tpu as pltpu

F32 = jnp.float32
BF16 = jnp.bfloat16

HEAD_DIM = 128
LANES = 128
ROPE_THETA = 10000.0
EPS = 1e-6
NEG_INF = -1e30
POS_INF = 1e30
PICKED = -3e38

NSA_HEADS = 8
NSA_KV_GROUPS = 2
NSA_GROUP_HEADS = NSA_HEADS // NSA_KV_GROUPS
NSA_CMP_LEN = 32
NSA_CMP_STRIDE = 16
NSA_SEL_LEN = 64
NSA_SEL_TOPK = 16
NSA_WINDOW = 512

MLA_HEADS = 8
MLA_Q_RANK = 512
MLA_KV_RANK = 256
MLA_NOPE = 128
MLA_ROPE = 64
MLA_V = 128

DIL_PATTERNS = ((128, 1), (512, 4), (2048, 16))
DIL_HEADS = 8
DIL_SPAN = 128

VMEM_LIMIT = 56 * 1024 * 1024

Z0_Q = 0
Z0_KCMP = 1024
Z0_KSLC = 1280
Z0_KWIN = 1536
Z0_VCMP = 1792
Z0_VSLC = 2048
Z0_VWIN = 2304
Z0_CQ = 2560
Z0_CKV = 3072
Z0_GATE = 3328
Z0_KROPE = 3456
Z0_COLS = 3584

TAB_ROPE128, TAB_IDENT, TAB_ROPE64, TAB_ROPE128_Q, TAB_ROPE64_Q, TAB_IDENT_Q = 0, 1, 2, 3, 4, 5

LOG2E = 1.4426950408889634
LN2 = 0.6931471805599453
QSCALE_128 = HEAD_DIM ** -0.5 * LOG2E
QSCALE_MLA = (MLA_NOPE + MLA_ROPE) ** -0.5 * LOG2E


def _cparams(sem):
    return pltpu.CompilerParams(dimension_semantics=sem, vmem_limit_bytes=VMEM_LIMIT)


def _rope_tables(seq):
    def cs(dim):
        inv = 1.0 / (ROPE_THETA ** (jnp.arange(0, dim, 2, dtype=F32) / dim))
        ang = jnp.arange(seq, dtype=F32)[:, None] * inv[None, :]
        return jnp.cos(ang), jnp.sin(ang)

    c128, s128 = cs(HEAD_DIM)
    c64, s64 = cs(MLA_ROPE)
    one = jnp.ones((seq, 32), F32)
    zero = jnp.zeros((seq, 32), F32)
    ident_c = jnp.ones((seq, LANES), F32)
    ident_s = jnp.zeros((seq, LANES), F32)
    r128_c = jnp.concatenate([c128, c128], axis=1)
    r128_s = jnp.concatenate([-s128, s128], axis=1)
    r64_c = jnp.concatenate([c64, one, c64, one], axis=1)
    r64_s = jnp.concatenate([-s64, zero, s64, zero], axis=1)
    cos = [r128_c, ident_c, r64_c, r128_c * QSCALE_128, r64_c * QSCALE_MLA, ident_c * QSCALE_MLA]
    sin = [r128_s, ident_s, r64_s, r128_s * QSCALE_128, r64_s * QSCALE_MLA, ident_s]
    return jnp.concatenate([jnp.stack(cos), jnp.stack(sin)], axis=2)


def _proj_res_body(*refs, kinds, tn, scale_q):
    x_ref, g_ref, w_ref = refs[:3]
    used = sorted({k for k in kinds if k not in (TAB_IDENT, TAB_IDENT_Q)})
    t_refs = dict(zip(used, refs[3:3 + len(used)]))
    o_ref, xn_ref = refs[3 + len(used):]
    x = x_ref[...].astype(F32)
    y = x * lax.rsqrt(jnp.mean(x * x, axis=-1, keepdims=True) + EPS)
    xn_ref[...] = (y * g_ref[...]).astype(BF16)
    nl = tn // LANES
    for j in range(w_ref.shape[1] // tn):
        acc = jnp.dot(xn_ref[...], w_ref[:, j * tn:(j + 1) * tn], preferred_element_type=F32)
        parts = []
        for k in range(nl):
            kind = kinds[j * nl + k]
            a = acc[:, k * LANES:(k + 1) * LANES]
            if kind == TAB_IDENT:
                parts.append(a)
            elif kind == TAB_IDENT_Q:
                parts.append(a * scale_q)
            else:
                tab = t_refs[kind]
                parts.append(a * tab[:, :LANES] + pltpu.roll(a, LANES // 2, 1) * tab[:, LANES:])
        o_ref[:, j * tn:(j + 1) * tn] = jnp.concatenate(parts, axis=1).astype(o_ref.dtype)


def _proj_res(x, x_col, d, g, w, seq, kinds, rot, tm=512, tn=512, out_dtype=BF16):
    t = x.shape[0]
    n = w.shape[1]
    tm = min(tm, seq)
    assert t % tm == 0 and seq % tm == 0 and n % tn == 0 and len(kinds) == n // LANES
    spb = seq // tm
    used = sorted({k for k in kinds if k not in (TAB_IDENT, TAB_IDENT_Q)})
    tab_specs = [pl.BlockSpec((None, tm, 2 * LANES), lambda i, kind=kind: (kind, i % spb, 0)) for kind in used]
    return pl.pallas_call(
        functools.partial(_proj_res_body, kinds=tuple(kinds), tn=tn, scale_q=QSCALE_MLA), grid=(t // tm,),
        in_specs=[pl.BlockSpec((tm, d), lambda i: (i, x_col)),
                  pl.BlockSpec((1, d), lambda i: (0, 0)),
                  pl.BlockSpec((d, n), lambda i: (0, 0), pipeline_mode=pl.Buffered(1))] + tab_specs,
        out_specs=pl.BlockSpec((tm, n), lambda i: (i, 0)),
        scratch_shapes=[pltpu.VMEM((tm, d), BF16)],
        out_shape=jax.ShapeDtypeStruct((t, n), out_dtype),
        compiler_params=_cparams(("parallel",)))(x, g.reshape(1, d).astype(F32), w, *([rot] * len(used)))


def _compress_body(c_ref, w1_ref, w2_ref, pe_ref, o_ref, *, nc):
    half = NSA_CMP_STRIDE * HEAD_DIM
    c = c_ref[...]
    a = jnp.dot(c, w1_ref[:half, :], preferred_element_type=F32)
    b = jnp.dot(c, w1_ref[half:, :], preferred_element_type=F32)
    pe = pe_ref[...]
    pe_hi = pe.astype(BF16)
    pe_lo = (pe - pe_hi.astype(F32)).astype(BF16)
    pe_term = (jnp.dot(pe_hi, w1_ref[...], preferred_element_type=F32)
               + jnp.dot(pe_lo, w1_ref[...], preferred_element_type=F32))
    hid = a + pltpu.roll(b, nc - 1, 0) + pe_term[0:1, :]
    act = jax.nn.gelu(hid)
    o_ref[...] = jnp.dot(act.astype(BF16), w2_ref[...], preferred_element_type=F32).astype(o_ref.dtype)


def _compress(chunks, w1, w2, pe):
    _, b, g, nc, cw = chunks.shape
    return pl.pallas_call(
        functools.partial(_compress_body, nc=nc), grid=(2, b, g),
        in_specs=[pl.BlockSpec((None, None, None, nc, cw), lambda s, i, j: (s, i, j, 0, 0)),
                  pl.BlockSpec((None, 2 * cw, HEAD_DIM), lambda s, i, j: (s, 0, 0)),
                  pl.BlockSpec((None, HEAD_DIM, HEAD_DIM), lambda s, i, j: (s, 0, 0)),
                  pl.BlockSpec((None, 8, 2 * cw), lambda s, i, j: (s, 0, 0))],
        out_specs=pl.BlockSpec((None, None, None, nc, HEAD_DIM), lambda s, i, j: (s, i, j, 0, 0)),
        out_shape=jax.ShapeDtypeStruct((2, b, g, nc, HEAD_DIM), BF16),
        compiler_params=_cparams(("parallel", "parallel", "parallel")))(chunks, w1, w2, pe)


def _nsa_cmp_body(q_ref, k_ref, v_ref, m_ref, o_ref, bias_ref, *, tq, nc, nsp, topk):
    t = pl.program_id(2) * tq + lax.broadcasted_iota(jnp.int32, (tq, 1), 0)
    n = lax.broadcasted_iota(jnp.int32, (1, nc), 1)
    vis = (n * NSA_CMP_STRIDE + (NSA_CMP_LEN - 1)) <= t
    any_vis = (t >= NSA_CMP_LEN - 1).astype(F32)
    k = k_ref[...]
    v = v_ref[...]
    psum = jnp.zeros((tq, nc), F32)
    for h in range(NSA_GROUP_HEADS):
        sl = slice(h * HEAD_DIM, (h + 1) * HEAD_DIM)
        s = lax.dot_general(q_ref[:, sl], k, (((1,), (1,)), ((), ())), preferred_element_type=F32)
        s = jnp.where(vis, s, NEG_INF)
        e = jnp.exp2(s - jnp.max(s, axis=-1, keepdims=True))
        p = e * (any_vis / jnp.sum(e, axis=-1, keepdims=True))
        o_ref[:, sl] = jnp.dot(p.astype(BF16), v, preferred_element_type=F32).astype(o_ref.dtype)
        psum = psum + p
    hi = psum.astype(BF16)
    lo = (psum - hi.astype(F32)).astype(BF16)
    imp = (jnp.dot(hi, m_ref[...], preferred_element_type=F32)
           + jnp.dot(lo, m_ref[...], preferred_element_type=F32))

    blk = lax.broadcasted_iota(jnp.int32, (1, nsp), 1)
    blkf = blk.astype(F32)
    cur = lax.shift_right_logical(t, 6)
    forced = (blk == 0) | (blk == cur) | (blk == cur - 1)
    causal = blk * NSA_SEL_LEN <= t
    score = jnp.where(forced, PICKED, jnp.where(causal, imp, NEG_INF))

    def take_one(_, score):
        mx = jnp.max(score, axis=-1, keepdims=True)
        first = jnp.min(jnp.where(score == mx, blkf, float(nsp)), axis=-1, keepdims=True)
        return jnp.where(blkf == first, PICKED, score)

    score = lax.fori_loop(0, topk - 3, take_one, score)
    sel = (score == PICKED) & causal
    bias_ref[...] = jnp.where(sel, 0.0, NEG_INF).astype(bias_ref.dtype)


def _nsa_cmp(z0, kv_cmp, ovl, b, seq, tq=512):
    nc = kv_cmp.shape[3]
    nsp = ovl.shape[1]
    tq = min(tq, seq)
    nq = seq // tq
    gw = NSA_GROUP_HEADS * HEAD_DIM
    topk = min(NSA_SEL_TOPK, seq // NSA_SEL_LEN)
    assert topk >= 3
    body = functools.partial(_nsa_cmp_body, tq=tq, nc=nc, nsp=nsp, topk=topk)
    return pl.pallas_call(
        body, grid=(b, NSA_KV_GROUPS, nq),
        in_specs=[pl.BlockSpec((tq, gw), lambda bi, g, i: (bi * nq + i, g)),
                  pl.BlockSpec((None, None, None, nc, HEAD_DIM), lambda bi, g, i: (0, bi, g, 0, 0)),
                  pl.BlockSpec((None, None, None, nc, HEAD_DIM), lambda bi, g, i: (1, bi, g, 0, 0)),
                  pl.BlockSpec((nc, nsp), lambda bi, g, i: (0, 0))],
        out_specs=[pl.BlockSpec((tq, gw), lambda bi, g, i: (bi * nq + i, g)),
                   pl.BlockSpec((None, None, tq, nsp), lambda bi, g, i: (bi, g, i, 0))],
        out_shape=[jax.ShapeDtypeStruct((b * seq, NSA_HEADS * HEAD_DIM), BF16),
                   jax.ShapeDtypeStruct((b, NSA_KV_GROUPS, seq, nsp), BF16)],
        compiler_params=_cparams(("parallel", "parallel", "parallel")))(z0, kv_cmp, kv_cmp, ovl)


def _lane_tile(x, n):
    return jnp.concatenate([x] * n, axis=1) if n > 1 else x


def _softmax_pv(s, v):
    tq, tk = s.shape
    d = v.shape[1]
    m = jnp.broadcast_to(jnp.max(s, axis=-1, keepdims=True), (tq, LANES))
    p = jnp.exp2((s - _lane_tile(m, tk // LANES)).astype(BF16))
    acc = jnp.dot(p, jnp.concatenate([v, jnp.ones((tk, LANES), BF16)], axis=1), preferred_element_type=F32)
    return acc[:, :d], acc[:, d:], m


def _nsa_win_body(q_ref, kp_ref, kc_ref, vp_ref, vc_ref, o_ref, *, tq):
    i = pl.program_id(2)
    t = i * tq + lax.broadcasted_iota(jnp.int32, (tq, 1), 0)
    pos = (i - 1) * tq + lax.broadcasted_iota(jnp.int32, (1, 2 * tq), 1)
    rel = t - pos
    mask = (rel >= 0) & (rel < NSA_WINDOW) & (pos >= 0)
    k = jnp.concatenate([kp_ref[...], kc_ref[...]], axis=0)
    v = jnp.concatenate([vp_ref[...], vc_ref[...]], axis=0)
    scores = [jnp.where(mask, lax.dot_general(q_ref[:, h * HEAD_DIM:(h + 1) * HEAD_DIM], k,
                                              (((1,), (1,)), ((), ())), preferred_element_type=F32), NEG_INF)
              for h in range(NSA_GROUP_HEADS)]
    for h in range(NSA_GROUP_HEADS):
        num, den, _ = _softmax_pv(scores[h], v)
        o_ref[:, h * HEAD_DIM:(h + 1) * HEAD_DIM] = (num / den).astype(o_ref.dtype)


def _nsa_win(z0, b, seq):
    tq = NSA_WINDOW
    assert seq % tq == 0
    nq = seq // tq
    gw = NSA_GROUP_HEADS * HEAD_DIM
    kcol, vcol = Z0_KWIN // HEAD_DIM, Z0_VWIN // HEAD_DIM
    prev = lambda bi, i: bi * nq + jnp.maximum(i - 1, 0)
    return pl.pallas_call(
        functools.partial(_nsa_win_body, tq=tq), grid=(b, NSA_KV_GROUPS, nq),
        in_specs=[pl.BlockSpec((tq, gw), lambda bi, g, i: (bi * nq + i, g)),
                  pl.BlockSpec((tq, HEAD_DIM), lambda bi, g, i: (prev(bi, i), kcol + g)),
                  pl.BlockSpec((tq, HEAD_DIM), lambda bi, g, i: (bi * nq + i, kcol + g)),
                  pl.BlockSpec((tq, HEAD_DIM), lambda bi, g, i: (prev(bi, i), vcol + g)),
                  pl.BlockSpec((tq, HEAD_DIM), lambda bi, g, i: (bi * nq + i, vcol + g))],
        out_specs=pl.BlockSpec((tq, gw), lambda bi, g, i: (bi * nq + i, g)),
        out_shape=jax.ShapeDtypeStruct((b * seq, NSA_HEADS * HEAD_DIM), BF16),
        compiler_params=_cparams(("parallel", "parallel", "parallel")))(z0, z0, z0, z0, z0)


def _tri_schedule(nq):
    qi = np.concatenate([np.full(i + 1, i) for i in range(nq)]).astype(np.int32)
    kj = np.concatenate([np.arange(i + 1) for i in range(nq)]).astype(np.int32)
    return jnp.asarray(qi), jnp.asarray(kj)


def _flash_init(m_scr, acc_scr):
    m_scr[...] = jnp.full(m_scr.shape, NEG_INF, F32)
    acc_scr[...] = jnp.zeros(acc_scr.shape, F32)


def _flash_update(scores, vs, m_scr, acc_scr):
    nh = len(scores)
    reps = scores[0].shape[1] // LANES
    ones = jnp.ones((vs[0].shape[0], LANES), BF16)
    m_prev = [m_scr[h] for h in range(nh)]
    m_new = [jnp.maximum(m_prev[h], jnp.max(scores[h], axis=-1, keepdims=True)) for h in range(nh)]
    ps = [jnp.exp2((scores[h] - _lane_tile(m_new[h], reps)).astype(BF16)) for h in range(nh)]
    alphas = [jnp.exp2(m_prev[h] - m_new[h]) for h in range(nh)]
    for h in range(nh):
        v_aug = jnp.concatenate([vs[h], ones], axis=1)
        acc_scr[h] = (_lane_tile(alphas[h], acc_scr.shape[2] // LANES) * acc_scr[h]
                      + jnp.dot(ps[h], v_aug, preferred_element_type=F32))
        m_scr[h] = m_new[h]


def _flash_out(acc_scr, h, d):
    acc = acc_scr[h]
    return acc[:, :d] / acc[:, d:]


def _flash_tile(qi, kj, tq, scores_and_values, m_scr, acc_scr, write_out):
    @pl.when(kj < qi)
    def _():
        scores, vs = scores_and_values()
        _flash_update(scores, vs, m_scr, acc_scr)

    @pl.when(kj == qi)
    def _():
        scores, vs = scores_and_values()
        row = lax.broadcasted_iota(jnp.int32, (tq, 1), 0)
        col = lax.broadcasted_iota(jnp.int32, (1, tq), 1)
        _flash_update([jnp.where(col <= row, s, NEG_INF) for s in scores], vs, m_scr, acc_scr)
        write_out()


def _nsa_sel_body(qi_ref, kj_ref, q_ref, k_ref, v_ref, oh_ref, bias_ref, o_ref,
                  qa_scr, m_scr, acc_scr, *, tq, per_slab):
    step = pl.program_id(2)
    qi = qi_ref[step]
    kj = kj_ref[step]

    @pl.when(kj == 0)
    def _():
        _flash_init(m_scr, acc_scr)
        for h in range(NSA_GROUP_HEADS):
            qa_scr[h, :, :HEAD_DIM] = q_ref[:, h * HEAD_DIM:(h + 1) * HEAD_DIM]

    @pl.when(kj % per_slab == 0)
    def _():
        for h in range(NSA_GROUP_HEADS):
            qa_scr[h, :, HEAD_DIM:] = bias_ref[...]

    def scores_and_values():
        k = jnp.concatenate([k_ref[...], oh_ref[...]], axis=1)
        scores = [lax.dot_general(qa_scr[h], k, (((1,), (1,)), ((), ())), preferred_element_type=F32)
                  for h in range(NSA_GROUP_HEADS)]
        return scores, [v_ref[...]] * NSA_GROUP_HEADS

    def write_out():
        for h in range(NSA_GROUP_HEADS):
            o_ref[:, h * HEAD_DIM:(h + 1) * HEAD_DIM] = _flash_out(acc_scr, h, HEAD_DIM).astype(o_ref.dtype)

    _flash_tile(qi, kj, tq, scores_and_values, m_scr, acc_scr, write_out)


def _nsa_sel(z0, bias, onehot, b, seq, tq=1024):
    tq = min(tq, seq)
    nq = seq // tq
    gw = NSA_GROUP_HEADS * HEAD_DIM
    kcol, vcol = Z0_KSLC // HEAD_DIM, Z0_VSLC // HEAD_DIM
    per_slab = max(LANES * NSA_SEL_LEN // tq, 1)
    qi, kj = _tri_schedule(nq)
    grid_spec = pltpu.PrefetchScalarGridSpec(
        num_scalar_prefetch=2, grid=(b, NSA_KV_GROUPS, int(qi.shape[0])),
        in_specs=[pl.BlockSpec((tq, gw), lambda bi, g, s, qi, kj: (bi * nq + qi[s], g)),
                  pl.BlockSpec((tq, HEAD_DIM), lambda bi, g, s, qi, kj: (bi * nq + kj[s], kcol + g)),
                  pl.BlockSpec((tq, HEAD_DIM), lambda bi, g, s, qi, kj: (bi * nq + kj[s], vcol + g)),
                  pl.BlockSpec((tq, LANES), lambda bi, g, s, qi, kj: (kj[s], 0)),
                  pl.BlockSpec((None, None, tq, LANES),
                               lambda bi, g, s, qi, kj: (bi, g, qi[s], kj[s] // per_slab))],
        out_specs=pl.BlockSpec((tq, gw), lambda bi, g, s, qi, kj: (bi * nq + qi[s], g)),
        scratch_shapes=[pltpu.VMEM((NSA_GROUP_HEADS, tq, 2 * HEAD_DIM), BF16),
                        pltpu.VMEM((NSA_GROUP_HEADS, tq, LANES), F32),
                        pltpu.VMEM((NSA_GROUP_HEADS, tq, HEAD_DIM + LANES), F32)])
    return pl.pallas_call(
        functools.partial(_nsa_sel_body, tq=tq, per_slab=per_slab), grid_spec=grid_spec,
        out_shape=jax.ShapeDtypeStruct((b * seq, NSA_HEADS * HEAD_DIM), BF16),
        compiler_params=_cparams(("parallel", "parallel", "arbitrary")))(qi, kj, z0, z0, z0, onehot, bias)


MLA_STEP_HEADS = 4


def _mla_body(qi_ref, kj_ref, q_ref, kn_ref, kp_ref, v_ref, o_ref, m_scr, acc_scr, *, tq):
    step = pl.program_id(2)
    qi = qi_ref[step]
    kj = kj_ref[step]

    @pl.when(kj == 0)
    def _():
        _flash_init(m_scr, acc_scr)

    def scores_and_values():
        kp = kp_ref[...]
        scores, vs = [], []
        for h in range(MLA_STEP_HEADS):
            k = jnp.concatenate([kn_ref[:, h * MLA_NOPE:(h + 1) * MLA_NOPE], kp], axis=1)
            scores.append(lax.dot_general(q_ref[:, h * 2 * LANES:(h + 1) * 2 * LANES], k,
                                          (((1,), (1,)), ((), ())), preferred_element_type=F32))
            vs.append(v_ref[:, h * MLA_V:(h + 1) * MLA_V])
        return scores, vs

    def write_out():
        for h in range(MLA_STEP_HEADS):
            o_ref[:, h * MLA_V:(h + 1) * MLA_V] = _flash_out(acc_scr, h, MLA_V).astype(o_ref.dtype)

    _flash_tile(qi, kj, tq, scores_and_values, m_scr, acc_scr, write_out)


def _mla(qcat, kv, z0, b, seq, tq=1024):
    tq = min(tq, seq)
    nq = seq // tq
    nh = MLA_STEP_HEADS
    ngrp = MLA_HEADS // nh
    kpcol = Z0_KROPE // LANES
    qi, kj = _tri_schedule(nq)
    grid_spec = pltpu.PrefetchScalarGridSpec(
        num_scalar_prefetch=2, grid=(b, ngrp, int(qi.shape[0])),
        in_specs=[pl.BlockSpec((tq, nh * 2 * LANES), lambda bi, h, s, qi, kj: (bi * nq + qi[s], h)),
                  pl.BlockSpec((tq, nh * MLA_NOPE), lambda bi, h, s, qi, kj: (bi * nq + kj[s], h)),
                  pl.BlockSpec((tq, LANES), lambda bi, h, s, qi, kj: (bi * nq + kj[s], kpcol)),
                  pl.BlockSpec((tq, nh * MLA_V), lambda bi, h, s, qi, kj: (bi * nq + kj[s], ngrp + h))],
        out_specs=pl.BlockSpec((tq, nh * MLA_V), lambda bi, h, s, qi, kj: (bi * nq + qi[s], h)),
        scratch_shapes=[pltpu.VMEM((nh, tq, LANES), F32), pltpu.VMEM((nh, tq, MLA_V + LANES), F32)])
    return pl.pallas_call(
        functools.partial(_mla_body, tq=tq), grid_spec=grid_spec,
        out_shape=jax.ShapeDtypeStruct((b * seq, MLA_HEADS * MLA_V), BF16),
        compiler_params=_cparams(("parallel", "parallel", "arbitrary")))(qi, kj, qcat, kv, z0, kv)


def _dil_body(q_ref, kp_ref, kc_ref, vp_ref, vc_ref, o_ref, lse_ref, *, tq):
    i = pl.program_id(2)
    u = i * tq + lax.broadcasted_iota(jnp.int32, (tq, 1), 0)
    pos = i * tq - DIL_SPAN + lax.broadcasted_iota(jnp.int32, (1, tq + DIL_SPAN), 1)
    rel = u - pos
    mask = (rel >= 0) & (rel <= DIL_SPAN) & (pos >= 0)
    lane = lax.broadcasted_iota(jnp.int32, (tq, LANES), 1)
    lse_all = jnp.zeros((tq, LANES), F32)
    scores, vs = [], []
    for h in range(DIL_HEADS):
        sl = slice(h * HEAD_DIM, (h + 1) * HEAD_DIM)
        k = jnp.concatenate([kp_ref[:, sl], kc_ref[:, sl]], axis=0)
        vs.append(jnp.concatenate([vp_ref[:, sl], vc_ref[:, sl]], axis=0))
        s = lax.dot_general(q_ref[:, sl], k, (((1,), (1,)), ((), ())), preferred_element_type=F32)
        scores.append(jnp.where(mask, s, NEG_INF))
    for h in range(DIL_HEADS):
        num, den, m = _softmax_pv(scores[h], vs[h])
        o_ref[:, h * HEAD_DIM:(h + 1) * HEAD_DIM] = (num / den).astype(o_ref.dtype)
        lse_all = jnp.where(lane == h, m * LN2 + jnp.log(den), lse_all)
    lse_ref[...] = lse_all


def _deinterleave_body(*refs, r, rows):
    x_refs, o_ref = refs[:-1], refs[-1]
    hw = len(x_refs) * LANES
    for h, x_ref in enumerate(x_refs):
        for c in range(r):
            o_ref[:, c * hw + h * LANES:c * hw + (h + 1) * LANES] = (
                x_ref[pl.ds(c, rows, stride=r), :].astype(o_ref.dtype))


def _deinterleave(z, col0, r, b, seq, rows=128):
    hw = DIL_HEADS * HEAD_DIM
    sub = seq // r
    rows = min(rows, sub)
    nblk = sub // rows
    return pl.pallas_call(
        functools.partial(_deinterleave_body, r=r, rows=rows), grid=(b, nblk, 3),
        in_specs=[pl.BlockSpec((r * rows, LANES),
                               lambda bi, i, j, h=h: (bi * nblk + i, (col0 + j) * DIL_HEADS + h))
                  for h in range(DIL_HEADS)],
        out_specs=pl.BlockSpec((None, rows, r * hw), lambda bi, i, j: (bi, i, j)),
        out_shape=jax.ShapeDtypeStruct((b, sub, 3 * r * hw), BF16),
        compiler_params=_cparams(("parallel", "parallel", "parallel")))(*([z] * DIL_HEADS))


def _dilated(zv, r, b, seq, tq=512):
    sub = seq // r
    tq = min(tq, sub)
    assert sub % tq == 0 and tq % DIL_SPAN == 0
    nq = sub // tq
    hw = DIL_HEADS * HEAD_DIM
    ratio = tq // DIL_SPAN
    prev = lambda i: jnp.maximum(i * ratio - 1, 0)
    o, lse = pl.pallas_call(
        functools.partial(_dil_body, tq=tq), grid=(b, r, nq),
        in_specs=[pl.BlockSpec((None, tq, hw), lambda bi, c, i: (bi, i, c)),
                  pl.BlockSpec((None, DIL_SPAN, hw), lambda bi, c, i: (bi, prev(i), r + c)),
                  pl.BlockSpec((None, tq, hw), lambda bi, c, i: (bi, i, r + c)),
                  pl.BlockSpec((None, DIL_SPAN, hw), lambda bi, c, i: (bi, prev(i), 2 * r + c)),
                  pl.BlockSpec((None, tq, hw), lambda bi, c, i: (bi, i, 2 * r + c))],
        out_specs=[pl.BlockSpec((None, tq, hw), lambda bi, c, i: (bi, i, c)),
                   pl.BlockSpec((None, tq, LANES), lambda bi, c, i: (bi, i, c))],
        out_shape=[jax.ShapeDtypeStruct((b, sub, r * hw), BF16),
                   jax.ShapeDtypeStruct((b, sub, r * LANES), F32)],
        compiler_params=_cparams(("parallel", "parallel", "parallel")))(zv, zv, zv, zv, zv)
    return o.reshape(b * seq, hw), lse.reshape(b * seq, LANES)


def _finish(h_ref, m, g_ref, o_ref):
    y = m * lax.rsqrt(jnp.mean(m * m, axis=-1, keepdims=True) + EPS)
    o_ref[...] = h_ref[...] + y * g_ref[...]


def _out0_body(h_ref, oc_ref, os_ref, ow_ref, gate_ref, ob_ref, wa_ref, wb_ref, g_ref, o_ref):
    gate = jax.nn.sigmoid(gate_ref[...].astype(F32))
    parts = []
    for h in range(NSA_HEADS):
        sl = slice(h * HEAD_DIM, (h + 1) * HEAD_DIM)
        parts.append(gate[:, 3 * h:3 * h + 1] * oc_ref[:, sl].astype(F32)
                     + gate[:, 3 * h + 1:3 * h + 2] * os_ref[:, sl].astype(F32)
                     + gate[:, 3 * h + 2:3 * h + 3] * ow_ref[:, sl].astype(F32))
    oa = jnp.concatenate(parts, axis=1).astype(BF16)
    m = (jnp.dot(oa, wa_ref[...], preferred_element_type=F32)
         + jnp.dot(ob_ref[...], wb_ref[...], preferred_element_type=F32))
    _finish(h_ref, m, g_ref, o_ref)


def _out1_body(h_ref, o0_ref, o1_ref, o2_ref, l0_ref, l1_ref, l2_ref, w_ref, g_ref, o_ref):
    l0, l1, l2 = l0_ref[...], l1_ref[...], l2_ref[...]
    mx = jnp.maximum(jnp.maximum(l0, l1), l2)
    e0, e1, e2 = jnp.exp(l0 - mx), jnp.exp(l1 - mx), jnp.exp(l2 - mx)
    tot = e0 + e1 + e2
    a0, a1, a2 = e0 / tot, e1 / tot, e2 / tot
    parts = []
    for h in range(DIL_HEADS):
        sl = slice(h * HEAD_DIM, (h + 1) * HEAD_DIM)
        parts.append(a0[:, h:h + 1] * o0_ref[:, sl].astype(F32)
                     + a1[:, h:h + 1] * o1_ref[:, sl].astype(F32)
                     + a2[:, h:h + 1] * o2_ref[:, sl].astype(F32))
    o = jnp.concatenate(parts, axis=1).astype(BF16)
    _finish(h_ref, jnp.dot(o, w_ref[...], preferred_element_type=F32), g_ref, o_ref)


def _row_spec(tm, w, col=0):
    return pl.BlockSpec((tm, w), lambda i: (i, col))


def _full_spec(shape):
    return pl.BlockSpec(shape, lambda i: (0,) * len(shape))


def _out0(h, o_c, o_s, o_w, z0, o_b, wa, wb, g, tm=256):
    t, d = h.shape
    tm = min(tm, t)
    ow = NSA_HEADS * HEAD_DIM
    return pl.pallas_call(
        _out0_body, grid=(t // tm,),
        in_specs=[_row_spec(tm, d), _row_spec(tm, ow), _row_spec(tm, ow), _row_spec(tm, ow),
                  _row_spec(tm, LANES, Z0_GATE // LANES), _row_spec(tm, ow),
                  _full_spec(wa.shape), _full_spec(wb.shape), _full_spec((1, d))],
        out_specs=_row_spec(tm, d), out_shape=jax.ShapeDtypeStruct((t, d), F32),
        compiler_params=_cparams(("parallel",)))(h, o_c, o_s, o_w, z0, o_b, wa, wb, g.reshape(1, d))


def _out1(h, os_, lses, w, g, tm=256):
    t, d = h.shape
    tm = min(tm, t)
    ow = DIL_HEADS * HEAD_DIM
    return pl.pallas_call(
        _out1_body, grid=(t // tm,),
        in_specs=[_row_spec(tm, d)] + [_row_spec(tm, ow)] * 3 + [_row_spec(tm, LANES)] * 3
                 + [_full_spec(w.shape), _full_spec((1, d))],
        out_specs=_row_spec(tm, d), out_shape=jax.ShapeDtypeStruct((t, d), F32),
        compiler_params=_cparams(("parallel",)))(h, *os_, *lses, w, g.reshape(1, d))


def _mlp_body(h_ref, g1_ref, w1_ref, w2_ref, g2_ref, o_ref, xn_ref, acc_ref):
    f = pl.program_id(1)

    @pl.when(f == 0)
    def _():
        x = h_ref[...]
        y = x * lax.rsqrt(jnp.mean(x * x, axis=-1, keepdims=True) + EPS)
        xn_ref[...] = (y * g1_ref[...]).astype(BF16)
        acc_ref[...] = jnp.zeros(acc_ref.shape, F32)

    a = jnp.maximum(jnp.dot(xn_ref[...], w1_ref[...], preferred_element_type=F32), 0.0)
    acc_ref[...] += jnp.dot((a * a).astype(BF16), w2_ref[...], preferred_element_type=F32)

    @pl.when(f == pl.num_programs(1) - 1)
    def _():
        _finish(h_ref, acc_ref[...], g2_ref, o_ref)


def _mlp(h, g1, w1, w2, g2, tm=512, tf=1024):
    t, d = h.shape
    ff = w1.shape[1]
    tm = min(tm, t)
    return pl.pallas_call(
        _mlp_body, grid=(t // tm, ff // tf),
        in_specs=[pl.BlockSpec((tm, d), lambda i, f: (i, 0)),
                  pl.BlockSpec((1, d), lambda i, f: (0, 0)),
                  pl.BlockSpec((d, tf), lambda i, f: (0, f)),
                  pl.BlockSpec((tf, d), lambda i, f: (f, 0)),
                  pl.BlockSpec((1, d), lambda i, f: (0, 0))],
        out_specs=pl.BlockSpec((tm, d), lambda i, f: (i, 0)),
        out_shape=jax.ShapeDtypeStruct((t, d), F32),
        scratch_shapes=[pltpu.VMEM((tm, d), BF16), pltpu.VMEM((tm, d), F32)],
        compiler_params=_cparams(("parallel", "arbitrary")))(h, g1.reshape(1, d), w1, w2, g2.reshape(1, d))


def _rope64_tile(w):
    z = jnp.zeros((w.shape[0], 32), w.dtype)
    return jnp.concatenate([w[:, :32], z, w[:, 32:], z], axis=1)


def _layer0_w_in(w_in):
    d = w_in.shape[0]
    o1 = NSA_HEADS * HEAD_DIM
    o2 = o1 + 3 * 2 * NSA_KV_GROUPS * HEAD_DIM
    o3 = o2 + 3 * NSA_HEADS
    o4 = o3 + MLA_Q_RANK
    o5 = o4 + MLA_KV_RANK
    kv = w_in[:, o1:o2].reshape(d, 3, 2, NSA_KV_GROUPS * HEAD_DIM)
    kv = kv.transpose(0, 2, 1, 3).reshape(d, o2 - o1)
    gate = jnp.pad(w_in[:, o2:o3], ((0, 0), (0, LANES - (o3 - o2))))
    w = jnp.concatenate([w_in[:, :o1], kv, w_in[:, o3:o4], w_in[:, o4:o5], gate, _rope64_tile(w_in[:, o5:])], 1)
    assert w.shape[1] == Z0_COLS
    return w.astype(BF16)


def _mla_w_uq(w_uq):
    d = w_uq.shape[0]
    w = w_uq.reshape(d, MLA_HEADS, MLA_NOPE + MLA_ROPE)
    tiles = [jnp.concatenate([w[:, h, :MLA_NOPE], _rope64_tile(w[:, h, MLA_NOPE:])], 1) for h in range(MLA_HEADS)]
    return jnp.concatenate(tiles, axis=1).astype(BF16)


def _mla_w_ukv(w_ukv):
    d = w_ukv.shape[0]
    w = w_ukv.reshape(d, MLA_HEADS, MLA_NOPE + MLA_V)
    return jnp.concatenate([w[:, :, :MLA_NOPE].reshape(d, -1), w[:, :, MLA_NOPE:].reshape(d, -1)], 1).astype(BF16)


def _layer1_w_in(w_in):
    return w_in.astype(BF16)


def _overlap_matrix(nc, n_cmp, n_slc, nsp):
    ratio = NSA_SEL_LEN // NSA_CMP_STRIDE
    m = np.zeros((nc, nsp), np.float32)
    for off in range(1 - NSA_CMP_LEN // NSA_CMP_STRIDE, ratio):
        n = np.arange(n_slc) * ratio + off
        ok = (n >= 0) & (n < n_cmp)
        m[n[ok], np.arange(n_slc)[ok]] = 1.0
    return jnp.asarray(m, BF16)


def kernel(x, l0_norm_mix_pre, l0_w_in, l0_cmp_pe_k, l0_cmp_w1_k, l0_cmp_w2_k, l0_cmp_pe_v, l0_cmp_w1_v, l0_cmp_w2_v, l0_mla_q_norm, l0_mla_w_uq, l0_mla_kv_norm, l0_mla_w_ukv, l0_w_out, l0_norm_mix_post, l0_norm_ffn_pre, l0_w_ff1, l0_w_ff2, l0_norm_ffn_post, l1_norm_mix_pre, l1_w_in, l1_w_out, l1_norm_mix_post, l1_norm_ffn_pre, l1_w_ff1, l1_w_ff2, l1_norm_ffn_post):
    b, seq, d = x.shape
    t = b * seq
    assert seq % NSA_WINDOW == 0 and seq % (DIL_PATTERNS[-1][1] * DIL_SPAN) == 0
    rot = _rope_tables(seq)
    h = x.reshape(t, d)

    tabs0 = ([TAB_ROPE128_Q] * NSA_HEADS + [TAB_ROPE128] * (3 * NSA_KV_GROUPS)
             + [TAB_IDENT] * ((Z0_KROPE - Z0_VCMP) // LANES) + [TAB_ROPE64])
    z0 = _proj_res(h, 0, d, l0_norm_mix_pre, _layer0_w_in(l0_w_in), seq, tabs0, rot)

    nc = seq // NSA_CMP_STRIDE
    n_cmp = (seq - NSA_CMP_LEN) // NSA_CMP_STRIDE + 1
    n_slc = seq // NSA_SEL_LEN
    nsp = -(-n_slc // LANES) * LANES
    gd = NSA_KV_GROUPS * HEAD_DIM

    def chunks(col):
        c = z0[:, col:col + gd].reshape(b, nc, NSA_CMP_STRIDE, NSA_KV_GROUPS, HEAD_DIM)
        return c.transpose(0, 3, 1, 2, 4).reshape(b, NSA_KV_GROUPS, nc, NSA_CMP_STRIDE * HEAD_DIM)

    pe = jnp.stack([l0_cmp_pe_k.reshape(1, -1), l0_cmp_pe_v.reshape(1, -1)])
    kv_cmp = _compress(jnp.stack([chunks(Z0_KCMP), chunks(Z0_VCMP)]),
                       jnp.stack([l0_cmp_w1_k, l0_cmp_w1_v]).astype(BF16),
                       jnp.stack([l0_cmp_w2_k, l0_cmp_w2_v]).astype(BF16),
                       jnp.broadcast_to(pe, (2, 8, pe.shape[-1])).astype(F32))
    o_c, bias = _nsa_cmp(z0, kv_cmp, _overlap_matrix(nc, n_cmp, n_slc, nsp), b, seq)
    o_w = _nsa_win(z0, b, seq)
    blk_lane = (jnp.arange(seq, dtype=jnp.int32) // NSA_SEL_LEN) % LANES
    onehot = (blk_lane[:, None] == jnp.arange(LANES, dtype=jnp.int32)[None, :]).astype(BF16)
    o_s = _nsa_sel(z0, bias, onehot, b, seq)

    qcat = _proj_res(z0, Z0_CQ // MLA_Q_RANK, MLA_Q_RANK, l0_mla_q_norm, _mla_w_uq(l0_mla_w_uq), seq,
                     [TAB_IDENT_Q, TAB_ROPE64_Q] * MLA_HEADS, rot)
    kv = _proj_res(z0, Z0_CKV // MLA_KV_RANK, MLA_KV_RANK, l0_mla_kv_norm, _mla_w_ukv(l0_mla_w_ukv), seq,
                   [TAB_IDENT] * (2 * MLA_HEADS), rot)
    o_b = _mla(qcat, kv, z0, b, seq)

    ow = NSA_HEADS * HEAD_DIM
    w_out0 = l0_w_out.astype(BF16)
    h = _out0(h, o_c, o_s, o_w, z0, o_b, w_out0[:ow], w_out0[ow:], l0_norm_mix_post)
    h = _mlp(h, l0_norm_ffn_pre, l0_w_ff1.astype(BF16), l0_w_ff2.astype(BF16), l0_norm_ffn_post)

    hw = DIL_HEADS * HEAD_DIM
    tabs_g = ([TAB_ROPE128_Q] * DIL_HEADS + [TAB_ROPE128] * DIL_HEADS
              + [TAB_IDENT] * DIL_HEADS)
    w1_in = _layer1_w_in(l1_w_in)
    outs = []
    for p, (_, r) in enumerate(DIL_PATTERNS):
        w_p = w1_in[:, 3 * hw * p:3 * hw * (p + 1)]
        if r == 1:
            z1 = _proj_res(h, 0, d, l1_norm_mix_pre, w_p, seq, tabs_g, rot)
            outs.append(_dilated(z1.reshape(b, seq, 3 * hw), 1, b, seq))
        else:
            z1 = _proj_res(h, 0, d, l1_norm_mix_pre, w_p, seq, tabs_g, rot, out_dtype=F32)
            outs.append(_dilated(_deinterleave(z1, 0, r, b, seq), r, b, seq))
    h = _out1(h, [o for o, _ in outs], [l for _, l in outs], l1_w_out.astype(BF16), l1_norm_mix_post)
    h = _mlp(h, l1_norm_ffn_pre, l1_w_ff1.astype(BF16), l1_w_ff2.astype(BF16), l1_norm_ffn_post)
    return h.reshape(b, seq, d)
```

```python
import functools

import numpy as np
import jax
import jax.numpy as jnp
from jax import lax
from jax.experimental import pallas as pl
from jax.experimental.pallas import tpu as pltpu

F32 = jnp.float32
BF16 = jnp.bfloat16

HEAD_DIM = 128
LANES = 128
ROPE_THETA = 10000.0
EPS = 1e-6
NEG_INF = -1e30
POS_INF = 1e30
PICKED = -3e38

NSA_HEADS = 8
NSA_KV_GROUPS = 2
NSA_GROUP_HEADS = NSA_HEADS // NSA_KV_GROUPS
NSA_CMP_LEN = 32
NSA_CMP_STRIDE = 16
NSA_SEL_LEN = 64
NSA_SEL_TOPK = 16
NSA_WINDOW = 512

MLA_HEADS = 8
MLA_Q_RANK = 512
MLA_KV_RANK = 256
MLA_NOPE = 128
MLA_ROPE = 64
MLA_V = 128

DIL_PATTERNS = ((128, 1), (512, 4), (2048, 16))
DIL_HEADS = 8
DIL_SPAN = 128

VMEM_LIMIT = 56 * 1024 * 1024

Z0_Q = 0
Z0_KCMP = 1024
Z0_KSLC = 1280
Z0_KWIN = 1536
Z0_VCMP = 1792
Z0_VSLC = 2048
Z0_VWIN = 2304
Z0_CQ = 2560
Z0_CKV = 3072
Z0_GATE = 3328
Z0_KROPE = 3456
Z0_COLS = 3584

TAB_ROPE128, TAB_IDENT, TAB_ROPE64, TAB_ROPE128_Q, TAB_ROPE64_Q, TAB_IDENT_Q = 0, 1, 2, 3, 4, 5

LOG2E = 1.4426950408889634
LN2 = 0.6931471805599453
QSCALE_128 = HEAD_DIM ** -0.5 * LOG2E
QSCALE_MLA = (MLA_NOPE + MLA_ROPE) ** -0.5 * LOG2E


def _cparams(sem):
    return pltpu.CompilerParams(dimension_semantics=sem, vmem_limit_bytes=VMEM_LIMIT)


def _rope_tables(seq):
    def cs(dim):
        inv = 1.0 / (ROPE_THETA ** (jnp.arange(0, dim, 2, dtype=F32) / dim))
        ang = jnp.arange(seq, dtype=F32)[:, None] * inv[None, :]
        return jnp.cos(ang), jnp.sin(ang)

    c128, s128 = cs(HEAD_DIM)
    c64, s64 = cs(MLA_ROPE)
    one = jnp.ones((seq, 32), F32)
    zero = jnp.zeros((seq, 32), F32)
    ident_c = jnp.ones((seq, LANES), F32)
    ident_s = jnp.zeros((seq, LANES), F32)
    r128_c = jnp.concatenate([c128, c128], axis=1)
    r128_s = jnp.concatenate([-s128, s128], axis=1)
    r64_c = jnp.concatenate([c64, one, c64, one], axis=1)
    r64_s = jnp.concatenate([-s64, zero, s64, zero], axis=1)
    cos = [r128_c, ident_c, r64_c, r128_c * QSCALE_128, r64_c * QSCALE_MLA, ident_c * QSCALE_MLA]
    sin = [r128_s, ident_s, r64_s, r128_s * QSCALE_128, r64_s * QSCALE_MLA, ident_s]
    return jnp.concatenate([jnp.stack(cos), jnp.stack(sin)], axis=2)


def _proj_res_body(*refs, kinds, tn, scale_q, dil):
    x_ref, g_ref, w_ref = refs[:3]
    used = sorted({k for k in kinds if k not in (TAB_IDENT, TAB_IDENT_Q)})
    t_refs = dict(zip(used, refs[3:3 + len(used)]))
    o_ref, xn_ref = refs[3 + len(used):5 + len(used)]
    stage_refs = refs[5 + len(used):]
    tm = x_ref.shape[0]
    x = x_ref[...].astype(F32)
    y = x * lax.rsqrt(jnp.mean(x * x, axis=-1, keepdims=True) + EPS)
    xn_ref[...] = (y * g_ref[...]).astype(BF16)
    nl = tn // LANES
    for j in range(w_ref.shape[1] // tn):
        acc = jnp.dot(xn_ref[...], w_ref[:, j * tn:(j + 1) * tn], preferred_element_type=F32)
        parts = []
        for k in range(nl):
            kind = kinds[j * nl + k]
            a = acc[:, k * LANES:(k + 1) * LANES]
            if kind == TAB_IDENT:
                parts.append(a)
            elif kind == TAB_IDENT_Q:
                parts.append(a * scale_q)
            else:
                tab = t_refs[kind]
                parts.append(a * tab[:, :LANES] + pltpu.roll(a, LANES // 2, 1) * tab[:, LANES:])
        if dil == 1:
            o_ref[:, j * tn:(j + 1) * tn] = jnp.concatenate(parts, axis=1).astype(o_ref.dtype)
            continue
        hw = DIL_HEADS * HEAD_DIM
        for k in range(nl):
            g = j * nl + k
            stage_refs[k][...] = parts[k]
            for c in range(dil):
                col = ((g // DIL_HEADS) * dil + c) * hw + (g % DIL_HEADS) * LANES
                o_ref[:, col:col + LANES] = stage_refs[k][pl.ds(c, tm // dil, stride=dil), :].astype(o_ref.dtype)


def _proj_res(x, x_col, d, g, w, seq, kinds, rot, tm=512, tn=512, dil=1):
    t = x.shape[0]
    n = w.shape[1]
    tm = min(tm, seq)
    assert t % tm == 0 and seq % tm == 0 and n % tn == 0 and len(kinds) == n // LANES and tm % (16 * dil) == 0
    spb = seq // tm
    used = sorted({k for k in kinds if k not in (TAB_IDENT, TAB_IDENT_Q)})
    tab_specs = [pl.BlockSpec((None, tm, 2 * LANES), lambda i, kind=kind: (kind, i % spb, 0)) for kind in used]
    scratch = [pltpu.VMEM((tm, d), BF16)]
    if dil > 1:
        scratch += [pltpu.VMEM((tm, LANES), F32)] * (tn // LANES)
    return pl.pallas_call(
        functools.partial(_proj_res_body, kinds=tuple(kinds), tn=tn, scale_q=QSCALE_MLA, dil=dil),
        grid=(t // tm,),
        in_specs=[pl.BlockSpec((tm, d), lambda i: (i, x_col)),
                  pl.BlockSpec((1, d), lambda i: (0, 0)),
                  pl.BlockSpec((d, n), lambda i: (0, 0), pipeline_mode=pl.Buffered(1))] + tab_specs,
        out_specs=pl.BlockSpec((tm // dil, dil * n), lambda i: (i, 0)),
        scratch_shapes=scratch,
        out_shape=jax.ShapeDtypeStruct((t // dil, dil * n), BF16),
        compiler_params=_cparams(("parallel",)))(x, g.reshape(1, d).astype(F32), w, *([rot] * len(used)))


def _compress_body(c_ref, w1_ref, w2_ref, pe_ref, o_ref, *, nc):
    half = NSA_CMP_STRIDE * HEAD_DIM
    c = c_ref[...]
    a = jnp.dot(c, w1_ref[:half, :], preferred_element_type=F32)
    b = jnp.dot(c, w1_ref[half:, :], preferred_element_type=F32)
    pe = pe_ref[...]
    pe_hi = pe.astype(BF16)
    pe_lo = (pe - pe_hi.astype(F32)).astype(BF16)
    pe_term = (jnp.dot(pe_hi, w1_ref[...], preferred_element_type=F32)
               + jnp.dot(pe_lo, w1_ref[...], preferred_element_type=F32))
    hid = a + pltpu.roll(b, nc - 1, 0) + pe_term[0:1, :]
    act = jax.nn.gelu(hid)
    o_ref[...] = jnp.dot(act.astype(BF16), w2_ref[...], preferred_element_type=F32).astype(o_ref.dtype)


def _compress(chunks, w1, w2, pe):
    _, b, g, nc, cw = chunks.shape
    return pl.pallas_call(
        functools.partial(_compress_body, nc=nc), grid=(2, b, g),
        in_specs=[pl.BlockSpec((None, None, None, nc, cw), lambda s, i, j: (s, i, j, 0, 0)),
                  pl.BlockSpec((None, 2 * cw, HEAD_DIM), lambda s, i, j: (s, 0, 0)),
                  pl.BlockSpec((None, HEAD_DIM, HEAD_DIM), lambda s, i, j: (s, 0, 0)),
                  pl.BlockSpec((None, 8, 2 * cw), lambda s, i, j: (s, 0, 0))],
        out_specs=pl.BlockSpec((None, None, None, nc, HEAD_DIM), lambda s, i, j: (s, i, j, 0, 0)),
        out_shape=jax.ShapeDtypeStruct((2, b, g, nc, HEAD_DIM), BF16),
        compiler_params=_cparams(("parallel", "parallel", "parallel")))(chunks, w1, w2, pe)


def _nsa_cmp_body(q_ref, k_ref, v_ref, m_ref, o_ref, bias_ref, *, tq, nc, nsp, topk):
    t = pl.program_id(2) * tq + lax.broadcasted_iota(jnp.int32, (tq, 1), 0)
    n = lax.broadcasted_iota(jnp.int32, (1, nc), 1)
    vis = (n * NSA_CMP_STRIDE + (NSA_CMP_LEN - 1)) <= t
    any_vis = (t >= NSA_CMP_LEN - 1).astype(F32)
    k = k_ref[...]
    v = v_ref[...]
    psum = jnp.zeros((tq, nc), F32)
    for h in range(NSA_GROUP_HEADS):
        sl = slice(h * HEAD_DIM, (h + 1) * HEAD_DIM)
        s = lax.dot_general(q_ref[:, sl], k, (((1,), (1,)), ((), ())), preferred_element_type=F32)
        s = jnp.where(vis, s, NEG_INF)
        e = jnp.exp2(s - jnp.max(s, axis=-1, keepdims=True))
        p = e * (any_vis / jnp.sum(e, axis=-1, keepdims=True))
        o_ref[:, sl] = jnp.dot(p.astype(BF16), v, preferred_element_type=F32).astype(o_ref.dtype)
        psum = psum + p
    hi = psum.astype(BF16)
    lo = (psum - hi.astype(F32)).astype(BF16)
    imp = (jnp.dot(hi, m_ref[...], preferred_element_type=F32)
           + jnp.dot(lo, m_ref[...], preferred_element_type=F32))

    blk = lax.broadcasted_iota(jnp.int32, (1, nsp), 1)
    blkf = blk.astype(F32)
    cur = lax.shift_right_logical(t, 6)
    forced = (blk == 0) | (blk == cur) | (blk == cur - 1)
    causal = blk * NSA_SEL_LEN <= t
    score = jnp.where(forced, PICKED, jnp.where(causal, imp, NEG_INF))

    def take_one(_, score):
        mx = jnp.max(score, axis=-1, keepdims=True)
        first = jnp.min(jnp.where(score == mx, blkf, float(nsp)), axis=-1, keepdims=True)
        return jnp.where(blkf == first, PICKED, score)

    score = lax.fori_loop(0, topk - 3, take_one, score)
    sel = (score == PICKED) & causal
    bias_ref[...] = jnp.where(sel, 0.0, NEG_INF).astype(bias_ref.dtype)


def _nsa_cmp(z0, kv_cmp, ovl, b, seq, tq=512):
    nc = kv_cmp.shape[3]
    nsp = ovl.shape[1]
    tq = min(tq, seq)
    nq = seq // tq
    gw = NSA_GROUP_HEADS * HEAD_DIM
    topk = min(NSA_SEL_TOPK, seq // NSA_SEL_LEN)
    assert topk >= 3
    body = functools.partial(_nsa_cmp_body, tq=tq, nc=nc, nsp=nsp, topk=topk)
    return pl.pallas_call(
        body, grid=(b, NSA_KV_GROUPS, nq),
        in_specs=[pl.BlockSpec((tq, gw), lambda bi, g, i: (bi * nq + i, g)),
                  pl.BlockSpec((None, None, None, nc, HEAD_DIM), lambda bi, g, i: (0, bi, g, 0, 0)),
                  pl.BlockSpec((None, None, None, nc, HEAD_DIM), lambda bi, g, i: (1, bi, g, 0, 0)),
                  pl.BlockSpec((nc, nsp), lambda bi, g, i: (0, 0))],
        out_specs=[pl.BlockSpec((tq, gw), lambda bi, g, i: (bi * nq + i, g)),
                   pl.BlockSpec((None, None, tq, nsp), lambda bi, g, i: (bi, g, i, 0))],
        out_shape=[jax.ShapeDtypeStruct((b * seq, NSA_HEADS * HEAD_DIM), BF16),
                   jax.ShapeDtypeStruct((b, NSA_KV_GROUPS, seq, nsp), BF16)],
        compiler_params=_cparams(("parallel", "parallel", "parallel")))(z0, kv_cmp, kv_cmp, ovl)


def _lane_tile(x, n):
    return jnp.concatenate([x] * n, axis=1) if n > 1 else x


def _softmax_pv(s, v):
    tq, tk = s.shape
    d = v.shape[1]
    m = jnp.broadcast_to(jnp.max(s, axis=-1, keepdims=True), (tq, LANES))
    p = jnp.exp2((s - _lane_tile(m, tk // LANES)).astype(BF16))
    acc = jnp.dot(p, jnp.concatenate([v, jnp.ones((tk, LANES), BF16)], axis=1), preferred_element_type=F32)
    return acc[:, :d], acc[:, d:], m


def _nsa_win_body(q_ref, kp_ref, kc_ref, vp_ref, vc_ref, o_ref, *, tq):
    i = pl.program_id(2)
    t = i * tq + lax.broadcasted_iota(jnp.int32, (tq, 1), 0)
    pos = (i - 1) * tq + lax.broadcasted_iota(jnp.int32, (1, 2 * tq), 1)
    rel = t - pos
    mask = (rel >= 0) & (rel < NSA_WINDOW) & (pos >= 0)
    k = jnp.concatenate([kp_ref[...], kc_ref[...]], axis=0)
    v = jnp.concatenate([vp_ref[...], vc_ref[...]], axis=0)
    scores = [jnp.where(mask, lax.dot_general(q_ref[:, h * HEAD_DIM:(h + 1) * HEAD_DIM], k,
                                              (((1,), (1,)), ((), ())), preferred_element_type=F32), NEG_INF)
              for h in range(NSA_GROUP_HEADS)]
    for h in range(NSA_GROUP_HEADS):
        num, den, _ = _softmax_pv(scores[h], v)
        o_ref[:, h * HEAD_DIM:(h + 1) * HEAD_DIM] = (num / den).astype(o_ref.dtype)


def _nsa_win(z0, b, seq):
    tq = NSA_WINDOW
    assert seq % tq == 0
    nq = seq // tq
    gw = NSA_GROUP_HEADS * HEAD_DIM
    kcol, vcol = Z0_KWIN // HEAD_DIM, Z0_VWIN // HEAD_DIM
    prev = lambda bi, i: bi * nq + jnp.maximum(i - 1, 0)
    return pl.pallas_call(
        functools.partial(_nsa_win_body, tq=tq), grid=(b, NSA_KV_GROUPS, nq),
        in_specs=[pl.BlockSpec((tq, gw), lambda bi, g, i: (bi * nq + i, g)),
                  pl.BlockSpec((tq, HEAD_DIM), lambda bi, g, i: (prev(bi, i), kcol + g)),
                  pl.BlockSpec((tq, HEAD_DIM), lambda bi, g, i: (bi * nq + i, kcol + g)),
                  pl.BlockSpec((tq, HEAD_DIM), lambda bi, g, i: (prev(bi, i), vcol + g)),
                  pl.BlockSpec((tq, HEAD_DIM), lambda bi, g, i: (bi * nq + i, vcol + g))],
        out_specs=pl.BlockSpec((tq, gw), lambda bi, g, i: (bi * nq + i, g)),
        out_shape=jax.ShapeDtypeStruct((b * seq, NSA_HEADS * HEAD_DIM), BF16),
        compiler_params=_cparams(("parallel", "parallel", "parallel")))(z0, z0, z0, z0, z0)


def _tri_schedule(nq):
    qi = np.concatenate([np.full(i + 1, i) for i in range(nq)]).astype(np.int32)
    kj = np.concatenate([np.arange(i + 1) for i in range(nq)]).astype(np.int32)
    return jnp.asarray(qi), jnp.asarray(kj)


def _flash_init(m_scr, acc_scr):
    m_scr[...] = jnp.full(m_scr.shape, NEG_INF, F32)
    acc_scr[...] = jnp.zeros(acc_scr.shape, F32)


def _flash_update(scores, vs, m_scr, acc_scr):
    nh = len(scores)
    reps = scores[0].shape[1] // LANES
    ones = jnp.ones((vs[0].shape[0], LANES), BF16)
    m_prev = [m_scr[h] for h in range(nh)]
    m_new = [jnp.maximum(m_prev[h], jnp.max(scores[h], axis=-1, keepdims=True)) for h in range(nh)]
    ps = [jnp.exp2((scores[h] - _lane_tile(m_new[h], reps)).astype(BF16)) for h in range(nh)]
    alphas = [jnp.exp2(m_prev[h] - m_new[h]) for h in range(nh)]
    for h in range(nh):
        v_aug = jnp.concatenate([vs[h], ones], axis=1)
        acc_scr[h] = (_lane_tile(alphas[h], acc_scr.shape[2] // LANES) * acc_scr[h]
                      + jnp.dot(ps[h], v_aug, preferred_element_type=F32))
        m_scr[h] = m_new[h]


def _flash_out(acc_scr, h, d):
    acc = acc_scr[h]
    return acc[:, :d] / acc[:, d:]


def _flash_tile(qi, kj, tq, scores_and_values, m_scr, acc_scr, write_out):
    @pl.when(kj < qi)
    def _():
        scores, vs = scores_and_values()
        _flash_update(scores, vs, m_scr, acc_scr)

    @pl.when(kj == qi)
    def _():
        scores, vs = scores_and_values()
        row = lax.broadcasted_iota(jnp.int32, (tq, 1), 0)
        col = lax.broadcasted_iota(jnp.int32, (1, tq), 1)
        _flash_update([jnp.where(col <= row, s, NEG_INF) for s in scores], vs, m_scr, acc_scr)
        write_out()


def _nsa_sel_body(qi_ref, kj_ref, q_ref, k_ref, v_ref, oh_ref, bias_ref, o_ref,
                  qa_scr, m_scr, acc_scr, *, tq, per_slab):
    step = pl.program_id(2)
    qi = qi_ref[step]
    kj = kj_ref[step]

    @pl.when(kj == 0)
    def _():
        _flash_init(m_scr, acc_scr)
        for h in range(NSA_GROUP_HEADS):
            qa_scr[h, :, :HEAD_DIM] = q_ref[:, h * HEAD_DIM:(h + 1) * HEAD_DIM]

    @pl.when(kj % per_slab == 0)
    def _():
        for h in range(NSA_GROUP_HEADS):
            qa_scr[h, :, HEAD_DIM:] = bias_ref[...]

    def scores_and_values():
        k = jnp.concatenate([k_ref[...], oh_ref[...]], axis=1)
        scores = [lax.dot_general(qa_scr[h], k, (((1,), (1,)), ((), ())), preferred_element_type=F32)
                  for h in range(NSA_GROUP_HEADS)]
        return scores, [v_ref[...]] * NSA_GROUP_HEADS

    def write_out():
        for h in range(NSA_GROUP_HEADS):
            o_ref[:, h * HEAD_DIM:(h + 1) * HEAD_DIM] = _flash_out(acc_scr, h, HEAD_DIM).astype(o_ref.dtype)

    _flash_tile(qi, kj, tq, scores_and_values, m_scr, acc_scr, write_out)


def _nsa_sel(z0, bias, onehot, b, seq, tq=1024):
    tq = min(tq, seq)
    nq = seq // tq
    gw = NSA_GROUP_HEADS * HEAD_DIM
    kcol, vcol = Z0_KSLC // HEAD_DIM, Z0_VSLC // HEAD_DIM
    per_slab = max(LANES * NSA_SEL_LEN // tq, 1)
    qi, kj = _tri_schedule(nq)
    grid_spec = pltpu.PrefetchScalarGridSpec(
        num_scalar_prefetch=2, grid=(b, NSA_KV_GROUPS, int(qi.shape[0])),
        in_specs=[pl.BlockSpec((tq, gw), lambda bi, g, s, qi, kj: (bi * nq + qi[s], g)),
                  pl.BlockSpec((tq, HEAD_DIM), lambda bi, g, s, qi, kj: (bi * nq + kj[s], kcol + g)),
                  pl.BlockSpec((tq, HEAD_DIM), lambda bi, g, s, qi, kj: (bi * nq + kj[s], vcol + g)),
                  pl.BlockSpec((tq, LANES), lambda bi, g, s, qi, kj: (kj[s], 0)),
                  pl.BlockSpec((None, None, tq, LANES),
                               lambda bi, g, s, qi, kj: (bi, g, qi[s], kj[s] // per_slab))],
        out_specs=pl.BlockSpec((tq, gw), lambda bi, g, s, qi, kj: (bi * nq + qi[s], g)),
        scratch_shapes=[pltpu.VMEM((NSA_GROUP_HEADS, tq, 2 * HEAD_DIM), BF16),
                        pltpu.VMEM((NSA_GROUP_HEADS, tq, LANES), F32),
                        pltpu.VMEM((NSA_GROUP_HEADS, tq, HEAD_DIM + LANES), F32)])
    return pl.pallas_call(
        functools.partial(_nsa_sel_body, tq=tq, per_slab=per_slab), grid_spec=grid_spec,
        out_shape=jax.ShapeDtypeStruct((b * seq, NSA_HEADS * HEAD_DIM), BF16),
        compiler_params=_cparams(("parallel", "parallel", "arbitrary")))(qi, kj, z0, z0, z0, onehot, bias)


MLA_STEP_HEADS = 4


def _mla_body(qi_ref, kj_ref, q_ref, kn_ref, kp_ref, v_ref, o_ref, m_scr, acc_scr, *, tq):
    step = pl.program_id(2)
    qi = qi_ref[step]
    kj = kj_ref[step]

    @pl.when(kj == 0)
    def _():
        _flash_init(m_scr, acc_scr)

    def scores_and_values():
        kp = kp_ref[...]
        scores, vs = [], []
        for h in range(MLA_STEP_HEADS):
            k = jnp.concatenate([kn_ref[:, h * MLA_NOPE:(h + 1) * MLA_NOPE], kp], axis=1)
            scores.append(lax.dot_general(q_ref[:, h * 2 * LANES:(h + 1) * 2 * LANES], k,
                                          (((1,), (1,)), ((), ())), preferred_element_type=F32))
            vs.append(v_ref[:, h * MLA_V:(h + 1) * MLA_V])
        return scores, vs

    def write_out():
        for h in range(MLA_STEP_HEADS):
            o_ref[:, h * MLA_V:(h + 1) * MLA_V] = _flash_out(acc_scr, h, MLA_V).astype(o_ref.dtype)

    _flash_tile(qi, kj, tq, scores_and_values, m_scr, acc_scr, write_out)


def _mla(qcat, kv, z0, b, seq, tq=1024):
    tq = min(tq, seq)
    nq = seq // tq
    nh = MLA_STEP_HEADS
    ngrp = MLA_HEADS // nh
    kpcol = Z0_KROPE // LANES
    qi, kj = _tri_schedule(nq)
    grid_spec = pltpu.PrefetchScalarGridSpec(
        num_scalar_prefetch=2, grid=(b, ngrp, int(qi.shape[0])),
        in_specs=[pl.BlockSpec((tq, nh * 2 * LANES), lambda bi, h, s, qi, kj: (bi * nq + qi[s], h)),
                  pl.BlockSpec((tq, nh * MLA_NOPE), lambda bi, h, s, qi, kj: (bi * nq + kj[s], h)),
                  pl.BlockSpec((tq, LANES), lambda bi, h, s, qi, kj: (bi * nq + kj[s], kpcol)),
                  pl.BlockSpec((tq, nh * MLA_V), lambda bi, h, s, qi, kj: (bi * nq + kj[s], ngrp + h))],
        out_specs=pl.BlockSpec((tq, nh * MLA_V), lambda bi, h, s, qi, kj: (bi * nq + qi[s], h)),
        scratch_shapes=[pltpu.VMEM((nh, tq, LANES), F32), pltpu.VMEM((nh, tq, MLA_V + LANES), F32)])
    return pl.pallas_call(
        functools.partial(_mla_body, tq=tq), grid_spec=grid_spec,
        out_shape=jax.ShapeDtypeStruct((b * seq, MLA_HEADS * MLA_V), BF16),
        compiler_params=_cparams(("parallel", "parallel", "arbitrary")))(qi, kj, qcat, kv, z0, kv)


def _dil_body(q_ref, kp_ref, kc_ref, vp_ref, vc_ref, o_ref, lse_ref, *, tq):
    i = pl.program_id(2)
    u = i * tq + lax.broadcasted_iota(jnp.int32, (tq, 1), 0)
    pos = i * tq - DIL_SPAN + lax.broadcasted_iota(jnp.int32, (1, tq + DIL_SPAN), 1)
    rel = u - pos
    mask = (rel >= 0) & (rel <= DIL_SPAN) & (pos >= 0)
    lane = lax.broadcasted_iota(jnp.int32, (tq, LANES), 1)
    lse_all = jnp.zeros((tq, LANES), F32)
    scores, vs = [], []
    for h in range(DIL_HEADS):
        sl = slice(h * HEAD_DIM, (h + 1) * HEAD_DIM)
        k = jnp.concatenate([kp_ref[:, sl], kc_ref[:, sl]], axis=0)
        vs.append(jnp.concatenate([vp_ref[:, sl], vc_ref[:, sl]], axis=0))
        s = lax.dot_general(q_ref[:, sl], k, (((1,), (1,)), ((), ())), preferred_element_type=F32)
        scores.append(jnp.where(mask, s, NEG_INF))
    for h in range(DIL_HEADS):
        num, den, m = _softmax_pv(scores[h], vs[h])
        o_ref[:, h * HEAD_DIM:(h + 1) * HEAD_DIM] = (num / den).astype(o_ref.dtype)
        lse_all = jnp.where(lane == h, m * LN2 + jnp.log(den), lse_all)
    lse_ref[...] = lse_all


def _dilated(zv, r, b, seq, tq=512):
    sub = seq // r
    tq = min(tq, sub)
    assert sub % tq == 0 and tq % DIL_SPAN == 0
    nq = sub // tq
    hw = DIL_HEADS * HEAD_DIM
    ratio = tq // DIL_SPAN
    prev = lambda i: jnp.maximum(i * ratio - 1, 0)
    o, lse = pl.pallas_call(
        functools.partial(_dil_body, tq=tq), grid=(b, r, nq),
        in_specs=[pl.BlockSpec((None, tq, hw), lambda bi, c, i: (bi, i, c)),
                  pl.BlockSpec((None, DIL_SPAN, hw), lambda bi, c, i: (bi, prev(i), r + c)),
                  pl.BlockSpec((None, tq, hw), lambda bi, c, i: (bi, i, r + c)),
                  pl.BlockSpec((None, DIL_SPAN, hw), lambda bi, c, i: (bi, prev(i), 2 * r + c)),
                  pl.BlockSpec((None, tq, hw), lambda bi, c, i: (bi, i, 2 * r + c))],
        out_specs=[pl.BlockSpec((None, tq, hw), lambda bi, c, i: (bi, i, c)),
                   pl.BlockSpec((None, tq, LANES), lambda bi, c, i: (bi, i, c))],
        out_shape=[jax.ShapeDtypeStruct((b, sub, r * hw), BF16),
                   jax.ShapeDtypeStruct((b, sub, r * LANES), F32)],
        compiler_params=_cparams(("parallel", "parallel", "parallel")))(zv, zv, zv, zv, zv)
    return o.reshape(b * seq, hw), lse.reshape(b * seq, LANES)


def _finish(h_ref, m, g_ref, o_ref):
    y = m * lax.rsqrt(jnp.mean(m * m, axis=-1, keepdims=True) + EPS)
    o_ref[...] = h_ref[...] + y * g_ref[...]


def _out0_body(h_ref, oc_ref, os_ref, ow_ref, gate_ref, ob_ref, wa_ref, wb_ref, g_ref, o_ref):
    gate = jax.nn.sigmoid(gate_ref[...].astype(F32))
    parts = []
    for h in range(NSA_HEADS):
        sl = slice(h * HEAD_DIM, (h + 1) * HEAD_DIM)
        parts.append(gate[:, 3 * h:3 * h + 1] * oc_ref[:, sl].astype(F32)
                     + gate[:, 3 * h + 1:3 * h + 2] * os_ref[:, sl].astype(F32)
                     + gate[:, 3 * h + 2:3 * h + 3] * ow_ref[:, sl].astype(F32))
    oa = jnp.concatenate(parts, axis=1).astype(BF16)
    m = (jnp.dot(oa, wa_ref[...], preferred_element_type=F32)
         + jnp.dot(ob_ref[...], wb_ref[...], preferred_element_type=F32))
    _finish(h_ref, m, g_ref, o_ref)


def _out1_body(h_ref, o0_ref, o1_ref, o2_ref, l0_ref, l1_ref, l2_ref, w_ref, g_ref, o_ref):
    l0, l1, l2 = l0_ref[...], l1_ref[...], l2_ref[...]
    mx = jnp.maximum(jnp.maximum(l0, l1), l2)
    e0, e1, e2 = jnp.exp(l0 - mx), jnp.exp(l1 - mx), jnp.exp(l2 - mx)
    tot = e0 + e1 + e2
    a0, a1, a2 = e0 / tot, e1 / tot, e2 / tot
    parts = []
    for h in range(DIL_HEADS):
        sl = slice(h * HEAD_DIM, (h + 1) * HEAD_DIM)
        parts.append(a0[:, h:h + 1] * o0_ref[:, sl].astype(F32)
                     + a1[:, h:h + 1] * o1_ref[:, sl].astype(F32)
                     + a2[:, h:h + 1] * o2_ref[:, sl].astype(F32))
    o = jnp.concatenate(parts, axis=1).astype(BF16)
    _finish(h_ref, jnp.dot(o, w_ref[...], preferred_element_type=F32), g_ref, o_ref)


def _row_spec(tm, w, col=0):
    return pl.BlockSpec((tm, w), lambda i: (i, col))


def _full_spec(shape):
    return pl.BlockSpec(shape, lambda i: (0,) * len(shape))


def _out0(h, o_c, o_s, o_w, z0, o_b, wa, wb, g, tm=256):
    t, d = h.shape
    tm = min(tm, t)
    ow = NSA_HEADS * HEAD_DIM
    return pl.pallas_call(
        _out0_body, grid=(t // tm,),
        in_specs=[_row_spec(tm, d), _row_spec(tm, ow), _row_spec(tm, ow), _row_spec(tm, ow),
                  _row_spec(tm, LANES, Z0_GATE // LANES), _row_spec(tm, ow),
                  _full_spec(wa.shape), _full_spec(wb.shape), _full_spec((1, d))],
        out_specs=_row_spec(tm, d), out_shape=jax.ShapeDtypeStruct((t, d), F32),
        compiler_params=_cparams(("parallel",)))(h, o_c, o_s, o_w, z0, o_b, wa, wb, g.reshape(1, d))


def _out1(h, os_, lses, w, g, tm=256):
    t, d = h.shape
    tm = min(tm, t)
    ow = DIL_HEADS * HEAD_DIM
    return pl.pallas_call(
        _out1_body, grid=(t // tm,),
        in_specs=[_row_spec(tm, d)] + [_row_spec(tm, ow)] * 3 + [_row_spec(tm, LANES)] * 3
                 + [_full_spec(w.shape), _full_spec((1, d))],
        out_specs=_row_spec(tm, d), out_shape=jax.ShapeDtypeStruct((t, d), F32),
        compiler_params=_cparams(("parallel",)))(h, *os_, *lses, w, g.reshape(1, d))


def _mlp_body(h_ref, g1_ref, w1_ref, w2_ref, g2_ref, o_ref, xn_ref, acc_ref):
    f = pl.program_id(1)
    last = pl.num_programs(1) - 1

    def partial_sum(xn):
        a = jnp.maximum(jnp.dot(xn, w1_ref[...], preferred_element_type=F32), 0.0)
        return jnp.dot((a * a).astype(BF16), w2_ref[...], preferred_element_type=F32)

    @pl.when(f == 0)
    def _():
        x = h_ref[...]
        y = x * lax.rsqrt(jnp.mean(x * x, axis=-1, keepdims=True) + EPS)
        xn = (y * g1_ref[...]).astype(BF16)
        xn_ref[...] = xn
        acc_ref[...] = partial_sum(xn)

    @pl.when((f > 0) & (f < last))
    def _():
        acc_ref[...] += partial_sum(xn_ref[...])

    @pl.when(f == last)
    def _():
        _finish(h_ref, acc_ref[...] + partial_sum(xn_ref[...]), g2_ref, o_ref)


def _mlp(h, g1, w1, w2, g2, tm=512, tf=1024):
    t, d = h.shape
    ff = w1.shape[1]
    tm = min(tm, t)
    return pl.pallas_call(
        _mlp_body, grid=(t // tm, ff // tf),
        in_specs=[pl.BlockSpec((tm, d), lambda i, f: (i, 0)),
                  pl.BlockSpec((1, d), lambda i, f: (0, 0)),
                  pl.BlockSpec((d, tf), lambda i, f: (0, f)),
                  pl.BlockSpec((tf, d), lambda i, f: (f, 0)),
                  pl.BlockSpec((1, d), lambda i, f: (0, 0))],
        out_specs=pl.BlockSpec((tm, d), lambda i, f: (i, 0)),
        out_shape=jax.ShapeDtypeStruct((t, d), F32),
        scratch_shapes=[pltpu.VMEM((tm, d), BF16), pltpu.VMEM((tm, d), F32)],
        compiler_params=_cparams(("parallel", "arbitrary")))(h, g1.reshape(1, d), w1, w2, g2.reshape(1, d))


def _rope64_tile(w):
    z = jnp.zeros((w.shape[0], 32), w.dtype)
    return jnp.concatenate([w[:, :32], z, w[:, 32:], z], axis=1)


def _layer0_w_in(w_in):
    d = w_in.shape[0]
    o1 = NSA_HEADS * HEAD_DIM
    o2 = o1 + 3 * 2 * NSA_KV_GROUPS * HEAD_DIM
    o3 = o2 + 3 * NSA_HEADS
    o4 = o3 + MLA_Q_RANK
    o5 = o4 + MLA_KV_RANK
    kv = w_in[:, o1:o2].reshape(d, 3, 2, NSA_KV_GROUPS * HEAD_DIM)
    kv = kv.transpose(0, 2, 1, 3).reshape(d, o2 - o1)
    gate = jnp.pad(w_in[:, o2:o3], ((0, 0), (0, LANES - (o3 - o2))))
    w = jnp.concatenate([w_in[:, :o1], kv, w_in[:, o3:o4], w_in[:, o4:o5], gate, _rope64_tile(w_in[:, o5:])], 1)
    assert w.shape[1] == Z0_COLS
    return w.astype(BF16)


def _mla_w_uq(w_uq):
    d = w_uq.shape[0]
    w = w_uq.reshape(d, MLA_HEADS, MLA_NOPE + MLA_ROPE)
    tiles = [jnp.concatenate([w[:, h, :MLA_NOPE], _rope64_tile(w[:, h, MLA_NOPE:])], 1) for h in range(MLA_HEADS)]
    return jnp.concatenate(tiles, axis=1).astype(BF16)


def _mla_w_ukv(w_ukv):
    d = w_ukv.shape[0]
    w = w_ukv.reshape(d, MLA_HEADS, MLA_NOPE + MLA_V)
    return jnp.concatenate([w[:, :, :MLA_NOPE].reshape(d, -1), w[:, :, MLA_NOPE:].reshape(d, -1)], 1).astype(BF16)


def _layer1_w_in(w_in):
    return w_in.astype(BF16)


def _overlap_matrix(nc, n_cmp, n_slc, nsp):
    ratio = NSA_SEL_LEN // NSA_CMP_STRIDE
    m = np.zeros((nc, nsp), np.float32)
    for off in range(1 - NSA_CMP_LEN // NSA_CMP_STRIDE, ratio):
        n = np.arange(n_slc) * ratio + off
        ok = (n >= 0) & (n < n_cmp)
        m[n[ok], np.arange(n_slc)[ok]] = 1.0
    return jnp.asarray(m, BF16)


def kernel(x, l0_norm_mix_pre, l0_w_in, l0_cmp_pe_k, l0_cmp_w1_k, l0_cmp_w2_k, l0_cmp_pe_v, l0_cmp_w1_v, l0_cmp_w2_v, l0_mla_q_norm, l0_mla_w_uq, l0_mla_kv_norm, l0_mla_w_ukv, l0_w_out, l0_norm_mix_post, l0_norm_ffn_pre, l0_w_ff1, l0_w_ff2, l0_norm_ffn_post, l1_norm_mix_pre, l1_w_in, l1_w_out, l1_norm_mix_post, l1_norm_ffn_pre, l1_w_ff1, l1_w_ff2, l1_norm_ffn_post):
    b, seq, d = x.shape
    t = b * seq
    assert seq % NSA_WINDOW == 0 and seq % (DIL_PATTERNS[-1][1] * DIL_SPAN) == 0
    rot = _rope_tables(seq)
    h = x.reshape(t, d)

    tabs0 = ([TAB_ROPE128_Q] * NSA_HEADS + [TAB_ROPE128] * (3 * NSA_KV_GROUPS)
             + [TAB_IDENT] * ((Z0_KROPE - Z0_VCMP) // LANES) + [TAB_ROPE64])
    z0 = _proj_res(h, 0, d, l0_norm_mix_pre, _layer0_w_in(l0_w_in), seq, tabs0, rot)

    nc = seq // NSA_CMP_STRIDE
    n_cmp = (seq - NSA_CMP_LEN) // NSA_CMP_STRIDE + 1
    n_slc = seq // NSA_SEL_LEN
    nsp = -(-n_slc // LANES) * LANES
    gd = NSA_KV_GROUPS * HEAD_DIM

    def chunks(col):
        c = z0[:, col:col + gd].reshape(b, nc, NSA_CMP_STRIDE, NSA_KV_GROUPS, HEAD_DIM)
        return c.transpose(0, 3, 1, 2, 4).reshape(b, NSA_KV_GROUPS, nc, NSA_CMP_STRIDE * HEAD_DIM)

    pe = jnp.stack([l0_cmp_pe_k.reshape(1, -1), l0_cmp_pe_v.reshape(1, -1)])
    kv_cmp = _compress(jnp.stack([chunks(Z0_KCMP), chunks(Z0_VCMP)]),
                       jnp.stack([l0_cmp_w1_k, l0_cmp_w1_v]).astype(BF16),
                       jnp.stack([l0_cmp_w2_k, l0_cmp_w2_v]).astype(BF16),
                       jnp.broadcast_to(pe, (2, 8, pe.shape[-1])).astype(F32))
    o_c, bias = _nsa_cmp(z0, kv_cmp, _overlap_matrix(nc, n_cmp, n_slc, nsp), b, seq)
    o_w = _nsa_win(z0, b, seq)
    blk_lane = (jnp.arange(seq, dtype=jnp.int32) // NSA_SEL_LEN) % LANES
    onehot = (blk_lane[:, None] == jnp.arange(LANES, dtype=jnp.int32)[None, :]).astype(BF16)
    o_s = _nsa_sel(z0, bias, onehot, b, seq)

    qcat = _proj_res(z0, Z0_CQ // MLA_Q_RANK, MLA_Q_RANK, l0_mla_q_norm, _mla_w_uq(l0_mla_w_uq), seq,
                     [TAB_IDENT_Q, TAB_ROPE64_Q] * MLA_HEADS, rot)
    kv = _proj_res(z0, Z0_CKV // MLA_KV_RANK, MLA_KV_RANK, l0_mla_kv_norm, _mla_w_ukv(l0_mla_w_ukv), seq,
                   [TAB_IDENT] * (2 * MLA_HEADS), rot)
    o_b = _mla(qcat, kv, z0, b, seq)

    ow = NSA_HEADS * HEAD_DIM
    w_out0 = l0_w_out.astype(BF16)
    h = _out0(h, o_c, o_s, o_w, z0, o_b, w_out0[:ow], w_out0[ow:], l0_norm_mix_post)
    h = _mlp(h, l0_norm_ffn_pre, l0_w_ff1.astype(BF16), l0_w_ff2.astype(BF16), l0_norm_ffn_post)

    hw = DIL_HEADS * HEAD_DIM
    tabs_g = ([TAB_ROPE128_Q] * DIL_HEADS + [TAB_ROPE128] * DIL_HEADS
              + [TAB_IDENT] * DIL_HEADS)
    w1_in = _layer1_w_in(l1_w_in)
    outs = []
    for p, (_, r) in enumerate(DIL_PATTERNS):
        w_p = w1_in[:, 3 * hw * p:3 * hw * (p + 1)]
        z1 = _proj_res(h, 0, d, l1_norm_mix_pre, w_p, seq, tabs_g, rot, dil=r)
        outs.append(_dilated(z1.reshape(b, seq // r, 3 * r * hw), r, b, seq))
    h = _out1(h, [o for o, _ in outs], [l for _, l in outs], l1_w_out.astype(BF16), l1_norm_mix_post)
    h = _mlp(h, l1_norm_ffn_pre, l1_w_ff1.astype(BF16), l1_w_ff2.astype(BF16), l1_norm_ffn_post)
    return h.reshape(b, seq, d)
```

```python
import functools

import numpy as np
import jax
import jax.numpy as jnp
from jax import lax
from jax.experimental import pallas as pl
from jax.experimental.pallas import tpu as pltpu

F32 = jnp.float32
BF16 = jnp.bfloat16

HEAD_DIM = 128
LANES = 128
ROPE_THETA = 10000.0
EPS = 1e-6
NEG_INF = -1e30
POS_INF = 1e30
PICKED = -3e38

NSA_HEADS = 8
NSA_KV_GROUPS = 2
NSA_GROUP_HEADS = NSA_HEADS // NSA_KV_GROUPS
NSA_CMP_LEN = 32
NSA_CMP_STRIDE = 16
NSA_SEL_LEN = 64
NSA_SEL_TOPK = 16
NSA_WINDOW = 512

MLA_HEADS = 8
MLA_Q_RANK = 512
MLA_KV_RANK = 256
MLA_NOPE = 128
MLA_ROPE = 64
MLA_V = 128

DIL_PATTERNS = ((128, 1), (512, 4), (2048, 16))
DIL_HEADS = 8
DIL_SPAN = 128

VMEM_LIMIT = 56 * 1024 * 1024
MAX_ROW_STRIDE = 4

Z0_Q = 0
Z0_KCMP = 1024
Z0_KSLC = 1280
Z0_KWIN = 1536
Z0_VCMP = 1792
Z0_VSLC = 2048
Z0_VWIN = 2304
Z0_CQ = 2560
Z0_CKV = 3072
Z0_GATE = 3328
Z0_KROPE = 3456
Z0_COLS = 3584

TAB_ROPE128, TAB_IDENT, TAB_ROPE64, TAB_ROPE128_Q, TAB_ROPE64_Q, TAB_IDENT_Q = 0, 1, 2, 3, 4, 5

LOG2E = 1.4426950408889634
LN2 = 0.6931471805599453
QSCALE_128 = HEAD_DIM ** -0.5 * LOG2E
QSCALE_MLA = (MLA_NOPE + MLA_ROPE) ** -0.5 * LOG2E


def _cparams(sem):
    return pltpu.CompilerParams(dimension_semantics=sem, vmem_limit_bytes=VMEM_LIMIT)


def _rope_tables(seq):
    def cs(dim):
        inv = 1.0 / (ROPE_THETA ** (jnp.arange(0, dim, 2, dtype=F32) / dim))
        ang = jnp.arange(seq, dtype=F32)[:, None] * inv[None, :]
        return jnp.cos(ang), jnp.sin(ang)

    c128, s128 = cs(HEAD_DIM)
    c64, s64 = cs(MLA_ROPE)
    one = jnp.ones((seq, 32), F32)
    zero = jnp.zeros((seq, 32), F32)
    ident_c = jnp.ones((seq, LANES), F32)
    ident_s = jnp.zeros((seq, LANES), F32)
    r128_c = jnp.concatenate([c128, c128], axis=1)
    r128_s = jnp.concatenate([-s128, s128], axis=1)
    r64_c = jnp.concatenate([c64, one, c64, one], axis=1)
    r64_s = jnp.concatenate([-s64, zero, s64, zero], axis=1)
    cos = [r128_c, ident_c, r64_c, r128_c * QSCALE_128, r64_c * QSCALE_MLA, ident_c * QSCALE_MLA]
    sin = [r128_s, ident_s, r64_s, r128_s * QSCALE_128, r64_s * QSCALE_MLA, ident_s]
    return jnp.concatenate([jnp.stack(cos), jnp.stack(sin)], axis=2)


def _proj_res_body(*refs, kinds, tn, scale_q, dil):
    x_ref, g_ref, w_ref = refs[:3]
    used = sorted({k for k in kinds if k not in (TAB_IDENT, TAB_IDENT_Q)})
    t_refs = dict(zip(used, refs[3:3 + len(used)]))
    o_ref, xn_ref = refs[3 + len(used):5 + len(used)]
    stage_refs = refs[5 + len(used):]
    tm = x_ref.shape[0]
    x = x_ref[...].astype(F32)
    y = x * lax.rsqrt(jnp.mean(x * x, axis=-1, keepdims=True) + EPS)
    xn_ref[...] = (y * g_ref[...]).astype(BF16)
    nl = tn // LANES
    for j in range(w_ref.shape[1] // tn):
        acc = jnp.dot(xn_ref[...], w_ref[:, j * tn:(j + 1) * tn], preferred_element_type=F32)
        parts = []
        for k in range(nl):
            kind = kinds[j * nl + k]
            a = acc[:, k * LANES:(k + 1) * LANES]
            if kind == TAB_IDENT:
                parts.append(a)
            elif kind == TAB_IDENT_Q:
                parts.append(a * scale_q)
            else:
                tab = t_refs[kind]
                parts.append(a * tab[:, :LANES] + pltpu.roll(a, LANES // 2, 1) * tab[:, LANES:])
        if dil == 1:
            o_ref[:, j * tn:(j + 1) * tn] = jnp.concatenate(parts, axis=1).astype(o_ref.dtype)
            continue
        hw = DIL_HEADS * HEAD_DIM
        s1 = min(dil, MAX_ROW_STRIDE)
        s2 = dil // s1
        for k in range(nl):
            g = j * nl + k
            first, second = stage_refs[2 * k], stage_refs[2 * k + 1]
            first[...] = parts[k]
            if s2 > 1:
                for c1 in range(s1):
                    second[c1 * (tm // s1):(c1 + 1) * (tm // s1), :] = first[pl.ds(c1, tm // s1, stride=s1), :]
            for c in range(dil):
                c1, c2 = c % s1, c // s1
                if s2 > 1:
                    piece = second[pl.ds(c1 * (tm // s1) + c2, tm // dil, stride=s2), :]
                else:
                    piece = first[pl.ds(c, tm // dil, stride=dil), :]
                col = ((g // DIL_HEADS) * dil + c) * hw + (g % DIL_HEADS) * LANES
                o_ref[:, col:col + LANES] = piece.astype(o_ref.dtype)


def _proj_res(x, x_col, d, g, w, seq, kinds, rot, tm=512, tn=512, dil=1):
    t = x.shape[0]
    n = w.shape[1]
    tm = min(tm, seq)
    assert t % tm == 0 and seq % tm == 0 and n % tn == 0 and len(kinds) == n // LANES and tm % (16 * dil) == 0
    spb = seq // tm
    used = sorted({k for k in kinds if k not in (TAB_IDENT, TAB_IDENT_Q)})
    tab_specs = [pl.BlockSpec((None, tm, 2 * LANES), lambda i, kind=kind: (kind, i % spb, 0)) for kind in used]
    scratch = [pltpu.VMEM((tm, d), BF16)]
    if dil > 1:
        scratch += [pltpu.VMEM((tm, LANES), F32)] * (2 * (tn // LANES))
    return pl.pallas_call(
        functools.partial(_proj_res_body, kinds=tuple(kinds), tn=tn, scale_q=QSCALE_MLA, dil=dil),
        grid=(t // tm,),
        in_specs=[pl.BlockSpec((tm, d), lambda i: (i, x_col)),
                  pl.BlockSpec((1, d), lambda i: (0, 0)),
                  pl.BlockSpec((d, n), lambda i: (0, 0), pipeline_mode=pl.Buffered(1))] + tab_specs,
        out_specs=pl.BlockSpec((tm // dil, dil * n), lambda i: (i, 0)),
        scratch_shapes=scratch,
        out_shape=jax.ShapeDtypeStruct((t // dil, dil * n), BF16),
        compiler_params=_cparams(("parallel",)))(x, g.reshape(1, d).astype(F32), w, *([rot] * len(used)))


def _compress_body(c_ref, w1_ref, w2_ref, pe_ref, o_ref, *, nc):
    half = NSA_CMP_STRIDE * HEAD_DIM
    c = c_ref[...]
    a = jnp.dot(c, w1_ref[:half, :], preferred_element_type=F32)
    b = jnp.dot(c, w1_ref[half:, :], preferred_element_type=F32)
    pe = pe_ref[...]
    pe_hi = pe.astype(BF16)
    pe_lo = (pe - pe_hi.astype(F32)).astype(BF16)
    pe_term = (jnp.dot(pe_hi, w1_ref[...], preferred_element_type=F32)
               + jnp.dot(pe_lo, w1_ref[...], preferred_element_type=F32))
    hid = a + pltpu.roll(b, nc - 1, 0) + pe_term[0:1, :]
    act = jax.nn.gelu(hid)
    o_ref[...] = jnp.dot(act.astype(BF16), w2_ref[...], preferred_element_type=F32).astype(o_ref.dtype)


def _compress(chunks, w1, w2, pe):
    _, b, g, nc, cw = chunks.shape
    return pl.pallas_call(
        functools.partial(_compress_body, nc=nc), grid=(2, b, g),
        in_specs=[pl.BlockSpec((None, None, None, nc, cw), lambda s, i, j: (s, i, j, 0, 0)),
                  pl.BlockSpec((None, 2 * cw, HEAD_DIM), lambda s, i, j: (s, 0, 0)),
                  pl.BlockSpec((None, HEAD_DIM, HEAD_DIM), lambda s, i, j: (s, 0, 0)),
                  pl.BlockSpec((None, 8, 2 * cw), lambda s, i, j: (s, 0, 0))],
        out_specs=pl.BlockSpec((None, None, None, nc, HEAD_DIM), lambda s, i, j: (s, i, j, 0, 0)),
        out_shape=jax.ShapeDtypeStruct((2, b, g, nc, HEAD_DIM), BF16),
        compiler_params=_cparams(("parallel", "parallel", "parallel")))(chunks, w1, w2, pe)


def _nsa_cmp_body(q_ref, k_ref, v_ref, m_ref, o_ref, bias_ref, *, tq, nc, nsp, topk):
    t = pl.program_id(2) * tq + lax.broadcasted_iota(jnp.int32, (tq, 1), 0)
    n = lax.broadcasted_iota(jnp.int32, (1, nc), 1)
    vis = (n * NSA_CMP_STRIDE + (NSA_CMP_LEN - 1)) <= t
    any_vis = (t >= NSA_CMP_LEN - 1).astype(F32)
    k = k_ref[...]
    v = v_ref[...]
    psum = jnp.zeros((tq, nc), F32)
    for h in range(NSA_GROUP_HEADS):
        sl = slice(h * HEAD_DIM, (h + 1) * HEAD_DIM)
        s = lax.dot_general(q_ref[:, sl], k, (((1,), (1,)), ((), ())), preferred_element_type=F32)
        s = jnp.where(vis, s, NEG_INF)
        e = jnp.exp2(s - jnp.max(s, axis=-1, keepdims=True))
        p = e * (any_vis / jnp.sum(e, axis=-1, keepdims=True))
        o_ref[:, sl] = jnp.dot(p.astype(BF16), v, preferred_element_type=F32).astype(o_ref.dtype)
        psum = psum + p
    hi = psum.astype(BF16)
    lo = (psum - hi.astype(F32)).astype(BF16)
    imp = (jnp.dot(hi, m_ref[...], preferred_element_type=F32)
           + jnp.dot(lo, m_ref[...], preferred_element_type=F32))

    blk = lax.broadcasted_iota(jnp.int32, (1, nsp), 1)
    blkf = blk.astype(F32)
    cur = lax.shift_right_logical(t, 6)
    forced = (blk == 0) | (blk == cur) | (blk == cur - 1)
    causal = blk * NSA_SEL_LEN <= t
    score = jnp.where(forced, PICKED, jnp.where(causal, imp, NEG_INF))

    def take_one(_, score):
        mx = jnp.max(score, axis=-1, keepdims=True)
        first = jnp.min(jnp.where(score == mx, blkf, float(nsp)), axis=-1, keepdims=True)
        return jnp.where(blkf == first, PICKED, score)

    score = lax.fori_loop(0, topk - 3, take_one, score)
    sel = (score == PICKED) & causal
    bias_ref[...] = jnp.where(sel, 0.0, NEG_INF).astype(bias_ref.dtype)


def _nsa_cmp(z0, kv_cmp, ovl, b, seq, tq=512):
    nc = kv_cmp.shape[3]
    nsp = ovl.shape[1]
    tq = min(tq, seq)
    nq = seq // tq
    gw = NSA_GROUP_HEADS * HEAD_DIM
    topk = min(NSA_SEL_TOPK, seq // NSA_SEL_LEN)
    assert topk >= 3
    body = functools.partial(_nsa_cmp_body, tq=tq, nc=nc, nsp=nsp, topk=topk)
    return pl.pallas_call(
        body, grid=(b, NSA_KV_GROUPS, nq),
        in_specs=[pl.BlockSpec((tq, gw), lambda bi, g, i: (bi * nq + i, g)),
                  pl.BlockSpec((None, None, None, nc, HEAD_DIM), lambda bi, g, i: (0, bi, g, 0, 0)),
                  pl.BlockSpec((None, None, None, nc, HEAD_DIM), lambda bi, g, i: (1, bi, g, 0, 0)),
                  pl.BlockSpec((nc, nsp), lambda bi, g, i: (0, 0))],
        out_specs=[pl.BlockSpec((tq, gw), lambda bi, g, i: (bi * nq + i, g)),
                   pl.BlockSpec((None, None, tq, nsp), lambda bi, g, i: (bi, g, i, 0))],
        out_shape=[jax.ShapeDtypeStruct((b * seq, NSA_HEADS * HEAD_DIM), BF16),
                   jax.ShapeDtypeStruct((b, NSA_KV_GROUPS, seq, nsp), BF16)],
        compiler_params=_cparams(("parallel", "parallel", "parallel")))(z0, kv_cmp, kv_cmp, ovl)


def _lane_tile(x, n):
    return jnp.concatenate([x] * n, axis=1) if n > 1 else x


def _softmax_pv(s, v):
    tq, tk = s.shape
    d = v.shape[1]
    m = jnp.broadcast_to(jnp.max(s, axis=-1, keepdims=True), (tq, LANES))
    p = jnp.exp2((s - _lane_tile(m, tk // LANES)).astype(BF16))
    acc = jnp.dot(p, jnp.concatenate([v, jnp.ones((tk, LANES), BF16)], axis=1), preferred_element_type=F32)
    return acc[:, :d], acc[:, d:], m


WIN_SUB = 256
DIL_SUB = 128


def _band_subtiles(i, tq, w, lo, hi, sub):
    ts = min(sub, tq)
    qq = lax.broadcasted_iota(jnp.int32, (ts, 1), 0)
    kk = lax.broadcasted_iota(jnp.int32, (1, ts + w), 1)
    rel = qq + w - kk
    band = (rel >= lo) & (rel <= hi)
    out = []
    for a in range(tq // ts):
        r0 = a * ts
        if r0 < w:
            mask = band & ((kk >= w - r0) | (i > 0))
            take = lambda p, c, lanes, r0=r0: jnp.concatenate([p[r0:, lanes], c[:r0 + ts, lanes]], axis=0)
        else:
            mask = band
            take = lambda p, c, lanes, r0=r0: c[r0 - w:r0 + ts, lanes]
        out.append((r0, ts, mask, take))
    return out


def _nsa_win_body(q_ref, kp_ref, kc_ref, vp_ref, vc_ref, o_ref, *, tq):
    i = pl.program_id(2)
    tiles = _band_subtiles(i, tq, NSA_WINDOW, 0, NSA_WINDOW - 1, WIN_SUB)
    for r0, ts, mask, take in tiles:
        k = take(kp_ref, kc_ref, slice(None))
        v = take(vp_ref, vc_ref, slice(None))
        scores = [jnp.where(mask, lax.dot_general(q_ref[r0:r0 + ts, h * HEAD_DIM:(h + 1) * HEAD_DIM], k,
                                                  (((1,), (1,)), ((), ())), preferred_element_type=F32),
                            NEG_INF) for h in range(NSA_GROUP_HEADS)]
        for h in range(NSA_GROUP_HEADS):
            num, den, _ = _softmax_pv(scores[h], v)
            o_ref[r0:r0 + ts, h * HEAD_DIM:(h + 1) * HEAD_DIM] = (num / den).astype(o_ref.dtype)


def _nsa_win(z0, b, seq):
    tq = NSA_WINDOW
    assert seq % tq == 0
    nq = seq // tq
    gw = NSA_GROUP_HEADS * HEAD_DIM
    kcol, vcol = Z0_KWIN // HEAD_DIM, Z0_VWIN // HEAD_DIM
    prev = lambda bi, i: bi * nq + jnp.maximum(i - 1, 0)
    return pl.pallas_call(
        functools.partial(_nsa_win_body, tq=tq), grid=(b, NSA_KV_GROUPS, nq),
        in_specs=[pl.BlockSpec((tq, gw), lambda bi, g, i: (bi * nq + i, g)),
                  pl.BlockSpec((tq, HEAD_DIM), lambda bi, g, i: (prev(bi, i), kcol + g)),
                  pl.BlockSpec((tq, HEAD_DIM), lambda bi, g, i: (bi * nq + i, kcol + g)),
                  pl.BlockSpec((tq, HEAD_DIM), lambda bi, g, i: (prev(bi, i), vcol + g)),
                  pl.BlockSpec((tq, HEAD_DIM), lambda bi, g, i: (bi * nq + i, vcol + g))],
        out_specs=pl.BlockSpec((tq, gw), lambda bi, g, i: (bi * nq + i, g)),
        out_shape=jax.ShapeDtypeStruct((b * seq, NSA_HEADS * HEAD_DIM), BF16),
        compiler_params=_cparams(("parallel", "parallel", "parallel")))(z0, z0, z0, z0, z0)


def _tri_schedule(nq):
    qi = np.concatenate([np.full(i + 1, i) for i in range(nq)]).astype(np.int32)
    kj = np.concatenate([np.arange(i + 1) for i in range(nq)]).astype(np.int32)
    return jnp.asarray(qi), jnp.asarray(kj)


def _flash_init(m_scr, acc_scr):
    m_scr[...] = jnp.full(m_scr.shape, NEG_INF, F32)
    acc_scr[...] = jnp.zeros(acc_scr.shape, F32)


def _flash_update(scores, vs, m_scr, acc_scr):
    nh = len(scores)
    reps = scores[0].shape[1] // LANES
    ones = jnp.ones((vs[0].shape[0], LANES), BF16)
    m_prev = [m_scr[h] for h in range(nh)]
    m_new = [jnp.maximum(m_prev[h], jnp.max(scores[h], axis=-1, keepdims=True)) for h in range(nh)]
    ps = [jnp.exp2((scores[h] - _lane_tile(m_new[h], reps)).astype(BF16)) for h in range(nh)]
    alphas = [jnp.exp2(m_prev[h] - m_new[h]) for h in range(nh)]
    for h in range(nh):
        v_aug = jnp.concatenate([vs[h], ones], axis=1)
        acc_scr[h] = (_lane_tile(alphas[h], acc_scr.shape[2] // LANES) * acc_scr[h]
                      + jnp.dot(ps[h], v_aug, preferred_element_type=F32))
        m_scr[h] = m_new[h]


def _flash_out(acc_scr, h, d):
    acc = acc_scr[h]
    return acc[:, :d] / acc[:, d:]


def _flash_tile(qi, kj, tq, scores_and_values, m_scr, acc_scr, write_out):
    @pl.when(kj < qi)
    def _():
        scores, vs = scores_and_values()
        _flash_update(scores, vs, m_scr, acc_scr)

    @pl.when(kj == qi)
    def _():
        scores, vs = scores_and_values()
        row = lax.broadcasted_iota(jnp.int32, (tq, 1), 0)
        col = lax.broadcasted_iota(jnp.int32, (1, tq), 1)
        _flash_update([jnp.where(col <= row, s, NEG_INF) for s in scores], vs, m_scr, acc_scr)
        write_out()


def _nsa_sel_body(qi_ref, kj_ref, q_ref, k_ref, v_ref, oh_ref, bias_ref, o_ref,
                  qa_scr, m_scr, acc_scr, *, tq, per_slab):
    step = pl.program_id(2)
    qi = qi_ref[step]
    kj = kj_ref[step]

    @pl.when(kj == 0)
    def _():
        _flash_init(m_scr, acc_scr)
        for h in range(NSA_GROUP_HEADS):
            qa_scr[h, :, :HEAD_DIM] = q_ref[:, h * HEAD_DIM:(h + 1) * HEAD_DIM]

    @pl.when(kj % per_slab == 0)
    def _():
        for h in range(NSA_GROUP_HEADS):
            qa_scr[h, :, HEAD_DIM:] = bias_ref[...]

    def scores_and_values():
        k = jnp.concatenate([k_ref[...], oh_ref[...]], axis=1)
        scores = [lax.dot_general(qa_scr[h], k, (((1,), (1,)), ((), ())), preferred_element_type=F32)
                  for h in range(NSA_GROUP_HEADS)]
        return scores, [v_ref[...]] * NSA_GROUP_HEADS

    def write_out():
        for h in range(NSA_GROUP_HEADS):
            o_ref[:, h * HEAD_DIM:(h + 1) * HEAD_DIM] = _flash_out(acc_scr, h, HEAD_DIM).astype(o_ref.dtype)

    _flash_tile(qi, kj, tq, scores_and_values, m_scr, acc_scr, write_out)


def _nsa_sel(z0, bias, onehot, b, seq, tq=1024):
    tq = min(tq, seq)
    nq = seq // tq
    gw = NSA_GROUP_HEADS * HEAD_DIM
    kcol, vcol = Z0_KSLC // HEAD_DIM, Z0_VSLC // HEAD_DIM
    per_slab = max(LANES * NSA_SEL_LEN // tq, 1)
    qi, kj = _tri_schedule(nq)
    grid_spec = pltpu.PrefetchScalarGridSpec(
        num_scalar_prefetch=2, grid=(b, NSA_KV_GROUPS, int(qi.shape[0])),
        in_specs=[pl.BlockSpec((tq, gw), lambda bi, g, s, qi, kj: (bi * nq + qi[s], g)),
                  pl.BlockSpec((tq, HEAD_DIM), lambda bi, g, s, qi, kj: (bi * nq + kj[s], kcol + g)),
                  pl.BlockSpec((tq, HEAD_DIM), lambda bi, g, s, qi, kj: (bi * nq + kj[s], vcol + g)),
                  pl.BlockSpec((tq, LANES), lambda bi, g, s, qi, kj: (kj[s], 0)),
                  pl.BlockSpec((None, None, tq, LANES),
                               lambda bi, g, s, qi, kj: (bi, g, qi[s], kj[s] // per_slab))],
        out_specs=pl.BlockSpec((tq, gw), lambda bi, g, s, qi, kj: (bi * nq + qi[s], g)),
        scratch_shapes=[pltpu.VMEM((NSA_GROUP_HEADS, tq, 2 * HEAD_DIM), BF16),
                        pltpu.VMEM((NSA_GROUP_HEADS, tq, LANES), F32),
                        pltpu.VMEM((NSA_GROUP_HEADS, tq, HEAD_DIM + LANES), F32)])
    return pl.pallas_call(
        functools.partial(_nsa_sel_body, tq=tq, per_slab=per_slab), grid_spec=grid_spec,
        out_shape=jax.ShapeDtypeStruct((b * seq, NSA_HEADS * HEAD_DIM), BF16),
        compiler_params=_cparams(("parallel", "parallel", "arbitrary")))(qi, kj, z0, z0, z0, onehot, bias)


MLA_STEP_HEADS = 4


def _mla_body(qi_ref, kj_ref, q_ref, kn_ref, kp_ref, v_ref, o_ref, m_scr, acc_scr, *, tq):
    step = pl.program_id(2)
    qi = qi_ref[step]
    kj = kj_ref[step]

    @pl.when(kj == 0)
    def _():
        _flash_init(m_scr, acc_scr)

    def scores_and_values():
        kp = kp_ref[...]
        scores, vs = [], []
        for h in range(MLA_STEP_HEADS):
            k = jnp.concatenate([kn_ref[:, h * MLA_NOPE:(h + 1) * MLA_NOPE], kp], axis=1)
            scores.append(lax.dot_general(q_ref[:, h * 2 * LANES:(h + 1) * 2 * LANES], k,
                                          (((1,), (1,)), ((), ())), preferred_element_type=F32))
            vs.append(v_ref[:, h * MLA_V:(h + 1) * MLA_V])
        return scores, vs

    def write_out():
        for h in range(MLA_STEP_HEADS):
            o_ref[:, h * MLA_V:(h + 1) * MLA_V] = _flash_out(acc_scr, h, MLA_V).astype(o_ref.dtype)

    _flash_tile(qi, kj, tq, scores_and_values, m_scr, acc_scr, write_out)


def _mla(qcat, kv, z0, b, seq, tq=1024):
    tq = min(tq, seq)
    nq = seq // tq
    nh = MLA_STEP_HEADS
    ngrp = MLA_HEADS // nh
    kpcol = Z0_KROPE // LANES
    qi, kj = _tri_schedule(nq)
    grid_spec = pltpu.PrefetchScalarGridSpec(
        num_scalar_prefetch=2, grid=(b, ngrp, int(qi.shape[0])),
        in_specs=[pl.BlockSpec((tq, nh * 2 * LANES), lambda bi, h, s, qi, kj: (bi * nq + qi[s], h)),
                  pl.BlockSpec((tq, nh * MLA_NOPE), lambda bi, h, s, qi, kj: (bi * nq + kj[s], h)),
                  pl.BlockSpec((tq, LANES), lambda bi, h, s, qi, kj: (bi * nq + kj[s], kpcol)),
                  pl.BlockSpec((tq, nh * MLA_V), lambda bi, h, s, qi, kj: (bi * nq + kj[s], ngrp + h))],
        out_specs=pl.BlockSpec((tq, nh * MLA_V), lambda bi, h, s, qi, kj: (bi * nq + qi[s], h)),
        scratch_shapes=[pltpu.VMEM((nh, tq, LANES), F32), pltpu.VMEM((nh, tq, MLA_V + LANES), F32)])
    return pl.pallas_call(
        functools.partial(_mla_body, tq=tq), grid_spec=grid_spec,
        out_shape=jax.ShapeDtypeStruct((b * seq, MLA_HEADS * MLA_V), BF16),
        compiler_params=_cparams(("parallel", "parallel", "arbitrary")))(qi, kj, qcat, kv, z0, kv)


def _dil_body(q_ref, kp_ref, kc_ref, vp_ref, vc_ref, o_ref, lse_ref, *, tq):
    i = pl.program_id(2)
    for r0, ts, mask, take in _band_subtiles(i, tq, DIL_SPAN, 0, DIL_SPAN, DIL_SUB):
        lane = lax.broadcasted_iota(jnp.int32, (ts, LANES), 1)
        lse_all = jnp.zeros((ts, LANES), F32)
        for h in range(DIL_HEADS):
            sl = slice(h * HEAD_DIM, (h + 1) * HEAD_DIM)
            s = lax.dot_general(q_ref[r0:r0 + ts, sl], take(kp_ref, kc_ref, sl), (((1,), (1,)), ((), ())),
                                preferred_element_type=F32)
            num, den, m = _softmax_pv(jnp.where(mask, s, NEG_INF), take(vp_ref, vc_ref, sl))
            o_ref[r0:r0 + ts, sl] = (num / den).astype(o_ref.dtype)
            lse_all = jnp.where(lane == h, m * LN2 + jnp.log(den), lse_all)
        lse_ref[r0:r0 + ts, :] = lse_all


def _dilated(zv, r, b, seq, tq=512):
    sub = seq // r
    tq = min(tq, sub)
    assert sub % tq == 0 and tq % DIL_SPAN == 0
    nq = sub // tq
    hw = DIL_HEADS * HEAD_DIM
    ratio = tq // DIL_SPAN
    prev = lambda i: jnp.maximum(i * ratio - 1, 0)
    o, lse = pl.pallas_call(
        functools.partial(_dil_body, tq=tq), grid=(b, r, nq),
        in_specs=[pl.BlockSpec((None, tq, hw), lambda bi, c, i: (bi, i, c)),
                  pl.BlockSpec((None, DIL_SPAN, hw), lambda bi, c, i: (bi, prev(i), r + c)),
                  pl.BlockSpec((None, tq, hw), lambda bi, c, i: (bi, i, r + c)),
                  pl.BlockSpec((None, DIL_SPAN, hw), lambda bi, c, i: (bi, prev(i), 2 * r + c)),
                  pl.BlockSpec((None, tq, hw), lambda bi, c, i: (bi, i, 2 * r + c))],
        out_specs=[pl.BlockSpec((None, tq, hw), lambda bi, c, i: (bi, i, c)),
                   pl.BlockSpec((None, tq, LANES), lambda bi, c, i: (bi, i, c))],
        out_shape=[jax.ShapeDtypeStruct((b, sub, r * hw), BF16),
                   jax.ShapeDtypeStruct((b, sub, r * LANES), F32)],
        compiler_params=_cparams(("parallel", "parallel", "parallel")))(zv, zv, zv, zv, zv)
    return o.reshape(b * seq, hw), lse.reshape(b * seq, LANES)


def _finish(h_ref, m, g_ref, o_ref):
    y = m * lax.rsqrt(jnp.mean(m * m, axis=-1, keepdims=True) + EPS)
    o_ref[...] = h_ref[...] + y * g_ref[...]


def _out0_body(h_ref, oc_ref, os_ref, ow_ref, gate_ref, ob_ref, wa_ref, wb_ref, g_ref, o_ref):
    gate = jax.nn.sigmoid(gate_ref[...].astype(F32))
    parts = []
    for h in range(NSA_HEADS):
        sl = slice(h * HEAD_DIM, (h + 1) * HEAD_DIM)
        parts.append(gate[:, 3 * h:3 * h + 1] * oc_ref[:, sl].astype(F32)
                     + gate[:, 3 * h + 1:3 * h + 2] * os_ref[:, sl].astype(F32)
                     + gate[:, 3 * h + 2:3 * h + 3] * ow_ref[:, sl].astype(F32))
    oa = jnp.concatenate(parts, axis=1).astype(BF16)
    m = (jnp.dot(oa, wa_ref[...], preferred_element_type=F32)
         + jnp.dot(ob_ref[...], wb_ref[...], preferred_element_type=F32))
    _finish(h_ref, m, g_ref, o_ref)


def _out1_body(h_ref, o0_ref, o1_ref, o2_ref, l0_ref, l1_ref, l2_ref, w_ref, g_ref, o_ref):
    l0, l1, l2 = l0_ref[...], l1_ref[...], l2_ref[...]
    mx = jnp.maximum(jnp.maximum(l0, l1), l2)
    e0, e1, e2 = jnp.exp(l0 - mx), jnp.exp(l1 - mx), jnp.exp(l2 - mx)
    tot = e0 + e1 + e2
    a0, a1, a2 = e0 / tot, e1 / tot, e2 / tot
    parts = []
    for h in range(DIL_HEADS):
        sl = slice(h * HEAD_DIM, (h + 1) * HEAD_DIM)
        parts.append(a0[:, h:h + 1] * o0_ref[:, sl].astype(F32)
                     + a1[:, h:h + 1] * o1_ref[:, sl].astype(F32)
                     + a2[:, h:h + 1] * o2_ref[:, sl].astype(F32))
    o = jnp.concatenate(parts, axis=1).astype(BF16)
    _finish(h_ref, jnp.dot(o, w_ref[...], preferred_element_type=F32), g_ref, o_ref)


def _row_spec(tm, w, col=0):
    return pl.BlockSpec((tm, w), lambda i: (i, col))


def _full_spec(shape):
    return pl.BlockSpec(shape, lambda i: (0,) * len(shape))


def _out0(h, o_c, o_s, o_w, z0, o_b, wa, wb, g, tm=256):
    t, d = h.shape
    tm = min(tm, t)
    ow = NSA_HEADS * HEAD_DIM
    return pl.pallas_call(
        _out0_body, grid=(t // tm,),
        in_specs=[_row_spec(tm, d), _row_spec(tm, ow), _row_spec(tm, ow), _row_spec(tm, ow),
                  _row_spec(tm, LANES, Z0_GATE // LANES), _row_spec(tm, ow),
                  _full_spec(wa.shape), _full_spec(wb.shape), _full_spec((1, d))],
        out_specs=_row_spec(tm, d), out_shape=jax.ShapeDtypeStruct((t, d), F32),
        compiler_params=_cparams(("parallel",)))(h, o_c, o_s, o_w, z0, o_b, wa, wb, g.reshape(1, d))


def _out1(h, os_, lses, w, g, tm=256):
    t, d = h.shape
    tm = min(tm, t)
    ow = DIL_HEADS * HEAD_DIM
    return pl.pallas_call(
        _out1_body, grid=(t // tm,),
        in_specs=[_row_spec(tm, d)] + [_row_spec(tm, ow)] * 3 + [_row_spec(tm, LANES)] * 3
                 + [_full_spec(w.shape), _full_spec((1, d))],
        out_specs=_row_spec(tm, d), out_shape=jax.ShapeDtypeStruct((t, d), F32),
        compiler_params=_cparams(("parallel",)))(h, *os_, *lses, w, g.reshape(1, d))


def _mlp_body(h_ref, g1_ref, w1_ref, w2_ref, g2_ref, o_ref, xn_ref, acc_ref):
    f = pl.program_id(1)
    last = pl.num_programs(1) - 1

    def partial_sum(xn):
        a = jnp.maximum(jnp.dot(xn, w1_ref[...], preferred_element_type=F32), 0.0)
        return jnp.dot((a * a).astype(BF16), w2_ref[...], preferred_element_type=F32)

    @pl.when(f == 0)
    def _():
        x = h_ref[...]
        y = x * lax.rsqrt(jnp.mean(x * x, axis=-1, keepdims=True) + EPS)
        xn = (y * g1_ref[...]).astype(BF16)
        xn_ref[...] = xn
        acc_ref[...] = partial_sum(xn)

    @pl.when((f > 0) & (f < last))
    def _():
        acc_ref[...] += partial_sum(xn_ref[...])

    @pl.when(f == last)
    def _():
        _finish(h_ref, acc_ref[...] + partial_sum(xn_ref[...]), g2_ref, o_ref)


def _mlp(h, g1, w1, w2, g2, tm=512, tf=1024):
    t, d = h.shape
    ff = w1.shape[1]
    tm = min(tm, t)
    return pl.pallas_call(
        _mlp_body, grid=(t // tm, ff // tf),
        in_specs=[pl.BlockSpec((tm, d), lambda i, f: (i, 0)),
                  pl.BlockSpec((1, d), lambda i, f: (0, 0)),
                  pl.BlockSpec((d, tf), lambda i, f: (0, f)),
                  pl.BlockSpec((tf, d), lambda i, f: (f, 0)),
                  pl.BlockSpec((1, d), lambda i, f: (0, 0))],
        out_specs=pl.BlockSpec((tm, d), lambda i, f: (i, 0)),
        out_shape=jax.ShapeDtypeStruct((t, d), F32),
        scratch_shapes=[pltpu.VMEM((tm, d), BF16), pltpu.VMEM((tm, d), F32)],
        compiler_params=_cparams(("parallel", "arbitrary")))(h, g1.reshape(1, d), w1, w2, g2.reshape(1, d))


def _rope64_tile(w):
    z = jnp.zeros((w.shape[0], 32), w.dtype)
    return jnp.concatenate([w[:, :32], z, w[:, 32:], z], axis=1)


def _layer0_w_in(w_in):
    d = w_in.shape[0]
    o1 = NSA_HEADS * HEAD_DIM
    o2 = o1 + 3 * 2 * NSA_KV_GROUPS * HEAD_DIM
    o3 = o2 + 3 * NSA_HEADS
    o4 = o3 + MLA_Q_RANK
    o5 = o4 + MLA_KV_RANK
    kv = w_in[:, o1:o2].reshape(d, 3, 2, NSA_KV_GROUPS * HEAD_DIM)
    kv = kv.transpose(0, 2, 1, 3).reshape(d, o2 - o1)
    gate = jnp.pad(w_in[:, o2:o3], ((0, 0), (0, LANES - (o3 - o2))))
    w = jnp.concatenate([w_in[:, :o1], kv, w_in[:, o3:o4], w_in[:, o4:o5], gate, _rope64_tile(w_in[:, o5:])], 1)
    assert w.shape[1] == Z0_COLS
    return w.astype(BF16)


def _mla_w_uq(w_uq):
    d = w_uq.shape[0]
    w = w_uq.reshape(d, MLA_HEADS, MLA_NOPE + MLA_ROPE)
    tiles = [jnp.concatenate([w[:, h, :MLA_NOPE], _rope64_tile(w[:, h, MLA_NOPE:])], 1) for h in range(MLA_HEADS)]
    return jnp.concatenate(tiles, axis=1).astype(BF16)


def _mla_w_ukv(w_ukv):
    d = w_ukv.shape[0]
    w = w_ukv.reshape(d, MLA_HEADS, MLA_NOPE + MLA_V)
    return jnp.concatenate([w[:, :, :MLA_NOPE].reshape(d, -1), w[:, :, MLA_NOPE:].reshape(d, -1)], 1).astype(BF16)


def _layer1_w_in(w_in):
    return w_in.astype(BF16)


def _overlap_matrix(nc, n_cmp, n_slc, nsp):
    ratio = NSA_SEL_LEN // NSA_CMP_STRIDE
    m = np.zeros((nc, nsp), np.float32)
    for off in range(1 - NSA_CMP_LEN // NSA_CMP_STRIDE, ratio):
        n = np.arange(n_slc) * ratio + off
        ok = (n >= 0) & (n < n_cmp)
        m[n[ok], np.arange(n_slc)[ok]] = 1.0
    return jnp.asarray(m, BF16)


def kernel(x, l0_norm_mix_pre, l0_w_in, l0_cmp_pe_k, l0_cmp_w1_k, l0_cmp_w2_k, l0_cmp_pe_v, l0_cmp_w1_v, l0_cmp_w2_v, l0_mla_q_norm, l0_mla_w_uq, l0_mla_kv_norm, l0_mla_w_ukv, l0_w_out, l0_norm_mix_post, l0_norm_ffn_pre, l0_w_ff1, l0_w_ff2, l0_norm_ffn_post, l1_norm_mix_pre, l1_w_in, l1_w_out, l1_norm_mix_post, l1_norm_ffn_pre, l1_w_ff1, l1_w_ff2, l1_norm_ffn_post):
    b, seq, d = x.shape
    t = b * seq
    assert seq % NSA_WINDOW == 0 and seq % (DIL_PATTERNS[-1][1] * DIL_SPAN) == 0
    rot = _rope_tables(seq)
    h = x.reshape(t, d)

    tabs0 = ([TAB_ROPE128_Q] * NSA_HEADS + [TAB_ROPE128] * (3 * NSA_KV_GROUPS)
             + [TAB_IDENT] * ((Z0_KROPE - Z0_VCMP) // LANES) + [TAB_ROPE64])
    z0 = _proj_res(h, 0, d, l0_norm_mix_pre, _layer0_w_in(l0_w_in), seq, tabs0, rot)

    nc = seq // NSA_CMP_STRIDE
    n_cmp = (seq - NSA_CMP_LEN) // NSA_CMP_STRIDE + 1
    n_slc = seq // NSA_SEL_LEN
    nsp = -(-n_slc // LANES) * LANES
    gd = NSA_KV_GROUPS * HEAD_DIM

    def chunks(col):
        c = z0[:, col:col + gd].reshape(b, nc, NSA_CMP_STRIDE, NSA_KV_GROUPS, HEAD_DIM)
        return c.transpose(0, 3, 1, 2, 4).reshape(b, NSA_KV_GROUPS, nc, NSA_CMP_STRIDE * HEAD_DIM)

    pe = jnp.stack([l0_cmp_pe_k.reshape(1, -1), l0_cmp_pe_v.reshape(1, -1)])
    kv_cmp = _compress(jnp.stack([chunks(Z0_KCMP), chunks(Z0_VCMP)]),
                       jnp.stack([l0_cmp_w1_k, l0_cmp_w1_v]).astype(BF16),
                       jnp.stack([l0_cmp_w2_k, l0_cmp_w2_v]).astype(BF16),
                       jnp.broadcast_to(pe, (2, 8, pe.shape[-1])).astype(F32))
    o_c, bias = _nsa_cmp(z0, kv_cmp, _overlap_matrix(nc, n_cmp, n_slc, nsp), b, seq)
    o_w = _nsa_win(z0, b, seq)
    blk_lane = (jnp.arange(seq, dtype=jnp.int32) // NSA_SEL_LEN) % LANES
    onehot = (blk_lane[:, None] == jnp.arange(LANES, dtype=jnp.int32)[None, :]).astype(BF16)
    o_s = _nsa_sel(z0, bias, onehot, b, seq)

    qcat = _proj_res(z0, Z0_CQ // MLA_Q_RANK, MLA_Q_RANK, l0_mla_q_norm, _mla_w_uq(l0_mla_w_uq), seq,
                     [TAB_IDENT_Q, TAB_ROPE64_Q] * MLA_HEADS, rot)
    kv = _proj_res(z0, Z0_CKV // MLA_KV_RANK, MLA_KV_RANK, l0_mla_kv_norm, _mla_w_ukv(l0_mla_w_ukv), seq,
                   [TAB_IDENT] * (2 * MLA_HEADS), rot)
    o_b = _mla(qcat, kv, z0, b, seq)

    ow = NSA_HEADS * HEAD_DIM
    w_out0 = l0_w_out.astype(BF16)
    h = _out0(h, o_c, o_s, o_w, z0, o_b, w_out0[:ow], w_out0[ow:], l0_norm_mix_post)
    h = _mlp(h, l0_norm_ffn_pre, l0_w_ff1.astype(BF16), l0_w_ff2.astype(BF16), l0_norm_ffn_post)

    hw = DIL_HEADS * HEAD_DIM
    tabs_g = ([TAB_ROPE128_Q] * DIL_HEADS + [TAB_ROPE128] * DIL_HEADS
              + [TAB_IDENT] * DIL_HEADS)
    w1_in = _layer1_w_in(l1_w_in)
    outs = []
    for p, (_, r) in enumerate(DIL_PATTERNS):
        w_p = w1_in[:, 3 * hw * p:3 * hw * (p + 1)]
        z1 = _proj_res(h, 0, d, l1_norm_mix_pre, w_p, seq, tabs_g, rot, dil=r)
        outs.append(_dilated(z1.reshape(b, seq // r, 3 * r * hw), r, b, seq))
    h = _out1(h, [o for o, _ in outs], [l for _, l in outs], l1_w_out.astype(BF16), l1_norm_mix_post)
    h = _mlp(h, l1_norm_ffn_pre, l1_w_ff1.astype(BF16), l1_w_ff2.astype(BF16), l1_norm_ffn_post)
    return h.reshape(b, seq, d)
```

```python
import functools

import numpy as np
import jax
import jax.numpy as jnp
from jax import lax
from jax.experimental import pallas as pl
from jax.experimental.pallas import tpu as pltpu

F32 = jnp.float32
BF16 = jnp.bfloat16

HEAD_DIM = 128
LANES = 128
ROPE_THETA = 10000.0
EPS = 1e-6
NEG_INF = -1e30
POS_INF = 1e30
PICKED = -3e38

NSA_HEADS = 8
NSA_KV_GROUPS = 2
NSA_GROUP_HEADS = NSA_HEADS // NSA_KV_GROUPS
NSA_CMP_LEN = 32
NSA_CMP_STRIDE = 16
NSA_SEL_LEN = 64
NSA_SEL_TOPK = 16
NSA_WINDOW = 512

MLA_HEADS = 8
MLA_Q_RANK = 512
MLA_KV_RANK = 256
MLA_NOPE = 128
MLA_ROPE = 64
MLA_V = 128

DIL_PATTERNS = ((128, 1), (512, 4), (2048, 16))
DIL_HEADS = 8
DIL_SPAN = 128

VMEM_LIMIT = 56 * 1024 * 1024
MAX_ROW_STRIDE = 4

Z0_Q = 0
Z0_KCMP = 1024
Z0_KSLC = 1280
Z0_KWIN = 1536
Z0_VCMP = 1792
Z0_VSLC = 2048
Z0_VWIN = 2304
Z0_CQ = 2560
Z0_CKV = 3072
Z0_GATE = 3328
Z0_KROPE = 3456
Z0_COLS = 3584

TAB_ROPE128, TAB_IDENT, TAB_ROPE64, TAB_ROPE128_Q, TAB_ROPE64_Q, TAB_IDENT_Q = 0, 1, 2, 3, 4, 5

LOG2E = 1.4426950408889634
LN2 = 0.6931471805599453
QSCALE_128 = HEAD_DIM ** -0.5 * LOG2E
QSCALE_MLA = (MLA_NOPE + MLA_ROPE) ** -0.5 * LOG2E


def _cparams(sem):
    return pltpu.CompilerParams(dimension_semantics=sem, vmem_limit_bytes=VMEM_LIMIT)


def _rope_tables(seq):
    def cs(dim):
        inv = 1.0 / (ROPE_THETA ** (jnp.arange(0, dim, 2, dtype=F32) / dim))
        ang = jnp.arange(seq, dtype=F32)[:, None] * inv[None, :]
        return jnp.cos(ang), jnp.sin(ang)

    c128, s128 = cs(HEAD_DIM)
    c64, s64 = cs(MLA_ROPE)
    one = jnp.ones((seq, 32), F32)
    zero = jnp.zeros((seq, 32), F32)
    ident_c = jnp.ones((seq, LANES), F32)
    ident_s = jnp.zeros((seq, LANES), F32)
    r128_c = jnp.concatenate([c128, c128], axis=1)
    r128_s = jnp.concatenate([-s128, s128], axis=1)
    r64_c = jnp.concatenate([c64, one, c64, one], axis=1)
    r64_s = jnp.concatenate([-s64, zero, s64, zero], axis=1)
    cos = [r128_c, ident_c, r64_c, r128_c * QSCALE_128, r64_c * QSCALE_MLA, ident_c * QSCALE_MLA]
    sin = [r128_s, ident_s, r64_s, r128_s * QSCALE_128, r64_s * QSCALE_MLA, ident_s]
    return jnp.concatenate([jnp.stack(cos), jnp.stack(sin)], axis=2)


def _proj_res_body(*refs, kinds, tn, scale_q, dil):
    x_ref, g_ref, w_ref = refs[:3]
    used = sorted({k for k in kinds if k not in (TAB_IDENT, TAB_IDENT_Q)})
    t_refs = dict(zip(used, refs[3:3 + len(used)]))
    o_ref, xn_ref = refs[3 + len(used):5 + len(used)]
    stage_refs = refs[5 + len(used):]
    tm = x_ref.shape[0]
    x = x_ref[...].astype(F32)
    y = x * lax.rsqrt(jnp.mean(x * x, axis=-1, keepdims=True) + EPS)
    xn_ref[...] = (y * g_ref[...]).astype(BF16)
    nl = tn // LANES
    for j in range(w_ref.shape[1] // tn):
        acc = jnp.dot(xn_ref[...], w_ref[:, j * tn:(j + 1) * tn], preferred_element_type=F32)
        parts = []
        for k in range(nl):
            kind = kinds[j * nl + k]
            a = acc[:, k * LANES:(k + 1) * LANES]
            if kind == TAB_IDENT:
                parts.append(a)
            elif kind == TAB_IDENT_Q:
                parts.append(a * scale_q)
            else:
                tab = t_refs[kind]
                parts.append(a * tab[:, :LANES] + pltpu.roll(a, LANES // 2, 1) * tab[:, LANES:])
        if dil == 1:
            o_ref[:, j * tn:(j + 1) * tn] = jnp.concatenate(parts, axis=1).astype(o_ref.dtype)
            continue
        hw = DIL_HEADS * HEAD_DIM
        s1 = min(dil, MAX_ROW_STRIDE)
        s2 = dil // s1
        for k in range(nl):
            g = j * nl + k
            first, second = stage_refs[2 * k], stage_refs[2 * k + 1]
            first[...] = parts[k]
            if s2 > 1:
                for c1 in range(s1):
                    second[c1 * (tm // s1):(c1 + 1) * (tm // s1), :] = first[pl.ds(c1, tm // s1, stride=s1), :]
            for c in range(dil):
                c1, c2 = c % s1, c // s1
                if s2 > 1:
                    piece = second[pl.ds(c1 * (tm // s1) + c2, tm // dil, stride=s2), :]
                else:
                    piece = first[pl.ds(c, tm // dil, stride=dil), :]
                col = ((g // DIL_HEADS) * dil + c) * hw + (g % DIL_HEADS) * LANES
                o_ref[:, col:col + LANES] = piece.astype(o_ref.dtype)


def _proj_res(x, x_col, d, g, w, seq, kinds, rot, tm=512, tn=512, dil=1):
    t = x.shape[0]
    n = w.shape[1]
    tm = min(tm, seq)
    assert t % tm == 0 and seq % tm == 0 and n % tn == 0 and len(kinds) == n // LANES and tm % (16 * dil) == 0
    spb = seq // tm
    used = sorted({k for k in kinds if k not in (TAB_IDENT, TAB_IDENT_Q)})
    tab_specs = [pl.BlockSpec((None, tm, 2 * LANES), lambda i, kind=kind: (kind, i % spb, 0)) for kind in used]
    scratch = [pltpu.VMEM((tm, d), BF16)]
    if dil > 1:
        scratch += [pltpu.VMEM((tm, LANES), F32)] * (2 * (tn // LANES))
    return pl.pallas_call(
        functools.partial(_proj_res_body, kinds=tuple(kinds), tn=tn, scale_q=QSCALE_MLA, dil=dil),
        grid=(t // tm,),
        in_specs=[pl.BlockSpec((tm, d), lambda i: (i, x_col)),
                  pl.BlockSpec((1, d), lambda i: (0, 0)),
                  pl.BlockSpec((d, n), lambda i: (0, 0), pipeline_mode=pl.Buffered(1))] + tab_specs,
        out_specs=pl.BlockSpec((tm // dil, dil * n), lambda i: (i, 0)),
        scratch_shapes=scratch,
        out_shape=jax.ShapeDtypeStruct((t // dil, dil * n), BF16),
        compiler_params=_cparams(("parallel",)))(x, g.reshape(1, d).astype(F32), w, *([rot] * len(used)))


def _compress_body(c_ref, w1_ref, w2_ref, pe_ref, o_ref, *, nc):
    half = NSA_CMP_STRIDE * HEAD_DIM
    c = c_ref[...]
    a = jnp.dot(c, w1_ref[:half, :], preferred_element_type=F32)
    b = jnp.dot(c, w1_ref[half:, :], preferred_element_type=F32)
    pe = pe_ref[...]
    pe_hi = pe.astype(BF16)
    pe_lo = (pe - pe_hi.astype(F32)).astype(BF16)
    pe_term = (jnp.dot(pe_hi, w1_ref[...], preferred_element_type=F32)
               + jnp.dot(pe_lo, w1_ref[...], preferred_element_type=F32))
    hid = a + pltpu.roll(b, nc - 1, 0) + pe_term[0:1, :]
    act = jax.nn.gelu(hid)
    o_ref[...] = jnp.dot(act.astype(BF16), w2_ref[...], preferred_element_type=F32).astype(o_ref.dtype)


def _compress(chunks, w1, w2, pe):
    _, b, g, nc, cw = chunks.shape
    return pl.pallas_call(
        functools.partial(_compress_body, nc=nc), grid=(2, b, g),
        in_specs=[pl.BlockSpec((None, None, None, nc, cw), lambda s, i, j: (s, i, j, 0, 0)),
                  pl.BlockSpec((None, 2 * cw, HEAD_DIM), lambda s, i, j: (s, 0, 0)),
                  pl.BlockSpec((None, HEAD_DIM, HEAD_DIM), lambda s, i, j: (s, 0, 0)),
                  pl.BlockSpec((None, 8, 2 * cw), lambda s, i, j: (s, 0, 0))],
        out_specs=pl.BlockSpec((None, None, None, nc, HEAD_DIM), lambda s, i, j: (s, i, j, 0, 0)),
        out_shape=jax.ShapeDtypeStruct((2, b, g, nc, HEAD_DIM), BF16),
        compiler_params=_cparams(("parallel", "parallel", "parallel")))(chunks, w1, w2, pe)


def _nsa_cmp_body(q_ref, k_ref, v_ref, m_ref, o_ref, bias_ref, imp_scr, *, tq, nc, nsp, topk, nvar,
                  tiles_per_var):
    i = pl.program_id(2)
    t = i * tq + lax.broadcasted_iota(jnp.int32, (tq, 1), 0)
    any_vis = (t >= NSA_CMP_LEN - 1).astype(F32)

    def attend(ncols):
        n = lax.broadcasted_iota(jnp.int32, (1, ncols), 1)
        vis = (n * NSA_CMP_STRIDE + (NSA_CMP_LEN - 1)) <= t
        k = k_ref[:ncols, :]
        v = v_ref[:ncols, :]
        psum = jnp.zeros((tq, ncols), F32)
        for h in range(NSA_GROUP_HEADS):
            sl = slice(h * HEAD_DIM, (h + 1) * HEAD_DIM)
            s = lax.dot_general(q_ref[:, sl], k, (((1,), (1,)), ((), ())), preferred_element_type=F32)
            s = jnp.where(vis, s, NEG_INF)
            e = jnp.exp2(s - jnp.max(s, axis=-1, keepdims=True))
            p = e * (any_vis / jnp.sum(e, axis=-1, keepdims=True))
            o_ref[:, sl] = jnp.dot(p.astype(BF16), v, preferred_element_type=F32).astype(o_ref.dtype)
            psum = psum + p
        hi = psum.astype(BF16)
        lo = (psum - hi.astype(F32)).astype(BF16)
        imp_scr[...] = (jnp.dot(hi, m_ref[:ncols, :], preferred_element_type=F32)
                        + jnp.dot(lo, m_ref[:ncols, :], preferred_element_type=F32))

    for var in range(nvar):
        @pl.when((i >= var * tiles_per_var) & (i < (var + 1) * tiles_per_var))
        def _(var=var):
            attend((var + 1) * nc // nvar)

    imp = imp_scr[...]
    blk = lax.broadcasted_iota(jnp.int32, (1, nsp), 1)
    blkf = blk.astype(F32)
    cur = lax.shift_right_logical(t, 6)
    forced = (blk == 0) | (blk == cur) | (blk == cur - 1)
    causal = blk * NSA_SEL_LEN <= t
    score = jnp.where(forced, PICKED, jnp.where(causal, imp, NEG_INF))

    def take_one(_, score):
        mx = jnp.max(score, axis=-1, keepdims=True)
        first = jnp.min(jnp.where(score == mx, blkf, float(nsp)), axis=-1, keepdims=True)
        return jnp.where(blkf == first, PICKED, score)

    score = lax.fori_loop(0, topk - 3, take_one, score)
    sel = (score == PICKED) & causal
    bias_ref[...] = jnp.where(sel, 0.0, NEG_INF).astype(bias_ref.dtype)


def _nsa_cmp(z0, kv_cmp, ovl, b, seq, tq=512):
    nc = kv_cmp.shape[3]
    nsp = ovl.shape[1]
    tq = min(tq, seq)
    nq = seq // tq
    gw = NSA_GROUP_HEADS * HEAD_DIM
    topk = min(NSA_SEL_TOPK, seq // NSA_SEL_LEN)
    assert topk >= 3
    nvar = 4 if (nq % 4 == 0 and nc % (4 * 2 * LANES) == 0) else 1
    body = functools.partial(_nsa_cmp_body, tq=tq, nc=nc, nsp=nsp, topk=topk, nvar=nvar, tiles_per_var=nq // nvar)
    return pl.pallas_call(
        body, grid=(b, NSA_KV_GROUPS, nq),
        scratch_shapes=[pltpu.VMEM((tq, nsp), F32)],
        in_specs=[pl.BlockSpec((tq, gw), lambda bi, g, i: (bi * nq + i, g)),
                  pl.BlockSpec((None, None, None, nc, HEAD_DIM), lambda bi, g, i: (0, bi, g, 0, 0)),
                  pl.BlockSpec((None, None, None, nc, HEAD_DIM), lambda bi, g, i: (1, bi, g, 0, 0)),
                  pl.BlockSpec((nc, nsp), lambda bi, g, i: (0, 0))],
        out_specs=[pl.BlockSpec((tq, gw), lambda bi, g, i: (bi * nq + i, g)),
                   pl.BlockSpec((None, None, tq, nsp), lambda bi, g, i: (bi, g, i, 0))],
        out_shape=[jax.ShapeDtypeStruct((b * seq, NSA_HEADS * HEAD_DIM), BF16),
                   jax.ShapeDtypeStruct((b, NSA_KV_GROUPS, seq, nsp), BF16)],
        compiler_params=_cparams(("parallel", "parallel", "parallel")))(z0, kv_cmp, kv_cmp, ovl)


def _lane_tile(x, n):
    return jnp.concatenate([x] * n, axis=1) if n > 1 else x


def _softmax_pv(s, v):
    tq, tk = s.shape
    d = v.shape[1]
    m = jnp.broadcast_to(jnp.max(s, axis=-1, keepdims=True), (tq, LANES))
    p = jnp.exp2((s - _lane_tile(m, tk // LANES)).astype(BF16))
    acc = jnp.dot(p, jnp.concatenate([v, jnp.ones((tk, LANES), BF16)], axis=1), preferred_element_type=F32)
    return acc[:, :d], acc[:, d:], m


WIN_SUB = 256
DIL_SUB = 128


def _band_subtiles(i, tq, w, lo, hi, sub):
    ts = min(sub, tq)
    qq = lax.broadcasted_iota(jnp.int32, (ts, 1), 0)
    kk = lax.broadcasted_iota(jnp.int32, (1, ts + w), 1)
    rel = qq + w - kk
    band = (rel >= lo) & (rel <= hi)
    out = []
    for a in range(tq // ts):
        r0 = a * ts
        if r0 < w:
            mask = band & ((kk >= w - r0) | (i > 0))
            take = lambda p, c, lanes, r0=r0: jnp.concatenate([p[r0:, lanes], c[:r0 + ts, lanes]], axis=0)
        else:
            mask = band
            take = lambda p, c, lanes, r0=r0: c[r0 - w:r0 + ts, lanes]
        out.append((r0, ts, mask, take))
    return out


def _nsa_win_body(q_ref, kp_ref, kc_ref, vp_ref, vc_ref, o_ref, *, tq):
    i = pl.program_id(2)
    tiles = _band_subtiles(i, tq, NSA_WINDOW, 0, NSA_WINDOW - 1, WIN_SUB)
    for r0, ts, mask, take in tiles:
        k = take(kp_ref, kc_ref, slice(None))
        v = take(vp_ref, vc_ref, slice(None))
        scores = [jnp.where(mask, lax.dot_general(q_ref[r0:r0 + ts, h * HEAD_DIM:(h + 1) * HEAD_DIM], k,
                                                  (((1,), (1,)), ((), ())), preferred_element_type=F32),
                            NEG_INF) for h in range(NSA_GROUP_HEADS)]
        for h in range(NSA_GROUP_HEADS):
            num, den, _ = _softmax_pv(scores[h], v)
            o_ref[r0:r0 + ts, h * HEAD_DIM:(h + 1) * HEAD_DIM] = (num / den).astype(o_ref.dtype)


def _nsa_win(z0, b, seq):
    tq = NSA_WINDOW
    assert seq % tq == 0
    nq = seq // tq
    gw = NSA_GROUP_HEADS * HEAD_DIM
    kcol, vcol = Z0_KWIN // HEAD_DIM, Z0_VWIN // HEAD_DIM
    prev = lambda bi, i: bi * nq + jnp.maximum(i - 1, 0)
    return pl.pallas_call(
        functools.partial(_nsa_win_body, tq=tq), grid=(b, NSA_KV_GROUPS, nq),
        in_specs=[pl.BlockSpec((tq, gw), lambda bi, g, i: (bi * nq + i, g)),
                  pl.BlockSpec((tq, HEAD_DIM), lambda bi, g, i: (prev(bi, i), kcol + g)),
                  pl.BlockSpec((tq, HEAD_DIM), lambda bi, g, i: (bi * nq + i, kcol + g)),
                  pl.BlockSpec((tq, HEAD_DIM), lambda bi, g, i: (prev(bi, i), vcol + g)),
                  pl.BlockSpec((tq, HEAD_DIM), lambda bi, g, i: (bi * nq + i, vcol + g))],
        out_specs=pl.BlockSpec((tq, gw), lambda bi, g, i: (bi * nq + i, g)),
        out_shape=jax.ShapeDtypeStruct((b * seq, NSA_HEADS * HEAD_DIM), BF16),
        compiler_params=_cparams(("parallel", "parallel", "parallel")))(z0, z0, z0, z0, z0)


def _tri_schedule(nq):
    qi = np.concatenate([np.full(i + 1, i) for i in range(nq)]).astype(np.int32)
    kj = np.concatenate([np.arange(i + 1) for i in range(nq)]).astype(np.int32)
    return jnp.asarray(qi), jnp.asarray(kj)


def _flash_init(m_scr, acc_scr):
    m_scr[...] = jnp.full(m_scr.shape, NEG_INF, F32)
    acc_scr[...] = jnp.zeros(acc_scr.shape, F32)


def _flash_update(scores, vs, m_scr, acc_scr, rows=slice(None)):
    nh = len(scores)
    reps = scores[0].shape[1] // LANES
    ones = jnp.ones((vs[0].shape[0], LANES), BF16)
    m_prev = [m_scr[h, rows] for h in range(nh)]
    m_new = [jnp.maximum(m_prev[h], jnp.max(scores[h], axis=-1, keepdims=True)) for h in range(nh)]
    ps = [jnp.exp2((scores[h] - _lane_tile(m_new[h], reps)).astype(BF16)) for h in range(nh)]
    alphas = [jnp.exp2(m_prev[h] - m_new[h]) for h in range(nh)]
    for h in range(nh):
        v_aug = jnp.concatenate([vs[h], ones], axis=1)
        acc_scr[h, rows] = (_lane_tile(alphas[h], acc_scr.shape[2] // LANES) * acc_scr[h, rows]
                            + jnp.dot(ps[h], v_aug, preferred_element_type=F32))
        m_scr[h, rows] = m_new[h]


def _flash_out(acc_scr, h, d):
    acc = acc_scr[h]
    return acc[:, :d] / acc[:, d:]


DIAG_SUB = 256


def _flash_tile(qi, kj, tq, scores_and_values, m_scr, acc_scr, write_out):
    @pl.when(kj < qi)
    def _():
        scores, vs = scores_and_values(slice(None), tq)
        _flash_update(scores, vs, m_scr, acc_scr)

    @pl.when(kj == qi)
    def _():
        ts = min(DIAG_SUB, tq)
        for r0 in range(0, tq, ts):
            scores, vs = scores_and_values(slice(r0, r0 + ts), r0 + ts)
            row = r0 + lax.broadcasted_iota(jnp.int32, (ts, 1), 0)
            col = lax.broadcasted_iota(jnp.int32, (1, r0 + ts), 1)
            _flash_update([jnp.where(col <= row, s, NEG_INF) for s in scores], vs, m_scr, acc_scr,
                          slice(r0, r0 + ts))
        write_out()


def _nsa_sel_body(qi_ref, kj_ref, q_ref, k_ref, v_ref, oh_ref, bias_ref, o_ref,
                  qa_scr, m_scr, acc_scr, *, tq, per_slab):
    step = pl.program_id(2)
    qi = qi_ref[step]
    kj = kj_ref[step]

    @pl.when(kj == 0)
    def _():
        _flash_init(m_scr, acc_scr)
        for h in range(NSA_GROUP_HEADS):
            qa_scr[h, :, :HEAD_DIM] = q_ref[:, h * HEAD_DIM:(h + 1) * HEAD_DIM]

    @pl.when(kj % per_slab == 0)
    def _():
        for h in range(NSA_GROUP_HEADS):
            qa_scr[h, :, HEAD_DIM:] = bias_ref[...]

    def scores_and_values(rows, nkeys):
        k = jnp.concatenate([k_ref[:nkeys, :], oh_ref[:nkeys, :]], axis=1)
        scores = [lax.dot_general(qa_scr[h, rows], k, (((1,), (1,)), ((), ())), preferred_element_type=F32)
                  for h in range(NSA_GROUP_HEADS)]
        return scores, [v_ref[:nkeys, :]] * NSA_GROUP_HEADS

    def write_out():
        for h in range(NSA_GROUP_HEADS):
            o_ref[:, h * HEAD_DIM:(h + 1) * HEAD_DIM] = _flash_out(acc_scr, h, HEAD_DIM).astype(o_ref.dtype)

    _flash_tile(qi, kj, tq, scores_and_values, m_scr, acc_scr, write_out)


def _nsa_sel(z0, bias, onehot, b, seq, tq=1024):
    tq = min(tq, seq)
    nq = seq // tq
    gw = NSA_GROUP_HEADS * HEAD_DIM
    kcol, vcol = Z0_KSLC // HEAD_DIM, Z0_VSLC // HEAD_DIM
    per_slab = max(LANES * NSA_SEL_LEN // tq, 1)
    qi, kj = _tri_schedule(nq)
    grid_spec = pltpu.PrefetchScalarGridSpec(
        num_scalar_prefetch=2, grid=(b, NSA_KV_GROUPS, int(qi.shape[0])),
        in_specs=[pl.BlockSpec((tq, gw), lambda bi, g, s, qi, kj: (bi * nq + qi[s], g)),
                  pl.BlockSpec((tq, HEAD_DIM), lambda bi, g, s, qi, kj: (bi * nq + kj[s], kcol + g)),
                  pl.BlockSpec((tq, HEAD_DIM), lambda bi, g, s, qi, kj: (bi * nq + kj[s], vcol + g)),
                  pl.BlockSpec((tq, LANES), lambda bi, g, s, qi, kj: (kj[s], 0)),
                  pl.BlockSpec((None, None, tq, LANES),
                               lambda bi, g, s, qi, kj: (bi, g, qi[s], kj[s] // per_slab))],
        out_specs=pl.BlockSpec((tq, gw), lambda bi, g, s, qi, kj: (bi * nq + qi[s], g)),
        scratch_shapes=[pltpu.VMEM((NSA_GROUP_HEADS, tq, 2 * HEAD_DIM), BF16),
                        pltpu.VMEM((NSA_GROUP_HEADS, tq, LANES), F32),
                        pltpu.VMEM((NSA_GROUP_HEADS, tq, HEAD_DIM + LANES), F32)])
    return pl.pallas_call(
        functools.partial(_nsa_sel_body, tq=tq, per_slab=per_slab), grid_spec=grid_spec,
        out_shape=jax.ShapeDtypeStruct((b * seq, NSA_HEADS * HEAD_DIM), BF16),
        compiler_params=_cparams(("parallel", "parallel", "arbitrary")))(qi, kj, z0, z0, z0, onehot, bias)


MLA_STEP_HEADS = 4


def _mla_body(qi_ref, kj_ref, q_ref, kn_ref, kp_ref, v_ref, o_ref, m_scr, acc_scr, *, tq):
    step = pl.program_id(2)
    qi = qi_ref[step]
    kj = kj_ref[step]

    @pl.when(kj == 0)
    def _():
        _flash_init(m_scr, acc_scr)

    def scores_and_values(rows, nkeys):
        kp = kp_ref[:nkeys, :]
        scores, vs = [], []
        for h in range(MLA_STEP_HEADS):
            k = jnp.concatenate([kn_ref[:nkeys, h * MLA_NOPE:(h + 1) * MLA_NOPE], kp], axis=1)
            scores.append(lax.dot_general(q_ref[rows, h * 2 * LANES:(h + 1) * 2 * LANES], k,
                                          (((1,), (1,)), ((), ())), preferred_element_type=F32))
            vs.append(v_ref[:nkeys, h * MLA_V:(h + 1) * MLA_V])
        return scores, vs

    def write_out():
        for h in range(MLA_STEP_HEADS):
            o_ref[:, h * MLA_V:(h + 1) * MLA_V] = _flash_out(acc_scr, h, MLA_V).astype(o_ref.dtype)

    _flash_tile(qi, kj, tq, scores_and_values, m_scr, acc_scr, write_out)


def _mla(qcat, kv, z0, b, seq, tq=1024):
    tq = min(tq, seq)
    nq = seq // tq
    nh = MLA_STEP_HEADS
    ngrp = MLA_HEADS // nh
    kpcol = Z0_KROPE // LANES
    qi, kj = _tri_schedule(nq)
    grid_spec = pltpu.PrefetchScalarGridSpec(
        num_scalar_prefetch=2, grid=(b, ngrp, int(qi.shape[0])),
        in_specs=[pl.BlockSpec((tq, nh * 2 * LANES), lambda bi, h, s, qi, kj: (bi * nq + qi[s], h)),
                  pl.BlockSpec((tq, nh * MLA_NOPE), lambda bi, h, s, qi, kj: (bi * nq + kj[s], h)),
                  pl.BlockSpec((tq, LANES), lambda bi, h, s, qi, kj: (bi * nq + kj[s], kpcol)),
                  pl.BlockSpec((tq, nh * MLA_V), lambda bi, h, s, qi, kj: (bi * nq + kj[s], ngrp + h))],
        out_specs=pl.BlockSpec((tq, nh * MLA_V), lambda bi, h, s, qi, kj: (bi * nq + qi[s], h)),
        scratch_shapes=[pltpu.VMEM((nh, tq, LANES), F32), pltpu.VMEM((nh, tq, MLA_V + LANES), F32)])
    return pl.pallas_call(
        functools.partial(_mla_body, tq=tq), grid_spec=grid_spec,
        out_shape=jax.ShapeDtypeStruct((b * seq, MLA_HEADS * MLA_V), BF16),
        compiler_params=_cparams(("parallel", "parallel", "arbitrary")))(qi, kj, qcat, kv, z0, kv)


def _dil_body(q_ref, kp_ref, kc_ref, vp_ref, vc_ref, o_ref, lse_ref, *, tq):
    i = pl.program_id(2)
    for r0, ts, mask, take in _band_subtiles(i, tq, DIL_SPAN, 0, DIL_SPAN, DIL_SUB):
        lane = lax.broadcasted_iota(jnp.int32, (ts, LANES), 1)
        lse_all = jnp.zeros((ts, LANES), F32)
        for h in range(DIL_HEADS):
            sl = slice(h * HEAD_DIM, (h + 1) * HEAD_DIM)
            s = lax.dot_general(q_ref[r0:r0 + ts, sl], take(kp_ref, kc_ref, sl), (((1,), (1,)), ((), ())),
                                preferred_element_type=F32)
            num, den, m = _softmax_pv(jnp.where(mask, s, NEG_INF), take(vp_ref, vc_ref, sl))
            o_ref[r0:r0 + ts, sl] = (num / den).astype(o_ref.dtype)
            lse_all = jnp.where(lane == h, m * LN2 + jnp.log(den), lse_all)
        lse_ref[r0:r0 + ts, :] = lse_all


def _dilated(zv, r, b, seq, tq=512):
    sub = seq // r
    tq = min(tq, sub)
    assert sub % tq == 0 and tq % DIL_SPAN == 0
    nq = sub // tq
    hw = DIL_HEADS * HEAD_DIM
    ratio = tq // DIL_SPAN
    prev = lambda i: jnp.maximum(i * ratio - 1, 0)
    o, lse = pl.pallas_call(
        functools.partial(_dil_body, tq=tq), grid=(b, r, nq),
        in_specs=[pl.BlockSpec((None, tq, hw), lambda bi, c, i: (bi, i, c)),
                  pl.BlockSpec((None, DIL_SPAN, hw), lambda bi, c, i: (bi, prev(i), r + c)),
                  pl.BlockSpec((None, tq, hw), lambda bi, c, i: (bi, i, r + c)),
                  pl.BlockSpec((None, DIL_SPAN, hw), lambda bi, c, i: (bi, prev(i), 2 * r + c)),
                  pl.BlockSpec((None, tq, hw), lambda bi, c, i: (bi, i, 2 * r + c))],
        out_specs=[pl.BlockSpec((None, tq, hw), lambda bi, c, i: (bi, i, c)),
                   pl.BlockSpec((None, tq, LANES), lambda bi, c, i: (bi, i, c))],
        out_shape=[jax.ShapeDtypeStruct((b, sub, r * hw), BF16),
                   jax.ShapeDtypeStruct((b, sub, r * LANES), F32)],
        compiler_params=_cparams(("parallel", "parallel", "parallel")))(zv, zv, zv, zv, zv)
    return o.reshape(b * seq, hw), lse.reshape(b * seq, LANES)


def _finish(h_ref, m, g_ref, o_ref):
    y = m * lax.rsqrt(jnp.mean(m * m, axis=-1, keepdims=True) + EPS)
    o_ref[...] = h_ref[...] + y * g_ref[...]


def _out0_body(h_ref, oc_ref, os_ref, ow_ref, gate_ref, ob_ref, wa_ref, wb_ref, g_ref, o_ref):
    gate = jax.nn.sigmoid(gate_ref[...].astype(F32))
    parts = []
    for h in range(NSA_HEADS):
        sl = slice(h * HEAD_DIM, (h + 1) * HEAD_DIM)
        parts.append(gate[:, 3 * h:3 * h + 1] * oc_ref[:, sl].astype(F32)
                     + gate[:, 3 * h + 1:3 * h + 2] * os_ref[:, sl].astype(F32)
                     + gate[:, 3 * h + 2:3 * h + 3] * ow_ref[:, sl].astype(F32))
    oa = jnp.concatenate(parts, axis=1).astype(BF16)
    m = (jnp.dot(oa, wa_ref[...], preferred_element_type=F32)
         + jnp.dot(ob_ref[...], wb_ref[...], preferred_element_type=F32))
    _finish(h_ref, m, g_ref, o_ref)


def _out1_body(h_ref, o0_ref, o1_ref, o2_ref, l0_ref, l1_ref, l2_ref, w_ref, g_ref, o_ref):
    l0, l1, l2 = l0_ref[...], l1_ref[...], l2_ref[...]
    mx = jnp.maximum(jnp.maximum(l0, l1), l2)
    e0, e1, e2 = jnp.exp(l0 - mx), jnp.exp(l1 - mx), jnp.exp(l2 - mx)
    tot = e0 + e1 + e2
    a0, a1, a2 = e0 / tot, e1 / tot, e2 / tot
    parts = []
    for h in range(DIL_HEADS):
        sl = slice(h * HEAD_DIM, (h + 1) * HEAD_DIM)
        parts.append(a0[:, h:h + 1] * o0_ref[:, sl].astype(F32)
                     + a1[:, h:h + 1] * o1_ref[:, sl].astype(F32)
                     + a2[:, h:h + 1] * o2_ref[:, sl].astype(F32))
    o = jnp.concatenate(parts, axis=1).astype(BF16)
    _finish(h_ref, jnp.dot(o, w_ref[...], preferred_element_type=F32), g_ref, o_ref)


def _row_spec(tm, w, col=0):
    return pl.BlockSpec((tm, w), lambda i: (i, col))


def _full_spec(shape):
    return pl.BlockSpec(shape, lambda i: (0,) * len(shape))


def _out0(h, o_c, o_s, o_w, z0, o_b, wa, wb, g, tm=256):
    t, d = h.shape
    tm = min(tm, t)
    ow = NSA_HEADS * HEAD_DIM
    return pl.pallas_call(
        _out0_body, grid=(t // tm,),
        in_specs=[_row_spec(tm, d), _row_spec(tm, ow), _row_spec(tm, ow), _row_spec(tm, ow),
                  _row_spec(tm, LANES, Z0_GATE // LANES), _row_spec(tm, ow),
                  _full_spec(wa.shape), _full_spec(wb.shape), _full_spec((1, d))],
        out_specs=_row_spec(tm, d), out_shape=jax.ShapeDtypeStruct((t, d), F32),
        compiler_params=_cparams(("parallel",)))(h, o_c, o_s, o_w, z0, o_b, wa, wb, g.reshape(1, d))


def _out1(h, os_, lses, w, g, tm=256):
    t, d = h.shape
    tm = min(tm, t)
    ow = DIL_HEADS * HEAD_DIM
    return pl.pallas_call(
        _out1_body, grid=(t // tm,),
        in_specs=[_row_spec(tm, d)] + [_row_spec(tm, ow)] * 3 + [_row_spec(tm, LANES)] * 3
                 + [_full_spec(w.shape), _full_spec((1, d))],
        out_specs=_row_spec(tm, d), out_shape=jax.ShapeDtypeStruct((t, d), F32),
        compiler_params=_cparams(("parallel",)))(h, *os_, *lses, w, g.reshape(1, d))


def _mlp_body(h_ref, g1_ref, w1_ref, w2_ref, g2_ref, o_ref, xn_ref, acc_ref):
    f = pl.program_id(1)
    last = pl.num_programs(1) - 1

    def partial_sum(xn):
        a = jnp.maximum(jnp.dot(xn, w1_ref[...], preferred_element_type=F32), 0.0)
        return jnp.dot((a * a).astype(BF16), w2_ref[...], preferred_element_type=F32)

    @pl.when(f == 0)
    def _():
        x = h_ref[...]
        y = x * lax.rsqrt(jnp.mean(x * x, axis=-1, keepdims=True) + EPS)
        xn = (y * g1_ref[...]).astype(BF16)
        xn_ref[...] = xn
        acc_ref[...] = partial_sum(xn)

    @pl.when((f > 0) & (f < last))
    def _():
        acc_ref[...] += partial_sum(xn_ref[...])

    @pl.when(f == last)
    def _():
        _finish(h_ref, acc_ref[...] + partial_sum(xn_ref[...]), g2_ref, o_ref)


def _mlp(h, g1, w1, w2, g2, tm=512, tf=1024):
    t, d = h.shape
    ff = w1.shape[1]
    tm = min(tm, t)
    return pl.pallas_call(
        _mlp_body, grid=(t // tm, ff // tf),
        in_specs=[pl.BlockSpec((tm, d), lambda i, f: (i, 0)),
                  pl.BlockSpec((1, d), lambda i, f: (0, 0)),
                  pl.BlockSpec((d, tf), lambda i, f: (0, f)),
                  pl.BlockSpec((tf, d), lambda i, f: (f, 0)),
                  pl.BlockSpec((1, d), lambda i, f: (0, 0))],
        out_specs=pl.BlockSpec((tm, d), lambda i, f: (i, 0)),
        out_shape=jax.ShapeDtypeStruct((t, d), F32),
        scratch_shapes=[pltpu.VMEM((tm, d), BF16), pltpu.VMEM((tm, d), F32)],
        compiler_params=_cparams(("parallel", "arbitrary")))(h, g1.reshape(1, d), w1, w2, g2.reshape(1, d))


def _rope64_tile(w):
    z = jnp.zeros((w.shape[0], 32), w.dtype)
    return jnp.concatenate([w[:, :32], z, w[:, 32:], z], axis=1)


def _layer0_w_in(w_in):
    d = w_in.shape[0]
    o1 = NSA_HEADS * HEAD_DIM
    o2 = o1 + 3 * 2 * NSA_KV_GROUPS * HEAD_DIM
    o3 = o2 + 3 * NSA_HEADS
    o4 = o3 + MLA_Q_RANK
    o5 = o4 + MLA_KV_RANK
    kv = w_in[:, o1:o2].reshape(d, 3, 2, NSA_KV_GROUPS * HEAD_DIM)
    kv = kv.transpose(0, 2, 1, 3).reshape(d, o2 - o1)
    gate = jnp.pad(w_in[:, o2:o3], ((0, 0), (0, LANES - (o3 - o2))))
    w = jnp.concatenate([w_in[:, :o1], kv, w_in[:, o3:o4], w_in[:, o4:o5], gate, _rope64_tile(w_in[:, o5:])], 1)
    assert w.shape[1] == Z0_COLS
    return w.astype(BF16)


def _mla_w_uq(w_uq):
    d = w_uq.shape[0]
    w = w_uq.reshape(d, MLA_HEADS, MLA_NOPE + MLA_ROPE)
    tiles = [jnp.concatenate([w[:, h, :MLA_NOPE], _rope64_tile(w[:, h, MLA_NOPE:])], 1) for h in range(MLA_HEADS)]
    return jnp.concatenate(tiles, axis=1).astype(BF16)


def _mla_w_ukv(w_ukv):
    d = w_ukv.shape[0]
    w = w_ukv.reshape(d, MLA_HEADS, MLA_NOPE + MLA_V)
    return jnp.concatenate([w[:, :, :MLA_NOPE].reshape(d, -1), w[:, :, MLA_NOPE:].reshape(d, -1)], 1).astype(BF16)


def _layer1_w_in(w_in):
    return w_in.astype(BF16)


def _overlap_matrix(nc, n_cmp, n_slc, nsp):
    ratio = NSA_SEL_LEN // NSA_CMP_STRIDE
    m = np.zeros((nc, nsp), np.float32)
    for off in range(1 - NSA_CMP_LEN // NSA_CMP_STRIDE, ratio):
        n = np.arange(n_slc) * ratio + off
        ok = (n >= 0) & (n < n_cmp)
        m[n[ok], np.arange(n_slc)[ok]] = 1.0
    return jnp.asarray(m, BF16)


def kernel(x, l0_norm_mix_pre, l0_w_in, l0_cmp_pe_k, l0_cmp_w1_k, l0_cmp_w2_k, l0_cmp_pe_v, l0_cmp_w1_v, l0_cmp_w2_v, l0_mla_q_norm, l0_mla_w_uq, l0_mla_kv_norm, l0_mla_w_ukv, l0_w_out, l0_norm_mix_post, l0_norm_ffn_pre, l0_w_ff1, l0_w_ff2, l0_norm_ffn_post, l1_norm_mix_pre, l1_w_in, l1_w_out, l1_norm_mix_post, l1_norm_ffn_pre, l1_w_ff1, l1_w_ff2, l1_norm_ffn_post):
    b, seq, d = x.shape
    t = b * seq
    assert seq % NSA_WINDOW == 0 and seq % (DIL_PATTERNS[-1][1] * DIL_SPAN) == 0
    rot = _rope_tables(seq)
    h = x.reshape(t, d)

    tabs0 = ([TAB_ROPE128_Q] * NSA_HEADS + [TAB_ROPE128] * (3 * NSA_KV_GROUPS)
             + [TAB_IDENT] * ((Z0_KROPE - Z0_VCMP) // LANES) + [TAB_ROPE64])
    z0 = _proj_res(h, 0, d, l0_norm_mix_pre, _layer0_w_in(l0_w_in), seq, tabs0, rot)

    nc = seq // NSA_CMP_STRIDE
    n_cmp = (seq - NSA_CMP_LEN) // NSA_CMP_STRIDE + 1
    n_slc = seq // NSA_SEL_LEN
    nsp = -(-n_slc // LANES) * LANES
    gd = NSA_KV_GROUPS * HEAD_DIM

    def chunks(col):
        c = z0[:, col:col + gd].reshape(b, nc, NSA_CMP_STRIDE, NSA_KV_GROUPS, HEAD_DIM)
        return c.transpose(0, 3, 1, 2, 4).reshape(b, NSA_KV_GROUPS, nc, NSA_CMP_STRIDE * HEAD_DIM)

    pe = jnp.stack([l0_cmp_pe_k.reshape(1, -1), l0_cmp_pe_v.reshape(1, -1)])
    kv_cmp = _compress(jnp.stack([chunks(Z0_KCMP), chunks(Z0_VCMP)]),
                       jnp.stack([l0_cmp_w1_k, l0_cmp_w1_v]).astype(BF16),
                       jnp.stack([l0_cmp_w2_k, l0_cmp_w2_v]).astype(BF16),
                       jnp.broadcast_to(pe, (2, 8, pe.shape[-1])).astype(F32))
    o_c, bias = _nsa_cmp(z0, kv_cmp, _overlap_matrix(nc, n_cmp, n_slc, nsp), b, seq)
    o_w = _nsa_win(z0, b, seq)
    blk_lane = (jnp.arange(seq, dtype=jnp.int32) // NSA_SEL_LEN) % LANES
    onehot = (blk_lane[:, None] == jnp.arange(LANES, dtype=jnp.int32)[None, :]).astype(BF16)
    o_s = _nsa_sel(z0, bias, onehot, b, seq)

    qcat = _proj_res(z0, Z0_CQ // MLA_Q_RANK, MLA_Q_RANK, l0_mla_q_norm, _mla_w_uq(l0_mla_w_uq), seq,
                     [TAB_IDENT_Q, TAB_ROPE64_Q] * MLA_HEADS, rot)
    kv = _proj_res(z0, Z0_CKV // MLA_KV_RANK, MLA_KV_RANK, l0_mla_kv_norm, _mla_w_ukv(l0_mla_w_ukv), seq,
                   [TAB_IDENT] * (2 * MLA_HEADS), rot)
    o_b = _mla(qcat, kv, z0, b, seq)

    ow = NSA_HEADS * HEAD_DIM
    w_out0 = l0_w_out.astype(BF16)
    h = _out0(h, o_c, o_s, o_w, z0, o_b, w_out0[:ow], w_out0[ow:], l0_norm_mix_post)
    h = _mlp(h, l0_norm_ffn_pre, l0_w_ff1.astype(BF16), l0_w_ff2.astype(BF16), l0_norm_ffn_post)

    hw = DIL_HEADS * HEAD_DIM
    tabs_g = ([TAB_ROPE128_Q] * DIL_HEADS + [TAB_ROPE128] * DIL_HEADS
              + [TAB_IDENT] * DIL_HEADS)
    w1_in = _layer1_w_in(l1_w_in)
    outs = []
    for p, (_, r) in enumerate(DIL_PATTERNS):
        w_p = w1_in[:, 3 * hw * p:3 * hw * (p + 1)]
        z1 = _proj_res(h, 0, d, l1_norm_mix_pre, w_p, seq, tabs_g, rot, dil=r)
        outs.append(_dilated(z1.reshape(b, seq // r, 3 * r * hw), r, b, seq))
    h = _out1(h, [o for o, _ in outs], [l for _, l in outs], l1_w_out.astype(BF16), l1_norm_mix_post)
    h = _mlp(h, l1_norm_ffn_pre, l1_w_ff1.astype(BF16), l1_w_ff2.astype(BF16), l1_norm_ffn_post)
    return h.reshape(b, seq, d)
```

```python
import functools

import numpy as np
import jax
import jax.numpy as jnp
from jax import lax
from jax.experimental import pallas as pl
from jax.experimental.pallas import tpu as pltpu

F32 = jnp.float32
BF16 = jnp.bfloat16

HEAD_DIM = 128
LANES = 128
ROPE_THETA = 10000.0
EPS = 1e-6
NEG_INF = -1e30
POS_INF = 1e30
PICKED = -3e38

NSA_HEADS = 8
NSA_KV_GROUPS = 2
NSA_GROUP_HEADS = NSA_HEADS // NSA_KV_GROUPS
NSA_CMP_LEN = 32
NSA_CMP_STRIDE = 16
NSA_SEL_LEN = 64
NSA_SEL_TOPK = 16
NSA_WINDOW = 512

MLA_HEADS = 8
MLA_Q_RANK = 512
MLA_KV_RANK = 256
MLA_NOPE = 128
MLA_ROPE = 64
MLA_V = 128

DIL_PATTERNS = ((128, 1), (512, 4), (2048, 16))
DIL_HEADS = 8
DIL_SPAN = 128

VMEM_LIMIT = 56 * 1024 * 1024
MAX_ROW_STRIDE = 4

Z0_Q = 0
Z0_KCMP = 1024
Z0_KSLC = 1280
Z0_KWIN = 1536
Z0_VCMP = 1792
Z0_VSLC = 2048
Z0_VWIN = 2304
Z0_CQ = 2560
Z0_CKV = 3072
Z0_GATE = 3328
Z0_KROPE = 3456
Z0_COLS = 3584

TAB_ROPE128, TAB_IDENT, TAB_ROPE64, TAB_ROPE128_Q, TAB_ROPE64_Q, TAB_IDENT_Q = 0, 1, 2, 3, 4, 5

LOG2E = 1.4426950408889634
LN2 = 0.6931471805599453
QSCALE_128 = HEAD_DIM ** -0.5 * LOG2E
QSCALE_MLA = (MLA_NOPE + MLA_ROPE) ** -0.5 * LOG2E


def _cparams(sem):
    return pltpu.CompilerParams(dimension_semantics=sem, vmem_limit_bytes=VMEM_LIMIT)


def _rope_tables(seq):
    def cs(dim):
        inv = 1.0 / (ROPE_THETA ** (jnp.arange(0, dim, 2, dtype=F32) / dim))
        ang = jnp.arange(seq, dtype=F32)[:, None] * inv[None, :]
        return jnp.cos(ang), jnp.sin(ang)

    c128, s128 = cs(HEAD_DIM)
    c64, s64 = cs(MLA_ROPE)
    one = jnp.ones((seq, 32), F32)
    zero = jnp.zeros((seq, 32), F32)
    ident_c = jnp.ones((seq, LANES), F32)
    ident_s = jnp.zeros((seq, LANES), F32)
    r128_c = jnp.concatenate([c128, c128], axis=1)
    r128_s = jnp.concatenate([-s128, s128], axis=1)
    r64_c = jnp.concatenate([c64, one, c64, one], axis=1)
    r64_s = jnp.concatenate([-s64, zero, s64, zero], axis=1)
    cos = [r128_c, ident_c, r64_c, r128_c * QSCALE_128, r64_c * QSCALE_MLA, ident_c * QSCALE_MLA]
    sin = [r128_s, ident_s, r64_s, r128_s * QSCALE_128, r64_s * QSCALE_MLA, ident_s]
    return jnp.concatenate([jnp.stack(cos), jnp.stack(sin)], axis=2)


def _proj_res_body(*refs, kinds, tn, scale_q, dil):
    x_ref, g_ref, w_ref = refs[:3]
    used = sorted({k for k in kinds if k not in (TAB_IDENT, TAB_IDENT_Q)})
    t_refs = dict(zip(used, refs[3:3 + len(used)]))
    o_ref, xn_ref = refs[3 + len(used):5 + len(used)]
    stage_refs = refs[5 + len(used):]
    tm = x_ref.shape[0]
    x = x_ref[...].astype(F32)
    y = x * lax.rsqrt(jnp.mean(x * x, axis=-1, keepdims=True) + EPS)
    xn_ref[...] = (y * g_ref[...]).astype(BF16)
    nl = tn // LANES
    for j in range(w_ref.shape[1] // tn):
        acc = jnp.dot(xn_ref[...], w_ref[:, j * tn:(j + 1) * tn], preferred_element_type=F32)
        parts = []
        for k in range(nl):
            kind = kinds[j * nl + k]
            a = acc[:, k * LANES:(k + 1) * LANES]
            if kind == TAB_IDENT:
                parts.append(a)
            elif kind == TAB_IDENT_Q:
                parts.append(a * scale_q)
            else:
                tab = t_refs[kind]
                parts.append(a * tab[:, :LANES] + pltpu.roll(a, LANES // 2, 1) * tab[:, LANES:])
        if dil == 1:
            o_ref[:, j * tn:(j + 1) * tn] = jnp.concatenate(parts, axis=1).astype(o_ref.dtype)
            continue
        hw = DIL_HEADS * HEAD_DIM
        s1 = min(dil, MAX_ROW_STRIDE)
        s2 = dil // s1
        for k in range(nl):
            g = j * nl + k
            first, second = stage_refs[2 * k], stage_refs[2 * k + 1]
            first[...] = parts[k]
            if s2 > 1:
                for c1 in range(s1):
                    second[c1 * (tm // s1):(c1 + 1) * (tm // s1), :] = first[pl.ds(c1, tm // s1, stride=s1), :]
            for c in range(dil):
                c1, c2 = c % s1, c // s1
                if s2 > 1:
                    piece = second[pl.ds(c1 * (tm // s1) + c2, tm // dil, stride=s2), :]
                else:
                    piece = first[pl.ds(c, tm // dil, stride=dil), :]
                col = ((g // DIL_HEADS) * dil + c) * hw + (g % DIL_HEADS) * LANES
                o_ref[:, col:col + LANES] = piece.astype(o_ref.dtype)


def _proj_res(x, x_col, d, g, w, seq, kinds, rot, tm=512, tn=512, dil=1, w_col=0):
    t = x.shape[0]
    n = len(kinds) * LANES
    assert w.shape[1] % n == 0
    tm = min(tm, seq)
    assert t % tm == 0 and seq % tm == 0 and n % tn == 0 and len(kinds) == n // LANES and tm % (16 * dil) == 0
    spb = seq // tm
    used = sorted({k for k in kinds if k not in (TAB_IDENT, TAB_IDENT_Q)})
    tab_specs = [pl.BlockSpec((None, tm, 2 * LANES), lambda i, kind=kind: (kind, i % spb, 0)) for kind in used]
    scratch = [pltpu.VMEM((tm, d), BF16)]
    if dil > 1:
        scratch += [pltpu.VMEM((tm, LANES), F32)] * (2 * (tn // LANES))
    return pl.pallas_call(
        functools.partial(_proj_res_body, kinds=tuple(kinds), tn=tn, scale_q=QSCALE_MLA, dil=dil),
        grid=(t // tm,),
        in_specs=[pl.BlockSpec((tm, d), lambda i: (i, x_col)),
                  pl.BlockSpec((1, d), lambda i: (0, 0)),
                  pl.BlockSpec((d, n), lambda i: (0, w_col), pipeline_mode=pl.Buffered(1))] + tab_specs,
        out_specs=pl.BlockSpec((tm // dil, dil * n), lambda i: (i, 0)),
        scratch_shapes=scratch,
        out_shape=jax.ShapeDtypeStruct((t // dil, dil * n), BF16),
        compiler_params=_cparams(("parallel",)))(x, g.reshape(1, d).astype(F32), w, *([rot] * len(used)))


def _compress_body(c_ref, w1_ref, w2_ref, pe_ref, o_ref, *, nc):
    half = NSA_CMP_STRIDE * HEAD_DIM
    c = c_ref[...]
    a = jnp.dot(c, w1_ref[:half, :], preferred_element_type=F32)
    b = jnp.dot(c, w1_ref[half:, :], preferred_element_type=F32)
    pe = pe_ref[...]
    pe_hi = pe.astype(BF16)
    pe_lo = (pe - pe_hi.astype(F32)).astype(BF16)
    pe_term = (jnp.dot(pe_hi, w1_ref[...], preferred_element_type=F32)
               + jnp.dot(pe_lo, w1_ref[...], preferred_element_type=F32))
    hid = a + pltpu.roll(b, nc - 1, 0) + pe_term[0:1, :]
    act = jax.nn.gelu(hid)
    o_ref[...] = jnp.dot(act.astype(BF16), w2_ref[...], preferred_element_type=F32).astype(o_ref.dtype)


def _compress(chunks, w1, w2, pe):
    _, b, g, nc, cw = chunks.shape
    return pl.pallas_call(
        functools.partial(_compress_body, nc=nc), grid=(2, b, g),
        in_specs=[pl.BlockSpec((None, None, None, nc, cw), lambda s, i, j: (s, i, j, 0, 0)),
                  pl.BlockSpec((None, 2 * cw, HEAD_DIM), lambda s, i, j: (s, 0, 0)),
                  pl.BlockSpec((None, HEAD_DIM, HEAD_DIM), lambda s, i, j: (s, 0, 0)),
                  pl.BlockSpec((None, 8, 2 * cw), lambda s, i, j: (s, 0, 0))],
        out_specs=pl.BlockSpec((None, None, None, nc, HEAD_DIM), lambda s, i, j: (s, i, j, 0, 0)),
        out_shape=jax.ShapeDtypeStruct((2, b, g, nc, HEAD_DIM), BF16),
        compiler_params=_cparams(("parallel", "parallel", "parallel")))(chunks, w1, w2, pe)


def _nsa_cmp_body(q_ref, k_ref, v_ref, m_ref, o_ref, bias_ref, imp_scr, *, tq, nc, nsp, topk, nvar,
                  tiles_per_var):
    i = pl.program_id(2)
    t = i * tq + lax.broadcasted_iota(jnp.int32, (tq, 1), 0)
    any_vis = (t >= NSA_CMP_LEN - 1).astype(F32)

    def attend(ncols):
        n = lax.broadcasted_iota(jnp.int32, (1, ncols), 1)
        vis = (n * NSA_CMP_STRIDE + (NSA_CMP_LEN - 1)) <= t
        k = k_ref[:ncols, :]
        v = v_ref[:ncols, :]
        psum = jnp.zeros((tq, ncols), F32)
        for h in range(NSA_GROUP_HEADS):
            sl = slice(h * HEAD_DIM, (h + 1) * HEAD_DIM)
            s = lax.dot_general(q_ref[:, sl], k, (((1,), (1,)), ((), ())), preferred_element_type=F32)
            s = jnp.where(vis, s, NEG_INF)
            e = jnp.exp2(s - jnp.max(s, axis=-1, keepdims=True))
            p = e * (any_vis / jnp.sum(e, axis=-1, keepdims=True))
            o_ref[:, sl] = jnp.dot(p.astype(BF16), v, preferred_element_type=F32).astype(o_ref.dtype)
            psum = psum + p
        hi = psum.astype(BF16)
        lo = (psum - hi.astype(F32)).astype(BF16)
        imp_scr[...] = (jnp.dot(hi, m_ref[:ncols, :], preferred_element_type=F32)
                        + jnp.dot(lo, m_ref[:ncols, :], preferred_element_type=F32))

    for var in range(nvar):
        @pl.when((i >= var * tiles_per_var) & (i < (var + 1) * tiles_per_var))
        def _(var=var):
            attend((var + 1) * nc // nvar)

    imp = imp_scr[...].T
    tl = i * tq + lax.broadcasted_iota(jnp.int32, (1, tq), 1)
    blk = lax.broadcasted_iota(jnp.int32, (nsp, 1), 0)
    blkf = blk.astype(F32)
    cur = lax.shift_right_logical(tl, 6)
    forced = (blk == 0) | (blk == cur) | (blk == cur - 1)
    causal = blk * NSA_SEL_LEN <= tl
    score = jnp.where(forced, PICKED, jnp.where(causal, imp, NEG_INF))

    def take_one(_, score):
        mx = jnp.max(score, axis=0, keepdims=True)
        first = jnp.min(jnp.where(score == mx, blkf, float(nsp)), axis=0, keepdims=True)
        return jnp.where(blkf == first, PICKED, score)

    score = lax.fori_loop(0, topk - 3, take_one, score)
    sel = (score == PICKED) & causal
    bias_ref[...] = jnp.where(sel, 0.0, NEG_INF).T.astype(bias_ref.dtype)


def _nsa_cmp(z0, kv_cmp, ovl, b, seq, tq=512):
    nc = kv_cmp.shape[3]
    nsp = ovl.shape[1]
    tq = min(tq, seq)
    nq = seq // tq
    gw = NSA_GROUP_HEADS * HEAD_DIM
    topk = min(NSA_SEL_TOPK, seq // NSA_SEL_LEN)
    assert topk >= 3
    nvar = 4 if (nq % 4 == 0 and nc % (4 * 2 * LANES) == 0) else 1
    body = functools.partial(_nsa_cmp_body, tq=tq, nc=nc, nsp=nsp, topk=topk, nvar=nvar, tiles_per_var=nq // nvar)
    return pl.pallas_call(
        body, grid=(b, NSA_KV_GROUPS, nq),
        scratch_shapes=[pltpu.VMEM((tq, nsp), F32)],
        in_specs=[pl.BlockSpec((tq, gw), lambda bi, g, i: (bi * nq + i, g)),
                  pl.BlockSpec((None, None, None, nc, HEAD_DIM), lambda bi, g, i: (0, bi, g, 0, 0)),
                  pl.BlockSpec((None, None, None, nc, HEAD_DIM), lambda bi, g, i: (1, bi, g, 0, 0)),
                  pl.BlockSpec((nc, nsp), lambda bi, g, i: (0, 0))],
        out_specs=[pl.BlockSpec((tq, gw), lambda bi, g, i: (bi * nq + i, g)),
                   pl.BlockSpec((None, None, tq, nsp), lambda bi, g, i: (bi, g, i, 0))],
        out_shape=[jax.ShapeDtypeStruct((b * seq, NSA_HEADS * HEAD_DIM), BF16),
                   jax.ShapeDtypeStruct((b, NSA_KV_GROUPS, seq, nsp), BF16)],
        compiler_params=_cparams(("parallel", "parallel", "parallel")))(z0, kv_cmp, kv_cmp, ovl)


def _lane_tile(x, n):
    return jnp.concatenate([x] * n, axis=1) if n > 1 else x


def _softmax_pv(s, v):
    tq, tk = s.shape
    d = v.shape[1]
    m = jnp.broadcast_to(jnp.max(s, axis=-1, keepdims=True), (tq, LANES))
    p = jnp.exp2((s - _lane_tile(m, tk // LANES)).astype(BF16))
    acc = jnp.dot(p, jnp.concatenate([v, jnp.ones((tk, LANES), BF16)], axis=1), preferred_element_type=F32)
    return acc[:, :d], acc[:, d:], m


WIN_SUB = 256
DIL_SUB = 128


def _band_subtiles(i, tq, w, lo, hi, sub):
    ts = min(sub, tq)
    qq = lax.broadcasted_iota(jnp.int32, (ts, 1), 0)
    kk = lax.broadcasted_iota(jnp.int32, (1, ts + w), 1)
    rel = qq + w - kk
    band = (rel >= lo) & (rel <= hi)
    out = []
    for a in range(tq // ts):
        r0 = a * ts
        if r0 < w:
            mask = band & ((kk >= w - r0) | (i > 0))
            take = lambda p, c, lanes, r0=r0: jnp.concatenate([p[r0:, lanes], c[:r0 + ts, lanes]], axis=0)
        else:
            mask = band
            take = lambda p, c, lanes, r0=r0: c[r0 - w:r0 + ts, lanes]
        out.append((r0, ts, mask, take))
    return out


def _nsa_win_body(q_ref, kp_ref, kc_ref, vp_ref, vc_ref, o_ref, *, tq):
    i = pl.program_id(2)
    tiles = _band_subtiles(i, tq, NSA_WINDOW, 0, NSA_WINDOW - 1, WIN_SUB)
    for r0, ts, mask, take in tiles:
        k = take(kp_ref, kc_ref, slice(None))
        v = take(vp_ref, vc_ref, slice(None))
        scores = [jnp.where(mask, lax.dot_general(q_ref[r0:r0 + ts, h * HEAD_DIM:(h + 1) * HEAD_DIM], k,
                                                  (((1,), (1,)), ((), ())), preferred_element_type=F32),
                            NEG_INF) for h in range(NSA_GROUP_HEADS)]
        for h in range(NSA_GROUP_HEADS):
            num, den, _ = _softmax_pv(scores[h], v)
            o_ref[r0:r0 + ts, h * HEAD_DIM:(h + 1) * HEAD_DIM] = (num / den).astype(o_ref.dtype)


def _nsa_win(z0, b, seq):
    tq = NSA_WINDOW
    assert seq % tq == 0
    nq = seq // tq
    gw = NSA_GROUP_HEADS * HEAD_DIM
    kcol, vcol = Z0_KWIN // HEAD_DIM, Z0_VWIN // HEAD_DIM
    prev = lambda bi, i: bi * nq + jnp.maximum(i - 1, 0)
    return pl.pallas_call(
        functools.partial(_nsa_win_body, tq=tq), grid=(b, NSA_KV_GROUPS, nq),
        in_specs=[pl.BlockSpec((tq, gw), lambda bi, g, i: (bi * nq + i, g)),
                  pl.BlockSpec((tq, HEAD_DIM), lambda bi, g, i: (prev(bi, i), kcol + g)),
                  pl.BlockSpec((tq, HEAD_DIM), lambda bi, g, i: (bi * nq + i, kcol + g)),
                  pl.BlockSpec((tq, HEAD_DIM), lambda bi, g, i: (prev(bi, i), vcol + g)),
                  pl.BlockSpec((tq, HEAD_DIM), lambda bi, g, i: (bi * nq + i, vcol + g))],
        out_specs=pl.BlockSpec((tq, gw), lambda bi, g, i: (bi * nq + i, g)),
        out_shape=jax.ShapeDtypeStruct((b * seq, NSA_HEADS * HEAD_DIM), BF16),
        compiler_params=_cparams(("parallel", "parallel", "parallel")))(z0, z0, z0, z0, z0)


def _tri_schedule(nq):
    qi = np.concatenate([np.full(i + 1, i) for i in range(nq)]).astype(np.int32)
    kj = np.concatenate([np.arange(i + 1) for i in range(nq)]).astype(np.int32)
    return jnp.asarray(qi), jnp.asarray(kj)


def _flash_init(m_scr, acc_scr):
    m_scr[...] = jnp.full(m_scr.shape, NEG_INF, F32)
    acc_scr[...] = jnp.zeros(acc_scr.shape, F32)


def _flash_update(scores, vs, m_scr, acc_scr, rows=slice(None)):
    nh = len(scores)
    reps = scores[0].shape[1] // LANES
    ones = jnp.ones((vs[0].shape[0], LANES), BF16)
    m_prev = [m_scr[h, rows] for h in range(nh)]
    m_new = [jnp.maximum(m_prev[h], jnp.max(scores[h], axis=-1, keepdims=True)) for h in range(nh)]
    ps = [jnp.exp2((scores[h] - _lane_tile(m_new[h], reps)).astype(BF16)) for h in range(nh)]
    alphas = [jnp.exp2(m_prev[h] - m_new[h]) for h in range(nh)]
    for h in range(nh):
        v_aug = jnp.concatenate([vs[h], ones], axis=1)
        acc_scr[h, rows] = (_lane_tile(alphas[h], acc_scr.shape[2] // LANES) * acc_scr[h, rows]
                            + jnp.dot(ps[h], v_aug, preferred_element_type=F32))
        m_scr[h, rows] = m_new[h]


def _flash_out(acc_scr, h, d):
    acc = acc_scr[h]
    return acc[:, :d] / acc[:, d:]


DIAG_SUB = 256


def _flash_tile(qi, kj, tq, scores_and_values, m_scr, acc_scr, write_out):
    @pl.when(kj < qi)
    def _():
        scores, vs = scores_and_values(slice(None), tq)
        _flash_update(scores, vs, m_scr, acc_scr)

    @pl.when(kj == qi)
    def _():
        ts = min(DIAG_SUB, tq)
        for r0 in range(0, tq, ts):
            scores, vs = scores_and_values(slice(r0, r0 + ts), r0 + ts)
            row = r0 + lax.broadcasted_iota(jnp.int32, (ts, 1), 0)
            col = lax.broadcasted_iota(jnp.int32, (1, r0 + ts), 1)
            _flash_update([jnp.where(col <= row, s, NEG_INF) for s in scores], vs, m_scr, acc_scr,
                          slice(r0, r0 + ts))
        write_out()


def _nsa_sel_body(qi_ref, kj_ref, q_ref, k_ref, v_ref, oh_ref, bias_ref, o_ref,
                  qa_scr, m_scr, acc_scr, *, tq, per_slab):
    step = pl.program_id(2)
    qi = qi_ref[step]
    kj = kj_ref[step]

    @pl.when(kj == 0)
    def _():
        _flash_init(m_scr, acc_scr)
        for h in range(NSA_GROUP_HEADS):
            qa_scr[h, :, :HEAD_DIM] = q_ref[:, h * HEAD_DIM:(h + 1) * HEAD_DIM]

    @pl.when(kj % per_slab == 0)
    def _():
        for h in range(NSA_GROUP_HEADS):
            qa_scr[h, :, HEAD_DIM:] = bias_ref[...]

    def scores_and_values(rows, nkeys):
        k = jnp.concatenate([k_ref[:nkeys, :], oh_ref[:nkeys, :]], axis=1)
        scores = [lax.dot_general(qa_scr[h, rows], k, (((1,), (1,)), ((), ())), preferred_element_type=F32)
                  for h in range(NSA_GROUP_HEADS)]
        return scores, [v_ref[:nkeys, :]] * NSA_GROUP_HEADS

    def write_out():
        for h in range(NSA_GROUP_HEADS):
            o_ref[:, h * HEAD_DIM:(h + 1) * HEAD_DIM] = _flash_out(acc_scr, h, HEAD_DIM).astype(o_ref.dtype)

    _flash_tile(qi, kj, tq, scores_and_values, m_scr, acc_scr, write_out)


def _nsa_sel(z0, bias, onehot, b, seq, tq=1024):
    tq = min(tq, seq)
    nq = seq // tq
    gw = NSA_GROUP_HEADS * HEAD_DIM
    kcol, vcol = Z0_KSLC // HEAD_DIM, Z0_VSLC // HEAD_DIM
    per_slab = max(LANES * NSA_SEL_LEN // tq, 1)
    qi, kj = _tri_schedule(nq)
    grid_spec = pltpu.PrefetchScalarGridSpec(
        num_scalar_prefetch=2, grid=(b, NSA_KV_GROUPS, int(qi.shape[0])),
        in_specs=[pl.BlockSpec((tq, gw), lambda bi, g, s, qi, kj: (bi * nq + qi[s], g)),
                  pl.BlockSpec((tq, HEAD_DIM), lambda bi, g, s, qi, kj: (bi * nq + kj[s], kcol + g)),
                  pl.BlockSpec((tq, HEAD_DIM), lambda bi, g, s, qi, kj: (bi * nq + kj[s], vcol + g)),
                  pl.BlockSpec((tq, LANES), lambda bi, g, s, qi, kj: (kj[s], 0)),
                  pl.BlockSpec((None, None, tq, LANES),
                               lambda bi, g, s, qi, kj: (bi, g, qi[s], kj[s] // per_slab))],
        out_specs=pl.BlockSpec((tq, gw), lambda bi, g, s, qi, kj: (bi * nq + qi[s], g)),
        scratch_shapes=[pltpu.VMEM((NSA_GROUP_HEADS, tq, 2 * HEAD_DIM), BF16),
                        pltpu.VMEM((NSA_GROUP_HEADS, tq, LANES), F32),
                        pltpu.VMEM((NSA_GROUP_HEADS, tq, HEAD_DIM + LANES), F32)])
    return pl.pallas_call(
        functools.partial(_nsa_sel_body, tq=tq, per_slab=per_slab), grid_spec=grid_spec,
        out_shape=jax.ShapeDtypeStruct((b * seq, NSA_HEADS * HEAD_DIM), BF16),
        compiler_params=_cparams(("parallel", "parallel", "arbitrary")))(qi, kj, z0, z0, z0, onehot, bias)


MLA_STEP_HEADS = 4


def _mla_body(qi_ref, kj_ref, q_ref, kn_ref, kp_ref, v_ref, o_ref, m_scr, acc_scr, *, tq):
    step = pl.program_id(2)
    qi = qi_ref[step]
    kj = kj_ref[step]

    @pl.when(kj == 0)
    def _():
        _flash_init(m_scr, acc_scr)

    def scores_and_values(rows, nkeys):
        kp = kp_ref[:nkeys, :]
        scores, vs = [], []
        for h in range(MLA_STEP_HEADS):
            k = jnp.concatenate([kn_ref[:nkeys, h * MLA_NOPE:(h + 1) * MLA_NOPE], kp], axis=1)
            scores.append(lax.dot_general(q_ref[rows, h * 2 * LANES:(h + 1) * 2 * LANES], k,
                                          (((1,), (1,)), ((), ())), preferred_element_type=F32))
            vs.append(v_ref[:nkeys, h * MLA_V:(h + 1) * MLA_V])
        return scores, vs

    def write_out():
        for h in range(MLA_STEP_HEADS):
            o_ref[:, h * MLA_V:(h + 1) * MLA_V] = _flash_out(acc_scr, h, MLA_V).astype(o_ref.dtype)

    _flash_tile(qi, kj, tq, scores_and_values, m_scr, acc_scr, write_out)


def _mla(qcat, kv, z0, b, seq, tq=1024):
    tq = min(tq, seq)
    nq = seq // tq
    nh = MLA_STEP_HEADS
    ngrp = MLA_HEADS // nh
    kpcol = Z0_KROPE // LANES
    qi, kj = _tri_schedule(nq)
    grid_spec = pltpu.PrefetchScalarGridSpec(
        num_scalar_prefetch=2, grid=(b, ngrp, int(qi.shape[0])),
        in_specs=[pl.BlockSpec((tq, nh * 2 * LANES), lambda bi, h, s, qi, kj: (bi * nq + qi[s], h)),
                  pl.BlockSpec((tq, nh * MLA_NOPE), lambda bi, h, s, qi, kj: (bi * nq + kj[s], h)),
                  pl.BlockSpec((tq, LANES), lambda bi, h, s, qi, kj: (bi * nq + kj[s], kpcol)),
                  pl.BlockSpec((tq, nh * MLA_V), lambda bi, h, s, qi, kj: (bi * nq + kj[s], ngrp + h))],
        out_specs=pl.BlockSpec((tq, nh * MLA_V), lambda bi, h, s, qi, kj: (bi * nq + qi[s], h)),
        scratch_shapes=[pltpu.VMEM((nh, tq, LANES), F32), pltpu.VMEM((nh, tq, MLA_V + LANES), F32)])
    return pl.pallas_call(
        functools.partial(_mla_body, tq=tq), grid_spec=grid_spec,
        out_shape=jax.ShapeDtypeStruct((b * seq, MLA_HEADS * MLA_V), BF16),
        compiler_params=_cparams(("parallel", "parallel", "arbitrary")))(qi, kj, qcat, kv, z0, kv)


def _dil_body(q_ref, kp_ref, kc_ref, vp_ref, vc_ref, o_ref, lse_ref, *, tq):
    i = pl.program_id(2)
    for r0, ts, mask, take in _band_subtiles(i, tq, DIL_SPAN, 0, DIL_SPAN, DIL_SUB):
        lane = lax.broadcasted_iota(jnp.int32, (ts, LANES), 1)
        lse_all = jnp.zeros((ts, LANES), F32)
        for h in range(DIL_HEADS):
            sl = slice(h * HEAD_DIM, (h + 1) * HEAD_DIM)
            s = lax.dot_general(q_ref[r0:r0 + ts, sl], take(kp_ref, kc_ref, sl), (((1,), (1,)), ((), ())),
                                preferred_element_type=F32)
            num, den, m = _softmax_pv(jnp.where(mask, s, NEG_INF), take(vp_ref, vc_ref, sl))
            o_ref[r0:r0 + ts, sl] = (num / den).astype(o_ref.dtype)
            lse_all = jnp.where(lane == h, m * LN2 + jnp.log(den), lse_all)
        lse_ref[r0:r0 + ts, :] = lse_all


def _dilated(zv, r, b, seq, tq=512):
    sub = seq // r
    tq = min(tq, sub)
    assert sub % tq == 0 and tq % DIL_SPAN == 0
    nq = sub // tq
    hw = DIL_HEADS * HEAD_DIM
    ratio = tq // DIL_SPAN
    prev = lambda i: jnp.maximum(i * ratio - 1, 0)
    o, lse = pl.pallas_call(
        functools.partial(_dil_body, tq=tq), grid=(b, r, nq),
        in_specs=[pl.BlockSpec((None, tq, hw), lambda bi, c, i: (bi, i, c)),
                  pl.BlockSpec((None, DIL_SPAN, hw), lambda bi, c, i: (bi, prev(i), r + c)),
                  pl.BlockSpec((None, tq, hw), lambda bi, c, i: (bi, i, r + c)),
                  pl.BlockSpec((None, DIL_SPAN, hw), lambda bi, c, i: (bi, prev(i), 2 * r + c)),
                  pl.BlockSpec((None, tq, hw), lambda bi, c, i: (bi, i, 2 * r + c))],
        out_specs=[pl.BlockSpec((None, tq, hw), lambda bi, c, i: (bi, i, c)),
                   pl.BlockSpec((None, tq, LANES), lambda bi, c, i: (bi, i, c))],
        out_shape=[jax.ShapeDtypeStruct((b, sub, r * hw), BF16),
                   jax.ShapeDtypeStruct((b, sub, r * LANES), F32)],
        compiler_params=_cparams(("parallel", "parallel", "parallel")))(zv, zv, zv, zv, zv)
    return o.reshape(b * seq, hw), lse.reshape(b * seq, LANES)


def _finish(h_ref, m, g_ref, o_ref):
    y = m * lax.rsqrt(jnp.mean(m * m, axis=-1, keepdims=True) + EPS)
    o_ref[...] = h_ref[...] + y * g_ref[...]


def _out0_body(h_ref, oc_ref, os_ref, ow_ref, gate_ref, ob_ref, wa_ref, wb_ref, g_ref, o_ref):
    gate = jax.nn.sigmoid(gate_ref[...].astype(F32))
    parts = []
    for h in range(NSA_HEADS):
        sl = slice(h * HEAD_DIM, (h + 1) * HEAD_DIM)
        parts.append(gate[:, 3 * h:3 * h + 1] * oc_ref[:, sl].astype(F32)
                     + gate[:, 3 * h + 1:3 * h + 2] * os_ref[:, sl].astype(F32)
                     + gate[:, 3 * h + 2:3 * h + 3] * ow_ref[:, sl].astype(F32))
    oa = jnp.concatenate(parts, axis=1).astype(BF16)
    m = (jnp.dot(oa, wa_ref[...], preferred_element_type=F32)
         + jnp.dot(ob_ref[...], wb_ref[...], preferred_element_type=F32))
    _finish(h_ref, m, g_ref, o_ref)


def _out1_body(h_ref, o0_ref, o1_ref, o2_ref, l0_ref, l1_ref, l2_ref, w_ref, g_ref, o_ref):
    l0, l1, l2 = l0_ref[...], l1_ref[...], l2_ref[...]
    mx = jnp.maximum(jnp.maximum(l0, l1), l2)
    e0, e1, e2 = jnp.exp(l0 - mx), jnp.exp(l1 - mx), jnp.exp(l2 - mx)
    tot = e0 + e1 + e2
    a0, a1, a2 = e0 / tot, e1 / tot, e2 / tot
    parts = []
    for h in range(DIL_HEADS):
        sl = slice(h * HEAD_DIM, (h + 1) * HEAD_DIM)
        parts.append(a0[:, h:h + 1] * o0_ref[:, sl].astype(F32)
                     + a1[:, h:h + 1] * o1_ref[:, sl].astype(F32)
                     + a2[:, h:h + 1] * o2_ref[:, sl].astype(F32))
    o = jnp.concatenate(parts, axis=1).astype(BF16)
    _finish(h_ref, jnp.dot(o, w_ref[...], preferred_element_type=F32), g_ref, o_ref)


def _row_spec(tm, w, col=0):
    return pl.BlockSpec((tm, w), lambda i: (i, col))


def _full_spec(shape):
    return pl.BlockSpec(shape, lambda i: (0,) * len(shape), pipeline_mode=pl.Buffered(1))


def _out0(h, o_c, o_s, o_w, z0, o_b, w, g, tm=512):
    t, d = h.shape
    tm = min(tm, t)
    ow = NSA_HEADS * HEAD_DIM
    w_half = lambda k: pl.BlockSpec((ow, d), lambda i: (k, 0), pipeline_mode=pl.Buffered(1))
    return pl.pallas_call(
        _out0_body, grid=(t // tm,),
        in_specs=[_row_spec(tm, d), _row_spec(tm, ow), _row_spec(tm, ow), _row_spec(tm, ow),
                  _row_spec(tm, LANES, Z0_GATE // LANES), _row_spec(tm, ow),
                  w_half(0), w_half(1), _full_spec((1, d))],
        out_specs=_row_spec(tm, d), out_shape=jax.ShapeDtypeStruct((t, d), F32),
        compiler_params=_cparams(("parallel",)))(h, o_c, o_s, o_w, z0, o_b, w, w, g.reshape(1, d))


def _out1(h, os_, lses, w, g, tm=512):
    t, d = h.shape
    tm = min(tm, t)
    ow = DIL_HEADS * HEAD_DIM
    return pl.pallas_call(
        _out1_body, grid=(t // tm,),
        in_specs=[_row_spec(tm, d)] + [_row_spec(tm, ow)] * 3 + [_row_spec(tm, LANES)] * 3
                 + [_full_spec(w.shape), _full_spec((1, d))],
        out_specs=_row_spec(tm, d), out_shape=jax.ShapeDtypeStruct((t, d), F32),
        compiler_params=_cparams(("parallel",)))(h, *os_, *lses, w, g.reshape(1, d))


def _mlp_body(h_ref, g1_ref, w1_ref, w2_ref, g2_ref, o_ref, xn_ref, acc_ref):
    f = pl.program_id(1)
    last = pl.num_programs(1) - 1

    def partial_sum(xn):
        a = jnp.maximum(jnp.dot(xn, w1_ref[...], preferred_element_type=F32), 0.0)
        return jnp.dot((a * a).astype(BF16), w2_ref[...], preferred_element_type=F32)

    @pl.when(f == 0)
    def _():
        x = h_ref[...]
        y = x * lax.rsqrt(jnp.mean(x * x, axis=-1, keepdims=True) + EPS)
        xn = (y * g1_ref[...]).astype(BF16)
        xn_ref[...] = xn
        acc_ref[...] = partial_sum(xn)

    @pl.when((f > 0) & (f < last))
    def _():
        acc_ref[...] += partial_sum(xn_ref[...])

    @pl.when(f == last)
    def _():
        _finish(h_ref, acc_ref[...] + partial_sum(xn_ref[...]), g2_ref, o_ref)


def _mlp(h, g1, w1, w2, g2, tm=512, tf=1024):
    t, d = h.shape
    ff = w1.shape[1]
    tm = min(tm, t)
    return pl.pallas_call(
        _mlp_body, grid=(t // tm, ff // tf),
        in_specs=[pl.BlockSpec((tm, d), lambda i, f: (i, 0)),
                  pl.BlockSpec((1, d), lambda i, f: (0, 0)),
                  pl.BlockSpec((d, tf), lambda i, f: (0, f)),
                  pl.BlockSpec((tf, d), lambda i, f: (f, 0)),
                  pl.BlockSpec((1, d), lambda i, f: (0, 0))],
        out_specs=pl.BlockSpec((tm, d), lambda i, f: (i, 0)),
        out_shape=jax.ShapeDtypeStruct((t, d), F32),
        scratch_shapes=[pltpu.VMEM((tm, d), BF16), pltpu.VMEM((tm, d), F32)],
        compiler_params=_cparams(("parallel", "arbitrary")))(h, g1.reshape(1, d), w1, w2, g2.reshape(1, d))


def _rope64_tile(w):
    z = jnp.zeros((w.shape[0], 32), w.dtype)
    return jnp.concatenate([w[:, :32], z, w[:, 32:], z], axis=1)


def _layer0_w_in(w_in):
    d = w_in.shape[0]
    o1 = NSA_HEADS * HEAD_DIM
    o2 = o1 + 3 * 2 * NSA_KV_GROUPS * HEAD_DIM
    o3 = o2 + 3 * NSA_HEADS
    o4 = o3 + MLA_Q_RANK
    o5 = o4 + MLA_KV_RANK
    kv = w_in[:, o1:o2].reshape(d, 3, 2, NSA_KV_GROUPS * HEAD_DIM)
    kv = kv.transpose(0, 2, 1, 3).reshape(d, o2 - o1)
    gate = jnp.pad(w_in[:, o2:o3], ((0, 0), (0, LANES - (o3 - o2))))
    w = jnp.concatenate([w_in[:, :o1], kv, w_in[:, o3:o4], w_in[:, o4:o5], gate, _rope64_tile(w_in[:, o5:])], 1)
    assert w.shape[1] == Z0_COLS
    return w.astype(BF16)


def _mla_w_uq(w_uq):
    d = w_uq.shape[0]
    w = w_uq.reshape(d, MLA_HEADS, MLA_NOPE + MLA_ROPE)
    tiles = [jnp.concatenate([w[:, h, :MLA_NOPE], _rope64_tile(w[:, h, MLA_NOPE:])], 1) for h in range(MLA_HEADS)]
    return jnp.concatenate(tiles, axis=1).astype(BF16)


def _mla_w_ukv(w_ukv):
    d = w_ukv.shape[0]
    w = w_ukv.reshape(d, MLA_HEADS, MLA_NOPE + MLA_V)
    return jnp.concatenate([w[:, :, :MLA_NOPE].reshape(d, -1), w[:, :, MLA_NOPE:].reshape(d, -1)], 1).astype(BF16)


def _layer1_w_in(w_in):
    return w_in.astype(BF16)


def _overlap_matrix(nc, n_cmp, n_slc, nsp):
    ratio = NSA_SEL_LEN // NSA_CMP_STRIDE
    m = np.zeros((nc, nsp), np.float32)
    for off in range(1 - NSA_CMP_LEN // NSA_CMP_STRIDE, ratio):
        n = np.arange(n_slc) * ratio + off
        ok = (n >= 0) & (n < n_cmp)
        m[n[ok], np.arange(n_slc)[ok]] = 1.0
    return jnp.asarray(m, BF16)


def kernel(x, l0_norm_mix_pre, l0_w_in, l0_cmp_pe_k, l0_cmp_w1_k, l0_cmp_w2_k, l0_cmp_pe_v, l0_cmp_w1_v, l0_cmp_w2_v, l0_mla_q_norm, l0_mla_w_uq, l0_mla_kv_norm, l0_mla_w_ukv, l0_w_out, l0_norm_mix_post, l0_norm_ffn_pre, l0_w_ff1, l0_w_ff2, l0_norm_ffn_post, l1_norm_mix_pre, l1_w_in, l1_w_out, l1_norm_mix_post, l1_norm_ffn_pre, l1_w_ff1, l1_w_ff2, l1_norm_ffn_post):
    b, seq, d = x.shape
    t = b * seq
    assert seq % NSA_WINDOW == 0 and seq % (DIL_PATTERNS[-1][1] * DIL_SPAN) == 0
    rot = _rope_tables(seq)
    h = x.reshape(t, d)

    tabs0 = ([TAB_ROPE128_Q] * NSA_HEADS + [TAB_ROPE128] * (3 * NSA_KV_GROUPS)
             + [TAB_IDENT] * ((Z0_KROPE - Z0_VCMP) // LANES) + [TAB_ROPE64])
    z0 = _proj_res(h, 0, d, l0_norm_mix_pre, _layer0_w_in(l0_w_in), seq, tabs0, rot)

    nc = seq // NSA_CMP_STRIDE
    n_cmp = (seq - NSA_CMP_LEN) // NSA_CMP_STRIDE + 1
    n_slc = seq // NSA_SEL_LEN
    nsp = -(-n_slc // LANES) * LANES
    gd = NSA_KV_GROUPS * HEAD_DIM

    def chunks(col):
        c = z0[:, col:col + gd].reshape(b, nc, NSA_CMP_STRIDE, NSA_KV_GROUPS, HEAD_DIM)
        return c.transpose(0, 3, 1, 2, 4).reshape(b, NSA_KV_GROUPS, nc, NSA_CMP_STRIDE * HEAD_DIM)

    pe = jnp.stack([l0_cmp_pe_k.reshape(1, -1), l0_cmp_pe_v.reshape(1, -1)])
    kv_cmp = _compress(jnp.stack([chunks(Z0_KCMP), chunks(Z0_VCMP)]),
                       jnp.stack([l0_cmp_w1_k, l0_cmp_w1_v]).astype(BF16),
                       jnp.stack([l0_cmp_w2_k, l0_cmp_w2_v]).astype(BF16),
                       jnp.broadcast_to(pe, (2, 8, pe.shape[-1])).astype(F32))
    o_c, bias = _nsa_cmp(z0, kv_cmp, _overlap_matrix(nc, n_cmp, n_slc, nsp), b, seq)
    o_w = _nsa_win(z0, b, seq)
    blk_lane = (jnp.arange(seq, dtype=jnp.int32) // NSA_SEL_LEN) % LANES
    onehot = (blk_lane[:, None] == jnp.arange(LANES, dtype=jnp.int32)[None, :]).astype(BF16)
    o_s = _nsa_sel(z0, bias, onehot, b, seq)

    qcat = _proj_res(z0, Z0_CQ // MLA_Q_RANK, MLA_Q_RANK, l0_mla_q_norm, _mla_w_uq(l0_mla_w_uq), seq,
                     [TAB_IDENT_Q, TAB_ROPE64_Q] * MLA_HEADS, rot)
    kv = _proj_res(z0, Z0_CKV // MLA_KV_RANK, MLA_KV_RANK, l0_mla_kv_norm, _mla_w_ukv(l0_mla_w_ukv), seq,
                   [TAB_IDENT] * (2 * MLA_HEADS), rot)
    o_b = _mla(qcat, kv, z0, b, seq)

    h = _out0(h, o_c, o_s, o_w, z0, o_b, l0_w_out.astype(BF16), l0_norm_mix_post)
    h = _mlp(h, l0_norm_ffn_pre, l0_w_ff1.astype(BF16), l0_w_ff2.astype(BF16), l0_norm_ffn_post)

    hw = DIL_HEADS * HEAD_DIM
    tabs_g = ([TAB_ROPE128_Q] * DIL_HEADS + [TAB_ROPE128] * DIL_HEADS
              + [TAB_IDENT] * DIL_HEADS)
    w1_in = _layer1_w_in(l1_w_in)
    outs = []
    for p, (_, r) in enumerate(DIL_PATTERNS):
        z1 = _proj_res(h, 0, d, l1_norm_mix_pre, w1_in, seq, tabs_g, rot, dil=r, w_col=p)
        outs.append(_dilated(z1.reshape(b, seq // r, 3 * r * hw), r, b, seq))
    h = _out1(h, [o for o, _ in outs], [l for _, l in outs], l1_w_out.astype(BF16), l1_norm_mix_post)
    h = _mlp(h, l1_norm_ffn_pre, l1_w_ff1.astype(BF16), l1_w_ff2.astype(BF16), l1_norm_ffn_post)
    return h.reshape(b, seq, d)
```

```python
import functools

import numpy as np
import jax
import jax.numpy as jnp
from jax import lax
from jax.experimental import pallas as pl
from jax.experimental.pallas import tpu as pltpu

F32 = jnp.float32
BF16 = jnp.bfloat16

HEAD_DIM = 128
LANES = 128
ROPE_THETA = 10000.0
EPS = 1e-6
NEG_INF = -1e30
POS_INF = 1e30
PICKED = -3e38

NSA_HEADS = 8
NSA_KV_GROUPS = 2
NSA_GROUP_HEADS = NSA_HEADS // NSA_KV_GROUPS
NSA_CMP_LEN = 32
NSA_CMP_STRIDE = 16
NSA_SEL_LEN = 64
NSA_SEL_TOPK = 16
NSA_WINDOW = 512

MLA_HEADS = 8
MLA_Q_RANK = 512
MLA_KV_RANK = 256
MLA_NOPE = 128
MLA_ROPE = 64
MLA_V = 128

DIL_PATTERNS = ((128, 1), (512, 4), (2048, 16))
DIL_HEADS = 8
DIL_SPAN = 128

VMEM_LIMIT = 56 * 1024 * 1024
MAX_ROW_STRIDE = 4

Z0_Q = 0
Z0_KCMP = 1024
Z0_KSLC = 1280
Z0_KWIN = 1536
Z0_VCMP = 1792
Z0_VSLC = 2048
Z0_VWIN = 2304
Z0_CQ = 2560
Z0_CKV = 3072
Z0_GATE = 3328
Z0_KROPE = 3456
Z0_COLS = 3584

TAB_ROPE128, TAB_IDENT, TAB_ROPE64, TAB_ROPE128_Q, TAB_ROPE64_Q, TAB_IDENT_Q = 0, 1, 2, 3, 4, 5

LOG2E = 1.4426950408889634
LN2 = 0.6931471805599453
QSCALE_128 = HEAD_DIM ** -0.5 * LOG2E
QSCALE_MLA = (MLA_NOPE + MLA_ROPE) ** -0.5 * LOG2E


def _cparams(sem):
    return pltpu.CompilerParams(dimension_semantics=sem, vmem_limit_bytes=VMEM_LIMIT)


def _rope_tables(seq):
    def cs(dim):
        inv = 1.0 / (ROPE_THETA ** (jnp.arange(0, dim, 2, dtype=F32) / dim))
        ang = jnp.arange(seq, dtype=F32)[:, None] * inv[None, :]
        return jnp.cos(ang), jnp.sin(ang)

    c128, s128 = cs(HEAD_DIM)
    c64, s64 = cs(MLA_ROPE)
    one = jnp.ones((seq, 32), F32)
    zero = jnp.zeros((seq, 32), F32)
    ident_c = jnp.ones((seq, LANES), F32)
    ident_s = jnp.zeros((seq, LANES), F32)
    r128_c = jnp.concatenate([c128, c128], axis=1)
    r128_s = jnp.concatenate([-s128, s128], axis=1)
    r64_c = jnp.concatenate([c64, one, c64, one], axis=1)
    r64_s = jnp.concatenate([-s64, zero, s64, zero], axis=1)
    cos = [r128_c, ident_c, r64_c, r128_c * QSCALE_128, r64_c * QSCALE_MLA, ident_c * QSCALE_MLA]
    sin = [r128_s, ident_s, r64_s, r128_s * QSCALE_128, r64_s * QSCALE_MLA, ident_s]
    return jnp.concatenate([jnp.stack(cos), jnp.stack(sin)], axis=2)


def _proj_res_body(*refs, kinds, tn, scale_q, dil):
    x_ref, g_ref, w_ref = refs[:3]
    used = sorted({k for k in kinds if k not in (TAB_IDENT, TAB_IDENT_Q)})
    t_refs = dict(zip(used, refs[3:3 + len(used)]))
    o_ref, xn_ref = refs[3 + len(used):5 + len(used)]
    stage_refs = refs[5 + len(used):]
    tm = x_ref.shape[0]
    x = x_ref[...].astype(F32)
    y = x * lax.rsqrt(jnp.mean(x * x, axis=-1, keepdims=True) + EPS)
    xn_ref[...] = (y * g_ref[...]).astype(BF16)
    nl = tn // LANES
    for j in range(w_ref.shape[1] // tn):
        acc = jnp.dot(xn_ref[...], w_ref[:, j * tn:(j + 1) * tn], preferred_element_type=F32)
        parts = []
        for k in range(nl):
            kind = kinds[j * nl + k]
            a = acc[:, k * LANES:(k + 1) * LANES]
            if kind == TAB_IDENT:
                parts.append(a)
            elif kind == TAB_IDENT_Q:
                parts.append(a * scale_q)
            else:
                tab = t_refs[kind]
                parts.append(a * tab[:, :LANES] + pltpu.roll(a, LANES // 2, 1) * tab[:, LANES:])
        if dil == 1:
            o_ref[:, j * tn:(j + 1) * tn] = jnp.concatenate(parts, axis=1).astype(o_ref.dtype)
            continue
        hw = DIL_HEADS * HEAD_DIM
        s1 = min(dil, MAX_ROW_STRIDE)
        s2 = dil // s1
        for k in range(nl):
            g = j * nl + k
            first, second = stage_refs[2 * k], stage_refs[2 * k + 1]
            first[...] = parts[k]
            if s2 > 1:
                for c1 in range(s1):
                    second[c1 * (tm // s1):(c1 + 1) * (tm // s1), :] = first[pl.ds(c1, tm // s1, stride=s1), :]
            for c in range(dil):
                c1, c2 = c % s1, c // s1
                if s2 > 1:
                    piece = second[pl.ds(c1 * (tm // s1) + c2, tm // dil, stride=s2), :]
                else:
                    piece = first[pl.ds(c, tm // dil, stride=dil), :]
                col = ((g // DIL_HEADS) * dil + c) * hw + (g % DIL_HEADS) * LANES
                o_ref[:, col:col + LANES] = piece.astype(o_ref.dtype)


def _proj_res(x, x_col, d, g, w, seq, kinds, rot, tm=512, tn=512, dil=1, w_col=0):
    t = x.shape[0]
    n = len(kinds) * LANES
    assert w.shape[1] % n == 0
    tm = min(tm, seq)
    assert t % tm == 0 and seq % tm == 0 and n % tn == 0 and len(kinds) == n // LANES and tm % (16 * dil) == 0
    spb = seq // tm
    used = sorted({k for k in kinds if k not in (TAB_IDENT, TAB_IDENT_Q)})
    tab_specs = [pl.BlockSpec((None, tm, 2 * LANES), lambda i, kind=kind: (kind, i % spb, 0)) for kind in used]
    scratch = [pltpu.VMEM((tm, d), BF16)]
    if dil > 1:
        scratch += [pltpu.VMEM((tm, LANES), F32)] * (2 * (tn // LANES))
    return pl.pallas_call(
        functools.partial(_proj_res_body, kinds=tuple(kinds), tn=tn, scale_q=QSCALE_MLA, dil=dil),
        grid=(t // tm,),
        in_specs=[pl.BlockSpec((tm, d), lambda i: (i, x_col)),
                  pl.BlockSpec((1, d), lambda i: (0, 0)),
                  pl.BlockSpec((d, n), lambda i: (0, w_col), pipeline_mode=pl.Buffered(1))] + tab_specs,
        out_specs=pl.BlockSpec((tm // dil, dil * n), lambda i: (i, 0)),
        scratch_shapes=scratch,
        out_shape=jax.ShapeDtypeStruct((t // dil, dil * n), BF16),
        compiler_params=_cparams(("parallel",)))(x, g.reshape(1, d).astype(F32), w, *([rot] * len(used)))


def _compress_body(c_ref, w1_ref, w2_ref, pe_ref, o_ref, *, nc):
    half = NSA_CMP_STRIDE * HEAD_DIM
    c = c_ref[...]
    a = jnp.dot(c, w1_ref[:half, :], preferred_element_type=F32)
    b = jnp.dot(c, w1_ref[half:, :], preferred_element_type=F32)
    pe = pe_ref[...]
    pe_hi = pe.astype(BF16)
    pe_lo = (pe - pe_hi.astype(F32)).astype(BF16)
    pe_term = (jnp.dot(pe_hi, w1_ref[...], preferred_element_type=F32)
               + jnp.dot(pe_lo, w1_ref[...], preferred_element_type=F32))
    hid = a + pltpu.roll(b, nc - 1, 0) + pe_term[0:1, :]
    act = jax.nn.gelu(hid)
    o_ref[...] = jnp.dot(act.astype(BF16), w2_ref[...], preferred_element_type=F32).astype(o_ref.dtype)


def _compress(chunks, w1, w2, pe):
    _, b, g, nc, cw = chunks.shape
    return pl.pallas_call(
        functools.partial(_compress_body, nc=nc), grid=(2, b, g),
        in_specs=[pl.BlockSpec((None, None, None, nc, cw), lambda s, i, j: (s, i, j, 0, 0)),
                  pl.BlockSpec((None, 2 * cw, HEAD_DIM), lambda s, i, j: (s, 0, 0)),
                  pl.BlockSpec((None, HEAD_DIM, HEAD_DIM), lambda s, i, j: (s, 0, 0)),
                  pl.BlockSpec((None, 8, 2 * cw), lambda s, i, j: (s, 0, 0))],
        out_specs=pl.BlockSpec((None, None, None, nc, HEAD_DIM), lambda s, i, j: (s, i, j, 0, 0)),
        out_shape=jax.ShapeDtypeStruct((2, b, g, nc, HEAD_DIM), BF16),
        compiler_params=_cparams(("parallel", "parallel", "parallel")))(chunks, w1, w2, pe)


def _nsa_cmp_body(q_ref, k_ref, v_ref, m_ref, o_ref, bias_ref, imp_scr, *, tq, nc, nsp, topk, nvar,
                  tiles_per_var):
    i = pl.program_id(2)
    t = i * tq + lax.broadcasted_iota(jnp.int32, (tq, 1), 0)
    any_vis = (t >= NSA_CMP_LEN - 1).astype(F32)

    def attend(ncols):
        n = lax.broadcasted_iota(jnp.int32, (1, ncols), 1)
        vis = (n * NSA_CMP_STRIDE + (NSA_CMP_LEN - 1)) <= t
        k = k_ref[:ncols, :]
        v = v_ref[:ncols, :]
        psum = jnp.zeros((tq, ncols), F32)
        for h in range(NSA_GROUP_HEADS):
            sl = slice(h * HEAD_DIM, (h + 1) * HEAD_DIM)
            s = lax.dot_general(q_ref[:, sl], k, (((1,), (1,)), ((), ())), preferred_element_type=F32)
            s = jnp.where(vis, s, NEG_INF)
            e = jnp.exp2(s - jnp.max(s, axis=-1, keepdims=True))
            p = e * (any_vis / jnp.sum(e, axis=-1, keepdims=True))
            o_ref[:, sl] = jnp.dot(p.astype(BF16), v, preferred_element_type=F32).astype(o_ref.dtype)
            psum = psum + p
        hi = psum.astype(BF16)
        lo = (psum - hi.astype(F32)).astype(BF16)
        imp_scr[...] = (jnp.dot(hi, m_ref[:ncols, :], preferred_element_type=F32)
                        + jnp.dot(lo, m_ref[:ncols, :], preferred_element_type=F32))

    for var in range(nvar):
        @pl.when((i >= var * tiles_per_var) & (i < (var + 1) * tiles_per_var))
        def _(var=var):
            attend((var + 1) * nc // nvar)

    imp = imp_scr[...].T
    tl = i * tq + lax.broadcasted_iota(jnp.int32, (1, tq), 1)
    blk = lax.broadcasted_iota(jnp.int32, (nsp, 1), 0)
    blkf = blk.astype(F32)
    cur = lax.shift_right_logical(tl, 6)
    forced = (blk == 0) | (blk == cur) | (blk == cur - 1)
    causal = blk * NSA_SEL_LEN <= tl
    score = jnp.where(forced, PICKED, jnp.where(causal, imp, NEG_INF))

    def take_one(_, score):
        mx = jnp.max(score, axis=0, keepdims=True)
        first = jnp.min(jnp.where(score == mx, blkf, float(nsp)), axis=0, keepdims=True)
        return jnp.where(blkf == first, PICKED, score)

    score = lax.fori_loop(0, topk - 3, take_one, score)
    sel = (score == PICKED) & causal
    bias_ref[...] = jnp.where(sel, 0.0, NEG_INF).T.astype(bias_ref.dtype)


def _nsa_cmp(z0, kv_cmp, ovl, b, seq, tq=512):
    nc = kv_cmp.shape[3]
    nsp = ovl.shape[1]
    tq = min(tq, seq)
    nq = seq // tq
    gw = NSA_GROUP_HEADS * HEAD_DIM
    topk = min(NSA_SEL_TOPK, seq // NSA_SEL_LEN)
    assert topk >= 3
    nvar = 4 if (nq % 4 == 0 and nc % (4 * 2 * LANES) == 0) else 1
    body = functools.partial(_nsa_cmp_body, tq=tq, nc=nc, nsp=nsp, topk=topk, nvar=nvar, tiles_per_var=nq // nvar)
    return pl.pallas_call(
        body, grid=(b, NSA_KV_GROUPS, nq),
        scratch_shapes=[pltpu.VMEM((tq, nsp), F32)],
        in_specs=[pl.BlockSpec((tq, gw), lambda bi, g, i: (bi * nq + i, g)),
                  pl.BlockSpec((None, None, None, nc, HEAD_DIM), lambda bi, g, i: (0, bi, g, 0, 0)),
                  pl.BlockSpec((None, None, None, nc, HEAD_DIM), lambda bi, g, i: (1, bi, g, 0, 0)),
                  pl.BlockSpec((nc, nsp), lambda bi, g, i: (0, 0))],
        out_specs=[pl.BlockSpec((tq, gw), lambda bi, g, i: (bi * nq + i, g)),
                   pl.BlockSpec((None, None, tq, nsp), lambda bi, g, i: (bi, g, i, 0))],
        out_shape=[jax.ShapeDtypeStruct((b * seq, NSA_HEADS * HEAD_DIM), BF16),
                   jax.ShapeDtypeStruct((b, NSA_KV_GROUPS, seq, nsp), BF16)],
        compiler_params=_cparams(("parallel", "parallel", "parallel")))(z0, kv_cmp, kv_cmp, ovl)


def _lane_tile(x, n):
    return jnp.concatenate([x] * n, axis=1) if n > 1 else x


def _softmax_pv(s, v):
    tq, tk = s.shape
    d = v.shape[1]
    m = jnp.broadcast_to(jnp.max(s, axis=-1, keepdims=True), (tq, LANES))
    p = jnp.exp2((s - _lane_tile(m, tk // LANES)).astype(BF16))
    acc = jnp.dot(p, jnp.concatenate([v, jnp.ones((tk, LANES), BF16)], axis=1), preferred_element_type=F32)
    return acc[:, :d], acc[:, d:], m


WIN_SUB = 256
DIL_SUB = 128


def _band_subtiles(i, tq, w, lo, hi, sub):
    ts = min(sub, tq)
    qq = lax.broadcasted_iota(jnp.int32, (ts, 1), 0)
    kk = lax.broadcasted_iota(jnp.int32, (1, ts + w), 1)
    rel = qq + w - kk
    band = (rel >= lo) & (rel <= hi)
    out = []
    for a in range(tq // ts):
        r0 = a * ts
        if r0 < w:
            mask = band & ((kk >= w - r0) | (i > 0))
            take = lambda p, c, lanes, r0=r0: jnp.concatenate([p[r0:, lanes], c[:r0 + ts, lanes]], axis=0)
        else:
            mask = band
            take = lambda p, c, lanes, r0=r0: c[r0 - w:r0 + ts, lanes]
        out.append((r0, ts, mask, take))
    return out


def _nsa_win_body(q_ref, kp_ref, kc_ref, vp_ref, vc_ref, o_ref, *, tq):
    i = pl.program_id(2)
    tiles = _band_subtiles(i, tq, NSA_WINDOW, 0, NSA_WINDOW - 1, WIN_SUB)
    for r0, ts, mask, take in tiles:
        k = take(kp_ref, kc_ref, slice(None))
        v = take(vp_ref, vc_ref, slice(None))
        scores = [jnp.where(mask, lax.dot_general(q_ref[r0:r0 + ts, h * HEAD_DIM:(h + 1) * HEAD_DIM], k,
                                                  (((1,), (1,)), ((), ())), preferred_element_type=F32),
                            NEG_INF) for h in range(NSA_GROUP_HEADS)]
        for h in range(NSA_GROUP_HEADS):
            num, den, _ = _softmax_pv(scores[h], v)
            o_ref[r0:r0 + ts, h * HEAD_DIM:(h + 1) * HEAD_DIM] = (num / den).astype(o_ref.dtype)


def _nsa_win(z0, b, seq):
    tq = NSA_WINDOW
    assert seq % tq == 0
    nq = seq // tq
    gw = NSA_GROUP_HEADS * HEAD_DIM
    kcol, vcol = Z0_KWIN // HEAD_DIM, Z0_VWIN // HEAD_DIM
    prev = lambda bi, i: bi * nq + jnp.maximum(i - 1, 0)
    return pl.pallas_call(
        functools.partial(_nsa_win_body, tq=tq), grid=(b, NSA_KV_GROUPS, nq),
        in_specs=[pl.BlockSpec((tq, gw), lambda bi, g, i: (bi * nq + i, g)),
                  pl.BlockSpec((tq, HEAD_DIM), lambda bi, g, i: (prev(bi, i), kcol + g)),
                  pl.BlockSpec((tq, HEAD_DIM), lambda bi, g, i: (bi * nq + i, kcol + g)),
                  pl.BlockSpec((tq, HEAD_DIM), lambda bi, g, i: (prev(bi, i), vcol + g)),
                  pl.BlockSpec((tq, HEAD_DIM), lambda bi, g, i: (bi * nq + i, vcol + g))],
        out_specs=pl.BlockSpec((tq, gw), lambda bi, g, i: (bi * nq + i, g)),
        out_shape=jax.ShapeDtypeStruct((b * seq, NSA_HEADS * HEAD_DIM), BF16),
        compiler_params=_cparams(("parallel", "parallel", "parallel")))(z0, z0, z0, z0, z0)


def _tri_schedule(nq):
    qi = np.concatenate([np.full(i + 1, i) for i in range(nq)]).astype(np.int32)
    kj = np.concatenate([np.arange(i + 1) for i in range(nq)]).astype(np.int32)
    return jnp.asarray(qi), jnp.asarray(kj)


def _flash_init(m_scr, acc_scr):
    m_scr[...] = jnp.full(m_scr.shape, NEG_INF, F32)
    acc_scr[...] = jnp.zeros(acc_scr.shape, F32)


def _flash_update(scores, vs, m_scr, acc_scr, rows=slice(None)):
    nh = len(scores)
    reps = scores[0].shape[1] // LANES
    ones = jnp.ones((vs[0].shape[0], LANES), BF16)
    m_prev = [m_scr[h, rows] for h in range(nh)]
    m_new = [jnp.maximum(m_prev[h], jnp.max(scores[h], axis=-1, keepdims=True)) for h in range(nh)]
    ps = [jnp.exp2((scores[h] - _lane_tile(m_new[h], reps)).astype(BF16)) for h in range(nh)]
    alphas = [jnp.exp2(m_prev[h] - m_new[h]) for h in range(nh)]
    for h in range(nh):
        v_aug = jnp.concatenate([vs[h], ones], axis=1)
        acc_scr[h, rows] = (_lane_tile(alphas[h], acc_scr.shape[2] // LANES) * acc_scr[h, rows]
                            + jnp.dot(ps[h], v_aug, preferred_element_type=F32))
        m_scr[h, rows] = m_new[h]


def _flash_out(acc_scr, h, d):
    acc = acc_scr[h]
    return acc[:, :d] / acc[:, d:]


DIAG_SUB = 256


def _flash_tile(qi, kj, tq, scores_and_values, m_scr, acc_scr, write_out):
    @pl.when(kj < qi)
    def _():
        scores, vs = scores_and_values(slice(None), tq)
        _flash_update(scores, vs, m_scr, acc_scr)

    @pl.when(kj == qi)
    def _():
        ts = min(DIAG_SUB, tq)
        for r0 in range(0, tq, ts):
            scores, vs = scores_and_values(slice(r0, r0 + ts), r0 + ts)
            row = r0 + lax.broadcasted_iota(jnp.int32, (ts, 1), 0)
            col = lax.broadcasted_iota(jnp.int32, (1, r0 + ts), 1)
            _flash_update([jnp.where(col <= row, s, NEG_INF) for s in scores], vs, m_scr, acc_scr,
                          slice(r0, r0 + ts))
        write_out()


def _nsa_sel_body(qi_ref, kj_ref, q_ref, k_ref, v_ref, oh_ref, bias_ref, o_ref,
                  qa_scr, m_scr, acc_scr, *, tq, per_slab):
    step = pl.program_id(2)
    qi = qi_ref[step]
    kj = kj_ref[step]

    @pl.when(kj == 0)
    def _():
        _flash_init(m_scr, acc_scr)
        for h in range(NSA_GROUP_HEADS):
            qa_scr[h, :, :HEAD_DIM] = q_ref[:, h * HEAD_DIM:(h + 1) * HEAD_DIM]

    @pl.when(kj % per_slab == 0)
    def _():
        for h in range(NSA_GROUP_HEADS):
            qa_scr[h, :, HEAD_DIM:] = bias_ref[...]

    def scores_and_values(rows, nkeys):
        k = jnp.concatenate([k_ref[:nkeys, :], oh_ref[:nkeys, :]], axis=1)
        scores = [lax.dot_general(qa_scr[h, rows], k, (((1,), (1,)), ((), ())), preferred_element_type=F32)
                  for h in range(NSA_GROUP_HEADS)]
        return scores, [v_ref[:nkeys, :]] * NSA_GROUP_HEADS

    def write_out():
        for h in range(NSA_GROUP_HEADS):
            o_ref[:, h * HEAD_DIM:(h + 1) * HEAD_DIM] = _flash_out(acc_scr, h, HEAD_DIM).astype(o_ref.dtype)

    _flash_tile(qi, kj, tq, scores_and_values, m_scr, acc_scr, write_out)


def _nsa_sel(z0, bias, onehot, b, seq, tq=1024):
    tq = min(tq, seq)
    nq = seq // tq
    gw = NSA_GROUP_HEADS * HEAD_DIM
    kcol, vcol = Z0_KSLC // HEAD_DIM, Z0_VSLC // HEAD_DIM
    per_slab = max(LANES * NSA_SEL_LEN // tq, 1)
    qi, kj = _tri_schedule(nq)
    grid_spec = pltpu.PrefetchScalarGridSpec(
        num_scalar_prefetch=2, grid=(b, NSA_KV_GROUPS, int(qi.shape[0])),
        in_specs=[pl.BlockSpec((tq, gw), lambda bi, g, s, qi, kj: (bi * nq + qi[s], g)),
                  pl.BlockSpec((tq, HEAD_DIM), lambda bi, g, s, qi, kj: (bi * nq + kj[s], kcol + g)),
                  pl.BlockSpec((tq, HEAD_DIM), lambda bi, g, s, qi, kj: (bi * nq + kj[s], vcol + g)),
                  pl.BlockSpec((tq, LANES), lambda bi, g, s, qi, kj: (kj[s], 0)),
                  pl.BlockSpec((None, None, tq, LANES),
                               lambda bi, g, s, qi, kj: (bi, g, qi[s], kj[s] // per_slab))],
        out_specs=pl.BlockSpec((tq, gw), lambda bi, g, s, qi, kj: (bi * nq + qi[s], g)),
        scratch_shapes=[pltpu.VMEM((NSA_GROUP_HEADS, tq, 2 * HEAD_DIM), BF16),
                        pltpu.VMEM((NSA_GROUP_HEADS, tq, LANES), F32),
                        pltpu.VMEM((NSA_GROUP_HEADS, tq, HEAD_DIM + LANES), F32)])
    return pl.pallas_call(
        functools.partial(_nsa_sel_body, tq=tq, per_slab=per_slab), grid_spec=grid_spec,
        out_shape=jax.ShapeDtypeStruct((b * seq, NSA_HEADS * HEAD_DIM), BF16),
        compiler_params=_cparams(("parallel", "parallel", "arbitrary")))(qi, kj, z0, z0, z0, onehot, bias)


MLA_STEP_HEADS = 4


def _mla_body(qi_ref, kj_ref, q_ref, kn_ref, kp_ref, v_ref, o_ref, m_scr, acc_scr, *, tq):
    step = pl.program_id(2)
    qi = qi_ref[step]
    kj = kj_ref[step]

    @pl.when(kj == 0)
    def _():
        _flash_init(m_scr, acc_scr)

    def scores_and_values(rows, nkeys):
        kp = kp_ref[:nkeys, :]
        scores, vs = [], []
        for h in range(MLA_STEP_HEADS):
            k = jnp.concatenate([kn_ref[:nkeys, h * MLA_NOPE:(h + 1) * MLA_NOPE], kp], axis=1)
            scores.append(lax.dot_general(q_ref[rows, h * 2 * LANES:(h + 1) * 2 * LANES], k,
                                          (((1,), (1,)), ((), ())), preferred_element_type=F32))
            vs.append(v_ref[:nkeys, h * MLA_V:(h + 1) * MLA_V])
        return scores, vs

    def write_out():
        for h in range(MLA_STEP_HEADS):
            o_ref[:, h * MLA_V:(h + 1) * MLA_V] = _flash_out(acc_scr, h, MLA_V).astype(o_ref.dtype)

    _flash_tile(qi, kj, tq, scores_and_values, m_scr, acc_scr, write_out)


def _mla(qcat, kv, z0, b, seq, tq=1024):
    tq = min(tq, seq)
    nq = seq // tq
    nh = MLA_STEP_HEADS
    ngrp = MLA_HEADS // nh
    kpcol = Z0_KROPE // LANES
    qi, kj = _tri_schedule(nq)
    grid_spec = pltpu.PrefetchScalarGridSpec(
        num_scalar_prefetch=2, grid=(b, ngrp, int(qi.shape[0])),
        in_specs=[pl.BlockSpec((tq, nh * 2 * LANES), lambda bi, h, s, qi, kj: (bi * nq + qi[s], h)),
                  pl.BlockSpec((tq, nh * MLA_NOPE), lambda bi, h, s, qi, kj: (bi * nq + kj[s], h)),
                  pl.BlockSpec((tq, LANES), lambda bi, h, s, qi, kj: (bi * nq + kj[s], kpcol)),
                  pl.BlockSpec((tq, nh * MLA_V), lambda bi, h, s, qi, kj: (bi * nq + kj[s], ngrp + h))],
        out_specs=pl.BlockSpec((tq, nh * MLA_V), lambda bi, h, s, qi, kj: (bi * nq + qi[s], h)),
        scratch_shapes=[pltpu.VMEM((nh, tq, LANES), F32), pltpu.VMEM((nh, tq, MLA_V + LANES), F32)])
    return pl.pallas_call(
        functools.partial(_mla_body, tq=tq), grid_spec=grid_spec,
        out_shape=jax.ShapeDtypeStruct((b * seq, MLA_HEADS * MLA_V), BF16),
        compiler_params=_cparams(("parallel", "parallel", "arbitrary")))(qi, kj, qcat, kv, z0, kv)


def _dil_body(q_ref, kp_ref, kc_ref, vp_ref, vc_ref, o_ref, lse_ref, *, tq):
    i = pl.program_id(2)
    for r0, ts, mask, take in _band_subtiles(i, tq, DIL_SPAN, 0, DIL_SPAN, DIL_SUB):
        lane = lax.broadcasted_iota(jnp.int32, (ts, LANES), 1)
        lse_all = jnp.zeros((ts, LANES), F32)
        for h in range(DIL_HEADS):
            sl = slice(h * HEAD_DIM, (h + 1) * HEAD_DIM)
            s = lax.dot_general(q_ref[r0:r0 + ts, sl], take(kp_ref, kc_ref, sl), (((1,), (1,)), ((), ())),
                                preferred_element_type=F32)
            num, den, m = _softmax_pv(jnp.where(mask, s, NEG_INF), take(vp_ref, vc_ref, sl))
            o_ref[r0:r0 + ts, sl] = (num / den).astype(o_ref.dtype)
            lse_all = jnp.where(lane == h, m * LN2 + jnp.log(den), lse_all)
        lse_ref[r0:r0 + ts, :] = lse_all


def _dilated(zv, r, b, seq, tq=512):
    sub = seq // r
    tq = min(tq, sub)
    assert sub % tq == 0 and tq % DIL_SPAN == 0
    nq = sub // tq
    hw = DIL_HEADS * HEAD_DIM
    ratio = tq // DIL_SPAN
    prev = lambda i: jnp.maximum(i * ratio - 1, 0)
    o, lse = pl.pallas_call(
        functools.partial(_dil_body, tq=tq), grid=(b, r, nq),
        in_specs=[pl.BlockSpec((None, tq, hw), lambda bi, c, i: (bi, i, c)),
                  pl.BlockSpec((None, DIL_SPAN, hw), lambda bi, c, i: (bi, prev(i), r + c)),
                  pl.BlockSpec((None, tq, hw), lambda bi, c, i: (bi, i, r + c)),
                  pl.BlockSpec((None, DIL_SPAN, hw), lambda bi, c, i: (bi, prev(i), 2 * r + c)),
                  pl.BlockSpec((None, tq, hw), lambda bi, c, i: (bi, i, 2 * r + c))],
        out_specs=[pl.BlockSpec((None, tq, hw), lambda bi, c, i: (bi, i, c)),
                   pl.BlockSpec((None, tq, LANES), lambda bi, c, i: (bi, i, c))],
        out_shape=[jax.ShapeDtypeStruct((b, sub, r * hw), BF16),
                   jax.ShapeDtypeStruct((b, sub, r * LANES), F32)],
        compiler_params=_cparams(("parallel", "parallel", "parallel")))(zv, zv, zv, zv, zv)
    return o.reshape(b * sub, r * hw), lse.reshape(b * sub, r * LANES)


def _finish(h_ref, m, g_ref, o_ref):
    y = m * lax.rsqrt(jnp.mean(m * m, axis=-1, keepdims=True) + EPS)
    o_ref[...] = h_ref[...] + y * g_ref[...]


def _out0_body(h_ref, oc_ref, os_ref, ow_ref, gate_ref, ob_ref, wa_ref, wb_ref, g_ref, o_ref):
    gate = jax.nn.sigmoid(gate_ref[...].astype(F32))
    parts = []
    for h in range(NSA_HEADS):
        sl = slice(h * HEAD_DIM, (h + 1) * HEAD_DIM)
        parts.append(gate[:, 3 * h:3 * h + 1] * oc_ref[:, sl].astype(F32)
                     + gate[:, 3 * h + 1:3 * h + 2] * os_ref[:, sl].astype(F32)
                     + gate[:, 3 * h + 2:3 * h + 3] * ow_ref[:, sl].astype(F32))
    oa = jnp.concatenate(parts, axis=1).astype(BF16)
    m = (jnp.dot(oa, wa_ref[...], preferred_element_type=F32)
         + jnp.dot(ob_ref[...], wb_ref[...], preferred_element_type=F32))
    _finish(h_ref, m, g_ref, o_ref)


def _to_token_order(piece, r, tm, first, second):
    s1 = min(r, MAX_ROW_STRIDE)
    s2 = r // s1
    if s2 == 1:
        for c in range(r):
            first[pl.ds(c, tm // r, stride=r), :] = piece(c)
        return
    for c in range(r):
        c1, c2 = c % s1, c // s1
        second[pl.ds(c1 * (tm // s1) + c2, tm // r, stride=s2), :] = piece(c)
    for c1 in range(s1):
        first[pl.ds(c1, tm // s1, stride=s1), :] = second[c1 * (tm // s1):(c1 + 1) * (tm // s1), :]


def _out1_body(*refs, dils, tm):
    n = len(dils)
    h_ref, o_refs, l_refs = refs[0], refs[1:1 + n], refs[1 + n:1 + 2 * n]
    w_ref, g_ref, o_ref = refs[1 + 2 * n:4 + 2 * n]
    stage = list(refs[4 + 2 * n:])
    hw = DIL_HEADS * HEAD_DIM
    outs, lses = [], []
    for p, r in enumerate(dils):
        if r == 1:
            outs.append([o_refs[p][:, h * HEAD_DIM:(h + 1) * HEAD_DIM].astype(F32) for h in range(DIL_HEADS)])
            lses.append(l_refs[p][...])
            continue
        bufs = []
        for j in range(DIL_HEADS + 1):
            first, second = stage.pop(0), stage.pop(0)
            if j < DIL_HEADS:
                piece = lambda c, j=j: o_refs[p][:, c * hw + j * HEAD_DIM:c * hw + (j + 1) * HEAD_DIM].astype(F32)
            else:
                piece = lambda c: l_refs[p][:, c * LANES:(c + 1) * LANES]
            _to_token_order(piece, r, tm, first, second)
            bufs.append(first)
        outs.append([bufs[h][...] for h in range(DIL_HEADS)])
        lses.append(bufs[DIL_HEADS][...])
    l0, l1, l2 = lses
    mx = jnp.maximum(jnp.maximum(l0, l1), l2)
    e0, e1, e2 = jnp.exp(l0 - mx), jnp.exp(l1 - mx), jnp.exp(l2 - mx)
    tot = e0 + e1 + e2
    a0, a1, a2 = e0 / tot, e1 / tot, e2 / tot
    parts = []
    for h in range(DIL_HEADS):
        parts.append(a0[:, h:h + 1] * outs[0][h] + a1[:, h:h + 1] * outs[1][h] + a2[:, h:h + 1] * outs[2][h])
    o = jnp.concatenate(parts, axis=1).astype(BF16)
    _finish(h_ref, jnp.dot(o, w_ref[...], preferred_element_type=F32), g_ref, o_ref)


def _row_spec(tm, w, col=0):
    return pl.BlockSpec((tm, w), lambda i: (i, col))


def _full_spec(shape):
    return pl.BlockSpec(shape, lambda i: (0,) * len(shape), pipeline_mode=pl.Buffered(1))


def _out0(h, o_c, o_s, o_w, z0, o_b, w, g, tm=512):
    t, d = h.shape
    tm = min(tm, t)
    ow = NSA_HEADS * HEAD_DIM
    w_half = lambda k: pl.BlockSpec((ow, d), lambda i: (k, 0), pipeline_mode=pl.Buffered(1))
    return pl.pallas_call(
        _out0_body, grid=(t // tm,),
        in_specs=[_row_spec(tm, d), _row_spec(tm, ow), _row_spec(tm, ow), _row_spec(tm, ow),
                  _row_spec(tm, LANES, Z0_GATE // LANES), _row_spec(tm, ow),
                  w_half(0), w_half(1), _full_spec((1, d))],
        out_specs=_row_spec(tm, d), out_shape=jax.ShapeDtypeStruct((t, d), F32),
        compiler_params=_cparams(("parallel",)))(h, o_c, o_s, o_w, z0, o_b, w, w, g.reshape(1, d))


def _out1(h, os_, lses, dils, w, g, tm=512):
    t, d = h.shape
    tm = min(tm, t)
    ow = DIL_HEADS * HEAD_DIM
    assert all(tm % (8 * r) == 0 for r in dils)
    n_stage = 2 * (DIL_HEADS + 1) * sum(1 for r in dils if r > 1)
    return pl.pallas_call(
        functools.partial(_out1_body, dils=tuple(dils), tm=tm), grid=(t // tm,),
        in_specs=[_row_spec(tm, d)] + [_row_spec(tm // r, r * ow) for r in dils]
                 + [_row_spec(tm // r, r * LANES) for r in dils] + [_full_spec(w.shape), _full_spec((1, d))],
        out_specs=_row_spec(tm, d), out_shape=jax.ShapeDtypeStruct((t, d), F32),
        scratch_shapes=[pltpu.VMEM((tm, LANES), F32)] * n_stage,
        compiler_params=_cparams(("parallel",)))(h, *os_, *lses, w, g.reshape(1, d))


def _mlp_body(h_ref, g1_ref, w1_ref, w2_ref, g2_ref, o_ref, xn_ref, acc_ref):
    f = pl.program_id(1)
    last = pl.num_programs(1) - 1

    def partial_sum(xn):
        a = jnp.maximum(jnp.dot(xn, w1_ref[...], preferred_element_type=F32), 0.0)
        return jnp.dot((a * a).astype(BF16), w2_ref[...], preferred_element_type=F32)

    @pl.when(f == 0)
    def _():
        x = h_ref[...]
        y = x * lax.rsqrt(jnp.mean(x * x, axis=-1, keepdims=True) + EPS)
        xn = (y * g1_ref[...]).astype(BF16)
        xn_ref[...] = xn
        acc_ref[...] = partial_sum(xn)

    @pl.when((f > 0) & (f < last))
    def _():
        acc_ref[...] += partial_sum(xn_ref[...])

    @pl.when(f == last)
    def _():
        _finish(h_ref, acc_ref[...] + partial_sum(xn_ref[...]), g2_ref, o_ref)


def _mlp(h, g1, w1, w2, g2, tm=512, tf=1024):
    t, d = h.shape
    ff = w1.shape[1]
    tm = min(tm, t)
    return pl.pallas_call(
        _mlp_body, grid=(t // tm, ff // tf),
        in_specs=[pl.BlockSpec((tm, d), lambda i, f: (i, 0)),
                  pl.BlockSpec((1, d), lambda i, f: (0, 0)),
                  pl.BlockSpec((d, tf), lambda i, f: (0, f)),
                  pl.BlockSpec((tf, d), lambda i, f: (f, 0)),
                  pl.BlockSpec((1, d), lambda i, f: (0, 0))],
        out_specs=pl.BlockSpec((tm, d), lambda i, f: (i, 0)),
        out_shape=jax.ShapeDtypeStruct((t, d), F32),
        scratch_shapes=[pltpu.VMEM((tm, d), BF16), pltpu.VMEM((tm, d), F32)],
        compiler_params=_cparams(("parallel", "arbitrary")))(h, g1.reshape(1, d), w1, w2, g2.reshape(1, d))


def _rope64_tile(w):
    z = jnp.zeros((w.shape[0], 32), w.dtype)
    return jnp.concatenate([w[:, :32], z, w[:, 32:], z], axis=1)


def _layer0_w_in(w_in):
    d = w_in.shape[0]
    o1 = NSA_HEADS * HEAD_DIM
    o2 = o1 + 3 * 2 * NSA_KV_GROUPS * HEAD_DIM
    o3 = o2 + 3 * NSA_HEADS
    o4 = o3 + MLA_Q_RANK
    o5 = o4 + MLA_KV_RANK
    kv = w_in[:, o1:o2].reshape(d, 3, 2, NSA_KV_GROUPS * HEAD_DIM)
    kv = kv.transpose(0, 2, 1, 3).reshape(d, o2 - o1)
    gate = jnp.pad(w_in[:, o2:o3], ((0, 0), (0, LANES - (o3 - o2))))
    w = jnp.concatenate([w_in[:, :o1], kv, w_in[:, o3:o4], w_in[:, o4:o5], gate, _rope64_tile(w_in[:, o5:])], 1)
    assert w.shape[1] == Z0_COLS
    return w.astype(BF16)


def _mla_w_uq(w_uq):
    d = w_uq.shape[0]
    w = w_uq.reshape(d, MLA_HEADS, MLA_NOPE + MLA_ROPE)
    tiles = [jnp.concatenate([w[:, h, :MLA_NOPE], _rope64_tile(w[:, h, MLA_NOPE:])], 1) for h in range(MLA_HEADS)]
    return jnp.concatenate(tiles, axis=1).astype(BF16)


def _mla_w_ukv(w_ukv):
    d = w_ukv.shape[0]
    w = w_ukv.reshape(d, MLA_HEADS, MLA_NOPE + MLA_V)
    return jnp.concatenate([w[:, :, :MLA_NOPE].reshape(d, -1), w[:, :, MLA_NOPE:].reshape(d, -1)], 1).astype(BF16)


def _layer1_w_in(w_in):
    return w_in.astype(BF16)


def _overlap_matrix(nc, n_cmp, n_slc, nsp):
    ratio = NSA_SEL_LEN // NSA_CMP_STRIDE
    m = np.zeros((nc, nsp), np.float32)
    for off in range(1 - NSA_CMP_LEN // NSA_CMP_STRIDE, ratio):
        n = np.arange(n_slc) * ratio + off
        ok = (n >= 0) & (n < n_cmp)
        m[n[ok], np.arange(n_slc)[ok]] = 1.0
    return jnp.asarray(m, BF16)


def kernel(x, l0_norm_mix_pre, l0_w_in, l0_cmp_pe_k, l0_cmp_w1_k, l0_cmp_w2_k, l0_cmp_pe_v, l0_cmp_w1_v, l0_cmp_w2_v, l0_mla_q_norm, l0_mla_w_uq, l0_mla_kv_norm, l0_mla_w_ukv, l0_w_out, l0_norm_mix_post, l0_norm_ffn_pre, l0_w_ff1, l0_w_ff2, l0_norm_ffn_post, l1_norm_mix_pre, l1_w_in, l1_w_out, l1_norm_mix_post, l1_norm_ffn_pre, l1_w_ff1, l1_w_ff2, l1_norm_ffn_post):
    b, seq, d = x.shape
    t = b * seq
    assert seq % NSA_WINDOW == 0 and seq % (DIL_PATTERNS[-1][1] * DIL_SPAN) == 0
    rot = _rope_tables(seq)
    h = x.reshape(t, d)

    tabs0 = ([TAB_ROPE128_Q] * NSA_HEADS + [TAB_ROPE128] * (3 * NSA_KV_GROUPS)
             + [TAB_IDENT] * ((Z0_KROPE - Z0_VCMP) // LANES) + [TAB_ROPE64])
    z0 = _proj_res(h, 0, d, l0_norm_mix_pre, _layer0_w_in(l0_w_in), seq, tabs0, rot)

    nc = seq // NSA_CMP_STRIDE
    n_cmp = (seq - NSA_CMP_LEN) // NSA_CMP_STRIDE + 1
    n_slc = seq // NSA_SEL_LEN
    nsp = -(-n_slc // LANES) * LANES
    gd = NSA_KV_GROUPS * HEAD_DIM

    def chunks(col):
        c = z0[:, col:col + gd].reshape(b, nc, NSA_CMP_STRIDE, NSA_KV_GROUPS, HEAD_DIM)
        return c.transpose(0, 3, 1, 2, 4).reshape(b, NSA_KV_GROUPS, nc, NSA_CMP_STRIDE * HEAD_DIM)

    pe = jnp.stack([l0_cmp_pe_k.reshape(1, -1), l0_cmp_pe_v.reshape(1, -1)])
    kv_cmp = _compress(jnp.stack([chunks(Z0_KCMP), chunks(Z0_VCMP)]),
                       jnp.stack([l0_cmp_w1_k, l0_cmp_w1_v]).astype(BF16),
                       jnp.stack([l0_cmp_w2_k, l0_cmp_w2_v]).astype(BF16),
                       jnp.broadcast_to(pe, (2, 8, pe.shape[-1])).astype(F32))
    o_c, bias = _nsa_cmp(z0, kv_cmp, _overlap_matrix(nc, n_cmp, n_slc, nsp), b, seq)
    o_w = _nsa_win(z0, b, seq)
    blk_lane = (jnp.arange(seq, dtype=jnp.int32) // NSA_SEL_LEN) % LANES
    onehot = (blk_lane[:, None] == jnp.arange(LANES, dtype=jnp.int32)[None, :]).astype(BF16)
    o_s = _nsa_sel(z0, bias, onehot, b, seq)

    qcat = _proj_res(z0, Z0_CQ // MLA_Q_RANK, MLA_Q_RANK, l0_mla_q_norm, _mla_w_uq(l0_mla_w_uq), seq,
                     [TAB_IDENT_Q, TAB_ROPE64_Q] * MLA_HEADS, rot)
    kv = _proj_res(z0, Z0_CKV // MLA_KV_RANK, MLA_KV_RANK, l0_mla_kv_norm, _mla_w_ukv(l0_mla_w_ukv), seq,
                   [TAB_IDENT] * (2 * MLA_HEADS), rot)
    o_b = _mla(qcat, kv, z0, b, seq)

    h = _out0(h, o_c, o_s, o_w, z0, o_b, l0_w_out.astype(BF16), l0_norm_mix_post)
    h = _mlp(h, l0_norm_ffn_pre, l0_w_ff1.astype(BF16), l0_w_ff2.astype(BF16), l0_norm_ffn_post)

    hw = DIL_HEADS * HEAD_DIM
    tabs_g = ([TAB_ROPE128_Q] * DIL_HEADS + [TAB_ROPE128] * DIL_HEADS
              + [TAB_IDENT] * DIL_HEADS)
    w1_in = _layer1_w_in(l1_w_in)
    outs = []
    for p, (_, r) in enumerate(DIL_PATTERNS):
        z1 = _proj_res(h, 0, d, l1_norm_mix_pre, w1_in, seq, tabs_g, rot, dil=r, w_col=p)
        outs.append(_dilated(z1.reshape(b, seq // r, 3 * r * hw), r, b, seq))
    h = _out1(h, [o for o, _ in outs], [l for _, l in outs], [r for _, r in DIL_PATTERNS],
              l1_w_out.astype(BF16), l1_norm_mix_post)
    h = _mlp(h, l1_norm_ffn_pre, l1_w_ff1.astype(BF16), l1_w_ff2.astype(BF16), l1_norm_ffn_post)
    return h.reshape(b, seq, d)
```

```python
import functools

import numpy as np
import jax
import jax.numpy as jnp
from jax import lax
from jax.experimental import pallas as pl
from jax.experimental.pallas import tpu as pltpu

F32 = jnp.float32
BF16 = jnp.bfloat16

HEAD_DIM = 128
LANES = 128
SUBLANES = 8
BF16_ROWS = 16
ROPE_THETA = 10000.0
EPS = 1e-6
NEG_INF = -1e30
POS_INF = 1e30
PICKED = -3e38

NSA_HEADS = 8
NSA_KV_GROUPS = 2
NSA_GROUP_HEADS = NSA_HEADS // NSA_KV_GROUPS
NSA_CMP_LEN = 32
NSA_CMP_STRIDE = 16
NSA_SEL_LEN = 64
NSA_SEL_TOPK = 16
NSA_WINDOW = 512

MLA_HEADS = 8
MLA_Q_RANK = 512
MLA_KV_RANK = 256
MLA_NOPE = 128
MLA_ROPE = 64
MLA_V = 128

DIL_PATTERNS = ((128, 1), (512, 4), (2048, 16))
DIL_HEADS = 8
DIL_SPAN = 128

VMEM_LIMIT = 56 * 1024 * 1024
MAX_ROW_STRIDE = 4

Z0_Q = 0
Z0_KCMP = 1024
Z0_KSLC = 1280
Z0_KWIN = 1536
Z0_VCMP = 1792
Z0_VSLC = 2048
Z0_VWIN = 2304
Z0_CQ = 2560
Z0_CKV = 3072
Z0_GATE = 3328
Z0_KROPE = 3456
Z0_COLS = 3584

TAB_ROPE128, TAB_IDENT, TAB_ROPE64, TAB_ROPE128_Q, TAB_ROPE64_Q, TAB_IDENT_Q = 0, 1, 2, 3, 4, 5
TABLE_KINDS = (TAB_ROPE128, TAB_ROPE64, TAB_ROPE128_Q, TAB_ROPE64_Q)

LOG2E = 1.4426950408889634
LN2 = 0.6931471805599453
QSCALE_128 = HEAD_DIM ** -0.5 * LOG2E
QSCALE_MLA = (MLA_NOPE + MLA_ROPE) ** -0.5 * LOG2E


def _cparams(sem):
    return pltpu.CompilerParams(dimension_semantics=sem, vmem_limit_bytes=VMEM_LIMIT)


def _rope_tables(seq):
    def cs(dim):
        inv = 1.0 / (ROPE_THETA ** (jnp.arange(0, dim, 2, dtype=F32) / dim))
        ang = jnp.arange(seq, dtype=F32)[:, None] * inv[None, :]
        return jnp.cos(ang), jnp.sin(ang)

    c128, s128 = cs(HEAD_DIM)
    c64, s64 = cs(MLA_ROPE)
    one = jnp.ones((seq, 32), F32)
    zero = jnp.zeros((seq, 32), F32)
    r128 = jnp.concatenate([c128, c128, -s128, s128], axis=1)
    r64 = jnp.concatenate([c64, one, c64, one, -s64, zero, s64, zero], axis=1)
    tables = {TAB_ROPE128: r128, TAB_ROPE64: r64, TAB_ROPE128_Q: r128 * QSCALE_128,
              TAB_ROPE64_Q: r64 * QSCALE_MLA}
    return jnp.stack([tables[k] for k in TABLE_KINDS])


def _proj_res_body(*refs, kinds, tn, scale_q, dil):
    x_ref, g_ref, w_ref = refs[:3]
    used = sorted({k for k in kinds if k not in (TAB_IDENT, TAB_IDENT_Q)})
    t_refs = dict(zip(used, refs[3:3 + len(used)]))
    o_ref, xn_ref = refs[3 + len(used):5 + len(used)]
    stage_refs = refs[5 + len(used):]
    tm = x_ref.shape[0]
    x = x_ref[...].astype(F32)
    y = x * lax.rsqrt(jnp.mean(x * x, axis=-1, keepdims=True) + EPS)
    xn_ref[...] = (y * g_ref[...]).astype(BF16)
    nl = tn // LANES
    for j in range(w_ref.shape[1] // tn):
        acc = jnp.dot(xn_ref[...], w_ref[:, j * tn:(j + 1) * tn], preferred_element_type=F32)
        parts = []
        for k in range(nl):
            kind = kinds[j * nl + k]
            a = acc[:, k * LANES:(k + 1) * LANES]
            if kind == TAB_IDENT:
                parts.append(a)
            elif kind == TAB_IDENT_Q:
                parts.append(a * scale_q)
            else:
                tab = t_refs[kind]
                parts.append(a * tab[:, :LANES] + pltpu.roll(a, LANES // 2, 1) * tab[:, LANES:])
        if dil == 1:
            o_ref[:, j * tn:(j + 1) * tn] = jnp.concatenate(parts, axis=1).astype(o_ref.dtype)
            continue
        hw = DIL_HEADS * HEAD_DIM
        s1 = min(dil, MAX_ROW_STRIDE)
        s2 = dil // s1
        for k in range(nl):
            g = j * nl + k
            first, second = stage_refs[2 * k], stage_refs[2 * k + 1]
            first[...] = parts[k]
            if s2 > 1:
                for c1 in range(s1):
                    second[c1 * (tm // s1):(c1 + 1) * (tm // s1), :] = first[pl.ds(c1, tm // s1, stride=s1), :]
            for c in range(dil):
                c1, c2 = c % s1, c // s1
                if s2 > 1:
                    piece = second[pl.ds(c1 * (tm // s1) + c2, tm // dil, stride=s2), :]
                else:
                    piece = first[pl.ds(c, tm // dil, stride=dil), :]
                col = ((g // DIL_HEADS) * dil + c) * hw + (g % DIL_HEADS) * LANES
                o_ref[:, col:col + LANES] = piece.astype(o_ref.dtype)


def _proj_res(x, x_col, d, g, w, seq, kinds, rot, tm=512, tn=512, dil=1, w_col=0):
    t = x.shape[0]
    n = len(kinds) * LANES
    assert w.shape[1] % n == 0
    tm = min(tm, seq)
    assert t % tm == 0 and seq % tm == 0 and n % tn == 0 and len(kinds) == n // LANES and tm % (BF16_ROWS * dil) == 0
    spb = seq // tm
    used = sorted({k for k in kinds if k not in (TAB_IDENT, TAB_IDENT_Q)})
    tab_specs = [pl.BlockSpec((None, tm, 2 * LANES), lambda i, row=TABLE_KINDS.index(kind): (row, i % spb, 0))
                 for kind in used]
    scratch = [pltpu.VMEM((tm, d), BF16)]
    if dil > 1:
        scratch += [pltpu.VMEM((tm, LANES), F32)] * (2 * (tn // LANES))
    return pl.pallas_call(
        functools.partial(_proj_res_body, kinds=tuple(kinds), tn=tn, scale_q=QSCALE_MLA, dil=dil),
        grid=(t // tm,),
        in_specs=[pl.BlockSpec((tm, d), lambda i: (i, x_col)),
                  pl.BlockSpec((1, d), lambda i: (0, 0)),
                  pl.BlockSpec((d, n), lambda i: (0, w_col), pipeline_mode=pl.Buffered(1))] + tab_specs,
        out_specs=pl.BlockSpec((tm // dil, dil * n), lambda i: (i, 0)),
        scratch_shapes=scratch,
        out_shape=jax.ShapeDtypeStruct((t // dil, dil * n), BF16),
        compiler_params=_cparams(("parallel",)))(x, g.reshape(1, d).astype(F32), w, *([rot] * len(used)))


def _compress_body(c_ref, w1_ref, w2_ref, pe_ref, o_ref, *, nc):
    half = NSA_CMP_STRIDE * HEAD_DIM
    c = c_ref[...]
    a = jnp.dot(c, w1_ref[:half, :], preferred_element_type=F32)
    b = jnp.dot(c, w1_ref[half:, :], preferred_element_type=F32)
    pe = pe_ref[...]
    pe_hi = pe.astype(BF16)
    pe_lo = (pe - pe_hi.astype(F32)).astype(BF16)
    pe_term = (jnp.dot(pe_hi, w1_ref[...], preferred_element_type=F32)
               + jnp.dot(pe_lo, w1_ref[...], preferred_element_type=F32))
    hid = a + pltpu.roll(b, nc - 1, 0) + pe_term[0:1, :]
    act = jax.nn.gelu(hid)
    o_ref[...] = jnp.dot(act.astype(BF16), w2_ref[...], preferred_element_type=F32).astype(o_ref.dtype)


def _compress(chunks, w1, w2, pe):
    _, b, g, nc, cw = chunks.shape
    return pl.pallas_call(
        functools.partial(_compress_body, nc=nc), grid=(2, b, g),
        in_specs=[pl.BlockSpec((None, None, None, nc, cw), lambda s, i, j: (s, i, j, 0, 0)),
                  pl.BlockSpec((None, 2 * cw, HEAD_DIM), lambda s, i, j: (s, 0, 0)),
                  pl.BlockSpec((None, HEAD_DIM, HEAD_DIM), lambda s, i, j: (s, 0, 0)),
                  pl.BlockSpec((None, SUBLANES, 2 * cw), lambda s, i, j: (s, 0, 0))],
        out_specs=pl.BlockSpec((None, None, None, nc, HEAD_DIM), lambda s, i, j: (s, i, j, 0, 0)),
        out_shape=jax.ShapeDtypeStruct((2, b, g, nc, HEAD_DIM), BF16),
        compiler_params=_cparams(("parallel", "parallel", "parallel")))(chunks, w1, w2, pe)


def _nsa_cmp_body(q_ref, k_ref, v_ref, m_ref, o_ref, bias_ref, imp_scr, *, tq, nc, nsp, topk, nvar,
                  tiles_per_var):
    i = pl.program_id(2)
    t = i * tq + lax.broadcasted_iota(jnp.int32, (tq, 1), 0)
    any_vis = (t >= NSA_CMP_LEN - 1).astype(F32)

    def attend(ncols):
        n = lax.broadcasted_iota(jnp.int32, (1, ncols), 1)
        vis = (n * NSA_CMP_STRIDE + (NSA_CMP_LEN - 1)) <= t
        k = k_ref[:ncols, :]
        v = v_ref[:ncols, :]
        psum = jnp.zeros((tq, ncols), F32)
        for h in range(NSA_GROUP_HEADS):
            sl = slice(h * HEAD_DIM, (h + 1) * HEAD_DIM)
            s = lax.dot_general(q_ref[:, sl], k, (((1,), (1,)), ((), ())), preferred_element_type=F32)
            s = jnp.where(vis, s, NEG_INF)
            e = jnp.exp2(s - jnp.max(s, axis=-1, keepdims=True))
            p = e * (any_vis / jnp.sum(e, axis=-1, keepdims=True))
            o_ref[:, sl] = jnp.dot(p.astype(BF16), v, preferred_element_type=F32).astype(o_ref.dtype)
            psum = psum + p
        hi = psum.astype(BF16)
        lo = (psum - hi.astype(F32)).astype(BF16)
        imp_scr[...] = (jnp.dot(hi, m_ref[:ncols, :], preferred_element_type=F32)
                        + jnp.dot(lo, m_ref[:ncols, :], preferred_element_type=F32))

    def select(nblk):
        imp = imp_scr[...].T[:nblk]
        tl = i * tq + lax.broadcasted_iota(jnp.int32, (1, tq), 1)
        blk = lax.broadcasted_iota(jnp.int32, (nblk, 1), 0)
        blkf = blk.astype(F32)
        cur = lax.shift_right_logical(tl, NSA_SEL_LEN.bit_length() - 1)
        forced = (blk == 0) | (blk == cur) | (blk == cur - 1)
        causal = blk * NSA_SEL_LEN <= tl
        score = jnp.where(forced, PICKED, jnp.where(causal, imp, NEG_INF))

        def take_one(_, score):
            mx = jnp.max(score, axis=0, keepdims=True)
            first = jnp.min(jnp.where(score == mx, blkf, float(nsp)), axis=0, keepdims=True)
            return jnp.where(blkf == first, PICKED, score)

        score = lax.fori_loop(0, topk - 3, take_one, score)
        sel = (score == PICKED) & causal
        bias = jnp.where(sel, 0.0, NEG_INF)
        if nblk < nsp:
            bias = jnp.concatenate([bias, jnp.full((nsp - nblk, tq), NEG_INF, F32)], axis=0)
        bias_ref[...] = bias.T.astype(bias_ref.dtype)

    for var in range(nvar):
        @pl.when((i >= var * tiles_per_var) & (i < (var + 1) * tiles_per_var))
        def _(var=var):
            attend((var + 1) * nc // nvar)
            select((var + 1) * nsp // nvar)


def _nsa_cmp(z0, kv_cmp, ovl, b, seq, tq=512):
    nc = kv_cmp.shape[3]
    nsp = ovl.shape[1]
    tq = min(tq, seq)
    nq = seq // tq
    gw = NSA_GROUP_HEADS * HEAD_DIM
    topk = min(NSA_SEL_TOPK, seq // NSA_SEL_LEN)
    assert topk >= 3
    nvar = 4 if (nq % 4 == 0 and nc % (4 * 2 * LANES) == 0) else 1
    body = functools.partial(_nsa_cmp_body, tq=tq, nc=nc, nsp=nsp, topk=topk, nvar=nvar, tiles_per_var=nq // nvar)
    return pl.pallas_call(
        body, grid=(b, NSA_KV_GROUPS, nq),
        scratch_shapes=[pltpu.VMEM((tq, nsp), F32)],
        in_specs=[pl.BlockSpec((tq, gw), lambda bi, g, i: (bi * nq + i, g)),
                  pl.BlockSpec((None, None, None, nc, HEAD_DIM), lambda bi, g, i: (0, bi, g, 0, 0)),
                  pl.BlockSpec((None, None, None, nc, HEAD_DIM), lambda bi, g, i: (1, bi, g, 0, 0)),
                  pl.BlockSpec((nc, nsp), lambda bi, g, i: (0, 0))],
        out_specs=[pl.BlockSpec((tq, gw), lambda bi, g, i: (bi * nq + i, g)),
                   pl.BlockSpec((None, None, tq, nsp), lambda bi, g, i: (bi, g, i, 0))],
        out_shape=[jax.ShapeDtypeStruct((b * seq, NSA_HEADS * HEAD_DIM), BF16),
                   jax.ShapeDtypeStruct((b, NSA_KV_GROUPS, seq, nsp), BF16)],
        compiler_params=_cparams(("parallel", "parallel", "parallel")))(z0, kv_cmp, kv_cmp, ovl)


def _lane_tile(x, n):
    return jnp.concatenate([x] * n, axis=1) if n > 1 else x


def _softmax_pv(s, v):
    tq, tk = s.shape
    d = v.shape[1]
    m = jnp.broadcast_to(jnp.max(s, axis=-1, keepdims=True), (tq, LANES))
    p = jnp.exp2((s - _lane_tile(m, tk // LANES)).astype(BF16))
    acc = jnp.dot(p, jnp.concatenate([v, jnp.ones((tk, LANES), BF16)], axis=1), preferred_element_type=F32)
    return acc[:, :d], acc[:, d:], m


WIN_SUB = 256
DIL_SUB = 128


def _band_subtiles(i, tq, w, lo, hi, sub):
    ts = min(sub, tq)
    qq = lax.broadcasted_iota(jnp.int32, (ts, 1), 0)
    kk = lax.broadcasted_iota(jnp.int32, (1, ts + w), 1)
    rel = qq + w - kk
    band = (rel >= lo) & (rel <= hi)
    out = []
    for a in range(tq // ts):
        r0 = a * ts
        if r0 < w:
            mask = band & ((kk >= w - r0) | (i > 0))
            take = lambda p, c, lanes, r0=r0: jnp.concatenate([p[r0:, lanes], c[:r0 + ts, lanes]], axis=0)
        else:
            mask = band
            take = lambda p, c, lanes, r0=r0: c[r0 - w:r0 + ts, lanes]
        out.append((r0, ts, mask, take))
    return out


def _nsa_win_body(q_ref, kp_ref, kc_ref, vp_ref, vc_ref, o_ref, *, tq):
    i = pl.program_id(2)
    tiles = _band_subtiles(i, tq, NSA_WINDOW, 0, NSA_WINDOW - 1, WIN_SUB)
    for r0, ts, mask, take in tiles:
        k = take(kp_ref, kc_ref, slice(None))
        v = take(vp_ref, vc_ref, slice(None))
        scores = [jnp.where(mask, lax.dot_general(q_ref[r0:r0 + ts, h * HEAD_DIM:(h + 1) * HEAD_DIM], k,
                                                  (((1,), (1,)), ((), ())), preferred_element_type=F32),
                            NEG_INF) for h in range(NSA_GROUP_HEADS)]
        for h in range(NSA_GROUP_HEADS):
            num, den, _ = _softmax_pv(scores[h], v)
            o_ref[r0:r0 + ts, h * HEAD_DIM:(h + 1) * HEAD_DIM] = (num / den).astype(o_ref.dtype)


def _nsa_win(z0, b, seq):
    tq = NSA_WINDOW
    assert seq % tq == 0
    nq = seq // tq
    gw = NSA_GROUP_HEADS * HEAD_DIM
    kcol, vcol = Z0_KWIN // HEAD_DIM, Z0_VWIN // HEAD_DIM
    prev = lambda bi, i: bi * nq + jnp.maximum(i - 1, 0)
    return pl.pallas_call(
        functools.partial(_nsa_win_body, tq=tq), grid=(b, NSA_KV_GROUPS, nq),
        in_specs=[pl.BlockSpec((tq, gw), lambda bi, g, i: (bi * nq + i, g)),
                  pl.BlockSpec((tq, HEAD_DIM), lambda bi, g, i: (prev(bi, i), kcol + g)),
                  pl.BlockSpec((tq, HEAD_DIM), lambda bi, g, i: (bi * nq + i, kcol + g)),
                  pl.BlockSpec((tq, HEAD_DIM), lambda bi, g, i: (prev(bi, i), vcol + g)),
                  pl.BlockSpec((tq, HEAD_DIM), lambda bi, g, i: (bi * nq + i, vcol + g))],
        out_specs=pl.BlockSpec((tq, gw), lambda bi, g, i: (bi * nq + i, g)),
        out_shape=jax.ShapeDtypeStruct((b * seq, NSA_HEADS * HEAD_DIM), BF16),
        compiler_params=_cparams(("parallel", "parallel", "parallel")))(z0, z0, z0, z0, z0)


def _tri_schedule(nq):
    qi = np.concatenate([np.full(i + 1, i) for i in range(nq)]).astype(np.int32)
    kj = np.concatenate([np.arange(i + 1) for i in range(nq)]).astype(np.int32)
    return jnp.asarray(qi), jnp.asarray(kj)


def _flash_init(m_scr, acc_scr):
    m_scr[...] = jnp.full(m_scr.shape, NEG_INF, F32)
    acc_scr[...] = jnp.zeros(acc_scr.shape, F32)


def _flash_update(scores, vs, m_scr, acc_scr, rows=slice(None)):
    nh = len(scores)
    reps = scores[0].shape[1] // LANES
    ones = jnp.ones((vs[0].shape[0], LANES), BF16)
    m_prev = [m_scr[h, rows] for h in range(nh)]
    m_new = [jnp.maximum(m_prev[h], jnp.max(scores[h], axis=-1, keepdims=True)) for h in range(nh)]
    ps = [jnp.exp2((scores[h] - _lane_tile(m_new[h], reps)).astype(BF16)) for h in range(nh)]
    alphas = [jnp.exp2(m_prev[h] - m_new[h]) for h in range(nh)]
    for h in range(nh):
        v_aug = jnp.concatenate([vs[h], ones], axis=1)
        acc_scr[h, rows] = (_lane_tile(alphas[h], acc_scr.shape[2] // LANES) * acc_scr[h, rows]
                            + jnp.dot(ps[h], v_aug, preferred_element_type=F32))
        m_scr[h, rows] = m_new[h]


def _flash_out(acc_scr, h, d):
    acc = acc_scr[h]
    return acc[:, :d] / acc[:, d:]


DIAG_SUB = 256


def _flash_tile(qi, kj, tq, scores_and_values, m_scr, acc_scr, write_out):
    @pl.when(kj < qi)
    def _():
        scores, vs = scores_and_values(slice(None), tq)
        _flash_update(scores, vs, m_scr, acc_scr)

    @pl.when(kj == qi)
    def _():
        ts = min(DIAG_SUB, tq)
        for r0 in range(0, tq, ts):
            scores, vs = scores_and_values(slice(r0, r0 + ts), r0 + ts)
            row = r0 + lax.broadcasted_iota(jnp.int32, (ts, 1), 0)
            col = lax.broadcasted_iota(jnp.int32, (1, r0 + ts), 1)
            _flash_update([jnp.where(col <= row, s, NEG_INF) for s in scores], vs, m_scr, acc_scr,
                          slice(r0, r0 + ts))
        write_out()


def _nsa_sel_body(qi_ref, kj_ref, q_ref, k_ref, v_ref, oh_ref, bias_ref, o_ref,
                  qa_scr, m_scr, acc_scr, *, tq, per_slab):
    step = pl.program_id(2)
    qi = qi_ref[step]
    kj = kj_ref[step]

    @pl.when(kj == 0)
    def _():
        _flash_init(m_scr, acc_scr)
        for h in range(NSA_GROUP_HEADS):
            qa_scr[h, :, :HEAD_DIM] = q_ref[:, h * HEAD_DIM:(h + 1) * HEAD_DIM]

    @pl.when(kj % per_slab == 0)
    def _():
        for h in range(NSA_GROUP_HEADS):
            qa_scr[h, :, HEAD_DIM:] = bias_ref[...]

    def scores_and_values(rows, nkeys):
        k = jnp.concatenate([k_ref[:nkeys, :], oh_ref[:nkeys, :]], axis=1)
        scores = [lax.dot_general(qa_scr[h, rows], k, (((1,), (1,)), ((), ())), preferred_element_type=F32)
                  for h in range(NSA_GROUP_HEADS)]
        return scores, [v_ref[:nkeys, :]] * NSA_GROUP_HEADS

    def write_out():
        for h in range(NSA_GROUP_HEADS):
            o_ref[:, h * HEAD_DIM:(h + 1) * HEAD_DIM] = _flash_out(acc_scr, h, HEAD_DIM).astype(o_ref.dtype)

    _flash_tile(qi, kj, tq, scores_and_values, m_scr, acc_scr, write_out)


def _nsa_sel(z0, bias, onehot, b, seq, tq=1024):
    tq = min(tq, seq)
    nq = seq // tq
    gw = NSA_GROUP_HEADS * HEAD_DIM
    kcol, vcol = Z0_KSLC // HEAD_DIM, Z0_VSLC // HEAD_DIM
    per_slab = max(LANES * NSA_SEL_LEN // tq, 1)
    qi, kj = _tri_schedule(nq)
    grid_spec = pltpu.PrefetchScalarGridSpec(
        num_scalar_prefetch=2, grid=(b, NSA_KV_GROUPS, int(qi.shape[0])),
        in_specs=[pl.BlockSpec((tq, gw), lambda bi, g, s, qi, kj: (bi * nq + qi[s], g)),
                  pl.BlockSpec((tq, HEAD_DIM), lambda bi, g, s, qi, kj: (bi * nq + kj[s], kcol + g)),
                  pl.BlockSpec((tq, HEAD_DIM), lambda bi, g, s, qi, kj: (bi * nq + kj[s], vcol + g)),
                  pl.BlockSpec((tq, LANES), lambda bi, g, s, qi, kj: (kj[s], 0)),
                  pl.BlockSpec((None, None, tq, LANES),
                               lambda bi, g, s, qi, kj: (bi, g, qi[s], kj[s] // per_slab))],
        out_specs=pl.BlockSpec((tq, gw), lambda bi, g, s, qi, kj: (bi * nq + qi[s], g)),
        scratch_shapes=[pltpu.VMEM((NSA_GROUP_HEADS, tq, 2 * HEAD_DIM), BF16),
                        pltpu.VMEM((NSA_GROUP_HEADS, tq, LANES), F32),
                        pltpu.VMEM((NSA_GROUP_HEADS, tq, HEAD_DIM + LANES), F32)])
    return pl.pallas_call(
        functools.partial(_nsa_sel_body, tq=tq, per_slab=per_slab), grid_spec=grid_spec,
        out_shape=jax.ShapeDtypeStruct((b * seq, NSA_HEADS * HEAD_DIM), BF16),
        compiler_params=_cparams(("parallel", "parallel", "arbitrary")))(qi, kj, z0, z0, z0, onehot, bias)


MLA_STEP_HEADS = 4


def _mla_body(qi_ref, kj_ref, q_ref, kn_ref, kp_ref, v_ref, o_ref, m_scr, acc_scr, *, tq):
    step = pl.program_id(2)
    qi = qi_ref[step]
    kj = kj_ref[step]

    @pl.when(kj == 0)
    def _():
        _flash_init(m_scr, acc_scr)

    def scores_and_values(rows, nkeys):
        kp = kp_ref[:nkeys, :]
        scores, vs = [], []
        for h in range(MLA_STEP_HEADS):
            k = jnp.concatenate([kn_ref[:nkeys, h * MLA_NOPE:(h + 1) * MLA_NOPE], kp], axis=1)
            scores.append(lax.dot_general(q_ref[rows, h * 2 * LANES:(h + 1) * 2 * LANES], k,
                                          (((1,), (1,)), ((), ())), preferred_element_type=F32))
            vs.append(v_ref[:nkeys, h * MLA_V:(h + 1) * MLA_V])
        return scores, vs

    def write_out():
        for h in range(MLA_STEP_HEADS):
            o_ref[:, h * MLA_V:(h + 1) * MLA_V] = _flash_out(acc_scr, h, MLA_V).astype(o_ref.dtype)

    _flash_tile(qi, kj, tq, scores_and_values, m_scr, acc_scr, write_out)


def _mla(qcat, kv, z0, b, seq, tq=1024):
    tq = min(tq, seq)
    nq = seq // tq
    nh = MLA_STEP_HEADS
    ngrp = MLA_HEADS // nh
    kpcol = Z0_KROPE // LANES
    qi, kj = _tri_schedule(nq)
    grid_spec = pltpu.PrefetchScalarGridSpec(
        num_scalar_prefetch=2, grid=(b, ngrp, int(qi.shape[0])),
        in_specs=[pl.BlockSpec((tq, nh * 2 * LANES), lambda bi, h, s, qi, kj: (bi * nq + qi[s], h)),
                  pl.BlockSpec((tq, nh * MLA_NOPE), lambda bi, h, s, qi, kj: (bi * nq + kj[s], h)),
                  pl.BlockSpec((tq, LANES), lambda bi, h, s, qi, kj: (bi * nq + kj[s], kpcol)),
                  pl.BlockSpec((tq, nh * MLA_V), lambda bi, h, s, qi, kj: (bi * nq + kj[s], ngrp + h))],
        out_specs=pl.BlockSpec((tq, nh * MLA_V), lambda bi, h, s, qi, kj: (bi * nq + qi[s], h)),
        scratch_shapes=[pltpu.VMEM((nh, tq, LANES), F32), pltpu.VMEM((nh, tq, MLA_V + LANES), F32)])
    return pl.pallas_call(
        functools.partial(_mla_body, tq=tq), grid_spec=grid_spec,
        out_shape=jax.ShapeDtypeStruct((b * seq, MLA_HEADS * MLA_V), BF16),
        compiler_params=_cparams(("parallel", "parallel", "arbitrary")))(qi, kj, qcat, kv, z0, kv)


def _dil_body(q_ref, kp_ref, kc_ref, vp_ref, vc_ref, o_ref, lse_ref, *, tq):
    i = pl.program_id(2)
    for r0, ts, mask, take in _band_subtiles(i, tq, DIL_SPAN, 0, DIL_SPAN, DIL_SUB):
        lane = lax.broadcasted_iota(jnp.int32, (ts, LANES), 1)
        lse_all = jnp.zeros((ts, LANES), F32)
        for h in range(DIL_HEADS):
            sl = slice(h * HEAD_DIM, (h + 1) * HEAD_DIM)
            s = lax.dot_general(q_ref[r0:r0 + ts, sl], take(kp_ref, kc_ref, sl), (((1,), (1,)), ((), ())),
                                preferred_element_type=F32)
            num, den, m = _softmax_pv(jnp.where(mask, s, NEG_INF), take(vp_ref, vc_ref, sl))
            o_ref[r0:r0 + ts, sl] = (num / den).astype(o_ref.dtype)
            lse_all = jnp.where(lane == h, m * LN2 + jnp.log(den), lse_all)
        lse_ref[r0:r0 + ts, :] = lse_all


def _dilated(zv, r, b, seq, tq=512):
    sub = seq // r
    tq = min(tq, sub)
    assert sub % tq == 0 and tq % DIL_SPAN == 0
    nq = sub // tq
    hw = DIL_HEADS * HEAD_DIM
    ratio = tq // DIL_SPAN
    prev = lambda i: jnp.maximum(i * ratio - 1, 0)
    o, lse = pl.pallas_call(
        functools.partial(_dil_body, tq=tq), grid=(b, r, nq),
        in_specs=[pl.BlockSpec((None, tq, hw), lambda bi, c, i: (bi, i, c)),
                  pl.BlockSpec((None, DIL_SPAN, hw), lambda bi, c, i: (bi, prev(i), r + c)),
                  pl.BlockSpec((None, tq, hw), lambda bi, c, i: (bi, i, r + c)),
                  pl.BlockSpec((None, DIL_SPAN, hw), lambda bi, c, i: (bi, prev(i), 2 * r + c)),
                  pl.BlockSpec((None, tq, hw), lambda bi, c, i: (bi, i, 2 * r + c))],
        out_specs=[pl.BlockSpec((None, tq, hw), lambda bi, c, i: (bi, i, c)),
                   pl.BlockSpec((None, tq, LANES), lambda bi, c, i: (bi, i, c))],
        out_shape=[jax.ShapeDtypeStruct((b, sub, r * hw), BF16),
                   jax.ShapeDtypeStruct((b, sub, r * LANES), F32)],
        compiler_params=_cparams(("parallel", "parallel", "parallel")))(zv, zv, zv, zv, zv)
    return o.reshape(b * sub, r * hw), lse.reshape(b * sub, r * LANES)


def _finish(h_ref, m, g_ref, o_ref):
    y = m * lax.rsqrt(jnp.mean(m * m, axis=-1, keepdims=True) + EPS)
    o_ref[...] = h_ref[...] + y * g_ref[...]


def _out0_body(h_ref, oc_ref, os_ref, ow_ref, gate_ref, ob_ref, wa_ref, wb_ref, g_ref, o_ref):
    gate = jax.nn.sigmoid(gate_ref[...].astype(F32))
    parts = []
    for h in range(NSA_HEADS):
        sl = slice(h * HEAD_DIM, (h + 1) * HEAD_DIM)
        parts.append(gate[:, 3 * h:3 * h + 1] * oc_ref[:, sl].astype(F32)
                     + gate[:, 3 * h + 1:3 * h + 2] * os_ref[:, sl].astype(F32)
                     + gate[:, 3 * h + 2:3 * h + 3] * ow_ref[:, sl].astype(F32))
    oa = jnp.concatenate(parts, axis=1).astype(BF16)
    m = (jnp.dot(oa, wa_ref[...], preferred_element_type=F32)
         + jnp.dot(ob_ref[...], wb_ref[...], preferred_element_type=F32))
    _finish(h_ref, m, g_ref, o_ref)


def _to_token_order(piece, r, tm, first, second):
    s1 = min(r, MAX_ROW_STRIDE)
    s2 = r // s1
    if s2 == 1:
        for c in range(r):
            first[pl.ds(c, tm // r, stride=r), :] = piece(c)
        return
    for c in range(r):
        c1, c2 = c % s1, c // s1
        second[pl.ds(c1 * (tm // s1) + c2, tm // r, stride=s2), :] = piece(c)
    for c1 in range(s1):
        first[pl.ds(c1, tm // s1, stride=s1), :] = second[c1 * (tm // s1):(c1 + 1) * (tm // s1), :]


def _out1_body(*refs, dils, tm):
    n = len(dils)
    h_ref, o_refs, l_refs = refs[0], refs[1:1 + n], refs[1 + n:1 + 2 * n]
    w_ref, g_ref, o_ref = refs[1 + 2 * n:4 + 2 * n]
    stage = list(refs[4 + 2 * n:])
    hw = DIL_HEADS * HEAD_DIM
    outs, lses = [], []
    for p, r in enumerate(dils):
        if r == 1:
            outs.append([o_refs[p][:, h * HEAD_DIM:(h + 1) * HEAD_DIM].astype(F32) for h in range(DIL_HEADS)])
            lses.append(l_refs[p][...])
            continue
        bufs = []
        for j in range(DIL_HEADS + 1):
            first, second = stage.pop(0), stage.pop(0)
            if j < DIL_HEADS:
                piece = lambda c, j=j: o_refs[p][:, c * hw + j * HEAD_DIM:c * hw + (j + 1) * HEAD_DIM].astype(F32)
            else:
                piece = lambda c: l_refs[p][:, c * LANES:(c + 1) * LANES]
            _to_token_order(piece, r, tm, first, second)
            bufs.append(first)
        outs.append([bufs[h][...] for h in range(DIL_HEADS)])
        lses.append(bufs[DIL_HEADS][...])
    l0, l1, l2 = lses
    mx = jnp.maximum(jnp.maximum(l0, l1), l2)
    e0, e1, e2 = jnp.exp(l0 - mx), jnp.exp(l1 - mx), jnp.exp(l2 - mx)
    tot = e0 + e1 + e2
    a0, a1, a2 = e0 / tot, e1 / tot, e2 / tot
    parts = []
    for h in range(DIL_HEADS):
        parts.append(a0[:, h:h + 1] * outs[0][h] + a1[:, h:h + 1] * outs[1][h] + a2[:, h:h + 1] * outs[2][h])
    o = jnp.concatenate(parts, axis=1).astype(BF16)
    _finish(h_ref, jnp.dot(o, w_ref[...], preferred_element_type=F32), g_ref, o_ref)


def _row_spec(tm, w, col=0):
    return pl.BlockSpec((tm, w), lambda i: (i, col))


def _full_spec(shape):
    return pl.BlockSpec(shape, lambda i: (0,) * len(shape), pipeline_mode=pl.Buffered(1))


def _out0(h, o_c, o_s, o_w, z0, o_b, w, g, tm=512):
    t, d = h.shape
    tm = min(tm, t)
    ow = NSA_HEADS * HEAD_DIM
    w_half = lambda k: pl.BlockSpec((ow, d), lambda i: (k, 0), pipeline_mode=pl.Buffered(1))
    return pl.pallas_call(
        _out0_body, grid=(t // tm,),
        in_specs=[_row_spec(tm, d), _row_spec(tm, ow), _row_spec(tm, ow), _row_spec(tm, ow),
                  _row_spec(tm, LANES, Z0_GATE // LANES), _row_spec(tm, ow),
                  w_half(0), w_half(1), _full_spec((1, d))],
        out_specs=_row_spec(tm, d), out_shape=jax.ShapeDtypeStruct((t, d), F32),
        compiler_params=_cparams(("parallel",)))(h, o_c, o_s, o_w, z0, o_b, w, w, g.reshape(1, d))


def _out1(h, os_, lses, dils, w, g, tm=512):
    t, d = h.shape
    tm = min(tm, t)
    ow = DIL_HEADS * HEAD_DIM
    assert all(tm % (BF16_ROWS * r) == 0 for r in dils)
    n_stage = 2 * (DIL_HEADS + 1) * sum(1 for r in dils if r > 1)
    return pl.pallas_call(
        functools.partial(_out1_body, dils=tuple(dils), tm=tm), grid=(t // tm,),
        in_specs=[_row_spec(tm, d)] + [_row_spec(tm // r, r * ow) for r in dils]
                 + [_row_spec(tm // r, r * LANES) for r in dils] + [_full_spec(w.shape), _full_spec((1, d))],
        out_specs=_row_spec(tm, d), out_shape=jax.ShapeDtypeStruct((t, d), F32),
        scratch_shapes=[pltpu.VMEM((tm, LANES), F32)] * n_stage,
        compiler_params=_cparams(("parallel",)))(h, *os_, *lses, w, g.reshape(1, d))


def _mlp_body(h_ref, g1_ref, w1_ref, w2_ref, g2_ref, o_ref, xn_ref, acc_ref):
    f = pl.program_id(1)
    last = pl.num_programs(1) - 1

    def partial_sum(xn):
        a = jnp.maximum(jnp.dot(xn, w1_ref[...], preferred_element_type=F32), 0.0)
        return jnp.dot((a * a).astype(BF16), w2_ref[...], preferred_element_type=F32)

    @pl.when(f == 0)
    def _():
        x = h_ref[...]
        y = x * lax.rsqrt(jnp.mean(x * x, axis=-1, keepdims=True) + EPS)
        xn = (y * g1_ref[...]).astype(BF16)
        xn_ref[...] = xn
        acc_ref[...] = partial_sum(xn)

    @pl.when((f > 0) & (f < last))
    def _():
        acc_ref[...] += partial_sum(xn_ref[...])

    @pl.when(f == last)
    def _():
        _finish(h_ref, acc_ref[...] + partial_sum(xn_ref[...]), g2_ref, o_ref)


def _mlp(h, g1, w1, w2, g2, tm=512, tf=1024):
    t, d = h.shape
    ff = w1.shape[1]
    tm = min(tm, t)
    return pl.pallas_call(
        _mlp_body, grid=(t // tm, ff // tf),
        in_specs=[pl.BlockSpec((tm, d), lambda i, f: (i, 0)),
                  pl.BlockSpec((1, d), lambda i, f: (0, 0)),
                  pl.BlockSpec((d, tf), lambda i, f: (0, f)),
                  pl.BlockSpec((tf, d), lambda i, f: (f, 0)),
                  pl.BlockSpec((1, d), lambda i, f: (0, 0))],
        out_specs=pl.BlockSpec((tm, d), lambda i, f: (i, 0)),
        out_shape=jax.ShapeDtypeStruct((t, d), F32),
        scratch_shapes=[pltpu.VMEM((tm, d), BF16), pltpu.VMEM((tm, d), F32)],
        compiler_params=_cparams(("parallel", "arbitrary")))(h, g1.reshape(1, d), w1, w2, g2.reshape(1, d))


def _rope64_tile(w):
    z = jnp.zeros((w.shape[0], 32), w.dtype)
    return jnp.concatenate([w[:, :32], z, w[:, 32:], z], axis=1)


def _layer0_w_in(w_in):
    d = w_in.shape[0]
    o1 = NSA_HEADS * HEAD_DIM
    o2 = o1 + 3 * 2 * NSA_KV_GROUPS * HEAD_DIM
    o3 = o2 + 3 * NSA_HEADS
    o4 = o3 + MLA_Q_RANK
    o5 = o4 + MLA_KV_RANK
    kv = w_in[:, o1:o2].reshape(d, 3, 2, NSA_KV_GROUPS * HEAD_DIM)
    kv = kv.transpose(0, 2, 1, 3).reshape(d, o2 - o1)
    gate = jnp.pad(w_in[:, o2:o3], ((0, 0), (0, LANES - (o3 - o2))))
    w = jnp.concatenate([w_in[:, :o1], kv, w_in[:, o3:o4], w_in[:, o4:o5], gate, _rope64_tile(w_in[:, o5:])], 1)
    assert w.shape[1] == Z0_COLS
    return w.astype(BF16)


def _mla_w_uq(w_uq):
    d = w_uq.shape[0]
    w = w_uq.reshape(d, MLA_HEADS, MLA_NOPE + MLA_ROPE)
    tiles = [jnp.concatenate([w[:, h, :MLA_NOPE], _rope64_tile(w[:, h, MLA_NOPE:])], 1) for h in range(MLA_HEADS)]
    return jnp.concatenate(tiles, axis=1).astype(BF16)


def _mla_w_ukv(w_ukv):
    d = w_ukv.shape[0]
    w = w_ukv.reshape(d, MLA_HEADS, MLA_NOPE + MLA_V)
    return jnp.concatenate([w[:, :, :MLA_NOPE].reshape(d, -1), w[:, :, MLA_NOPE:].reshape(d, -1)], 1).astype(BF16)


def _layer1_w_in(w_in):
    return w_in.astype(BF16)


def _overlap_matrix(nc, n_cmp, n_slc, nsp):
    ratio = NSA_SEL_LEN // NSA_CMP_STRIDE
    m = np.zeros((nc, nsp), np.float32)
    for off in range(1 - NSA_CMP_LEN // NSA_CMP_STRIDE, ratio):
        n = np.arange(n_slc) * ratio + off
        ok = (n >= 0) & (n < n_cmp)
        m[n[ok], np.arange(n_slc)[ok]] = 1.0
    return jnp.asarray(m, BF16)


def kernel(x, l0_norm_mix_pre, l0_w_in, l0_cmp_pe_k, l0_cmp_w1_k, l0_cmp_w2_k, l0_cmp_pe_v, l0_cmp_w1_v, l0_cmp_w2_v, l0_mla_q_norm, l0_mla_w_uq, l0_mla_kv_norm, l0_mla_w_ukv, l0_w_out, l0_norm_mix_post, l0_norm_ffn_pre, l0_w_ff1, l0_w_ff2, l0_norm_ffn_post, l1_norm_mix_pre, l1_w_in, l1_w_out, l1_norm_mix_post, l1_norm_ffn_pre, l1_w_ff1, l1_w_ff2, l1_norm_ffn_post):
    b, seq, d = x.shape
    t = b * seq
    assert seq % NSA_WINDOW == 0 and seq % (DIL_PATTERNS[-1][1] * DIL_SPAN) == 0
    rot = _rope_tables(seq)
    h = x.reshape(t, d)

    tabs0 = ([TAB_ROPE128_Q] * NSA_HEADS + [TAB_ROPE128] * (3 * NSA_KV_GROUPS)
             + [TAB_IDENT] * ((Z0_KROPE - Z0_VCMP) // LANES) + [TAB_ROPE64])
    z0 = _proj_res(h, 0, d, l0_norm_mix_pre, _layer0_w_in(l0_w_in), seq, tabs0, rot)

    nc = seq // NSA_CMP_STRIDE
    n_cmp = (seq - NSA_CMP_LEN) // NSA_CMP_STRIDE + 1
    n_slc = seq // NSA_SEL_LEN
    nsp = -(-n_slc // LANES) * LANES
    gd = NSA_KV_GROUPS * HEAD_DIM

    def chunks(col):
        c = z0[:, col:col + gd].reshape(b, nc, NSA_CMP_STRIDE, NSA_KV_GROUPS, HEAD_DIM)
        return c.transpose(0, 3, 1, 2, 4).reshape(b, NSA_KV_GROUPS, nc, NSA_CMP_STRIDE * HEAD_DIM)

    pe = jnp.stack([l0_cmp_pe_k.reshape(1, -1), l0_cmp_pe_v.reshape(1, -1)])
    kv_cmp = _compress(jnp.stack([chunks(Z0_KCMP), chunks(Z0_VCMP)]),
                       jnp.stack([l0_cmp_w1_k, l0_cmp_w1_v]).astype(BF16),
                       jnp.stack([l0_cmp_w2_k, l0_cmp_w2_v]).astype(BF16),
                       jnp.broadcast_to(pe, (2, SUBLANES, pe.shape[-1])).astype(F32))
    o_c, bias = _nsa_cmp(z0, kv_cmp, _overlap_matrix(nc, n_cmp, n_slc, nsp), b, seq)
    o_w = _nsa_win(z0, b, seq)
    blk_lane = (jnp.arange(seq, dtype=jnp.int32) // NSA_SEL_LEN) % LANES
    onehot = (blk_lane[:, None] == jnp.arange(LANES, dtype=jnp.int32)[None, :]).astype(BF16)
    o_s = _nsa_sel(z0, bias, onehot, b, seq)

    qcat = _proj_res(z0, Z0_CQ // MLA_Q_RANK, MLA_Q_RANK, l0_mla_q_norm, _mla_w_uq(l0_mla_w_uq), seq,
                     [TAB_IDENT_Q, TAB_ROPE64_Q] * MLA_HEADS, rot)
    kv = _proj_res(z0, Z0_CKV // MLA_KV_RANK, MLA_KV_RANK, l0_mla_kv_norm, _mla_w_ukv(l0_mla_w_ukv), seq,
                   [TAB_IDENT] * (2 * MLA_HEADS), rot)
    o_b = _mla(qcat, kv, z0, b, seq)

    h = _out0(h, o_c, o_s, o_w, z0, o_b, l0_w_out.astype(BF16), l0_norm_mix_post)
    h = _mlp(h, l0_norm_ffn_pre, l0_w_ff1.astype(BF16), l0_w_ff2.astype(BF16), l0_norm_ffn_post)

    hw = DIL_HEADS * HEAD_DIM
    tabs_g = ([TAB_ROPE128_Q] * DIL_HEADS + [TAB_ROPE128] * DIL_HEADS
              + [TAB_IDENT] * DIL_HEADS)
    w1_in = _layer1_w_in(l1_w_in)
    outs = []
    for p, (_, r) in enumerate(DIL_PATTERNS):
        z1 = _proj_res(h, 0, d, l1_norm_mix_pre, w1_in, seq, tabs_g, rot, dil=r, w_col=p)
        outs.append(_dilated(z1.reshape(b, seq // r, 3 * r * hw), r, b, seq))
    h = _out1(h, [o for o, _ in outs], [l for _, l in outs], [r for _, r in DIL_PATTERNS],
              l1_w_out.astype(BF16), l1_norm_mix_post)
    h = _mlp(h, l1_norm_ffn_pre, l1_w_ff1.astype(BF16), l1_w_ff2.astype(BF16), l1_norm_ffn_post)
    return h.reshape(b, seq, d)
```

```python
import functools

import numpy as np
import jax
import jax.numpy as jnp
from jax import lax
from jax.experimental import pallas as pl
from jax.experimental.pallas import tpu as pltpu

F32 = jnp.float32
BF16 = jnp.bfloat16

HEAD_DIM = 128
LANES = 128
SUBLANES = 8
BF16_ROWS = 16
ROPE_THETA = 10000.0
EPS = 1e-6
NEG_INF = -1e30
POS_INF = 1e30
PICKED = -3e38

NSA_HEADS = 8
NSA_KV_GROUPS = 2
NSA_GROUP_HEADS = NSA_HEADS // NSA_KV_GROUPS
NSA_CMP_LEN = 32
NSA_CMP_STRIDE = 16
NSA_SEL_LEN = 64
NSA_SEL_TOPK = 16
NSA_WINDOW = 512

MLA_HEADS = 8
MLA_Q_RANK = 512
MLA_KV_RANK = 256
MLA_NOPE = 128
MLA_ROPE = 64
MLA_V = 128

DIL_PATTERNS = ((128, 1), (512, 4), (2048, 16))
DIL_HEADS = 8
DIL_SPAN = 128

VMEM_LIMIT = 56 * 1024 * 1024
MAX_ROW_STRIDE = 4

Z0_Q = 0
Z0_KCMP = 1024
Z0_KSLC = 1280
Z0_KWIN = 1536
Z0_VCMP = 1792
Z0_VSLC = 2048
Z0_VWIN = 2304
Z0_CQ = 2560
Z0_CKV = 3072
Z0_GATE = 3328
Z0_KROPE = 3456
Z0_COLS = 3584

TAB_ROPE128, TAB_IDENT, TAB_ROPE64, TAB_ROPE128_Q, TAB_ROPE64_Q, TAB_IDENT_Q = 0, 1, 2, 3, 4, 5
TABLE_KINDS = (TAB_ROPE128, TAB_ROPE64, TAB_ROPE128_Q, TAB_ROPE64_Q)

LOG2E = 1.4426950408889634
LN2 = 0.6931471805599453
QSCALE_128 = HEAD_DIM ** -0.5 * LOG2E
QSCALE_MLA = (MLA_NOPE + MLA_ROPE) ** -0.5 * LOG2E


def _cparams(sem):
    return pltpu.CompilerParams(dimension_semantics=sem, vmem_limit_bytes=VMEM_LIMIT)


def _rope_tables(seq):
    def cs(dim):
        inv = 1.0 / (ROPE_THETA ** (jnp.arange(0, dim, 2, dtype=F32) / dim))
        ang = jnp.arange(seq, dtype=F32)[:, None] * inv[None, :]
        return jnp.cos(ang), jnp.sin(ang)

    c128, s128 = cs(HEAD_DIM)
    c64, s64 = cs(MLA_ROPE)
    one = jnp.ones((seq, 32), F32)
    zero = jnp.zeros((seq, 32), F32)
    r128 = jnp.concatenate([c128, c128, -s128, s128], axis=1)
    r64 = jnp.concatenate([c64, one, c64, one, -s64, zero, s64, zero], axis=1)
    tables = {TAB_ROPE128: r128, TAB_ROPE64: r64, TAB_ROPE128_Q: r128 * QSCALE_128,
              TAB_ROPE64_Q: r64 * QSCALE_MLA}
    return jnp.stack([tables[k] for k in TABLE_KINDS])


def _residue_rows(part, r, tm, first, second):
    s1 = min(r, MAX_ROW_STRIDE)
    s2 = r // s1
    first[...] = part
    if s2 == 1:
        return [first[pl.ds(c, tm // r, stride=r), :] for c in range(r)]
    for c1 in range(s1):
        second[c1 * (tm // s1):(c1 + 1) * (tm // s1), :] = first[pl.ds(c1, tm // s1, stride=s1), :]
    return [second[pl.ds((c % s1) * (tm // s1) + c // s1, tm // r, stride=s2), :] for c in range(r)]


def _proj_res_body(*refs, kinds, tn, scale_q, dil, chunk_tiles):
    x_ref, g_ref, w_ref = refs[:3]
    used = sorted({k for k in kinds if k not in (TAB_IDENT, TAB_IDENT_Q)})
    t_refs = dict(zip(used, refs[3:3 + len(used)]))
    n_out = 2 if chunk_tiles else 1
    o_ref = refs[3 + len(used)]
    c_ref = refs[4 + len(used)] if chunk_tiles else None
    xn_ref = refs[3 + len(used) + n_out]
    stage_refs = refs[4 + len(used) + n_out:]
    chunk_stage = stage_refs[len(stage_refs) - 2 * len(chunk_tiles):]
    tm = x_ref.shape[0]
    x = x_ref[...].astype(F32)
    y = x * lax.rsqrt(jnp.mean(x * x, axis=-1, keepdims=True) + EPS)
    xn_ref[...] = (y * g_ref[...]).astype(BF16)
    nl = tn // LANES
    for j in range(w_ref.shape[1] // tn):
        acc = jnp.dot(xn_ref[...], w_ref[:, j * tn:(j + 1) * tn], preferred_element_type=F32)
        parts = []
        for k in range(nl):
            kind = kinds[j * nl + k]
            a = acc[:, k * LANES:(k + 1) * LANES]
            if kind == TAB_IDENT:
                parts.append(a)
            elif kind == TAB_IDENT_Q:
                parts.append(a * scale_q)
            else:
                tab = t_refs[kind]
                parts.append(a * tab[:, :LANES] + pltpu.roll(a, LANES // 2, 1) * tab[:, LANES:])
        for k in range(nl):
            g = j * nl + k
            if g in chunk_tiles:
                e = chunk_tiles.index(g)
                rows = _residue_rows(parts[k], NSA_CMP_STRIDE, tm, chunk_stage[2 * e], chunk_stage[2 * e + 1])
                for c, piece in enumerate(rows):
                    c_ref[e, :, c * LANES:(c + 1) * LANES] = piece.astype(c_ref.dtype)
        if dil == 1:
            o_ref[:, j * tn:(j + 1) * tn] = jnp.concatenate(parts, axis=1).astype(o_ref.dtype)
            continue
        hw = DIL_HEADS * HEAD_DIM
        for k in range(nl):
            g = j * nl + k
            for c, piece in enumerate(_residue_rows(parts[k], dil, tm, stage_refs[2 * k], stage_refs[2 * k + 1])):
                col = ((g // DIL_HEADS) * dil + c) * hw + (g % DIL_HEADS) * LANES
                o_ref[:, col:col + LANES] = piece.astype(o_ref.dtype)


def _proj_res(x, x_col, d, g, w, seq, kinds, rot, tm=512, tn=512, dil=1, w_col=0, chunk_tiles=()):
    t = x.shape[0]
    n = len(kinds) * LANES
    assert w.shape[1] % n == 0
    tm = min(tm, seq)
    assert t % tm == 0 and seq % tm == 0 and n % tn == 0 and len(kinds) == n // LANES and tm % (BF16_ROWS * dil) == 0
    spb = seq // tm
    used = sorted({k for k in kinds if k not in (TAB_IDENT, TAB_IDENT_Q)})
    tab_specs = [pl.BlockSpec((None, tm, 2 * LANES), lambda i, row=TABLE_KINDS.index(kind): (row, i % spb, 0))
                 for kind in used]
    scratch = [pltpu.VMEM((tm, d), BF16)]
    if dil > 1:
        scratch += [pltpu.VMEM((tm, LANES), F32)] * (2 * (tn // LANES))
    scratch += [pltpu.VMEM((tm, LANES), F32)] * (2 * len(chunk_tiles))
    out_specs = [pl.BlockSpec((tm // dil, dil * n), lambda i: (i, 0))]
    out_shape = [jax.ShapeDtypeStruct((t // dil, dil * n), BF16)]
    if chunk_tiles:
        cw = NSA_CMP_STRIDE * LANES
        out_specs.append(pl.BlockSpec((len(chunk_tiles), tm // NSA_CMP_STRIDE, cw), lambda i: (0, i, 0)))
        out_shape.append(jax.ShapeDtypeStruct((len(chunk_tiles), t // NSA_CMP_STRIDE, cw), BF16))
    res = pl.pallas_call(
        functools.partial(_proj_res_body, kinds=tuple(kinds), tn=tn, scale_q=QSCALE_MLA, dil=dil,
                          chunk_tiles=tuple(chunk_tiles)),
        grid=(t // tm,),
        in_specs=[pl.BlockSpec((tm, d), lambda i: (i, x_col)),
                  pl.BlockSpec((1, d), lambda i: (0, 0)),
                  pl.BlockSpec((d, n), lambda i: (0, w_col), pipeline_mode=pl.Buffered(1))] + tab_specs,
        out_specs=out_specs, scratch_shapes=scratch, out_shape=out_shape,
        compiler_params=_cparams(("parallel",)))(x, g.reshape(1, d).astype(F32), w, *([rot] * len(used)))
    return res if chunk_tiles else res[0]


def _compress_body(c_ref, w1_ref, w2_ref, pe_ref, o_ref, *, nc):
    half = NSA_CMP_STRIDE * HEAD_DIM
    c = c_ref[...]
    a = jnp.dot(c, w1_ref[:half, :], preferred_element_type=F32)
    b = jnp.dot(c, w1_ref[half:, :], preferred_element_type=F32)
    pe = pe_ref[...]
    pe_hi = pe.astype(BF16)
    pe_lo = (pe - pe_hi.astype(F32)).astype(BF16)
    pe_term = (jnp.dot(pe_hi, w1_ref[...], preferred_element_type=F32)
               + jnp.dot(pe_lo, w1_ref[...], preferred_element_type=F32))
    hid = a + pltpu.roll(b, nc - 1, 0) + pe_term[0:1, :]
    act = jax.nn.gelu(hid)
    o_ref[...] = jnp.dot(act.astype(BF16), w2_ref[...], preferred_element_type=F32).astype(o_ref.dtype)


def _compress(chunks, w1, w2, pe):
    _, g, b, nc, cw = chunks.shape
    return pl.pallas_call(
        functools.partial(_compress_body, nc=nc), grid=(2, b, g),
        in_specs=[pl.BlockSpec((None, None, None, nc, cw), lambda s, i, j: (s, j, i, 0, 0)),
                  pl.BlockSpec((None, 2 * cw, HEAD_DIM), lambda s, i, j: (s, 0, 0)),
                  pl.BlockSpec((None, HEAD_DIM, HEAD_DIM), lambda s, i, j: (s, 0, 0)),
                  pl.BlockSpec((None, SUBLANES, 2 * cw), lambda s, i, j: (s, 0, 0))],
        out_specs=pl.BlockSpec((None, None, None, nc, HEAD_DIM), lambda s, i, j: (s, i, j, 0, 0)),
        out_shape=jax.ShapeDtypeStruct((2, b, g, nc, HEAD_DIM), BF16),
        compiler_params=_cparams(("parallel", "parallel", "parallel")))(chunks, w1, w2, pe)


def _nsa_cmp_body(q_ref, k_ref, v_ref, m_ref, o_ref, bias_ref, imp_scr, *, tq, nc, nsp, topk, nvar,
                  tiles_per_var):
    i = pl.program_id(2)
    t = i * tq + lax.broadcasted_iota(jnp.int32, (tq, 1), 0)
    any_vis = (t >= NSA_CMP_LEN - 1).astype(F32)

    def attend(ncols):
        n = lax.broadcasted_iota(jnp.int32, (1, ncols), 1)
        vis = (n * NSA_CMP_STRIDE + (NSA_CMP_LEN - 1)) <= t
        k = k_ref[:ncols, :]
        v = v_ref[:ncols, :]
        psum = jnp.zeros((tq, ncols), F32)
        for h in range(NSA_GROUP_HEADS):
            sl = slice(h * HEAD_DIM, (h + 1) * HEAD_DIM)
            s = lax.dot_general(q_ref[:, sl], k, (((1,), (1,)), ((), ())), preferred_element_type=F32)
            s = jnp.where(vis, s, NEG_INF)
            e = jnp.exp2(s - jnp.max(s, axis=-1, keepdims=True))
            p = e * (any_vis / jnp.sum(e, axis=-1, keepdims=True))
            o_ref[:, sl] = jnp.dot(p.astype(BF16), v, preferred_element_type=F32).astype(o_ref.dtype)
            psum = psum + p
        hi = psum.astype(BF16)
        lo = (psum - hi.astype(F32)).astype(BF16)
        imp_scr[...] = (jnp.dot(hi, m_ref[:ncols, :], preferred_element_type=F32)
                        + jnp.dot(lo, m_ref[:ncols, :], preferred_element_type=F32))

    def select(nblk):
        imp = imp_scr[...].T[:nblk]
        tl = i * tq + lax.broadcasted_iota(jnp.int32, (1, tq), 1)
        blk = lax.broadcasted_iota(jnp.int32, (nblk, 1), 0)
        blkf = blk.astype(F32)
        cur = lax.shift_right_logical(tl, NSA_SEL_LEN.bit_length() - 1)
        forced = (blk == 0) | (blk == cur) | (blk == cur - 1)
        causal = blk * NSA_SEL_LEN <= tl
        score = jnp.where(forced, PICKED, jnp.where(causal, imp, NEG_INF))

        def take_one(_, score):
            mx = jnp.max(score, axis=0, keepdims=True)
            first = jnp.min(jnp.where(score == mx, blkf, float(nsp)), axis=0, keepdims=True)
            return jnp.where(blkf == first, PICKED, score)

        score = lax.fori_loop(0, topk - 3, take_one, score)
        sel = (score == PICKED) & causal
        bias = jnp.where(sel, 0.0, NEG_INF)
        if nblk < nsp:
            bias = jnp.concatenate([bias, jnp.full((nsp - nblk, tq), NEG_INF, F32)], axis=0)
        bias_ref[...] = bias.T.astype(bias_ref.dtype)

    for var in range(nvar):
        @pl.when((i >= var * tiles_per_var) & (i < (var + 1) * tiles_per_var))
        def _(var=var):
            attend((var + 1) * nc // nvar)
            select((var + 1) * nsp // nvar)


def _nsa_cmp(z0, kv_cmp, ovl, b, seq, tq=512):
    nc = kv_cmp.shape[3]
    nsp = ovl.shape[1]
    tq = min(tq, seq)
    nq = seq // tq
    gw = NSA_GROUP_HEADS * HEAD_DIM
    topk = min(NSA_SEL_TOPK, seq // NSA_SEL_LEN)
    assert topk >= 3
    nvar = 4 if (nq % 4 == 0 and nc % (4 * 2 * LANES) == 0) else 1
    body = functools.partial(_nsa_cmp_body, tq=tq, nc=nc, nsp=nsp, topk=topk, nvar=nvar, tiles_per_var=nq // nvar)
    return pl.pallas_call(
        body, grid=(b, NSA_KV_GROUPS, nq),
        scratch_shapes=[pltpu.VMEM((tq, nsp), F32)],
        in_specs=[pl.BlockSpec((tq, gw), lambda bi, g, i: (bi * nq + i, g)),
                  pl.BlockSpec((None, None, None, nc, HEAD_DIM), lambda bi, g, i: (0, bi, g, 0, 0)),
                  pl.BlockSpec((None, None, None, nc, HEAD_DIM), lambda bi, g, i: (1, bi, g, 0, 0)),
                  pl.BlockSpec((nc, nsp), lambda bi, g, i: (0, 0))],
        out_specs=[pl.BlockSpec((tq, gw), lambda bi, g, i: (bi * nq + i, g)),
                   pl.BlockSpec((None, None, tq, nsp), lambda bi, g, i: (bi, g, i, 0))],
        out_shape=[jax.ShapeDtypeStruct((b * seq, NSA_HEADS * HEAD_DIM), BF16),
                   jax.ShapeDtypeStruct((b, NSA_KV_GROUPS, seq, nsp), BF16)],
        compiler_params=_cparams(("parallel", "parallel", "parallel")))(z0, kv_cmp, kv_cmp, ovl)


def _lane_tile(x, n):
    return jnp.concatenate([x] * n, axis=1) if n > 1 else x


def _softmax_pv(s, v):
    tq, tk = s.shape
    d = v.shape[1]
    m = jnp.broadcast_to(jnp.max(s, axis=-1, keepdims=True), (tq, LANES))
    p = jnp.exp2((s - _lane_tile(m, tk // LANES)).astype(BF16))
    acc = jnp.dot(p, jnp.concatenate([v, jnp.ones((tk, LANES), BF16)], axis=1), preferred_element_type=F32)
    return acc[:, :d], acc[:, d:], m


WIN_SUB = 256
DIL_SUB = 128


def _band_subtiles(i, tq, w, lo, hi, sub):
    ts = min(sub, tq)
    qq = lax.broadcasted_iota(jnp.int32, (ts, 1), 0)
    kk = lax.broadcasted_iota(jnp.int32, (1, ts + w), 1)
    rel = qq + w - kk
    band = (rel >= lo) & (rel <= hi)
    out = []
    for a in range(tq // ts):
        r0 = a * ts
        if r0 < w:
            mask = band & ((kk >= w - r0) | (i > 0))
            take = lambda p, c, lanes, r0=r0: jnp.concatenate([p[r0:, lanes], c[:r0 + ts, lanes]], axis=0)
        else:
            mask = band
            take = lambda p, c, lanes, r0=r0: c[r0 - w:r0 + ts, lanes]
        out.append((r0, ts, mask, take))
    return out


def _nsa_win_body(q_ref, kp_ref, kc_ref, vp_ref, vc_ref, o_ref, *, tq):
    i = pl.program_id(2)
    tiles = _band_subtiles(i, tq, NSA_WINDOW, 0, NSA_WINDOW - 1, WIN_SUB)
    for r0, ts, mask, take in tiles:
        k = take(kp_ref, kc_ref, slice(None))
        v = take(vp_ref, vc_ref, slice(None))
        scores = [jnp.where(mask, lax.dot_general(q_ref[r0:r0 + ts, h * HEAD_DIM:(h + 1) * HEAD_DIM], k,
                                                  (((1,), (1,)), ((), ())), preferred_element_type=F32),
                            NEG_INF) for h in range(NSA_GROUP_HEADS)]
        for h in range(NSA_GROUP_HEADS):
            num, den, _ = _softmax_pv(scores[h], v)
            o_ref[r0:r0 + ts, h * HEAD_DIM:(h + 1) * HEAD_DIM] = (num / den).astype(o_ref.dtype)


def _nsa_win(z0, b, seq):
    tq = NSA_WINDOW
    assert seq % tq == 0
    nq = seq // tq
    gw = NSA_GROUP_HEADS * HEAD_DIM
    kcol, vcol = Z0_KWIN // HEAD_DIM, Z0_VWIN // HEAD_DIM
    prev = lambda bi, i: bi * nq + jnp.maximum(i - 1, 0)
    return pl.pallas_call(
        functools.partial(_nsa_win_body, tq=tq), grid=(b, NSA_KV_GROUPS, nq),
        in_specs=[pl.BlockSpec((tq, gw), lambda bi, g, i: (bi * nq + i, g)),
                  pl.BlockSpec((tq, HEAD_DIM), lambda bi, g, i: (prev(bi, i), kcol + g)),
                  pl.BlockSpec((tq, HEAD_DIM), lambda bi, g, i: (bi * nq + i, kcol + g)),
                  pl.BlockSpec((tq, HEAD_DIM), lambda bi, g, i: (prev(bi, i), vcol + g)),
                  pl.BlockSpec((tq, HEAD_DIM), lambda bi, g, i: (bi * nq + i, vcol + g))],
        out_specs=pl.BlockSpec((tq, gw), lambda bi, g, i: (bi * nq + i, g)),
        out_shape=jax.ShapeDtypeStruct((b * seq, NSA_HEADS * HEAD_DIM), BF16),
        compiler_params=_cparams(("parallel", "parallel", "parallel")))(z0, z0, z0, z0, z0)


def _tri_schedule(nq):
    qi = np.concatenate([np.full(i + 1, i) for i in range(nq)]).astype(np.int32)
    kj = np.concatenate([np.arange(i + 1) for i in range(nq)]).astype(np.int32)
    return jnp.asarray(qi), jnp.asarray(kj)


def _flash_init(m_scr, acc_scr):
    m_scr[...] = jnp.full(m_scr.shape, NEG_INF, F32)
    acc_scr[...] = jnp.zeros(acc_scr.shape, F32)


def _flash_update(scores, vs, m_scr, acc_scr, rows=slice(None)):
    nh = len(scores)
    reps = scores[0].shape[1] // LANES
    ones = jnp.ones((vs[0].shape[0], LANES), BF16)
    m_prev = [m_scr[h, rows] for h in range(nh)]
    m_new = [jnp.maximum(m_prev[h], jnp.max(scores[h], axis=-1, keepdims=True)) for h in range(nh)]
    ps = [jnp.exp2((scores[h] - _lane_tile(m_new[h], reps)).astype(BF16)) for h in range(nh)]
    alphas = [jnp.exp2(m_prev[h] - m_new[h]) for h in range(nh)]
    for h in range(nh):
        v_aug = jnp.concatenate([vs[h], ones], axis=1)
        acc_scr[h, rows] = (_lane_tile(alphas[h], acc_scr.shape[2] // LANES) * acc_scr[h, rows]
                            + jnp.dot(ps[h], v_aug, preferred_element_type=F32))
        m_scr[h, rows] = m_new[h]


def _flash_out(acc_scr, h, d):
    acc = acc_scr[h]
    return acc[:, :d] / acc[:, d:]


DIAG_SUB = 256


def _flash_tile(qi, kj, tq, scores_and_values, m_scr, acc_scr, write_out):
    @pl.when(kj < qi)
    def _():
        scores, vs = scores_and_values(slice(None), tq)
        _flash_update(scores, vs, m_scr, acc_scr)

    @pl.when(kj == qi)
    def _():
        ts = min(DIAG_SUB, tq)
        for r0 in range(0, tq, ts):
            scores, vs = scores_and_values(slice(r0, r0 + ts), r0 + ts)
            row = r0 + lax.broadcasted_iota(jnp.int32, (ts, 1), 0)
            col = lax.broadcasted_iota(jnp.int32, (1, r0 + ts), 1)
            _flash_update([jnp.where(col <= row, s, NEG_INF) for s in scores], vs, m_scr, acc_scr,
                          slice(r0, r0 + ts))
        write_out()


def _nsa_sel_body(qi_ref, kj_ref, q_ref, k_ref, v_ref, oh_ref, bias_ref, o_ref,
                  qa_scr, m_scr, acc_scr, *, tq, per_slab):
    step = pl.program_id(2)
    qi = qi_ref[step]
    kj = kj_ref[step]

    @pl.when(kj == 0)
    def _():
        _flash_init(m_scr, acc_scr)
        for h in range(NSA_GROUP_HEADS):
            qa_scr[h, :, :HEAD_DIM] = q_ref[:, h * HEAD_DIM:(h + 1) * HEAD_DIM]

    @pl.when(kj % per_slab == 0)
    def _():
        for h in range(NSA_GROUP_HEADS):
            qa_scr[h, :, HEAD_DIM:] = bias_ref[...]

    def scores_and_values(rows, nkeys):
        k = jnp.concatenate([k_ref[:nkeys, :], oh_ref[:nkeys, :]], axis=1)
        scores = [lax.dot_general(qa_scr[h, rows], k, (((1,), (1,)), ((), ())), preferred_element_type=F32)
                  for h in range(NSA_GROUP_HEADS)]
        return scores, [v_ref[:nkeys, :]] * NSA_GROUP_HEADS

    def write_out():
        for h in range(NSA_GROUP_HEADS):
            o_ref[:, h * HEAD_DIM:(h + 1) * HEAD_DIM] = _flash_out(acc_scr, h, HEAD_DIM).astype(o_ref.dtype)

    _flash_tile(qi, kj, tq, scores_and_values, m_scr, acc_scr, write_out)


def _nsa_sel(z0, bias, onehot, b, seq, tq=1024):
    tq = min(tq, seq)
    nq = seq // tq
    gw = NSA_GROUP_HEADS * HEAD_DIM
    kcol, vcol = Z0_KSLC // HEAD_DIM, Z0_VSLC // HEAD_DIM
    per_slab = max(LANES * NSA_SEL_LEN // tq, 1)
    qi, kj = _tri_schedule(nq)
    grid_spec = pltpu.PrefetchScalarGridSpec(
        num_scalar_prefetch=2, grid=(b, NSA_KV_GROUPS, int(qi.shape[0])),
        in_specs=[pl.BlockSpec((tq, gw), lambda bi, g, s, qi, kj: (bi * nq + qi[s], g)),
                  pl.BlockSpec((tq, HEAD_DIM), lambda bi, g, s, qi, kj: (bi * nq + kj[s], kcol + g)),
                  pl.BlockSpec((tq, HEAD_DIM), lambda bi, g, s, qi, kj: (bi * nq + kj[s], vcol + g)),
                  pl.BlockSpec((tq, LANES), lambda bi, g, s, qi, kj: (kj[s], 0)),
                  pl.BlockSpec((None, None, tq, LANES),
                               lambda bi, g, s, qi, kj: (bi, g, qi[s], kj[s] // per_slab))],
        out_specs=pl.BlockSpec((tq, gw), lambda bi, g, s, qi, kj: (bi * nq + qi[s], g)),
        scratch_shapes=[pltpu.VMEM((NSA_GROUP_HEADS, tq, 2 * HEAD_DIM), BF16),
                        pltpu.VMEM((NSA_GROUP_HEADS, tq, LANES), F32),
                        pltpu.VMEM((NSA_GROUP_HEADS, tq, HEAD_DIM + LANES), F32)])
    return pl.pallas_call(
        functools.partial(_nsa_sel_body, tq=tq, per_slab=per_slab), grid_spec=grid_spec,
        out_shape=jax.ShapeDtypeStruct((b * seq, NSA_HEADS * HEAD_DIM), BF16),
        compiler_params=_cparams(("parallel", "parallel", "arbitrary")))(qi, kj, z0, z0, z0, onehot, bias)


MLA_STEP_HEADS = 4


def _mla_body(qi_ref, kj_ref, q_ref, kn_ref, kp_ref, v_ref, o_ref, m_scr, acc_scr, *, tq):
    step = pl.program_id(2)
    qi = qi_ref[step]
    kj = kj_ref[step]

    @pl.when(kj == 0)
    def _():
        _flash_init(m_scr, acc_scr)

    def scores_and_values(rows, nkeys):
        kp = kp_ref[:nkeys, :]
        scores, vs = [], []
        for h in range(MLA_STEP_HEADS):
            k = jnp.concatenate([kn_ref[:nkeys, h * MLA_NOPE:(h + 1) * MLA_NOPE], kp], axis=1)
            scores.append(lax.dot_general(q_ref[rows, h * 2 * LANES:(h + 1) * 2 * LANES], k,
                                          (((1,), (1,)), ((), ())), preferred_element_type=F32))
            vs.append(v_ref[:nkeys, h * MLA_V:(h + 1) * MLA_V])
        return scores, vs

    def write_out():
        for h in range(MLA_STEP_HEADS):
            o_ref[:, h * MLA_V:(h + 1) * MLA_V] = _flash_out(acc_scr, h, MLA_V).astype(o_ref.dtype)

    _flash_tile(qi, kj, tq, scores_and_values, m_scr, acc_scr, write_out)


def _mla(qcat, kv, z0, b, seq, tq=1024):
    tq = min(tq, seq)
    nq = seq // tq
    nh = MLA_STEP_HEADS
    ngrp = MLA_HEADS // nh
    kpcol = Z0_KROPE // LANES
    qi, kj = _tri_schedule(nq)
    grid_spec = pltpu.PrefetchScalarGridSpec(
        num_scalar_prefetch=2, grid=(b, ngrp, int(qi.shape[0])),
        in_specs=[pl.BlockSpec((tq, nh * 2 * LANES), lambda bi, h, s, qi, kj: (bi * nq + qi[s], h)),
                  pl.BlockSpec((tq, nh * MLA_NOPE), lambda bi, h, s, qi, kj: (bi * nq + kj[s], h)),
                  pl.BlockSpec((tq, LANES), lambda bi, h, s, qi, kj: (bi * nq + kj[s], kpcol)),
                  pl.BlockSpec((tq, nh * MLA_V), lambda bi, h, s, qi, kj: (bi * nq + kj[s], ngrp + h))],
        out_specs=pl.BlockSpec((tq, nh * MLA_V), lambda bi, h, s, qi, kj: (bi * nq + qi[s], h)),
        scratch_shapes=[pltpu.VMEM((nh, tq, LANES), F32), pltpu.VMEM((nh, tq, MLA_V + LANES), F32)])
    return pl.pallas_call(
        functools.partial(_mla_body, tq=tq), grid_spec=grid_spec,
        out_shape=jax.ShapeDtypeStruct((b * seq, MLA_HEADS * MLA_V), BF16),
        compiler_params=_cparams(("parallel", "parallel", "arbitrary")))(qi, kj, qcat, kv, z0, kv)


def _dil_body(q_ref, kp_ref, kc_ref, vp_ref, vc_ref, o_ref, lse_ref, *, tq):
    i = pl.program_id(2)
    for r0, ts, mask, take in _band_subtiles(i, tq, DIL_SPAN, 0, DIL_SPAN, DIL_SUB):
        lane = lax.broadcasted_iota(jnp.int32, (ts, LANES), 1)
        lse_all = jnp.zeros((ts, LANES), F32)
        for h in range(DIL_HEADS):
            sl = slice(h * HEAD_DIM, (h + 1) * HEAD_DIM)
            s = lax.dot_general(q_ref[r0:r0 + ts, sl], take(kp_ref, kc_ref, sl), (((1,), (1,)), ((), ())),
                                preferred_element_type=F32)
            num, den, m = _softmax_pv(jnp.where(mask, s, NEG_INF), take(vp_ref, vc_ref, sl))
            o_ref[r0:r0 + ts, sl] = (num / den).astype(o_ref.dtype)
            lse_all = jnp.where(lane == h, m * LN2 + jnp.log(den), lse_all)
        lse_ref[r0:r0 + ts, :] = lse_all


def _dilated(zv, r, b, seq, tq=512):
    sub = seq // r
    tq = min(tq, sub)
    assert sub % tq == 0 and tq % DIL_SPAN == 0
    nq = sub // tq
    hw = DIL_HEADS * HEAD_DIM
    ratio = tq // DIL_SPAN
    prev = lambda i: jnp.maximum(i * ratio - 1, 0)
    o, lse = pl.pallas_call(
        functools.partial(_dil_body, tq=tq), grid=(b, r, nq),
        in_specs=[pl.BlockSpec((None, tq, hw), lambda bi, c, i: (bi, i, c)),
                  pl.BlockSpec((None, DIL_SPAN, hw), lambda bi, c, i: (bi, prev(i), r + c)),
                  pl.BlockSpec((None, tq, hw), lambda bi, c, i: (bi, i, r + c)),
                  pl.BlockSpec((None, DIL_SPAN, hw), lambda bi, c, i: (bi, prev(i), 2 * r + c)),
                  pl.BlockSpec((None, tq, hw), lambda bi, c, i: (bi, i, 2 * r + c))],
        out_specs=[pl.BlockSpec((None, tq, hw), lambda bi, c, i: (bi, i, c)),
                   pl.BlockSpec((None, tq, LANES), lambda bi, c, i: (bi, i, c))],
        out_shape=[jax.ShapeDtypeStruct((b, sub, r * hw), BF16),
                   jax.ShapeDtypeStruct((b, sub, r * LANES), F32)],
        compiler_params=_cparams(("parallel", "parallel", "parallel")))(zv, zv, zv, zv, zv)
    return o.reshape(b * sub, r * hw), lse.reshape(b * sub, r * LANES)


def _finish(h_ref, m, g_ref, o_ref):
    y = m * lax.rsqrt(jnp.mean(m * m, axis=-1, keepdims=True) + EPS)
    o_ref[...] = h_ref[...] + y * g_ref[...]


def _out0_body(h_ref, oc_ref, os_ref, ow_ref, gate_ref, ob_ref, wa_ref, wb_ref, g_ref, o_ref):
    gate = jax.nn.sigmoid(gate_ref[...].astype(F32))
    parts = []
    for h in range(NSA_HEADS):
        sl = slice(h * HEAD_DIM, (h + 1) * HEAD_DIM)
        parts.append(gate[:, 3 * h:3 * h + 1] * oc_ref[:, sl].astype(F32)
                     + gate[:, 3 * h + 1:3 * h + 2] * os_ref[:, sl].astype(F32)
                     + gate[:, 3 * h + 2:3 * h + 3] * ow_ref[:, sl].astype(F32))
    oa = jnp.concatenate(parts, axis=1).astype(BF16)
    m = (jnp.dot(oa, wa_ref[...], preferred_element_type=F32)
         + jnp.dot(ob_ref[...], wb_ref[...], preferred_element_type=F32))
    _finish(h_ref, m, g_ref, o_ref)


def _to_token_order(piece, r, tm, first, second):
    s1 = min(r, MAX_ROW_STRIDE)
    s2 = r // s1
    if s2 == 1:
        for c in range(r):
            first[pl.ds(c, tm // r, stride=r), :] = piece(c)
        return
    for c in range(r):
        c1, c2 = c % s1, c // s1
        second[pl.ds(c1 * (tm // s1) + c2, tm // r, stride=s2), :] = piece(c)
    for c1 in range(s1):
        first[pl.ds(c1, tm // s1, stride=s1), :] = second[c1 * (tm // s1):(c1 + 1) * (tm // s1), :]


def _out1_body(*refs, dils, tm):
    n = len(dils)
    h_ref, o_refs, l_refs = refs[0], refs[1:1 + n], refs[1 + n:1 + 2 * n]
    w_ref, g_ref, o_ref = refs[1 + 2 * n:4 + 2 * n]
    stage = list(refs[4 + 2 * n:])
    hw = DIL_HEADS * HEAD_DIM
    outs, lses = [], []
    for p, r in enumerate(dils):
        if r == 1:
            outs.append([o_refs[p][:, h * HEAD_DIM:(h + 1) * HEAD_DIM].astype(F32) for h in range(DIL_HEADS)])
            lses.append(l_refs[p][...])
            continue
        bufs = []
        for j in range(DIL_HEADS + 1):
            first, second = stage.pop(0), stage.pop(0)
            if j < DIL_HEADS:
                piece = lambda c, j=j: o_refs[p][:, c * hw + j * HEAD_DIM:c * hw + (j + 1) * HEAD_DIM].astype(F32)
            else:
                piece = lambda c: l_refs[p][:, c * LANES:(c + 1) * LANES]
            _to_token_order(piece, r, tm, first, second)
            bufs.append(first)
        outs.append([bufs[h][...] for h in range(DIL_HEADS)])
        lses.append(bufs[DIL_HEADS][...])
    l0, l1, l2 = lses
    mx = jnp.maximum(jnp.maximum(l0, l1), l2)
    e0, e1, e2 = jnp.exp(l0 - mx), jnp.exp(l1 - mx), jnp.exp(l2 - mx)
    tot = e0 + e1 + e2
    a0, a1, a2 = e0 / tot, e1 / tot, e2 / tot
    parts = []
    for h in range(DIL_HEADS):
        parts.append(a0[:, h:h + 1] * outs[0][h] + a1[:, h:h + 1] * outs[1][h] + a2[:, h:h + 1] * outs[2][h])
    o = jnp.concatenate(parts, axis=1).astype(BF16)
    _finish(h_ref, jnp.dot(o, w_ref[...], preferred_element_type=F32), g_ref, o_ref)


def _row_spec(tm, w, col=0):
    return pl.BlockSpec((tm, w), lambda i: (i, col))


def _full_spec(shape):
    return pl.BlockSpec(shape, lambda i: (0,) * len(shape), pipeline_mode=pl.Buffered(1))


def _out0(h, o_c, o_s, o_w, z0, o_b, w, g, tm=512):
    t, d = h.shape
    tm = min(tm, t)
    ow = NSA_HEADS * HEAD_DIM
    w_half = lambda k: pl.BlockSpec((ow, d), lambda i: (k, 0), pipeline_mode=pl.Buffered(1))
    return pl.pallas_call(
        _out0_body, grid=(t // tm,),
        in_specs=[_row_spec(tm, d), _row_spec(tm, ow), _row_spec(tm, ow), _row_spec(tm, ow),
                  _row_spec(tm, LANES, Z0_GATE // LANES), _row_spec(tm, ow),
                  w_half(0), w_half(1), _full_spec((1, d))],
        out_specs=_row_spec(tm, d), out_shape=jax.ShapeDtypeStruct((t, d), F32),
        compiler_params=_cparams(("parallel",)))(h, o_c, o_s, o_w, z0, o_b, w, w, g.reshape(1, d))


def _out1(h, os_, lses, dils, w, g, tm=512):
    t, d = h.shape
    tm = min(tm, t)
    ow = DIL_HEADS * HEAD_DIM
    assert all(tm % (BF16_ROWS * r) == 0 for r in dils)
    n_stage = 2 * (DIL_HEADS + 1) * sum(1 for r in dils if r > 1)
    return pl.pallas_call(
        functools.partial(_out1_body, dils=tuple(dils), tm=tm), grid=(t // tm,),
        in_specs=[_row_spec(tm, d)] + [_row_spec(tm // r, r * ow) for r in dils]
                 + [_row_spec(tm // r, r * LANES) for r in dils] + [_full_spec(w.shape), _full_spec((1, d))],
        out_specs=_row_spec(tm, d), out_shape=jax.ShapeDtypeStruct((t, d), F32),
        scratch_shapes=[pltpu.VMEM((tm, LANES), F32)] * n_stage,
        compiler_params=_cparams(("parallel",)))(h, *os_, *lses, w, g.reshape(1, d))


def _mlp_body(h_ref, g1_ref, w1_ref, w2_ref, g2_ref, o_ref, xn_ref, acc_ref):
    f = pl.program_id(1)
    last = pl.num_programs(1) - 1

    def partial_sum(xn):
        a = jnp.maximum(jnp.dot(xn, w1_ref[...], preferred_element_type=F32), 0.0)
        return jnp.dot((a * a).astype(BF16), w2_ref[...], preferred_element_type=F32)

    @pl.when(f == 0)
    def _():
        x = h_ref[...]
        y = x * lax.rsqrt(jnp.mean(x * x, axis=-1, keepdims=True) + EPS)
        xn = (y * g1_ref[...]).astype(BF16)
        xn_ref[...] = xn
        acc_ref[...] = partial_sum(xn)

    @pl.when((f > 0) & (f < last))
    def _():
        acc_ref[...] += partial_sum(xn_ref[...])

    @pl.when(f == last)
    def _():
        _finish(h_ref, acc_ref[...] + partial_sum(xn_ref[...]), g2_ref, o_ref)


def _mlp(h, g1, w1, w2, g2, tm=512, tf=1024):
    t, d = h.shape
    ff = w1.shape[1]
    tm = min(tm, t)
    return pl.pallas_call(
        _mlp_body, grid=(t // tm, ff // tf),
        in_specs=[pl.BlockSpec((tm, d), lambda i, f: (i, 0)),
                  pl.BlockSpec((1, d), lambda i, f: (0, 0)),
                  pl.BlockSpec((d, tf), lambda i, f: (0, f)),
                  pl.BlockSpec((tf, d), lambda i, f: (f, 0)),
                  pl.BlockSpec((1, d), lambda i, f: (0, 0))],
        out_specs=pl.BlockSpec((tm, d), lambda i, f: (i, 0)),
        out_shape=jax.ShapeDtypeStruct((t, d), F32),
        scratch_shapes=[pltpu.VMEM((tm, d), BF16), pltpu.VMEM((tm, d), F32)],
        compiler_params=_cparams(("parallel", "arbitrary")))(h, g1.reshape(1, d), w1, w2, g2.reshape(1, d))


def _rope64_tile(w):
    z = jnp.zeros((w.shape[0], 32), w.dtype)
    return jnp.concatenate([w[:, :32], z, w[:, 32:], z], axis=1)


def _layer0_w_in(w_in):
    d = w_in.shape[0]
    o1 = NSA_HEADS * HEAD_DIM
    o2 = o1 + 3 * 2 * NSA_KV_GROUPS * HEAD_DIM
    o3 = o2 + 3 * NSA_HEADS
    o4 = o3 + MLA_Q_RANK
    o5 = o4 + MLA_KV_RANK
    kv = w_in[:, o1:o2].reshape(d, 3, 2, NSA_KV_GROUPS * HEAD_DIM)
    kv = kv.transpose(0, 2, 1, 3).reshape(d, o2 - o1)
    gate = jnp.pad(w_in[:, o2:o3], ((0, 0), (0, LANES - (o3 - o2))))
    w = jnp.concatenate([w_in[:, :o1], kv, w_in[:, o3:o4], w_in[:, o4:o5], gate, _rope64_tile(w_in[:, o5:])], 1)
    assert w.shape[1] == Z0_COLS
    return w.astype(BF16)


def _mla_w_uq(w_uq):
    d = w_uq.shape[0]
    w = w_uq.reshape(d, MLA_HEADS, MLA_NOPE + MLA_ROPE)
    tiles = [jnp.concatenate([w[:, h, :MLA_NOPE], _rope64_tile(w[:, h, MLA_NOPE:])], 1) for h in range(MLA_HEADS)]
    return jnp.concatenate(tiles, axis=1).astype(BF16)


def _mla_w_ukv(w_ukv):
    d = w_ukv.shape[0]
    w = w_ukv.reshape(d, MLA_HEADS, MLA_NOPE + MLA_V)
    return jnp.concatenate([w[:, :, :MLA_NOPE].reshape(d, -1), w[:, :, MLA_NOPE:].reshape(d, -1)], 1).astype(BF16)


def _layer1_w_in(w_in):
    return w_in.astype(BF16)


def _overlap_matrix(nc, n_cmp, n_slc, nsp):
    ratio = NSA_SEL_LEN // NSA_CMP_STRIDE
    m = np.zeros((nc, nsp), np.float32)
    for off in range(1 - NSA_CMP_LEN // NSA_CMP_STRIDE, ratio):
        n = np.arange(n_slc) * ratio + off
        ok = (n >= 0) & (n < n_cmp)
        m[n[ok], np.arange(n_slc)[ok]] = 1.0
    return jnp.asarray(m, BF16)


def kernel(x, l0_norm_mix_pre, l0_w_in, l0_cmp_pe_k, l0_cmp_w1_k, l0_cmp_w2_k, l0_cmp_pe_v, l0_cmp_w1_v, l0_cmp_w2_v, l0_mla_q_norm, l0_mla_w_uq, l0_mla_kv_norm, l0_mla_w_ukv, l0_w_out, l0_norm_mix_post, l0_norm_ffn_pre, l0_w_ff1, l0_w_ff2, l0_norm_ffn_post, l1_norm_mix_pre, l1_w_in, l1_w_out, l1_norm_mix_post, l1_norm_ffn_pre, l1_w_ff1, l1_w_ff2, l1_norm_ffn_post):
    b, seq, d = x.shape
    t = b * seq
    assert seq % NSA_WINDOW == 0 and seq % (DIL_PATTERNS[-1][1] * DIL_SPAN) == 0
    rot = _rope_tables(seq)
    h = x.reshape(t, d)

    tabs0 = ([TAB_ROPE128_Q] * NSA_HEADS + [TAB_ROPE128] * (3 * NSA_KV_GROUPS)
             + [TAB_IDENT] * ((Z0_KROPE - Z0_VCMP) // LANES) + [TAB_ROPE64])
    cmp_tiles = [(col + g * HEAD_DIM) // LANES for col in (Z0_KCMP, Z0_VCMP) for g in range(NSA_KV_GROUPS)]
    z0, chunks = _proj_res(h, 0, d, l0_norm_mix_pre, _layer0_w_in(l0_w_in), seq, tabs0, rot,
                           chunk_tiles=cmp_tiles)

    nc = seq // NSA_CMP_STRIDE
    n_cmp = (seq - NSA_CMP_LEN) // NSA_CMP_STRIDE + 1
    n_slc = seq // NSA_SEL_LEN
    nsp = -(-n_slc // LANES) * LANES

    pe = jnp.stack([l0_cmp_pe_k.reshape(1, -1), l0_cmp_pe_v.reshape(1, -1)])
    kv_cmp = _compress(chunks.reshape(2, NSA_KV_GROUPS, b, nc, NSA_CMP_STRIDE * HEAD_DIM),
                       jnp.stack([l0_cmp_w1_k, l0_cmp_w1_v]).astype(BF16),
                       jnp.stack([l0_cmp_w2_k, l0_cmp_w2_v]).astype(BF16),
                       jnp.broadcast_to(pe, (2, SUBLANES, pe.shape[-1])).astype(F32))
    o_c, bias = _nsa_cmp(z0, kv_cmp, _overlap_matrix(nc, n_cmp, n_slc, nsp), b, seq)
    o_w = _nsa_win(z0, b, seq)
    blk_lane = (jnp.arange(seq, dtype=jnp.int32) // NSA_SEL_LEN) % LANES
    onehot = (blk_lane[:, None] == jnp.arange(LANES, dtype=jnp.int32)[None, :]).astype(BF16)
    o_s = _nsa_sel(z0, bias, onehot, b, seq)

    qcat = _proj_res(z0, Z0_CQ // MLA_Q_RANK, MLA_Q_RANK, l0_mla_q_norm, _mla_w_uq(l0_mla_w_uq), seq,
                     [TAB_IDENT_Q, TAB_ROPE64_Q] * MLA_HEADS, rot)
    kv = _proj_res(z0, Z0_CKV // MLA_KV_RANK, MLA_KV_RANK, l0_mla_kv_norm, _mla_w_ukv(l0_mla_w_ukv), seq,
                   [TAB_IDENT] * (2 * MLA_HEADS), rot)
    o_b = _mla(qcat, kv, z0, b, seq)

    h = _out0(h, o_c, o_s, o_w, z0, o_b, l0_w_out.astype(BF16), l0_norm_mix_post)
    h = _mlp(h, l0_norm_ffn_pre, l0_w_ff1.astype(BF16), l0_w_ff2.astype(BF16), l0_norm_ffn_post)

    hw = DIL_HEADS * HEAD_DIM
    tabs_g = ([TAB_ROPE128_Q] * DIL_HEADS + [TAB_ROPE128] * DIL_HEADS
              + [TAB_IDENT] * DIL_HEADS)
    w1_in = _layer1_w_in(l1_w_in)
    outs = []
    for p, (_, r) in enumerate(DIL_PATTERNS):
        z1 = _proj_res(h, 0, d, l1_norm_mix_pre, w1_in, seq, tabs_g, rot, dil=r, w_col=p)
        outs.append(_dilated(z1.reshape(b, seq // r, 3 * r * hw), r, b, seq))
    h = _out1(h, [o for o, _ in outs], [l for _, l in outs], [r for _, r in DIL_PATTERNS],
              l1_w_out.astype(BF16), l1_norm_mix_post)
    h = _mlp(h, l1_norm_ffn_pre, l1_w_ff1.astype(BF16), l1_w_ff2.astype(BF16), l1_norm_ffn_post)
    return h.reshape(b, seq, d)
```

```python
import functools

import numpy as np
import jax
import jax.numpy as jnp
from jax import lax
from jax.experimental import pallas as pl
from jax.experimental.pallas import tpu as pltpu

F32 = jnp.float32
BF16 = jnp.bfloat16

HEAD_DIM = 128
LANES = 128
SUBLANES = 8
BF16_ROWS = 16
ROPE_THETA = 10000.0
EPS = 1e-6
NEG_INF = -1e30
POS_INF = 1e30
PICKED = -3e38

NSA_HEADS = 8
NSA_KV_GROUPS = 2
NSA_GROUP_HEADS = NSA_HEADS // NSA_KV_GROUPS
NSA_CMP_LEN = 32
NSA_CMP_STRIDE = 16
NSA_SEL_LEN = 64
NSA_SEL_TOPK = 16
NSA_WINDOW = 512

MLA_HEADS = 8
MLA_Q_RANK = 512
MLA_KV_RANK = 256
MLA_NOPE = 128
MLA_ROPE = 64
MLA_V = 128

DIL_PATTERNS = ((128, 1), (512, 4), (2048, 16))
DIL_HEADS = 8
DIL_SPAN = 128

VMEM_LIMIT = 56 * 1024 * 1024
MAX_ROW_STRIDE = 4

Z0_Q = 0
Z0_KCMP = 1024
Z0_KSLC = 1280
Z0_KWIN = 1536
Z0_VCMP = 1792
Z0_VSLC = 2048
Z0_VWIN = 2304
Z0_CQ = 2560
Z0_CKV = 3072
Z0_GATE = 3328
Z0_KROPE = 3456
Z0_COLS = 3584

TAB_ROPE128, TAB_IDENT, TAB_ROPE64, TAB_ROPE128_Q, TAB_ROPE64_Q, TAB_IDENT_Q = 0, 1, 2, 3, 4, 5
TABLE_ROW = {TAB_ROPE128: 0, TAB_ROPE128_Q: 0, TAB_ROPE64: 1, TAB_ROPE64_Q: 1}

LOG2E = 1.4426950408889634
LN2 = 0.6931471805599453
QSCALE_128 = HEAD_DIM ** -0.5 * LOG2E
QSCALE_MLA = (MLA_NOPE + MLA_ROPE) ** -0.5 * LOG2E
KIND_SCALE = {TAB_ROPE128_Q: QSCALE_128, TAB_ROPE64_Q: QSCALE_MLA, TAB_IDENT_Q: QSCALE_MLA}


def _cparams(sem):
    return pltpu.CompilerParams(dimension_semantics=sem, vmem_limit_bytes=VMEM_LIMIT)


def _rope_tables(seq):
    lane = jnp.arange(LANES, dtype=jnp.int32)
    pair = lane % (LANES // 2)
    pos = jnp.arange(seq, dtype=F32)[:, None]
    sign = jnp.where(lane < LANES // 2, -1.0, 1.0).astype(F32)[None, :]

    def table(dim):
        inv = 1.0 / (ROPE_THETA ** ((2 * pair).astype(F32) / dim))
        ang = pos * inv[None, :]
        active = (pair < dim // 2)[None, :]
        return jnp.concatenate([jnp.where(active, jnp.cos(ang), 1.0),
                                jnp.where(active, jnp.sin(ang), 0.0) * sign], axis=1)

    return jnp.stack([table(HEAD_DIM), table(MLA_ROPE)])


def _residue_rows(part, r, tm, first, second):
    s1 = min(r, MAX_ROW_STRIDE)
    s2 = r // s1
    first[...] = part
    if s2 == 1:
        return [first[pl.ds(c, tm // r, stride=r), :] for c in range(r)]
    for c1 in range(s1):
        second[c1 * (tm // s1):(c1 + 1) * (tm // s1), :] = first[pl.ds(c1, tm // s1, stride=s1), :]
    return [second[pl.ds((c % s1) * (tm // s1) + c // s1, tm // r, stride=s2), :] for c in range(r)]


def _proj_res_body(*refs, kinds, tn, dil, chunk_tiles):
    x_ref, g_ref, w_ref = refs[:3]
    used = sorted({TABLE_ROW[k] for k in kinds if k in TABLE_ROW})
    t_refs = dict(zip(used, refs[3:3 + len(used)]))
    n_out = 2 if chunk_tiles else 1
    o_ref = refs[3 + len(used)]
    c_ref = refs[4 + len(used)] if chunk_tiles else None
    xn_ref = refs[3 + len(used) + n_out]
    stage_refs = refs[4 + len(used) + n_out:]
    chunk_stage = stage_refs[len(stage_refs) - 2 * len(chunk_tiles):]
    tm = x_ref.shape[0]
    x = x_ref[...].astype(F32)
    y = x * lax.rsqrt(jnp.mean(x * x, axis=-1, keepdims=True) + EPS)
    xn_ref[...] = (y * g_ref[...]).astype(BF16)
    nl = tn // LANES
    for j in range(w_ref.shape[1] // tn):
        acc = jnp.dot(xn_ref[...], w_ref[:, j * tn:(j + 1) * tn], preferred_element_type=F32)
        parts = []
        for k in range(nl):
            kind = kinds[j * nl + k]
            a = acc[:, k * LANES:(k + 1) * LANES]
            if kind in TABLE_ROW:
                tab = t_refs[TABLE_ROW[kind]]
                a = a * tab[:, :LANES] + pltpu.roll(a, LANES // 2, 1) * tab[:, LANES:]
            parts.append(a * KIND_SCALE[kind] if kind in KIND_SCALE else a)
        for k in range(nl):
            g = j * nl + k
            if g in chunk_tiles:
                e = chunk_tiles.index(g)
                rows = _residue_rows(parts[k], NSA_CMP_STRIDE, tm, chunk_stage[2 * e], chunk_stage[2 * e + 1])
                for c, piece in enumerate(rows):
                    c_ref[e, :, c * LANES:(c + 1) * LANES] = piece.astype(c_ref.dtype)
        if dil == 1:
            o_ref[:, j * tn:(j + 1) * tn] = jnp.concatenate(parts, axis=1).astype(o_ref.dtype)
            continue
        hw = DIL_HEADS * HEAD_DIM
        for k in range(nl):
            g = j * nl + k
            for c, piece in enumerate(_residue_rows(parts[k], dil, tm, stage_refs[2 * k], stage_refs[2 * k + 1])):
                col = ((g // DIL_HEADS) * dil + c) * hw + (g % DIL_HEADS) * LANES
                o_ref[:, col:col + LANES] = piece.astype(o_ref.dtype)


def _proj_res(x, x_col, d, g, w, seq, kinds, rot, tm=512, tn=512, dil=1, w_col=0, chunk_tiles=()):
    t = x.shape[0]
    n = len(kinds) * LANES
    assert w.shape[1] % n == 0
    tm = min(tm, seq)
    assert t % tm == 0 and seq % tm == 0 and n % tn == 0 and len(kinds) == n // LANES and tm % (BF16_ROWS * dil) == 0
    spb = seq // tm
    used = sorted({TABLE_ROW[k] for k in kinds if k in TABLE_ROW})
    tab_specs = [pl.BlockSpec((None, tm, 2 * LANES), lambda i, row=row: (row, i % spb, 0)) for row in used]
    scratch = [pltpu.VMEM((tm, d), BF16)]
    if dil > 1:
        scratch += [pltpu.VMEM((tm, LANES), F32)] * (2 * (tn // LANES))
    scratch += [pltpu.VMEM((tm, LANES), F32)] * (2 * len(chunk_tiles))
    out_specs = [pl.BlockSpec((tm // dil, dil * n), lambda i: (i, 0))]
    out_shape = [jax.ShapeDtypeStruct((t // dil, dil * n), BF16)]
    if chunk_tiles:
        cw = NSA_CMP_STRIDE * LANES
        out_specs.append(pl.BlockSpec((len(chunk_tiles), tm // NSA_CMP_STRIDE, cw), lambda i: (0, i, 0)))
        out_shape.append(jax.ShapeDtypeStruct((len(chunk_tiles), t // NSA_CMP_STRIDE, cw), BF16))
    res = pl.pallas_call(
        functools.partial(_proj_res_body, kinds=tuple(kinds), tn=tn, dil=dil,
                          chunk_tiles=tuple(chunk_tiles)),
        grid=(t // tm,),
        in_specs=[pl.BlockSpec((tm, d), lambda i: (i, x_col)),
                  pl.BlockSpec((1, d), lambda i: (0, 0)),
                  pl.BlockSpec((d, n), lambda i: (0, w_col), pipeline_mode=pl.Buffered(1))] + tab_specs,
        out_specs=out_specs, scratch_shapes=scratch, out_shape=out_shape,
        compiler_params=_cparams(("parallel",)))(x, g.reshape(1, d).astype(F32), w, *([rot] * len(used)))
    return res if chunk_tiles else res[0]


def _compress_body(c_ref, w1_ref, w2_ref, pe_ref, o_ref, *, nc):
    half = NSA_CMP_STRIDE * HEAD_DIM
    c = c_ref[...]
    a = jnp.dot(c, w1_ref[:half, :], preferred_element_type=F32)
    b = jnp.dot(c, w1_ref[half:, :], preferred_element_type=F32)
    pe = pe_ref[...]
    pe_hi = pe.astype(BF16)
    pe_lo = (pe - pe_hi.astype(F32)).astype(BF16)
    pe_term = (jnp.dot(pe_hi, w1_ref[...], preferred_element_type=F32)
               + jnp.dot(pe_lo, w1_ref[...], preferred_element_type=F32))
    hid = a + pltpu.roll(b, nc - 1, 0) + pe_term[0:1, :]
    act = jax.nn.gelu(hid)
    o_ref[...] = jnp.dot(act.astype(BF16), w2_ref[...], preferred_element_type=F32).astype(o_ref.dtype)


def _compress(chunks, w1, w2, pe):
    _, g, b, nc, cw = chunks.shape
    return pl.pallas_call(
        functools.partial(_compress_body, nc=nc), grid=(2, b, g),
        in_specs=[pl.BlockSpec((None, None, None, nc, cw), lambda s, i, j: (s, j, i, 0, 0)),
                  pl.BlockSpec((None, 2 * cw, HEAD_DIM), lambda s, i, j: (s, 0, 0)),
                  pl.BlockSpec((None, HEAD_DIM, HEAD_DIM), lambda s, i, j: (s, 0, 0)),
                  pl.BlockSpec((None, SUBLANES, 2 * cw), lambda s, i, j: (s, 0, 0))],
        out_specs=pl.BlockSpec((None, None, None, nc, HEAD_DIM), lambda s, i, j: (s, i, j, 0, 0)),
        out_shape=jax.ShapeDtypeStruct((2, b, g, nc, HEAD_DIM), BF16),
        compiler_params=_cparams(("parallel", "parallel", "parallel")))(chunks, w1, w2, pe)


def _nsa_cmp_body(q_ref, k_ref, v_ref, m_ref, o_ref, bias_ref, imp_scr, *, tq, nc, nsp, topk, nvar,
                  tiles_per_var):
    i = pl.program_id(2)
    t = i * tq + lax.broadcasted_iota(jnp.int32, (tq, 1), 0)
    any_vis = (t >= NSA_CMP_LEN - 1).astype(F32)

    def attend(ncols):
        n = lax.broadcasted_iota(jnp.int32, (1, ncols), 1)
        vis = (n * NSA_CMP_STRIDE + (NSA_CMP_LEN - 1)) <= t
        k = k_ref[:ncols, :]
        v = v_ref[:ncols, :]
        psum = jnp.zeros((tq, ncols), F32)
        for h in range(NSA_GROUP_HEADS):
            sl = slice(h * HEAD_DIM, (h + 1) * HEAD_DIM)
            s = lax.dot_general(q_ref[:, sl], k, (((1,), (1,)), ((), ())), preferred_element_type=F32)
            s = jnp.where(vis, s, NEG_INF)
            e = jnp.exp2(s - jnp.max(s, axis=-1, keepdims=True))
            p = e * (any_vis / jnp.sum(e, axis=-1, keepdims=True))
            o_ref[:, sl] = jnp.dot(p.astype(BF16), v, preferred_element_type=F32).astype(o_ref.dtype)
            psum = psum + p
        hi = psum.astype(BF16)
        lo = (psum - hi.astype(F32)).astype(BF16)
        imp_scr[...] = (jnp.dot(hi, m_ref[:ncols, :], preferred_element_type=F32)
                        + jnp.dot(lo, m_ref[:ncols, :], preferred_element_type=F32))

    def select(nblk):
        imp = imp_scr[...].T[:nblk]
        tl = i * tq + lax.broadcasted_iota(jnp.int32, (1, tq), 1)
        blk = lax.broadcasted_iota(jnp.int32, (nblk, 1), 0)
        blkf = blk.astype(F32)
        cur = lax.shift_right_logical(tl, NSA_SEL_LEN.bit_length() - 1)
        forced = (blk == 0) | (blk == cur) | (blk == cur - 1)
        causal = blk * NSA_SEL_LEN <= tl
        score = jnp.where(forced, PICKED, jnp.where(causal, imp, NEG_INF))

        def take_one(_, score):
            mx = jnp.max(score, axis=0, keepdims=True)
            first = jnp.min(jnp.where(score == mx, blkf, float(nsp)), axis=0, keepdims=True)
            return jnp.where(blkf == first, PICKED, score)

        score = lax.fori_loop(0, topk - 3, take_one, score)
        sel = (score == PICKED) & causal
        bias = jnp.where(sel, 0.0, NEG_INF)
        if nblk < nsp:
            bias = jnp.concatenate([bias, jnp.full((nsp - nblk, tq), NEG_INF, F32)], axis=0)
        bias_ref[...] = bias.T.astype(bias_ref.dtype)

    for var in range(nvar):
        @pl.when((i >= var * tiles_per_var) & (i < (var + 1) * tiles_per_var))
        def _(var=var):
            attend((var + 1) * nc // nvar)
            select((var + 1) * nsp // nvar)


def _nsa_cmp(z0, kv_cmp, ovl, b, seq, tq=512):
    nc = kv_cmp.shape[3]
    nsp = ovl.shape[1]
    tq = min(tq, seq)
    nq = seq // tq
    gw = NSA_GROUP_HEADS * HEAD_DIM
    topk = min(NSA_SEL_TOPK, seq // NSA_SEL_LEN)
    assert topk >= 3
    nvar = 4 if (nq % 4 == 0 and nc % (4 * 2 * LANES) == 0) else 1
    body = functools.partial(_nsa_cmp_body, tq=tq, nc=nc, nsp=nsp, topk=topk, nvar=nvar, tiles_per_var=nq // nvar)
    return pl.pallas_call(
        body, grid=(b, NSA_KV_GROUPS, nq),
        scratch_shapes=[pltpu.VMEM((tq, nsp), F32)],
        in_specs=[pl.BlockSpec((tq, gw), lambda bi, g, i: (bi * nq + i, g)),
                  pl.BlockSpec((None, None, None, nc, HEAD_DIM), lambda bi, g, i: (0, bi, g, 0, 0)),
                  pl.BlockSpec((None, None, None, nc, HEAD_DIM), lambda bi, g, i: (1, bi, g, 0, 0)),
                  pl.BlockSpec((nc, nsp), lambda bi, g, i: (0, 0))],
        out_specs=[pl.BlockSpec((tq, gw), lambda bi, g, i: (bi * nq + i, g)),
                   pl.BlockSpec((None, None, tq, nsp), lambda bi, g, i: (bi, g, i, 0))],
        out_shape=[jax.ShapeDtypeStruct((b * seq, NSA_HEADS * HEAD_DIM), BF16),
                   jax.ShapeDtypeStruct((b, NSA_KV_GROUPS, seq, nsp), BF16)],
        compiler_params=_cparams(("parallel", "parallel", "parallel")))(z0, kv_cmp, kv_cmp, ovl)


def _lane_tile(x, n):
    return jnp.concatenate([x] * n, axis=1) if n > 1 else x


def _softmax_pv(s, v):
    tq, tk = s.shape
    d = v.shape[1]
    m = jnp.broadcast_to(jnp.max(s, axis=-1, keepdims=True), (tq, LANES))
    p = jnp.exp2((s - _lane_tile(m, tk // LANES)).astype(BF16))
    acc = jnp.dot(p, jnp.concatenate([v, jnp.ones((tk, LANES), BF16)], axis=1), preferred_element_type=F32)
    return acc[:, :d], acc[:, d:], m


WIN_SUB = 256
DIL_SUB = 128


def _band_subtiles(i, tq, w, lo, hi, sub):
    ts = min(sub, tq)
    qq = lax.broadcasted_iota(jnp.int32, (ts, 1), 0)
    kk = lax.broadcasted_iota(jnp.int32, (1, ts + w), 1)
    rel = qq + w - kk
    band = (rel >= lo) & (rel <= hi)
    out = []
    for a in range(tq // ts):
        r0 = a * ts
        if r0 < w:
            mask = band & ((kk >= w - r0) | (i > 0))
            take = lambda p, c, lanes, r0=r0: jnp.concatenate([p[r0:, lanes], c[:r0 + ts, lanes]], axis=0)
        else:
            mask = band
            take = lambda p, c, lanes, r0=r0: c[r0 - w:r0 + ts, lanes]
        out.append((r0, ts, mask, take))
    return out


def _nsa_win_body(q_ref, kp_ref, kc_ref, vp_ref, vc_ref, o_ref, *, tq):
    i = pl.program_id(2)
    tiles = _band_subtiles(i, tq, NSA_WINDOW, 0, NSA_WINDOW - 1, WIN_SUB)
    for r0, ts, mask, take in tiles:
        k = take(kp_ref, kc_ref, slice(None))
        v = take(vp_ref, vc_ref, slice(None))
        scores = [jnp.where(mask, lax.dot_general(q_ref[r0:r0 + ts, h * HEAD_DIM:(h + 1) * HEAD_DIM], k,
                                                  (((1,), (1,)), ((), ())), preferred_element_type=F32),
                            NEG_INF) for h in range(NSA_GROUP_HEADS)]
        for h in range(NSA_GROUP_HEADS):
            num, den, _ = _softmax_pv(scores[h], v)
            o_ref[r0:r0 + ts, h * HEAD_DIM:(h + 1) * HEAD_DIM] = (num / den).astype(o_ref.dtype)


def _nsa_win(z0, b, seq):
    tq = NSA_WINDOW
    assert seq % tq == 0
    nq = seq // tq
    gw = NSA_GROUP_HEADS * HEAD_DIM
    kcol, vcol = Z0_KWIN // HEAD_DIM, Z0_VWIN // HEAD_DIM
    prev = lambda bi, i: bi * nq + jnp.maximum(i - 1, 0)
    return pl.pallas_call(
        functools.partial(_nsa_win_body, tq=tq), grid=(b, NSA_KV_GROUPS, nq),
        in_specs=[pl.BlockSpec((tq, gw), lambda bi, g, i: (bi * nq + i, g)),
                  pl.BlockSpec((tq, HEAD_DIM), lambda bi, g, i: (prev(bi, i), kcol + g)),
                  pl.BlockSpec((tq, HEAD_DIM), lambda bi, g, i: (bi * nq + i, kcol + g)),
                  pl.BlockSpec((tq, HEAD_DIM), lambda bi, g, i: (prev(bi, i), vcol + g)),
                  pl.BlockSpec((tq, HEAD_DIM), lambda bi, g, i: (bi * nq + i, vcol + g))],
        out_specs=pl.BlockSpec((tq, gw), lambda bi, g, i: (bi * nq + i, g)),
        out_shape=jax.ShapeDtypeStruct((b * seq, NSA_HEADS * HEAD_DIM), BF16),
        compiler_params=_cparams(("parallel", "parallel", "parallel")))(z0, z0, z0, z0, z0)


def _tri_schedule(nq):
    qi = np.concatenate([np.full(i + 1, i) for i in range(nq)]).astype(np.int32)
    kj = np.concatenate([np.arange(i + 1) for i in range(nq)]).astype(np.int32)
    return jnp.asarray(qi), jnp.asarray(kj)


def _flash_init(m_scr, acc_scr):
    m_scr[...] = jnp.full(m_scr.shape, NEG_INF, F32)
    acc_scr[...] = jnp.zeros(acc_scr.shape, F32)


def _flash_update(scores, vs, m_scr, acc_scr, rows=slice(None)):
    nh = len(scores)
    reps = scores[0].shape[1] // LANES
    ones = jnp.ones((vs[0].shape[0], LANES), BF16)
    m_prev = [m_scr[h, rows] for h in range(nh)]
    m_new = [jnp.maximum(m_prev[h], jnp.max(scores[h], axis=-1, keepdims=True)) for h in range(nh)]
    ps = [jnp.exp2((scores[h] - _lane_tile(m_new[h], reps)).astype(BF16)) for h in range(nh)]
    alphas = [jnp.exp2(m_prev[h] - m_new[h]) for h in range(nh)]
    for h in range(nh):
        v_aug = jnp.concatenate([vs[h], ones], axis=1)
        acc_scr[h, rows] = (_lane_tile(alphas[h], acc_scr.shape[2] // LANES) * acc_scr[h, rows]
                            + jnp.dot(ps[h], v_aug, preferred_element_type=F32))
        m_scr[h, rows] = m_new[h]


def _flash_out(acc_scr, h, d):
    acc = acc_scr[h]
    return acc[:, :d] / acc[:, d:]


DIAG_SUB = 256


def _flash_tile(qi, kj, tq, scores_and_values, m_scr, acc_scr, write_out):
    @pl.when(kj < qi)
    def _():
        scores, vs = scores_and_values(slice(None), tq)
        _flash_update(scores, vs, m_scr, acc_scr)

    @pl.when(kj == qi)
    def _():
        ts = min(DIAG_SUB, tq)
        for r0 in range(0, tq, ts):
            scores, vs = scores_and_values(slice(r0, r0 + ts), r0 + ts)
            row = r0 + lax.broadcasted_iota(jnp.int32, (ts, 1), 0)
            col = lax.broadcasted_iota(jnp.int32, (1, r0 + ts), 1)
            _flash_update([jnp.where(col <= row, s, NEG_INF) for s in scores], vs, m_scr, acc_scr,
                          slice(r0, r0 + ts))
        write_out()


def _nsa_sel_body(qi_ref, kj_ref, q_ref, k_ref, v_ref, oh_ref, bias_ref, o_ref,
                  qa_scr, m_scr, acc_scr, *, tq, per_slab):
    step = pl.program_id(2)
    qi = qi_ref[step]
    kj = kj_ref[step]

    @pl.when(kj == 0)
    def _():
        _flash_init(m_scr, acc_scr)
        for h in range(NSA_GROUP_HEADS):
            qa_scr[h, :, :HEAD_DIM] = q_ref[:, h * HEAD_DIM:(h + 1) * HEAD_DIM]

    @pl.when(kj % per_slab == 0)
    def _():
        for h in range(NSA_GROUP_HEADS):
            qa_scr[h, :, HEAD_DIM:] = bias_ref[...]

    def scores_and_values(rows, nkeys):
        k = jnp.concatenate([k_ref[:nkeys, :], oh_ref[:nkeys, :]], axis=1)
        scores = [lax.dot_general(qa_scr[h, rows], k, (((1,), (1,)), ((), ())), preferred_element_type=F32)
                  for h in range(NSA_GROUP_HEADS)]
        return scores, [v_ref[:nkeys, :]] * NSA_GROUP_HEADS

    def write_out():
        for h in range(NSA_GROUP_HEADS):
            o_ref[:, h * HEAD_DIM:(h + 1) * HEAD_DIM] = _flash_out(acc_scr, h, HEAD_DIM).astype(o_ref.dtype)

    _flash_tile(qi, kj, tq, scores_and_values, m_scr, acc_scr, write_out)


def _nsa_sel(z0, bias, onehot, b, seq, tq=1024):
    tq = min(tq, seq)
    nq = seq // tq
    gw = NSA_GROUP_HEADS * HEAD_DIM
    kcol, vcol = Z0_KSLC // HEAD_DIM, Z0_VSLC // HEAD_DIM
    per_slab = max(LANES * NSA_SEL_LEN // tq, 1)
    qi, kj = _tri_schedule(nq)
    grid_spec = pltpu.PrefetchScalarGridSpec(
        num_scalar_prefetch=2, grid=(b, NSA_KV_GROUPS, int(qi.shape[0])),
        in_specs=[pl.BlockSpec((tq, gw), lambda bi, g, s, qi, kj: (bi * nq + qi[s], g)),
                  pl.BlockSpec((tq, HEAD_DIM), lambda bi, g, s, qi, kj: (bi * nq + kj[s], kcol + g)),
                  pl.BlockSpec((tq, HEAD_DIM), lambda bi, g, s, qi, kj: (bi * nq + kj[s], vcol + g)),
                  pl.BlockSpec((tq, LANES), lambda bi, g, s, qi, kj: (kj[s], 0)),
                  pl.BlockSpec((None, None, tq, LANES),
                               lambda bi, g, s, qi, kj: (bi, g, qi[s], kj[s] // per_slab))],
        out_specs=pl.BlockSpec((tq, gw), lambda bi, g, s, qi, kj: (bi * nq + qi[s], g)),
        scratch_shapes=[pltpu.VMEM((NSA_GROUP_HEADS, tq, 2 * HEAD_DIM), BF16),
                        pltpu.VMEM((NSA_GROUP_HEADS, tq, LANES), F32),
                        pltpu.VMEM((NSA_GROUP_HEADS, tq, HEAD_DIM + LANES), F32)])
    return pl.pallas_call(
        functools.partial(_nsa_sel_body, tq=tq, per_slab=per_slab), grid_spec=grid_spec,
        out_shape=jax.ShapeDtypeStruct((b * seq, NSA_HEADS * HEAD_DIM), BF16),
        compiler_params=_cparams(("parallel", "parallel", "arbitrary")))(qi, kj, z0, z0, z0, onehot, bias)


MLA_STEP_HEADS = 4


def _mla_body(qi_ref, kj_ref, q_ref, kn_ref, kp_ref, v_ref, o_ref, m_scr, acc_scr, *, tq):
    step = pl.program_id(2)
    qi = qi_ref[step]
    kj = kj_ref[step]

    @pl.when(kj == 0)
    def _():
        _flash_init(m_scr, acc_scr)

    def scores_and_values(rows, nkeys):
        kp = kp_ref[:nkeys, :]
        scores, vs = [], []
        for h in range(MLA_STEP_HEADS):
            k = jnp.concatenate([kn_ref[:nkeys, h * MLA_NOPE:(h + 1) * MLA_NOPE], kp], axis=1)
            scores.append(lax.dot_general(q_ref[rows, h * 2 * LANES:(h + 1) * 2 * LANES], k,
                                          (((1,), (1,)), ((), ())), preferred_element_type=F32))
            vs.append(v_ref[:nkeys, h * MLA_V:(h + 1) * MLA_V])
        return scores, vs

    def write_out():
        for h in range(MLA_STEP_HEADS):
            o_ref[:, h * MLA_V:(h + 1) * MLA_V] = _flash_out(acc_scr, h, MLA_V).astype(o_ref.dtype)

    _flash_tile(qi, kj, tq, scores_and_values, m_scr, acc_scr, write_out)


def _mla(qcat, kv, z0, b, seq, tq=1024):
    tq = min(tq, seq)
    nq = seq // tq
    nh = MLA_STEP_HEADS
    ngrp = MLA_HEADS // nh
    kpcol = Z0_KROPE // LANES
    qi, kj = _tri_schedule(nq)
    grid_spec = pltpu.PrefetchScalarGridSpec(
        num_scalar_prefetch=2, grid=(b, ngrp, int(qi.shape[0])),
        in_specs=[pl.BlockSpec((tq, nh * 2 * LANES), lambda bi, h, s, qi, kj: (bi * nq + qi[s], h)),
                  pl.BlockSpec((tq, nh * MLA_NOPE), lambda bi, h, s, qi, kj: (bi * nq + kj[s], h)),
                  pl.BlockSpec((tq, LANES), lambda bi, h, s, qi, kj: (bi * nq + kj[s], kpcol)),
                  pl.BlockSpec((tq, nh * MLA_V), lambda bi, h, s, qi, kj: (bi * nq + kj[s], ngrp + h))],
        out_specs=pl.BlockSpec((tq, nh * MLA_V), lambda bi, h, s, qi, kj: (bi * nq + qi[s], h)),
        scratch_shapes=[pltpu.VMEM((nh, tq, LANES), F32), pltpu.VMEM((nh, tq, MLA_V + LANES), F32)])
    return pl.pallas_call(
        functools.partial(_mla_body, tq=tq), grid_spec=grid_spec,
        out_shape=jax.ShapeDtypeStruct((b * seq, MLA_HEADS * MLA_V), BF16),
        compiler_params=_cparams(("parallel", "parallel", "arbitrary")))(qi, kj, qcat, kv, z0, kv)


def _dil_body(q_ref, kp_ref, kc_ref, vp_ref, vc_ref, o_ref, lse_ref, *, tq):
    i = pl.program_id(2)
    for r0, ts, mask, take in _band_subtiles(i, tq, DIL_SPAN, 0, DIL_SPAN, DIL_SUB):
        lane = lax.broadcasted_iota(jnp.int32, (ts, LANES), 1)
        lse_all = jnp.zeros((ts, LANES), F32)
        for h in range(DIL_HEADS):
            sl = slice(h * HEAD_DIM, (h + 1) * HEAD_DIM)
            s = lax.dot_general(q_ref[r0:r0 + ts, sl], take(kp_ref, kc_ref, sl), (((1,), (1,)), ((), ())),
                                preferred_element_type=F32)
            num, den, m = _softmax_pv(jnp.where(mask, s, NEG_INF), take(vp_ref, vc_ref, sl))
            o_ref[r0:r0 + ts, sl] = (num / den).astype(o_ref.dtype)
            lse_all = jnp.where(lane == h, m * LN2 + jnp.log(den), lse_all)
        lse_ref[r0:r0 + ts, :] = lse_all


def _dilated(zv, r, b, seq, tq=512):
    sub = seq // r
    tq = min(tq, sub)
    assert sub % tq == 0 and tq % DIL_SPAN == 0
    nq = sub // tq
    hw = DIL_HEADS * HEAD_DIM
    ratio = tq // DIL_SPAN
    prev = lambda i: jnp.maximum(i * ratio - 1, 0)
    o, lse = pl.pallas_call(
        functools.partial(_dil_body, tq=tq), grid=(b, r, nq),
        in_specs=[pl.BlockSpec((None, tq, hw), lambda bi, c, i: (bi, i, c)),
                  pl.BlockSpec((None, DIL_SPAN, hw), lambda bi, c, i: (bi, prev(i), r + c)),
                  pl.BlockSpec((None, tq, hw), lambda bi, c, i: (bi, i, r + c)),
                  pl.BlockSpec((None, DIL_SPAN, hw), lambda bi, c, i: (bi, prev(i), 2 * r + c)),
                  pl.BlockSpec((None, tq, hw), lambda bi, c, i: (bi, i, 2 * r + c))],
        out_specs=[pl.BlockSpec((None, tq, hw), lambda bi, c, i: (bi, i, c)),
                   pl.BlockSpec((None, tq, LANES), lambda bi, c, i: (bi, i, c))],
        out_shape=[jax.ShapeDtypeStruct((b, sub, r * hw), BF16),
                   jax.ShapeDtypeStruct((b, sub, r * LANES), F32)],
        compiler_params=_cparams(("parallel", "parallel", "parallel")))(zv, zv, zv, zv, zv)
    return o.reshape(b * sub, r * hw), lse.reshape(b * sub, r * LANES)


def _finish(h_ref, m, g_ref, o_ref):
    y = m * lax.rsqrt(jnp.mean(m * m, axis=-1, keepdims=True) + EPS)
    o_ref[...] = h_ref[...] + y * g_ref[...]


def _out0_body(h_ref, oc_ref, os_ref, ow_ref, gate_ref, ob_ref, wa_ref, wb_ref, g_ref, o_ref):
    gate = jax.nn.sigmoid(gate_ref[...].astype(F32))
    parts = []
    for h in range(NSA_HEADS):
        sl = slice(h * HEAD_DIM, (h + 1) * HEAD_DIM)
        parts.append(gate[:, 3 * h:3 * h + 1] * oc_ref[:, sl].astype(F32)
                     + gate[:, 3 * h + 1:3 * h + 2] * os_ref[:, sl].astype(F32)
                     + gate[:, 3 * h + 2:3 * h + 3] * ow_ref[:, sl].astype(F32))
    oa = jnp.concatenate(parts, axis=1).astype(BF16)
    m = (jnp.dot(oa, wa_ref[...], preferred_element_type=F32)
         + jnp.dot(ob_ref[...], wb_ref[...], preferred_element_type=F32))
    _finish(h_ref, m, g_ref, o_ref)


def _to_token_order(piece, r, tm, first, second):
    s1 = min(r, MAX_ROW_STRIDE)
    s2 = r // s1
    if s2 == 1:
        for c in range(r):
            first[pl.ds(c, tm // r, stride=r), :] = piece(c)
        return
    for c in range(r):
        c1, c2 = c % s1, c // s1
        second[pl.ds(c1 * (tm // s1) + c2, tm // r, stride=s2), :] = piece(c)
    for c1 in range(s1):
        first[pl.ds(c1, tm // s1, stride=s1), :] = second[c1 * (tm // s1):(c1 + 1) * (tm // s1), :]


def _out1_body(*refs, dils, tm):
    n = len(dils)
    h_ref, o_refs, l_refs = refs[0], refs[1:1 + n], refs[1 + n:1 + 2 * n]
    w_ref, g_ref, o_ref = refs[1 + 2 * n:4 + 2 * n]
    stage = list(refs[4 + 2 * n:])
    hw = DIL_HEADS * HEAD_DIM
    outs, lses = [], []
    for p, r in enumerate(dils):
        if r == 1:
            outs.append([o_refs[p][:, h * HEAD_DIM:(h + 1) * HEAD_DIM].astype(F32) for h in range(DIL_HEADS)])
            lses.append(l_refs[p][...])
            continue
        bufs = []
        for j in range(DIL_HEADS + 1):
            first, second = stage.pop(0), stage.pop(0)
            if j < DIL_HEADS:
                piece = lambda c, j=j: o_refs[p][:, c * hw + j * HEAD_DIM:c * hw + (j + 1) * HEAD_DIM].astype(F32)
            else:
                piece = lambda c: l_refs[p][:, c * LANES:(c + 1) * LANES]
            _to_token_order(piece, r, tm, first, second)
            bufs.append(first)
        outs.append([bufs[h][...] for h in range(DIL_HEADS)])
        lses.append(bufs[DIL_HEADS][...])
    l0, l1, l2 = lses
    mx = jnp.maximum(jnp.maximum(l0, l1), l2)
    e0, e1, e2 = jnp.exp(l0 - mx), jnp.exp(l1 - mx), jnp.exp(l2 - mx)
    tot = e0 + e1 + e2
    a0, a1, a2 = e0 / tot, e1 / tot, e2 / tot
    parts = []
    for h in range(DIL_HEADS):
        parts.append(a0[:, h:h + 1] * outs[0][h] + a1[:, h:h + 1] * outs[1][h] + a2[:, h:h + 1] * outs[2][h])
    o = jnp.concatenate(parts, axis=1).astype(BF16)
    _finish(h_ref, jnp.dot(o, w_ref[...], preferred_element_type=F32), g_ref, o_ref)


def _row_spec(tm, w, col=0):
    return pl.BlockSpec((tm, w), lambda i: (i, col))


def _full_spec(shape):
    return pl.BlockSpec(shape, lambda i: (0,) * len(shape), pipeline_mode=pl.Buffered(1))


def _out0(h, o_c, o_s, o_w, z0, o_b, w, g, tm=512):
    t, d = h.shape
    tm = min(tm, t)
    ow = NSA_HEADS * HEAD_DIM
    w_half = lambda k: pl.BlockSpec((ow, d), lambda i: (k, 0), pipeline_mode=pl.Buffered(1))
    return pl.pallas_call(
        _out0_body, grid=(t // tm,),
        in_specs=[_row_spec(tm, d), _row_spec(tm, ow), _row_spec(tm, ow), _row_spec(tm, ow),
                  _row_spec(tm, LANES, Z0_GATE // LANES), _row_spec(tm, ow),
                  w_half(0), w_half(1), _full_spec((1, d))],
        out_specs=_row_spec(tm, d), out_shape=jax.ShapeDtypeStruct((t, d), F32),
        compiler_params=_cparams(("parallel",)))(h, o_c, o_s, o_w, z0, o_b, w, w, g.reshape(1, d))


def _out1(h, os_, lses, dils, w, g, tm=512):
    t, d = h.shape
    tm = min(tm, t)
    ow = DIL_HEADS * HEAD_DIM
    assert all(tm % (BF16_ROWS * r) == 0 for r in dils)
    n_stage = 2 * (DIL_HEADS + 1) * sum(1 for r in dils if r > 1)
    return pl.pallas_call(
        functools.partial(_out1_body, dils=tuple(dils), tm=tm), grid=(t // tm,),
        in_specs=[_row_spec(tm, d)] + [_row_spec(tm // r, r * ow) for r in dils]
                 + [_row_spec(tm // r, r * LANES) for r in dils] + [_full_spec(w.shape), _full_spec((1, d))],
        out_specs=_row_spec(tm, d), out_shape=jax.ShapeDtypeStruct((t, d), F32),
        scratch_shapes=[pltpu.VMEM((tm, LANES), F32)] * n_stage,
        compiler_params=_cparams(("parallel",)))(h, *os_, *lses, w, g.reshape(1, d))


def _mlp_body(h_ref, g1_ref, w1_ref, w2_ref, g2_ref, o_ref, xn_ref, acc_ref):
    f = pl.program_id(1)
    last = pl.num_programs(1) - 1

    def partial_sum(xn):
        a = jnp.maximum(jnp.dot(xn, w1_ref[...], preferred_element_type=F32), 0.0)
        return jnp.dot((a * a).astype(BF16), w2_ref[...], preferred_element_type=F32)

    @pl.when(f == 0)
    def _():
        x = h_ref[...]
        y = x * lax.rsqrt(jnp.mean(x * x, axis=-1, keepdims=True) + EPS)
        xn = (y * g1_ref[...]).astype(BF16)
        xn_ref[...] = xn
        acc_ref[...] = partial_sum(xn)

    @pl.when((f > 0) & (f < last))
    def _():
        acc_ref[...] += partial_sum(xn_ref[...])

    @pl.when(f == last)
    def _():
        _finish(h_ref, acc_ref[...] + partial_sum(xn_ref[...]), g2_ref, o_ref)


def _mlp(h, g1, w1, w2, g2, tm=512, tf=1024):
    t, d = h.shape
    ff = w1.shape[1]
    tm = min(tm, t)
    return pl.pallas_call(
        _mlp_body, grid=(t // tm, ff // tf),
        in_specs=[pl.BlockSpec((tm, d), lambda i, f: (i, 0)),
                  pl.BlockSpec((1, d), lambda i, f: (0, 0)),
                  pl.BlockSpec((d, tf), lambda i, f: (0, f)),
                  pl.BlockSpec((tf, d), lambda i, f: (f, 0)),
                  pl.BlockSpec((1, d), lambda i, f: (0, 0))],
        out_specs=pl.BlockSpec((tm, d), lambda i, f: (i, 0)),
        out_shape=jax.ShapeDtypeStruct((t, d), F32),
        scratch_shapes=[pltpu.VMEM((tm, d), BF16), pltpu.VMEM((tm, d), F32)],
        compiler_params=_cparams(("parallel", "arbitrary")))(h, g1.reshape(1, d), w1, w2, g2.reshape(1, d))


def _rope64_tile(w):
    z = jnp.zeros((w.shape[0], 32), w.dtype)
    return jnp.concatenate([w[:, :32], z, w[:, 32:], z], axis=1)


def _layer0_w_in(w_in):
    d = w_in.shape[0]
    o1 = NSA_HEADS * HEAD_DIM
    o2 = o1 + 3 * 2 * NSA_KV_GROUPS * HEAD_DIM
    o3 = o2 + 3 * NSA_HEADS
    o4 = o3 + MLA_Q_RANK
    o5 = o4 + MLA_KV_RANK
    kv = w_in[:, o1:o2].reshape(d, 3, 2, NSA_KV_GROUPS * HEAD_DIM)
    kv = kv.transpose(0, 2, 1, 3).reshape(d, o2 - o1)
    gate = jnp.pad(w_in[:, o2:o3], ((0, 0), (0, LANES - (o3 - o2))))
    w = jnp.concatenate([w_in[:, :o1], kv, w_in[:, o3:o4], w_in[:, o4:o5], gate, _rope64_tile(w_in[:, o5:])], 1)
    assert w.shape[1] == Z0_COLS
    return w.astype(BF16)


def _mla_w_uq(w_uq):
    d = w_uq.shape[0]
    w = w_uq.reshape(d, MLA_HEADS, MLA_NOPE + MLA_ROPE)
    tiles = [jnp.concatenate([w[:, h, :MLA_NOPE], _rope64_tile(w[:, h, MLA_NOPE:])], 1) for h in range(MLA_HEADS)]
    return jnp.concatenate(tiles, axis=1).astype(BF16)


def _mla_w_ukv(w_ukv):
    d = w_ukv.shape[0]
    w = w_ukv.reshape(d, MLA_HEADS, MLA_NOPE + MLA_V)
    return jnp.concatenate([w[:, :, :MLA_NOPE].reshape(d, -1), w[:, :, MLA_NOPE:].reshape(d, -1)], 1).astype(BF16)


def _layer1_w_in(w_in):
    return w_in.astype(BF16)


def _overlap_matrix(nc, n_cmp, n_slc, nsp):
    ratio = NSA_SEL_LEN // NSA_CMP_STRIDE
    m = np.zeros((nc, nsp), np.float32)
    for off in range(1 - NSA_CMP_LEN // NSA_CMP_STRIDE, ratio):
        n = np.arange(n_slc) * ratio + off
        ok = (n >= 0) & (n < n_cmp)
        m[n[ok], np.arange(n_slc)[ok]] = 1.0
    return jnp.asarray(m, BF16)


def kernel(x, l0_norm_mix_pre, l0_w_in, l0_cmp_pe_k, l0_cmp_w1_k, l0_cmp_w2_k, l0_cmp_pe_v, l0_cmp_w1_v, l0_cmp_w2_v, l0_mla_q_norm, l0_mla_w_uq, l0_mla_kv_norm, l0_mla_w_ukv, l0_w_out, l0_norm_mix_post, l0_norm_ffn_pre, l0_w_ff1, l0_w_ff2, l0_norm_ffn_post, l1_norm_mix_pre, l1_w_in, l1_w_out, l1_norm_mix_post, l1_norm_ffn_pre, l1_w_ff1, l1_w_ff2, l1_norm_ffn_post):
    b, seq, d = x.shape
    t = b * seq
    assert seq % NSA_WINDOW == 0 and seq % (DIL_PATTERNS[-1][1] * DIL_SPAN) == 0
    rot = _rope_tables(seq)
    h = x.reshape(t, d)

    tabs0 = ([TAB_ROPE128_Q] * NSA_HEADS + [TAB_ROPE128] * (3 * NSA_KV_GROUPS)
             + [TAB_IDENT] * ((Z0_KROPE - Z0_VCMP) // LANES) + [TAB_ROPE64])
    cmp_tiles = [(col + g * HEAD_DIM) // LANES for col in (Z0_KCMP, Z0_VCMP) for g in range(NSA_KV_GROUPS)]
    z0, chunks = _proj_res(h, 0, d, l0_norm_mix_pre, _layer0_w_in(l0_w_in), seq, tabs0, rot,
                           chunk_tiles=cmp_tiles)

    nc = seq // NSA_CMP_STRIDE
    n_cmp = (seq - NSA_CMP_LEN) // NSA_CMP_STRIDE + 1
    n_slc = seq // NSA_SEL_LEN
    nsp = -(-n_slc // LANES) * LANES

    pe = jnp.stack([l0_cmp_pe_k.reshape(1, -1), l0_cmp_pe_v.reshape(1, -1)])
    kv_cmp = _compress(chunks.reshape(2, NSA_KV_GROUPS, b, nc, NSA_CMP_STRIDE * HEAD_DIM),
                       jnp.stack([l0_cmp_w1_k, l0_cmp_w1_v]).astype(BF16),
                       jnp.stack([l0_cmp_w2_k, l0_cmp_w2_v]).astype(BF16),
                       jnp.broadcast_to(pe, (2, SUBLANES, pe.shape[-1])).astype(F32))
    o_c, bias = _nsa_cmp(z0, kv_cmp, _overlap_matrix(nc, n_cmp, n_slc, nsp), b, seq)
    o_w = _nsa_win(z0, b, seq)
    blk_lane = (jnp.arange(seq, dtype=jnp.int32) // NSA_SEL_LEN) % LANES
    onehot = (blk_lane[:, None] == jnp.arange(LANES, dtype=jnp.int32)[None, :]).astype(BF16)
    o_s = _nsa_sel(z0, bias, onehot, b, seq)

    qcat = _proj_res(z0, Z0_CQ // MLA_Q_RANK, MLA_Q_RANK, l0_mla_q_norm, _mla_w_uq(l0_mla_w_uq), seq,
                     [TAB_IDENT_Q, TAB_ROPE64_Q] * MLA_HEADS, rot)
    kv = _proj_res(z0, Z0_CKV // MLA_KV_RANK, MLA_KV_RANK, l0_mla_kv_norm, _mla_w_ukv(l0_mla_w_ukv), seq,
                   [TAB_IDENT] * (2 * MLA_HEADS), rot)
    o_b = _mla(qcat, kv, z0, b, seq)

    h = _out0(h, o_c, o_s, o_w, z0, o_b, l0_w_out.astype(BF16), l0_norm_mix_post)
    h = _mlp(h, l0_norm_ffn_pre, l0_w_ff1.astype(BF16), l0_w_ff2.astype(BF16), l0_norm_ffn_post)

    hw = DIL_HEADS * HEAD_DIM
    tabs_g = ([TAB_ROPE128_Q] * DIL_HEADS + [TAB_ROPE128] * DIL_HEADS
              + [TAB_IDENT] * DIL_HEADS)
    w1_in = _layer1_w_in(l1_w_in)
    outs = []
    for p, (_, r) in enumerate(DIL_PATTERNS):
        z1 = _proj_res(h, 0, d, l1_norm_mix_pre, w1_in, seq, tabs_g, rot, dil=r, w_col=p)
        outs.append(_dilated(z1.reshape(b, seq // r, 3 * r * hw), r, b, seq))
    h = _out1(h, [o for o, _ in outs], [l for _, l in outs], [r for _, r in DIL_PATTERNS],
              l1_w_out.astype(BF16), l1_norm_mix_post)
    h = _mlp(h, l1_norm_ffn_pre, l1_w_ff1.astype(BF16), l1_w_ff2.astype(BF16), l1_norm_ffn_post)
    return h.reshape(b, seq, d)
```

```python
import functools

import numpy as np
import jax
import jax.numpy as jnp
from jax import lax
from jax.experimental import pallas as pl
from jax.experimental.pallas import tpu as pltpu

F32 = jnp.float32
BF16 = jnp.bfloat16

HEAD_DIM = 128
LANES = 128
SUBLANES = 8
BF16_ROWS = 16
ROPE_THETA = 10000.0
EPS = 1e-6
NEG_INF = -1e30
POS_INF = 1e30
PICKED = -3e38

NSA_HEADS = 8
NSA_KV_GROUPS = 2
NSA_GROUP_HEADS = NSA_HEADS // NSA_KV_GROUPS
NSA_CMP_LEN = 32
NSA_CMP_STRIDE = 16
NSA_SEL_LEN = 64
NSA_SEL_TOPK = 16
NSA_WINDOW = 512

MLA_HEADS = 8
MLA_Q_RANK = 512
MLA_KV_RANK = 256
MLA_NOPE = 128
MLA_ROPE = 64
MLA_V = 128

DIL_PATTERNS = ((128, 1), (512, 4), (2048, 16))
DIL_HEADS = 8
DIL_SPAN = 128

VMEM_LIMIT = 56 * 1024 * 1024
MAX_ROW_STRIDE = 4

Z0_Q = 0
Z0_KCMP = 1024
Z0_KSLC = 1280
Z0_KWIN = 1536
Z0_VCMP = 1792
Z0_VSLC = 2048
Z0_VWIN = 2304
Z0_CQ = 2560
Z0_CKV = 3072
Z0_GATE = 3328
Z0_KROPE = 3456
Z0_COLS = 3584

TAB_ROPE128, TAB_IDENT, TAB_ROPE64, TAB_ROPE128_Q, TAB_ROPE64_Q, TAB_IDENT_Q = 0, 1, 2, 3, 4, 5
TABLE_ROW = {TAB_ROPE128: 0, TAB_ROPE128_Q: 0, TAB_ROPE64: 1, TAB_ROPE64_Q: 1}

LOG2E = 1.4426950408889634
LN2 = 0.6931471805599453
QSCALE_128 = HEAD_DIM ** -0.5 * LOG2E
QSCALE_MLA = (MLA_NOPE + MLA_ROPE) ** -0.5 * LOG2E
KIND_SCALE = {TAB_ROPE128_Q: QSCALE_128, TAB_ROPE64_Q: QSCALE_MLA, TAB_IDENT_Q: QSCALE_MLA}


def _cparams(sem):
    return pltpu.CompilerParams(dimension_semantics=sem, vmem_limit_bytes=VMEM_LIMIT)


def _rope_tables(seq):
    lane = jnp.arange(LANES, dtype=jnp.int32)
    pair = lane % (LANES // 2)
    pos = jnp.arange(seq, dtype=F32)[:, None]
    sign = jnp.where(lane < LANES // 2, -1.0, 1.0).astype(F32)[None, :]

    def table(dim):
        inv = 1.0 / (ROPE_THETA ** ((2 * pair).astype(F32) / dim))
        ang = pos * inv[None, :]
        active = (pair < dim // 2)[None, :]
        return jnp.concatenate([jnp.where(active, jnp.cos(ang), 1.0),
                                jnp.where(active, jnp.sin(ang), 0.0) * sign], axis=1)

    return jnp.stack([table(HEAD_DIM), table(MLA_ROPE)])


def _residue_rows(part, r, tm, first, second):
    s1 = min(r, MAX_ROW_STRIDE)
    s2 = r // s1
    first[...] = part
    if s2 == 1:
        return [first[pl.ds(c, tm // r, stride=r), :] for c in range(r)]
    for c1 in range(s1):
        second[c1 * (tm // s1):(c1 + 1) * (tm // s1), :] = first[pl.ds(c1, tm // s1, stride=s1), :]
    return [second[pl.ds((c % s1) * (tm // s1) + c // s1, tm // r, stride=s2), :] for c in range(r)]


def _proj_res_body(*refs, kinds, tn, dil, chunk_tiles):
    x_ref, g_ref, w_ref = refs[:3]
    used = sorted({TABLE_ROW[k] for k in kinds if k in TABLE_ROW})
    t_refs = dict(zip(used, refs[3:3 + len(used)]))
    n_out = 2 if chunk_tiles else 1
    o_ref = refs[3 + len(used)]
    c_ref = refs[4 + len(used)] if chunk_tiles else None
    xn_ref = refs[3 + len(used) + n_out]
    stage_refs = refs[4 + len(used) + n_out:]
    chunk_stage = stage_refs[len(stage_refs) - 2 * len(chunk_tiles):]
    tm = x_ref.shape[0]
    x = x_ref[...].astype(F32)
    y = x * lax.rsqrt(jnp.mean(x * x, axis=-1, keepdims=True) + EPS)
    xn_ref[...] = (y * g_ref[...]).astype(BF16)
    nl = tn // LANES
    for j in range(w_ref.shape[1] // tn):
        acc = jnp.dot(xn_ref[...], w_ref[:, j * tn:(j + 1) * tn], preferred_element_type=F32)
        parts = []
        for k in range(nl):
            kind = kinds[j * nl + k]
            a = acc[:, k * LANES:(k + 1) * LANES]
            if kind in TABLE_ROW:
                tab = t_refs[TABLE_ROW[kind]]
                a = a * tab[:, :LANES] + pltpu.roll(a, LANES // 2, 1) * tab[:, LANES:]
            parts.append(a * KIND_SCALE[kind] if kind in KIND_SCALE else a)
        for k in range(nl):
            g = j * nl + k
            if g in chunk_tiles:
                e = chunk_tiles.index(g)
                rows = _residue_rows(parts[k], NSA_CMP_STRIDE, tm, chunk_stage[2 * e], chunk_stage[2 * e + 1])
                for c, piece in enumerate(rows):
                    c_ref[e, :, c * LANES:(c + 1) * LANES] = piece.astype(c_ref.dtype)
        if dil == 1:
            o_ref[:, j * tn:(j + 1) * tn] = jnp.concatenate(parts, axis=1).astype(o_ref.dtype)
            continue
        hw = DIL_HEADS * HEAD_DIM
        for k in range(nl):
            g = j * nl + k
            for c, piece in enumerate(_residue_rows(parts[k], dil, tm, stage_refs[2 * k], stage_refs[2 * k + 1])):
                col = ((g // DIL_HEADS) * dil + c) * hw + (g % DIL_HEADS) * LANES
                o_ref[:, col:col + LANES] = piece.astype(o_ref.dtype)


def _proj_res(x, x_col, d, g, w, seq, kinds, rot, tm=512, tn=512, dil=1, w_col=0, chunk_tiles=()):
    t = x.shape[0]
    n = len(kinds) * LANES
    assert w.shape[1] % n == 0
    tm = min(tm, seq)
    assert t % tm == 0 and seq % tm == 0 and n % tn == 0 and len(kinds) == n // LANES and tm % (BF16_ROWS * dil) == 0
    spb = seq // tm
    used = sorted({TABLE_ROW[k] for k in kinds if k in TABLE_ROW})
    tab_specs = [pl.BlockSpec((None, tm, 2 * LANES), lambda i, row=row: (row, i % spb, 0)) for row in used]
    scratch = [pltpu.VMEM((tm, d), BF16)]
    if dil > 1:
        scratch += [pltpu.VMEM((tm, LANES), F32)] * (2 * (tn // LANES))
    scratch += [pltpu.VMEM((tm, LANES), F32)] * (2 * len(chunk_tiles))
    out_specs = [pl.BlockSpec((tm // dil, dil * n), lambda i: (i, 0))]
    out_shape = [jax.ShapeDtypeStruct((t // dil, dil * n), BF16)]
    if chunk_tiles:
        cw = NSA_CMP_STRIDE * LANES
        out_specs.append(pl.BlockSpec((len(chunk_tiles), tm // NSA_CMP_STRIDE, cw), lambda i: (0, i, 0)))
        out_shape.append(jax.ShapeDtypeStruct((len(chunk_tiles), t // NSA_CMP_STRIDE, cw), BF16))
    res = pl.pallas_call(
        functools.partial(_proj_res_body, kinds=tuple(kinds), tn=tn, dil=dil,
                          chunk_tiles=tuple(chunk_tiles)),
        grid=(t // tm,),
        in_specs=[pl.BlockSpec((tm, d), lambda i: (i, x_col)),
                  pl.BlockSpec((1, d), lambda i: (0, 0)),
                  pl.BlockSpec((d, n), lambda i: (0, w_col), pipeline_mode=pl.Buffered(1))] + tab_specs,
        out_specs=out_specs, scratch_shapes=scratch, out_shape=out_shape,
        compiler_params=_cparams(("parallel",)))(x, g.reshape(1, d).astype(F32), w, *([rot] * len(used)))
    return res if chunk_tiles else res[0]


def _compress_body(c_ref, w1_ref, w2_ref, pe_ref, o_ref, *, nc):
    half = NSA_CMP_STRIDE * HEAD_DIM
    c = c_ref[...]
    a = jnp.dot(c, w1_ref[:half, :], preferred_element_type=F32)
    b = jnp.dot(c, w1_ref[half:, :], preferred_element_type=F32)
    pe = pe_ref[...]
    pe_hi = pe.astype(BF16)
    pe_lo = (pe - pe_hi.astype(F32)).astype(BF16)
    pe_term = (jnp.dot(pe_hi, w1_ref[...], preferred_element_type=F32)
               + jnp.dot(pe_lo, w1_ref[...], preferred_element_type=F32))
    hid = a + pltpu.roll(b, nc - 1, 0) + pe_term[0:1, :]
    act = jax.nn.gelu(hid)
    o_ref[...] = jnp.dot(act.astype(BF16), w2_ref[...], preferred_element_type=F32).astype(o_ref.dtype)


def _compress(chunks, w1, w2, pe):
    _, g, b, nc, cw = chunks.shape
    return pl.pallas_call(
        functools.partial(_compress_body, nc=nc), grid=(2, b, g),
        in_specs=[pl.BlockSpec((None, None, None, nc, cw), lambda s, i, j: (s, j, i, 0, 0)),
                  pl.BlockSpec((None, 2 * cw, HEAD_DIM), lambda s, i, j: (s, 0, 0)),
                  pl.BlockSpec((None, HEAD_DIM, HEAD_DIM), lambda s, i, j: (s, 0, 0)),
                  pl.BlockSpec((None, SUBLANES, 2 * cw), lambda s, i, j: (s, 0, 0))],
        out_specs=pl.BlockSpec((None, None, None, nc, HEAD_DIM), lambda s, i, j: (s, i, j, 0, 0)),
        out_shape=jax.ShapeDtypeStruct((2, b, g, nc, HEAD_DIM), BF16),
        compiler_params=_cparams(("parallel", "parallel", "parallel")))(chunks, w1, w2, pe)


def _nsa_cmp_body(q_ref, k_ref, v_ref, m_ref, o_ref, bias_ref, imp_scr, *, tq, nc, nsp, topk, nvar,
                  tiles_per_var):
    i = pl.program_id(2)
    t = i * tq + lax.broadcasted_iota(jnp.int32, (tq, 1), 0)
    any_vis = (t >= NSA_CMP_LEN - 1).astype(F32)

    def attend(ncols):
        n = lax.broadcasted_iota(jnp.int32, (1, ncols), 1)
        vis = (n * NSA_CMP_STRIDE + (NSA_CMP_LEN - 1)) <= t
        k = k_ref[:ncols, :]
        v = v_ref[:ncols, :]
        psum = jnp.zeros((tq, ncols), F32)
        for h in range(NSA_GROUP_HEADS):
            sl = slice(h * HEAD_DIM, (h + 1) * HEAD_DIM)
            s = lax.dot_general(q_ref[:, sl], k, (((1,), (1,)), ((), ())), preferred_element_type=F32)
            s = jnp.where(vis, s, NEG_INF)
            e = jnp.exp2(s - jnp.max(s, axis=-1, keepdims=True))
            p = e * (any_vis / jnp.sum(e, axis=-1, keepdims=True))
            o_ref[:, sl] = jnp.dot(p.astype(BF16), v, preferred_element_type=F32).astype(o_ref.dtype)
            psum = psum + p
        hi = psum.astype(BF16)
        lo = (psum - hi.astype(F32)).astype(BF16)
        imp_scr[...] = (jnp.dot(hi, m_ref[:ncols, :], preferred_element_type=F32)
                        + jnp.dot(lo, m_ref[:ncols, :], preferred_element_type=F32))

    def select(nblk):
        imp = imp_scr[...].T[:nblk]
        tl = i * tq + lax.broadcasted_iota(jnp.int32, (1, tq), 1)
        blk = lax.broadcasted_iota(jnp.int32, (nblk, 1), 0)
        blkf = blk.astype(F32)
        cur = lax.shift_right_logical(tl, NSA_SEL_LEN.bit_length() - 1)
        forced = (blk == 0) | (blk == cur) | (blk == cur - 1)
        causal = blk * NSA_SEL_LEN <= tl
        score = jnp.where(forced, PICKED, jnp.where(causal, imp, NEG_INF))

        def take_one(_, score):
            mx = jnp.max(score, axis=0, keepdims=True)
            first = jnp.min(jnp.where(score == mx, blkf, float(nsp)), axis=0, keepdims=True)
            return jnp.where(blkf == first, PICKED, score)

        score = lax.fori_loop(0, topk - 3, take_one, score)
        sel = (score == PICKED) & causal
        bias = jnp.where(sel, 0.0, NEG_INF)
        if nblk < nsp:
            bias = jnp.concatenate([bias, jnp.full((nsp - nblk, tq), NEG_INF, F32)], axis=0)
        bias_ref[...] = bias.T.astype(bias_ref.dtype)

    for var in range(nvar):
        @pl.when((i >= var * tiles_per_var) & (i < (var + 1) * tiles_per_var))
        def _(var=var):
            attend((var + 1) * nc // nvar)
            select((var + 1) * nsp // nvar)


def _nsa_cmp(z0, kv_cmp, ovl, b, seq, tq=512):
    nc = kv_cmp.shape[3]
    nsp = ovl.shape[1]
    tq = min(tq, seq)
    nq = seq // tq
    gw = NSA_GROUP_HEADS * HEAD_DIM
    topk = min(NSA_SEL_TOPK, seq // NSA_SEL_LEN)
    assert topk >= 3
    nvar = 4 if (nq % 4 == 0 and nc % (4 * 2 * LANES) == 0) else 1
    body = functools.partial(_nsa_cmp_body, tq=tq, nc=nc, nsp=nsp, topk=topk, nvar=nvar, tiles_per_var=nq // nvar)
    return pl.pallas_call(
        body, grid=(b, NSA_KV_GROUPS, nq),
        scratch_shapes=[pltpu.VMEM((tq, nsp), F32)],
        in_specs=[pl.BlockSpec((tq, gw), lambda bi, g, i: (bi * nq + i, g)),
                  pl.BlockSpec((None, None, None, nc, HEAD_DIM), lambda bi, g, i: (0, bi, g, 0, 0)),
                  pl.BlockSpec((None, None, None, nc, HEAD_DIM), lambda bi, g, i: (1, bi, g, 0, 0)),
                  pl.BlockSpec((nc, nsp), lambda bi, g, i: (0, 0))],
        out_specs=[pl.BlockSpec((tq, gw), lambda bi, g, i: (bi * nq + i, g)),
                   pl.BlockSpec((None, None, tq, nsp), lambda bi, g, i: (bi, g, i, 0))],
        out_shape=[jax.ShapeDtypeStruct((b * seq, NSA_HEADS * HEAD_DIM), BF16),
                   jax.ShapeDtypeStruct((b, NSA_KV_GROUPS, seq, nsp), BF16)],
        compiler_params=_cparams(("parallel", "parallel", "parallel")))(z0, kv_cmp, kv_cmp, ovl)


def _lane_tile(x, n):
    return jnp.concatenate([x] * n, axis=1) if n > 1 else x


def _softmax_pv(s, v):
    tq, tk = s.shape
    d = v.shape[1]
    m = jnp.broadcast_to(jnp.max(s, axis=-1, keepdims=True), (tq, LANES))
    p = jnp.exp2((s - _lane_tile(m, tk // LANES)).astype(BF16))
    acc = jnp.dot(p, jnp.concatenate([v, jnp.ones((tk, LANES), BF16)], axis=1), preferred_element_type=F32)
    return acc[:, :d], acc[:, d:], m


WIN_SUB = 256
DIL_SUB = 128


def _band_subtiles(i, tq, w, lo, hi, sub):
    ts = min(sub, tq)
    qq = lax.broadcasted_iota(jnp.int32, (ts, 1), 0)
    kk = lax.broadcasted_iota(jnp.int32, (1, ts + w), 1)
    rel = qq + w - kk
    band = (rel >= lo) & (rel <= hi)
    out = []
    for a in range(tq // ts):
        r0 = a * ts
        if r0 < w:
            mask = band & ((kk >= w - r0) | (i > 0))
            take = lambda p, c, lanes, r0=r0: jnp.concatenate([p[r0:, lanes], c[:r0 + ts, lanes]], axis=0)
        else:
            mask = band
            take = lambda p, c, lanes, r0=r0: c[r0 - w:r0 + ts, lanes]
        out.append((r0, ts, mask, take))
    return out


def _nsa_win_body(q_ref, kp_ref, kc_ref, vp_ref, vc_ref, o_ref, *, tq):
    i = pl.program_id(2)
    tiles = _band_subtiles(i, tq, NSA_WINDOW, 0, NSA_WINDOW - 1, WIN_SUB)
    for r0, ts, mask, take in tiles:
        k = take(kp_ref, kc_ref, slice(None))
        v = take(vp_ref, vc_ref, slice(None))
        scores = [jnp.where(mask, lax.dot_general(q_ref[r0:r0 + ts, h * HEAD_DIM:(h + 1) * HEAD_DIM], k,
                                                  (((1,), (1,)), ((), ())), preferred_element_type=F32),
                            NEG_INF) for h in range(NSA_GROUP_HEADS)]
        for h in range(NSA_GROUP_HEADS):
            num, den, _ = _softmax_pv(scores[h], v)
            o_ref[r0:r0 + ts, h * HEAD_DIM:(h + 1) * HEAD_DIM] = (num / den).astype(o_ref.dtype)


def _nsa_win(z0, b, seq):
    tq = NSA_WINDOW
    assert seq % tq == 0
    nq = seq // tq
    gw = NSA_GROUP_HEADS * HEAD_DIM
    kcol, vcol = Z0_KWIN // HEAD_DIM, Z0_VWIN // HEAD_DIM
    prev = lambda bi, i: bi * nq + jnp.maximum(i - 1, 0)
    return pl.pallas_call(
        functools.partial(_nsa_win_body, tq=tq), grid=(b, NSA_KV_GROUPS, nq),
        in_specs=[pl.BlockSpec((tq, gw), lambda bi, g, i: (bi * nq + i, g)),
                  pl.BlockSpec((tq, HEAD_DIM), lambda bi, g, i: (prev(bi, i), kcol + g)),
                  pl.BlockSpec((tq, HEAD_DIM), lambda bi, g, i: (bi * nq + i, kcol + g)),
                  pl.BlockSpec((tq, HEAD_DIM), lambda bi, g, i: (prev(bi, i), vcol + g)),
                  pl.BlockSpec((tq, HEAD_DIM), lambda bi, g, i: (bi * nq + i, vcol + g))],
        out_specs=pl.BlockSpec((tq, gw), lambda bi, g, i: (bi * nq + i, g)),
        out_shape=jax.ShapeDtypeStruct((b * seq, NSA_HEADS * HEAD_DIM), BF16),
        compiler_params=_cparams(("parallel", "parallel", "parallel")))(z0, z0, z0, z0, z0)


def _tri_schedule(nq):
    qi = np.concatenate([np.full(i + 1, i) for i in range(nq)]).astype(np.int32)
    kj = np.concatenate([np.arange(i + 1) for i in range(nq)]).astype(np.int32)
    return jnp.asarray(qi), jnp.asarray(kj)


def _flash_init(m_scr, acc_scr):
    m_scr[...] = jnp.full(m_scr.shape, NEG_INF, F32)
    acc_scr[...] = jnp.zeros(acc_scr.shape, F32)


def _flash_update(scores, vs, m_scr, acc_scr, rows=slice(None)):
    nh = len(scores)
    reps = scores[0].shape[1] // LANES
    ones = jnp.ones((vs[0].shape[0], LANES), BF16)
    m_prev = [m_scr[h, rows] for h in range(nh)]
    m_new = [jnp.maximum(m_prev[h], jnp.max(scores[h], axis=-1, keepdims=True)) for h in range(nh)]
    ps = [jnp.exp2((scores[h] - _lane_tile(m_new[h], reps)).astype(BF16)) for h in range(nh)]
    alphas = [jnp.exp2(m_prev[h] - m_new[h]) for h in range(nh)]
    for h in range(nh):
        v_aug = jnp.concatenate([vs[h], ones], axis=1)
        acc_scr[h, rows] = (_lane_tile(alphas[h], acc_scr.shape[2] // LANES) * acc_scr[h, rows]
                            + jnp.dot(ps[h], v_aug, preferred_element_type=F32))
        m_scr[h, rows] = m_new[h]


def _flash_out(acc_scr, h, d):
    acc = acc_scr[h]
    return acc[:, :d] / acc[:, d:]


DIAG_SUB = 256


def _flash_tile(qi, kj, tq, scores_and_values, m_scr, acc_scr, write_out):
    @pl.when(kj < qi)
    def _():
        scores, vs = scores_and_values(slice(None), tq)
        _flash_update(scores, vs, m_scr, acc_scr)

    @pl.when(kj == qi)
    def _():
        ts = min(DIAG_SUB, tq)
        for r0 in range(0, tq, ts):
            scores, vs = scores_and_values(slice(r0, r0 + ts), r0 + ts)
            row = r0 + lax.broadcasted_iota(jnp.int32, (ts, 1), 0)
            col = lax.broadcasted_iota(jnp.int32, (1, r0 + ts), 1)
            _flash_update([jnp.where(col <= row, s, NEG_INF) for s in scores], vs, m_scr, acc_scr,
                          slice(r0, r0 + ts))
        write_out()


def _nsa_sel_body(qi_ref, kj_ref, q_ref, k_ref, v_ref, oh_ref, bias_ref, o_ref,
                  qa_scr, m_scr, acc_scr, *, tq, per_slab):
    step = pl.program_id(2)
    qi = qi_ref[step]
    kj = kj_ref[step]

    @pl.when(kj == 0)
    def _():
        _flash_init(m_scr, acc_scr)
        for h in range(NSA_GROUP_HEADS):
            qa_scr[h, :, :HEAD_DIM] = q_ref[:, h * HEAD_DIM:(h + 1) * HEAD_DIM]

    @pl.when(kj % per_slab == 0)
    def _():
        for h in range(NSA_GROUP_HEADS):
            qa_scr[h, :, HEAD_DIM:] = bias_ref[...]

    def scores_and_values(rows, nkeys):
        k = jnp.concatenate([k_ref[:nkeys, :], oh_ref[:nkeys, :]], axis=1)
        scores = [lax.dot_general(qa_scr[h, rows], k, (((1,), (1,)), ((), ())), preferred_element_type=F32)
                  for h in range(NSA_GROUP_HEADS)]
        return scores, [v_ref[:nkeys, :]] * NSA_GROUP_HEADS

    def write_out():
        for h in range(NSA_GROUP_HEADS):
            o_ref[:, h * HEAD_DIM:(h + 1) * HEAD_DIM] = _flash_out(acc_scr, h, HEAD_DIM).astype(o_ref.dtype)

    _flash_tile(qi, kj, tq, scores_and_values, m_scr, acc_scr, write_out)


def _nsa_sel(z0, bias, onehot, b, seq, tq=1024):
    tq = min(tq, seq)
    nq = seq // tq
    gw = NSA_GROUP_HEADS * HEAD_DIM
    kcol, vcol = Z0_KSLC // HEAD_DIM, Z0_VSLC // HEAD_DIM
    per_slab = max(LANES * NSA_SEL_LEN // tq, 1)
    qi, kj = _tri_schedule(nq)
    grid_spec = pltpu.PrefetchScalarGridSpec(
        num_scalar_prefetch=2, grid=(b, NSA_KV_GROUPS, int(qi.shape[0])),
        in_specs=[pl.BlockSpec((tq, gw), lambda bi, g, s, qi, kj: (bi * nq + qi[s], g)),
                  pl.BlockSpec((tq, HEAD_DIM), lambda bi, g, s, qi, kj: (bi * nq + kj[s], kcol + g)),
                  pl.BlockSpec((tq, HEAD_DIM), lambda bi, g, s, qi, kj: (bi * nq + kj[s], vcol + g)),
                  pl.BlockSpec((tq, LANES), lambda bi, g, s, qi, kj: (kj[s], 0)),
                  pl.BlockSpec((None, None, tq, LANES),
                               lambda bi, g, s, qi, kj: (bi, g, qi[s], kj[s] // per_slab))],
        out_specs=pl.BlockSpec((tq, gw), lambda bi, g, s, qi, kj: (bi * nq + qi[s], g)),
        scratch_shapes=[pltpu.VMEM((NSA_GROUP_HEADS, tq, 2 * HEAD_DIM), BF16),
                        pltpu.VMEM((NSA_GROUP_HEADS, tq, LANES), F32),
                        pltpu.VMEM((NSA_GROUP_HEADS, tq, HEAD_DIM + LANES), F32)])
    return pl.pallas_call(
        functools.partial(_nsa_sel_body, tq=tq, per_slab=per_slab), grid_spec=grid_spec,
        out_shape=jax.ShapeDtypeStruct((b * seq, NSA_HEADS * HEAD_DIM), BF16),
        compiler_params=_cparams(("parallel", "parallel", "arbitrary")))(qi, kj, z0, z0, z0, onehot, bias)


MLA_STEP_HEADS = 4


def _mla_body(qi_ref, kj_ref, q_ref, kn_ref, kp_ref, v_ref, o_ref, m_scr, acc_scr, *, tq):
    step = pl.program_id(2)
    qi = qi_ref[step]
    kj = kj_ref[step]

    @pl.when(kj == 0)
    def _():
        _flash_init(m_scr, acc_scr)

    def scores_and_values(rows, nkeys):
        kp = kp_ref[:nkeys, :]
        scores, vs = [], []
        for h in range(MLA_STEP_HEADS):
            k = jnp.concatenate([kn_ref[:nkeys, h * MLA_NOPE:(h + 1) * MLA_NOPE], kp], axis=1)
            scores.append(lax.dot_general(q_ref[rows, h * 2 * LANES:(h + 1) * 2 * LANES], k,
                                          (((1,), (1,)), ((), ())), preferred_element_type=F32))
            vs.append(v_ref[:nkeys, h * MLA_V:(h + 1) * MLA_V])
        return scores, vs

    def write_out():
        for h in range(MLA_STEP_HEADS):
            o_ref[:, h * MLA_V:(h + 1) * MLA_V] = _flash_out(acc_scr, h, MLA_V).astype(o_ref.dtype)

    _flash_tile(qi, kj, tq, scores_and_values, m_scr, acc_scr, write_out)


def _mla(qcat, kv, z0, b, seq, tq=1024):
    tq = min(tq, seq)
    nq = seq // tq
    nh = MLA_STEP_HEADS
    ngrp = MLA_HEADS // nh
    kpcol = Z0_KROPE // LANES
    qi, kj = _tri_schedule(nq)
    grid_spec = pltpu.PrefetchScalarGridSpec(
        num_scalar_prefetch=2, grid=(b, ngrp, int(qi.shape[0])),
        in_specs=[pl.BlockSpec((tq, nh * 2 * LANES), lambda bi, h, s, qi, kj: (bi * nq + qi[s], h)),
                  pl.BlockSpec((tq, nh * MLA_NOPE), lambda bi, h, s, qi, kj: (bi * nq + kj[s], h)),
                  pl.BlockSpec((tq, LANES), lambda bi, h, s, qi, kj: (bi * nq + kj[s], kpcol)),
                  pl.BlockSpec((tq, nh * MLA_V), lambda bi, h, s, qi, kj: (bi * nq + kj[s], ngrp + h))],
        out_specs=pl.BlockSpec((tq, nh * MLA_V), lambda bi, h, s, qi, kj: (bi * nq + qi[s], h)),
        scratch_shapes=[pltpu.VMEM((nh, tq, LANES), F32), pltpu.VMEM((nh, tq, MLA_V + LANES), F32)])
    return pl.pallas_call(
        functools.partial(_mla_body, tq=tq), grid_spec=grid_spec,
        out_shape=jax.ShapeDtypeStruct((b * seq, MLA_HEADS * MLA_V), BF16),
        compiler_params=_cparams(("parallel", "parallel", "arbitrary")))(qi, kj, qcat, kv, z0, kv)


def _dil_body(q_ref, kp_ref, kc_ref, vp_ref, vc_ref, o_ref, lse_ref, *, tq):
    i = pl.program_id(2)
    for r0, ts, mask, take in _band_subtiles(i, tq, DIL_SPAN, 0, DIL_SPAN, DIL_SUB):
        lane = lax.broadcasted_iota(jnp.int32, (ts, LANES), 1)
        lse_all = jnp.zeros((ts, LANES), F32)
        for h in range(DIL_HEADS):
            sl = slice(h * HEAD_DIM, (h + 1) * HEAD_DIM)
            s = lax.dot_general(q_ref[r0:r0 + ts, sl], take(kp_ref, kc_ref, sl), (((1,), (1,)), ((), ())),
                                preferred_element_type=F32)
            num, den, m = _softmax_pv(jnp.where(mask, s, NEG_INF), take(vp_ref, vc_ref, sl))
            o_ref[r0:r0 + ts, sl] = (num / den).astype(o_ref.dtype)
            lse_all = jnp.where(lane == h, m * LN2 + jnp.log(den), lse_all)
        lse_ref[r0:r0 + ts, :] = lse_all


def _dilated(zv, r, b, seq, tq=1024):
    sub = seq // r
    tq = min(tq, sub)
    assert sub % tq == 0 and tq % DIL_SPAN == 0
    nq = sub // tq
    hw = DIL_HEADS * HEAD_DIM
    ratio = tq // DIL_SPAN
    prev = lambda i: jnp.maximum(i * ratio - 1, 0)
    o, lse = pl.pallas_call(
        functools.partial(_dil_body, tq=tq), grid=(b, r, nq),
        in_specs=[pl.BlockSpec((None, tq, hw), lambda bi, c, i: (bi, i, c)),
                  pl.BlockSpec((None, DIL_SPAN, hw), lambda bi, c, i: (bi, prev(i), r + c)),
                  pl.BlockSpec((None, tq, hw), lambda bi, c, i: (bi, i, r + c)),
                  pl.BlockSpec((None, DIL_SPAN, hw), lambda bi, c, i: (bi, prev(i), 2 * r + c)),
                  pl.BlockSpec((None, tq, hw), lambda bi, c, i: (bi, i, 2 * r + c))],
        out_specs=[pl.BlockSpec((None, tq, hw), lambda bi, c, i: (bi, i, c)),
                   pl.BlockSpec((None, tq, LANES), lambda bi, c, i: (bi, i, c))],
        out_shape=[jax.ShapeDtypeStruct((b, sub, r * hw), BF16),
                   jax.ShapeDtypeStruct((b, sub, r * LANES), F32)],
        compiler_params=_cparams(("parallel", "parallel", "parallel")))(zv, zv, zv, zv, zv)
    return o.reshape(b * sub, r * hw), lse.reshape(b * sub, r * LANES)


def _finish(h_ref, m, g_ref, o_ref):
    y = m * lax.rsqrt(jnp.mean(m * m, axis=-1, keepdims=True) + EPS)
    o_ref[...] = h_ref[...] + y * g_ref[...]


def _out0_body(h_ref, oc_ref, os_ref, ow_ref, gate_ref, ob_ref, wa_ref, wb_ref, g_ref, o_ref):
    gate = jax.nn.sigmoid(gate_ref[...].astype(F32))
    parts = []
    for h in range(NSA_HEADS):
        sl = slice(h * HEAD_DIM, (h + 1) * HEAD_DIM)
        parts.append(gate[:, 3 * h:3 * h + 1] * oc_ref[:, sl].astype(F32)
                     + gate[:, 3 * h + 1:3 * h + 2] * os_ref[:, sl].astype(F32)
                     + gate[:, 3 * h + 2:3 * h + 3] * ow_ref[:, sl].astype(F32))
    oa = jnp.concatenate(parts, axis=1).astype(BF16)
    m = (jnp.dot(oa, wa_ref[...], preferred_element_type=F32)
         + jnp.dot(ob_ref[...], wb_ref[...], preferred_element_type=F32))
    _finish(h_ref, m, g_ref, o_ref)


def _to_token_order(piece, r, tm, first, second):
    s1 = min(r, MAX_ROW_STRIDE)
    s2 = r // s1
    if s2 == 1:
        for c in range(r):
            first[pl.ds(c, tm // r, stride=r), :] = piece(c)
        return
    for c in range(r):
        c1, c2 = c % s1, c // s1
        second[pl.ds(c1 * (tm // s1) + c2, tm // r, stride=s2), :] = piece(c)
    for c1 in range(s1):
        first[pl.ds(c1, tm // s1, stride=s1), :] = second[c1 * (tm // s1):(c1 + 1) * (tm // s1), :]


def _out1_body(*refs, dils, tm):
    n = len(dils)
    h_ref, o_refs, l_refs = refs[0], refs[1:1 + n], refs[1 + n:1 + 2 * n]
    w_ref, g_ref, o_ref = refs[1 + 2 * n:4 + 2 * n]
    stage = list(refs[4 + 2 * n:])
    hw = DIL_HEADS * HEAD_DIM
    outs, lses = [], []
    for p, r in enumerate(dils):
        if r == 1:
            outs.append([o_refs[p][:, h * HEAD_DIM:(h + 1) * HEAD_DIM].astype(F32) for h in range(DIL_HEADS)])
            lses.append(l_refs[p][...])
            continue
        bufs = []
        for j in range(DIL_HEADS + 1):
            first, second = stage.pop(0), stage.pop(0)
            if j < DIL_HEADS:
                piece = lambda c, j=j: o_refs[p][:, c * hw + j * HEAD_DIM:c * hw + (j + 1) * HEAD_DIM].astype(F32)
            else:
                piece = lambda c: l_refs[p][:, c * LANES:(c + 1) * LANES]
            _to_token_order(piece, r, tm, first, second)
            bufs.append(first)
        outs.append([bufs[h][...] for h in range(DIL_HEADS)])
        lses.append(bufs[DIL_HEADS][...])
    l0, l1, l2 = lses
    mx = jnp.maximum(jnp.maximum(l0, l1), l2)
    e0, e1, e2 = jnp.exp(l0 - mx), jnp.exp(l1 - mx), jnp.exp(l2 - mx)
    tot = e0 + e1 + e2
    a0, a1, a2 = e0 / tot, e1 / tot, e2 / tot
    parts = []
    for h in range(DIL_HEADS):
        parts.append(a0[:, h:h + 1] * outs[0][h] + a1[:, h:h + 1] * outs[1][h] + a2[:, h:h + 1] * outs[2][h])
    m = None
    for k in range(DIL_HEADS // 2):
        o2 = jnp.concatenate(parts[2 * k:2 * k + 2], axis=1).astype(BF16)
        d2 = jnp.dot(o2, w_ref[2 * k * HEAD_DIM:(2 * k + 2) * HEAD_DIM, :], preferred_element_type=F32)
        m = d2 if m is None else m + d2
    _finish(h_ref, m, g_ref, o_ref)


def _row_spec(tm, w, col=0):
    return pl.BlockSpec((tm, w), lambda i: (i, col))


def _full_spec(shape):
    return pl.BlockSpec(shape, lambda i: (0,) * len(shape), pipeline_mode=pl.Buffered(1))


def _out0(h, o_c, o_s, o_w, z0, o_b, w, g, tm=512):
    t, d = h.shape
    tm = min(tm, t)
    ow = NSA_HEADS * HEAD_DIM
    w_half = lambda k: pl.BlockSpec((ow, d), lambda i: (k, 0), pipeline_mode=pl.Buffered(1))
    return pl.pallas_call(
        _out0_body, grid=(t // tm,),
        in_specs=[_row_spec(tm, d), _row_spec(tm, ow), _row_spec(tm, ow), _row_spec(tm, ow),
                  _row_spec(tm, LANES, Z0_GATE // LANES), _row_spec(tm, ow),
                  w_half(0), w_half(1), _full_spec((1, d))],
        out_specs=_row_spec(tm, d), out_shape=jax.ShapeDtypeStruct((t, d), F32),
        compiler_params=_cparams(("parallel",)))(h, o_c, o_s, o_w, z0, o_b, w, w, g.reshape(1, d))


def _out1(h, os_, lses, dils, w, g, tm=512):
    t, d = h.shape
    tm = min(tm, t)
    ow = DIL_HEADS * HEAD_DIM
    assert all(tm % (BF16_ROWS * r) == 0 for r in dils)
    n_stage = 2 * (DIL_HEADS + 1) * sum(1 for r in dils if r > 1)
    return pl.pallas_call(
        functools.partial(_out1_body, dils=tuple(dils), tm=tm), grid=(t // tm,),
        in_specs=[_row_spec(tm, d)] + [_row_spec(tm // r, r * ow) for r in dils]
                 + [_row_spec(tm // r, r * LANES) for r in dils] + [_full_spec(w.shape), _full_spec((1, d))],
        out_specs=_row_spec(tm, d), out_shape=jax.ShapeDtypeStruct((t, d), F32),
        scratch_shapes=[pltpu.VMEM((tm, LANES), F32)] * n_stage,
        compiler_params=_cparams(("parallel",)))(h, *os_, *lses, w, g.reshape(1, d))


def _mlp_body(h_ref, g1_ref, w1_ref, w2_ref, g2_ref, o_ref, xn_ref, acc_ref):
    f = pl.program_id(1)
    last = pl.num_programs(1) - 1

    def partial_sum(xn):
        a = jnp.maximum(jnp.dot(xn, w1_ref[...], preferred_element_type=F32), 0.0)
        return jnp.dot((a * a).astype(BF16), w2_ref[...], preferred_element_type=F32)

    @pl.when(f == 0)
    def _():
        x = h_ref[...]
        y = x * lax.rsqrt(jnp.mean(x * x, axis=-1, keepdims=True) + EPS)
        xn = (y * g1_ref[...]).astype(BF16)
        xn_ref[...] = xn
        acc_ref[...] = partial_sum(xn)

    @pl.when((f > 0) & (f < last))
    def _():
        acc_ref[...] += partial_sum(xn_ref[...])

    @pl.when(f == last)
    def _():
        _finish(h_ref, acc_ref[...] + partial_sum(xn_ref[...]), g2_ref, o_ref)


def _mlp(h, g1, w1, w2, g2, tm=512, tf=1024):
    t, d = h.shape
    ff = w1.shape[1]
    tm = min(tm, t)
    return pl.pallas_call(
        _mlp_body, grid=(t // tm, ff // tf),
        in_specs=[pl.BlockSpec((tm, d), lambda i, f: (i, 0)),
                  pl.BlockSpec((1, d), lambda i, f: (0, 0)),
                  pl.BlockSpec((d, tf), lambda i, f: (0, f)),
                  pl.BlockSpec((tf, d), lambda i, f: (f, 0)),
                  pl.BlockSpec((1, d), lambda i, f: (0, 0))],
        out_specs=pl.BlockSpec((tm, d), lambda i, f: (i, 0)),
        out_shape=jax.ShapeDtypeStruct((t, d), F32),
        scratch_shapes=[pltpu.VMEM((tm, d), BF16), pltpu.VMEM((tm, d), F32)],
        compiler_params=_cparams(("parallel", "arbitrary")))(h, g1.reshape(1, d), w1, w2, g2.reshape(1, d))


def _rope64_tile(w):
    z = jnp.zeros((w.shape[0], 32), w.dtype)
    return jnp.concatenate([w[:, :32], z, w[:, 32:], z], axis=1)


def _layer0_w_in(w_in):
    d = w_in.shape[0]
    o1 = NSA_HEADS * HEAD_DIM
    o2 = o1 + 3 * 2 * NSA_KV_GROUPS * HEAD_DIM
    o3 = o2 + 3 * NSA_HEADS
    o4 = o3 + MLA_Q_RANK
    o5 = o4 + MLA_KV_RANK
    kv = w_in[:, o1:o2].reshape(d, 3, 2, NSA_KV_GROUPS * HEAD_DIM)
    kv = kv.transpose(0, 2, 1, 3).reshape(d, o2 - o1)
    gate = jnp.pad(w_in[:, o2:o3], ((0, 0), (0, LANES - (o3 - o2))))
    w = jnp.concatenate([w_in[:, :o1], kv, w_in[:, o3:o4], w_in[:, o4:o5], gate, _rope64_tile(w_in[:, o5:])], 1)
    assert w.shape[1] == Z0_COLS
    return w.astype(BF16)


def _mla_w_uq(w_uq):
    d = w_uq.shape[0]
    w = w_uq.reshape(d, MLA_HEADS, MLA_NOPE + MLA_ROPE)
    tiles = [jnp.concatenate([w[:, h, :MLA_NOPE], _rope64_tile(w[:, h, MLA_NOPE:])], 1) for h in range(MLA_HEADS)]
    return jnp.concatenate(tiles, axis=1).astype(BF16)


def _mla_w_ukv(w_ukv):
    d = w_ukv.shape[0]
    w = w_ukv.reshape(d, MLA_HEADS, MLA_NOPE + MLA_V)
    return jnp.concatenate([w[:, :, :MLA_NOPE].reshape(d, -1), w[:, :, MLA_NOPE:].reshape(d, -1)], 1).astype(BF16)


def _layer1_w_in(w_in):
    return w_in.astype(BF16)


def _overlap_matrix(nc, n_cmp, n_slc, nsp):
    ratio = NSA_SEL_LEN // NSA_CMP_STRIDE
    m = np.zeros((nc, nsp), np.float32)
    for off in range(1 - NSA_CMP_LEN // NSA_CMP_STRIDE, ratio):
        n = np.arange(n_slc) * ratio + off
        ok = (n >= 0) & (n < n_cmp)
        m[n[ok], np.arange(n_slc)[ok]] = 1.0
    return jnp.asarray(m, BF16)


def kernel(x, l0_norm_mix_pre, l0_w_in, l0_cmp_pe_k, l0_cmp_w1_k, l0_cmp_w2_k, l0_cmp_pe_v, l0_cmp_w1_v, l0_cmp_w2_v, l0_mla_q_norm, l0_mla_w_uq, l0_mla_kv_norm, l0_mla_w_ukv, l0_w_out, l0_norm_mix_post, l0_norm_ffn_pre, l0_w_ff1, l0_w_ff2, l0_norm_ffn_post, l1_norm_mix_pre, l1_w_in, l1_w_out, l1_norm_mix_post, l1_norm_ffn_pre, l1_w_ff1, l1_w_ff2, l1_norm_ffn_post):
    b, seq, d = x.shape
    t = b * seq
    assert seq % NSA_WINDOW == 0 and seq % (DIL_PATTERNS[-1][1] * DIL_SPAN) == 0
    rot = _rope_tables(seq)
    h = x.reshape(t, d)

    tabs0 = ([TAB_ROPE128_Q] * NSA_HEADS + [TAB_ROPE128] * (3 * NSA_KV_GROUPS)
             + [TAB_IDENT] * ((Z0_KROPE - Z0_VCMP) // LANES) + [TAB_ROPE64])
    cmp_tiles = [(col + g * HEAD_DIM) // LANES for col in (Z0_KCMP, Z0_VCMP) for g in range(NSA_KV_GROUPS)]
    z0, chunks = _proj_res(h, 0, d, l0_norm_mix_pre, _layer0_w_in(l0_w_in), seq, tabs0, rot,
                           chunk_tiles=cmp_tiles)

    nc = seq // NSA_CMP_STRIDE
    n_cmp = (seq - NSA_CMP_LEN) // NSA_CMP_STRIDE + 1
    n_slc = seq // NSA_SEL_LEN
    nsp = -(-n_slc // LANES) * LANES

    pe = jnp.stack([l0_cmp_pe_k.reshape(1, -1), l0_cmp_pe_v.reshape(1, -1)])
    kv_cmp = _compress(chunks.reshape(2, NSA_KV_GROUPS, b, nc, NSA_CMP_STRIDE * HEAD_DIM),
                       jnp.stack([l0_cmp_w1_k, l0_cmp_w1_v]).astype(BF16),
                       jnp.stack([l0_cmp_w2_k, l0_cmp_w2_v]).astype(BF16),
                       jnp.broadcast_to(pe, (2, SUBLANES, pe.shape[-1])).astype(F32))
    o_c, bias = _nsa_cmp(z0, kv_cmp, _overlap_matrix(nc, n_cmp, n_slc, nsp), b, seq)
    o_w = _nsa_win(z0, b, seq)
    blk_lane = (jnp.arange(seq, dtype=jnp.int32) // NSA_SEL_LEN) % LANES
    onehot = (blk_lane[:, None] == jnp.arange(LANES, dtype=jnp.int32)[None, :]).astype(BF16)
    o_s = _nsa_sel(z0, bias, onehot, b, seq)

    qcat = _proj_res(z0, Z0_CQ // MLA_Q_RANK, MLA_Q_RANK, l0_mla_q_norm, _mla_w_uq(l0_mla_w_uq), seq,
                     [TAB_IDENT_Q, TAB_ROPE64_Q] * MLA_HEADS, rot)
    kv = _proj_res(z0, Z0_CKV // MLA_KV_RANK, MLA_KV_RANK, l0_mla_kv_norm, _mla_w_ukv(l0_mla_w_ukv), seq,
                   [TAB_IDENT] * (2 * MLA_HEADS), rot)
    o_b = _mla(qcat, kv, z0, b, seq)

    h = _out0(h, o_c, o_s, o_w, z0, o_b, l0_w_out.astype(BF16), l0_norm_mix_post)
    h = _mlp(h, l0_norm_ffn_pre, l0_w_ff1.astype(BF16), l0_w_ff2.astype(BF16), l0_norm_ffn_post)

    hw = DIL_HEADS * HEAD_DIM
    tabs_g = ([TAB_ROPE128_Q] * DIL_HEADS + [TAB_ROPE128] * DIL_HEADS
              + [TAB_IDENT] * DIL_HEADS)
    w1_in = _layer1_w_in(l1_w_in)
    outs = []
    for p, (_, r) in enumerate(DIL_PATTERNS):
        z1 = _proj_res(h, 0, d, l1_norm_mix_pre, w1_in, seq, tabs_g, rot, dil=r, w_col=p)
        outs.append(_dilated(z1.reshape(b, seq // r, 3 * r * hw), r, b, seq))
    h = _out1(h, [o for o, _ in outs], [l for _, l in outs], [r for _, r in DIL_PATTERNS],
              l1_w_out.astype(BF16), l1_norm_mix_post)
    h = _mlp(h, l1_norm_ffn_pre, l1_w_ff1.astype(BF16), l1_w_ff2.astype(BF16), l1_norm_ffn_post)
    return h.reshape(b, seq, d)
```
